```python
import math
import jax, jax.numpy as jnp
from jax import lax
import numpy as np

D_MODEL = 1024
BATCH = 32
SEQ = 256
DEPTH = 2
DEC_BATCH = 2
DEC_SEQ = 2048
PAST_LEN = 512

GRID_W = 64
D_MIX = D_MODEL
D_POOL = D_MIX // 4
D_SSM = D_MIX // 4
D_ATTN = D_MIX // 4
D_CHUNK = D_MIX // 4
POOL_WINDOWS = (2, 4, 8, 16)
POOL_GROUP = D_POOL // len(POOL_WINDOWS)
SSM_GROUP = 16
N_SSM_GROUPS = D_SSM // SSM_GROUP
SSM_STATE = 64
HEAD_DIM = 64
N_HEADS = D_ATTN // HEAD_DIM
N_KV_HEADS = 2
D_KV = N_KV_HEADS * HEAD_DIM
Q_BLOCK = 128
CHUNK = 128
N_CHUNK_HEADS = 4
CHUNK_GROUP = D_CHUNK // N_CHUNK_HEADS
D_FF = 2816
N_MOD = 9
D_IN = D_POOL + D_SSM + D_ATTN + 2 * D_KV + 2 * D_CHUNK
IN_SPLITS = (D_POOL, D_POOL + D_SSM, D_POOL + D_SSM + D_ATTN,
             D_POOL + D_SSM + D_ATTN + D_KV, D_POOL + D_SSM + D_ATTN + 2 * D_KV)
ALPHA = (2 * DEPTH) ** 0.25
BETA = (8 * DEPTH) ** -0.25
LN_EPS = 1e-5
RMS_EPS = 1e-6
ROPE_THETA = 10000.0

kernel_name = 'hybrid_diffusion_prefix_step'


def layer_norm(x, g, b):
    xf = x.astype(jnp.float32)
    mu = jnp.mean(xf, axis=-1, keepdims=True)
    var = jnp.mean(jnp.square(xf - mu), axis=-1, keepdims=True)
    return ((xf - mu) * lax.rsqrt(var + LN_EPS)).astype(x.dtype) * g + b


def rms_norm(x, g):
    xf = x.astype(jnp.float32)
    return (xf * lax.rsqrt(jnp.mean(jnp.square(xf), axis=-1, keepdims=True) + RMS_EPS)).astype(x.dtype) * g


def swiglu(h, w1, w2):
    g, u = jnp.split(h @ w1, 2, axis=-1)
    return (jax.nn.silu(g) * u) @ w2


def axial_rope_tables(n_tok):
    rows = n_tok // GRID_W
    row_idx = jnp.repeat(jnp.arange(rows), GRID_W).astype(jnp.float32)
    col_idx = jnp.tile(jnp.arange(GRID_W), rows).astype(jnp.float32)
    n_freq = HEAD_DIM // 4
    inv = ROPE_THETA ** (-jnp.arange(n_freq, dtype=jnp.float32) / n_freq)
    ang = jnp.concatenate([row_idx[:, None] * inv, col_idx[:, None] * inv], axis=-1)
    return jnp.cos(ang), jnp.sin(ang)


def apply_rope(x, cos, sin):
    x1, x2 = jnp.split(x.astype(jnp.float32), 2, axis=-1)
    c = cos[None, :, None, :]
    s = sin[None, :, None, :]
    return jnp.concatenate([x1 * c - x2 * s, x1 * s + x2 * c], axis=-1).astype(x.dtype)


def block_attention(q, k, v):
    B, Lq, H, HD = q.shape
    rep = H // N_KV_HEADS
    kf = k.astype(jnp.float32)
    vf = v.astype(jnp.float32)
    qb = jnp.moveaxis(q.astype(jnp.float32).reshape(B, Lq // Q_BLOCK, Q_BLOCK, N_KV_HEADS, rep, HD), 1, 0)
    scale = HD ** -0.5

    def one_block(q_blk):
        s = jnp.einsum('bqgrd,bkgd->bgrqk', q_blk, kf) * scale
        p = jax.nn.softmax(s, axis=-1)
        return jnp.einsum('bgrqk,bkgd->bqgrd', p, vf)

    o = lax.map(one_block, qb)
    return jnp.moveaxis(o, 0, 1).reshape(B, Lq, H * HD).astype(q.dtype)


def pool_mix(p, pool_w, pool_scale):
    B, L, _ = p.shape
    pf = p.astype(jnp.float32)
    t = jnp.arange(L)
    outs = []
    for gi, w in enumerate(POOL_WINDOWS):
        pg = pf[..., gi * POOL_GROUP:(gi + 1) * POOL_GROUP]
        cs = jnp.pad(jnp.cumsum(pg, axis=1), ((0, 0), (1, 0), (0, 0)))
        lo = jnp.clip(t - w // 2, 0, L)
        hi = jnp.clip(t + w // 2, 0, L)
        mean = (jnp.take(cs, hi, axis=1) - jnp.take(cs, lo, axis=1)) / (hi - lo).astype(jnp.float32)[None, :, None]
        outs.append(jnp.einsum('blc,cd->bld', mean - pg, pool_w[gi].astype(jnp.float32)))
    return (jnp.concatenate(outs, axis=-1) * pool_scale).astype(p.dtype)


def _ssm_combine(e1, e2):
    a1, b1 = e1
    a2, b2 = e2
    return a2 * a1, a2 * b1 + b2


def s5_mix(u, h0_re, h0_im, lp):
    B, L, _ = u.shape
    f32 = jnp.float32
    uf = u.astype(f32)
    ug = uf.reshape(B, L, N_SSM_GROUPS, SSM_GROUP).astype(jnp.complex64)
    ys, fins = [], []
    for dirn in range(2):
        lam = lax.complex(lp['ssm_lam_re'][dirn].astype(f32), lp['ssm_lam_im'][dirn].astype(f32))
        dt = jnp.exp(lp['ssm_log_step'][dirn].astype(f32))[:, None]
        lam_bar = jnp.exp(lam * dt)
        b_mat = lax.complex(lp['ssm_b_re'][dirn].astype(f32), lp['ssm_b_im'][dirn].astype(f32))
        b_bar = ((lam_bar - 1.0) / lam)[..., None] * b_mat
        c_mat = lax.complex(lp['ssm_c_re'][dirn].astype(f32), lp['ssm_c_im'][dirn].astype(f32))
        h0 = lax.complex(h0_re[:, dirn].astype(f32), h0_im[:, dirn].astype(f32))
        seq = ug if dirn == 0 else ug[:, ::-1]
        bu = jnp.einsum('blgc,gnc->blgn', seq, b_bar)
        bu = bu.at[:, 0].add(lam_bar * h0)
        a = jnp.broadcast_to(lam_bar, bu.shape)
        _, hs = lax.associative_scan(_ssm_combine, (a, bu), axis=1)
        yd = jnp.einsum('blgn,gcn->blgc', hs, c_mat).real
        ys.append(yd if dirn == 0 else yd[:, ::-1])
        fins.append(hs[:, -1])
    y = (ys[0] + ys[1]).reshape(B, L, D_SSM) + lp['ssm_d'] * uf
    g = jax.nn.gelu(y)
    out = g * jax.nn.sigmoid(g @ lp['ssm_w_glu'] + lp['ssm_b_glu'])
    fin = jnp.stack(fins, axis=1)
    return out.astype(u.dtype), fin.real, fin.imag


def chunk_mix(uv, ln_g, ln_b, w_s, b_s):
    z = jax.nn.gelu(uv)
    u, v = jnp.split(z, 2, axis=-1)
    v = layer_norm(v, ln_g, ln_b)
    B, L, _ = v.shape
    vc = v.reshape(B, L // CHUNK, CHUNK, N_CHUNK_HEADS, CHUNK_GROUP)
    mixed = jnp.einsum('hpq,bnqhc->bnphc', w_s, vc) + b_s.T[None, None, :, :, None]
    return u * mixed.reshape(B, L, D_CHUNK)


def token_mixer(h, lp, rope, ctx):
    B, L, _ = h.shape
    proj = h @ lp['w_in']
    p_pool, u_ssm, q, k, v, uv = jnp.split(proj, IN_SPLITS, axis=-1)
    y_pool = pool_mix(p_pool, lp['pool_w'], lp['pool_scale'])
    if ctx is None:
        h0_re = jnp.zeros((B, 2, N_SSM_GROUPS, SSM_STATE), jnp.float32)
        h0_im = jnp.zeros((B, 2, N_SSM_GROUPS, SSM_STATE), jnp.float32)
    else:
        h0_re, h0_im = ctx[2], ctx[3]
    y_ssm, fin_re, fin_im = s5_mix(u_ssm, h0_re, h0_im, lp)
    q = rms_norm(q.reshape(B, L, N_HEADS, HEAD_DIM), lp['q_norm'])
    k = rms_norm(k.reshape(B, L, N_KV_HEADS, HEAD_DIM), lp['k_norm'])
    v = v.reshape(B, L, N_KV_HEADS, HEAD_DIM)
    if ctx is None:
        k_all, v_all = k, v
    else:
        cos, sin = rope
        q = apply_rope(q, cos, sin)
        k_lat = apply_rope(k, cos, sin)
        k_all = jnp.concatenate([ctx[0].astype(k.dtype), k_lat], axis=1)
        v_all = jnp.concatenate([ctx[1].astype(v.dtype), v], axis=1)
    y_attn = block_attention(q, k_all, v_all)
    y_chunk = chunk_mix(uv, lp['chunk_ln_g'], lp['chunk_ln_b'], lp['chunk_w_s'], lp['chunk_b_s'])
    y = jnp.concatenate([y_pool, y_ssm, y_attn, y_chunk], axis=-1) @ lp['w_out']
    return y, (k, v, fin_re, fin_im)


def trunk_layer(x, cond, lp, rope, ctx):
    m = jax.nn.silu(cond) @ lp['w_mod'] + lp['b_mod']
    sh1, sc1, g1, sh2, sc2, g2, sh3, sc3, g3 = [t[:, None, :] for t in jnp.split(m, N_MOD, axis=-1)]
    ffn = swiglu(x * (1.0 + sc1) + sh1, lp['ffn_w1'][0], lp['ffn_w2'][0])
    x = layer_norm(ALPHA * x + 0.5 * g1 * ffn, lp['ln_g'][0], lp['ln_b'][0])
    y, ctx_out = token_mixer(x * (1.0 + sc2) + sh2, lp, rope, ctx)
    x = layer_norm(ALPHA * x + g2 * y, lp['ln_g'][1], lp['ln_b'][1])
    ffn = swiglu(x * (1.0 + sc3) + sh3, lp['ffn_w1'][1], lp['ffn_w2'][1])
    x = layer_norm(ALPHA * x + 0.5 * g3 * ffn, lp['ln_g'][2], lp['ln_b'][2])
    return x, ctx_out


def setup_inputs(seed: int = 0) -> dict:
    key = jax.random.key(seed)
    keys = jax.random.split(key, 34)
    f32 = jnp.float32

    def nrm(i, shape, scale):
        return scale * jax.random.normal(keys[i], shape, f32)

    G, N, C = N_SSM_GROUPS, SSM_STATE, SSM_GROUP
    lam_im_init = np.pi * jnp.arange(N, dtype=f32)
    return {
        'x_prompt': nrm(0, (BATCH, SEQ, D_MODEL), 1.0),
        'x_sample': nrm(1, (DEC_BATCH, DEC_SEQ, D_MODEL), 1.0),
        'cache_k': nrm(2, (DEC_BATCH, DEPTH, PAST_LEN, N_KV_HEADS, HEAD_DIM), 1.0),
        'cache_v': nrm(3, (DEC_BATCH, DEPTH, PAST_LEN, N_KV_HEADS, HEAD_DIM), 1.0),
        'state_ssm_re': nrm(4, (DEC_BATCH, DEPTH, 2, G, N), 0.3),
        'state_ssm_im': nrm(5, (DEC_BATCH, DEPTH, 2, G, N), 0.3),
        'c': nrm(6, (DEC_BATCH, D_MODEL), 1.0),
        'c_ctx': nrm(7, (D_MODEL,), 1.0),
        'w_mod': nrm(8, (DEPTH, D_MODEL, N_MOD * D_MODEL), 0.5 * D_MODEL ** -0.5),
        'b_mod': nrm(9, (DEPTH, N_MOD * D_MODEL), 0.02),
        'ln_g': 1.0 + nrm(10, (DEPTH, 3, D_MODEL), 0.02),
        'ln_b': nrm(11, (DEPTH, 3, D_MODEL), 0.02),
        'ffn_w1': nrm(12, (DEPTH, 2, D_MODEL, 2 * D_FF), D_MODEL ** -0.5),
        'ffn_w2': nrm(13, (DEPTH, 2, D_FF, D_MODEL), BETA * D_FF ** -0.5),
        'w_in': nrm(14, (DEPTH, D_MODEL, D_IN), D_MODEL ** -0.5),
        'w_out': nrm(15, (DEPTH, D_MIX, D_MODEL), BETA * D_MIX ** -0.5),
        'pool_w': nrm(16, (DEPTH, len(POOL_WINDOWS), POOL_GROUP, POOL_GROUP), POOL_GROUP ** -0.5),
        'pool_scale': 1.0 + nrm(17, (DEPTH, D_POOL), 0.02),
        'ssm_lam_re': -0.5 + nrm(18, (DEPTH, 2, G, N), 0.01),
        'ssm_lam_im': lam_im_init + nrm(19, (DEPTH, 2, G, N), 0.01),
        'ssm_log_step': jax.random.uniform(keys[20], (DEPTH, 2, G), f32, math.log(1e-3), math.log(1e-1)),
        'ssm_b_re': nrm(21, (DEPTH, 2, G, N, C), (2 * C) ** -0.5),
        'ssm_b_im': nrm(22, (DEPTH, 2, G, N, C), (2 * C) ** -0.5),
        'ssm_c_re': nrm(23, (DEPTH, 2, G, C, N), N ** -0.5),
        'ssm_c_im': nrm(24, (DEPTH, 2, G, C, N), N ** -0.5),
        'ssm_d': nrm(25, (DEPTH, D_SSM), 0.5),
        'ssm_w_glu': nrm(26, (DEPTH, D_SSM, D_SSM), D_SSM ** -0.5),
        'ssm_b_glu': nrm(27, (DEPTH, D_SSM), 0.02),
        'q_norm': 1.0 + nrm(28, (DEPTH, HEAD_DIM), 0.02),
        'k_norm': 1.0 + nrm(29, (DEPTH, HEAD_DIM), 0.02),
        'chunk_ln_g': 1.0 + nrm(30, (DEPTH, D_CHUNK), 0.02),
        'chunk_ln_b': nrm(31, (DEPTH, D_CHUNK), 0.02),
        'chunk_w_s': nrm(32, (DEPTH, N_CHUNK_HEADS, CHUNK, CHUNK), CHUNK ** -0.5),
        'chunk_b_s': 1.0 + nrm(33, (DEPTH, N_CHUNK_HEADS, CHUNK), 0.02),
    }


def reference(x_prompt, x_sample, cache_k, cache_v, state_ssm_re, state_ssm_im, c, c_ctx,
              w_mod, b_mod, ln_g, ln_b, ffn_w1, ffn_w2, w_in, w_out, pool_w, pool_scale,
              ssm_lam_re, ssm_lam_im, ssm_log_step, ssm_b_re, ssm_b_im, ssm_c_re, ssm_c_im,
              ssm_d, ssm_w_glu, ssm_b_glu, q_norm, k_norm, chunk_ln_g, chunk_ln_b, chunk_w_s, chunk_b_s):
    rope = axial_rope_tables(x_sample.shape[1])
    y_p = x_prompt
    y_s = x_sample
    ks, vs, s_re, s_im = [], [], [], []
    for l in range(DEPTH):
        lp = {
            'w_mod': w_mod[l], 'b_mod': b_mod[l], 'ln_g': ln_g[l], 'ln_b': ln_b[l],
            'ffn_w1': ffn_w1[l], 'ffn_w2': ffn_w2[l], 'w_in': w_in[l], 'w_out': w_out[l],
            'pool_w': pool_w[l], 'pool_scale': pool_scale[l],
            'ssm_lam_re': ssm_lam_re[l], 'ssm_lam_im': ssm_lam_im[l], 'ssm_log_step': ssm_log_step[l],
            'ssm_b_re': ssm_b_re[l], 'ssm_b_im': ssm_b_im[l], 'ssm_c_re': ssm_c_re[l], 'ssm_c_im': ssm_c_im[l],
            'ssm_d': ssm_d[l], 'ssm_w_glu': ssm_w_glu[l], 'ssm_b_glu': ssm_b_glu[l],
            'q_norm': q_norm[l], 'k_norm': k_norm[l],
            'chunk_ln_g': chunk_ln_g[l], 'chunk_ln_b': chunk_ln_b[l],
            'chunk_w_s': chunk_w_s[l], 'chunk_b_s': chunk_b_s[l],
        }
        y_p, (k_l, v_l, fr_l, fi_l) = trunk_layer(y_p, c_ctx[None, :], lp, None, None)
        ks.append(k_l)
        vs.append(v_l)
        s_re.append(fr_l)
        s_im.append(fi_l)
        y_s, _ = trunk_layer(y_s, c, lp, rope,
                             (cache_k[:, l], cache_v[:, l], state_ssm_re[:, l], state_ssm_im[:, l]))
    new_cache_k = jnp.stack(ks, axis=1)
    new_cache_v = jnp.stack(vs, axis=1)
    new_state_ssm_re = jnp.stack(s_re, axis=1)
    new_state_ssm_im = jnp.stack(s_im, axis=1)
    return (y_p, y_s, new_cache_k, new_cache_v, new_state_ssm_re, new_state_ssm_im)
```

```python
import functools
import math

import jax
import jax.numpy as jnp
from jax import lax
from jax.experimental import pallas as pl
from jax.experimental.pallas import tpu as pltpu

F32 = jnp.float32
BF16 = jnp.bfloat16

D_MODEL = 1024
BATCH = 32
SEQ = 256
DEPTH = 2
DEC_BATCH = 2
DEC_SEQ = 2048
PAST_LEN = 512
GRID_W = 64
D_POOL = 256
D_SSM = 256
D_ATTN = 256
D_CHUNK = 256
POOL_WINDOWS = (2, 4, 8, 16)
POOL_GROUP = 64
SSM_GROUP = 16
N_SSM_GROUPS = 16
SSM_STATE = 64
HEAD_DIM = 64
N_HEADS = 4
N_KV_HEADS = 2
D_KV = 128
CHUNK = 128
N_CHUNK_HEADS = 4
D_FF = 2816
N_MOD = 9
D_IN = 1536
ALPHA = (2 * DEPTH) ** 0.25
LN_EPS = 1e-5
RMS_EPS = 1e-6
ROPE_THETA = 10000.0

LANES = 128
TM = 256
N_CTX = BATCH * SEQ
N_LAT = DEC_BATCH * DEC_SEQ
N_TOK = N_CTX + N_LAT
CTX_TILES = N_CTX // TM
LAT_TILES_PER_SEQ = DEC_SEQ // TM
N_TILES = N_TOK // TM
N_COND = 8
POOL_PAD = 8
SSM_LANES = 2 * N_SSM_GROUPS * SSM_STATE
SSM_W = 1024
SSM_CTX_SEQS = 8
SSM_ROWS = 256
VMEM_LIMIT = 56 * 1024 * 1024


def _params(n_grid):
    return pltpu.CompilerParams(dimension_semantics=("arbitrary",) * n_grid,
                                vmem_limit_bytes=VMEM_LIMIT)


def _const_spec(shape):
    zeros = (0,) * len(shape)
    return pl.BlockSpec(shape, lambda *_: zeros, pipeline_mode=pl.Buffered(1))


def _cond_row(i):
    return jnp.where(i < CTX_TILES, 0, 1 + (i - CTX_TILES) // LAT_TILES_PER_SEQ)


def _layer_norm(y, g, b):
    mu = jnp.mean(y, axis=-1, keepdims=True)
    d = y - mu
    var = jnp.mean(d * d, axis=-1, keepdims=True)
    return d * lax.rsqrt(var + LN_EPS) * g + b


def _gelu(x):
    return 0.5 * x * (1.0 + jnp.tanh(math.sqrt(2.0 / math.pi) * (x + 0.044715 * (x * x * x))))


def _silu(x):
    return x * jax.nn.sigmoid(x)


def _mod_kernel(cond_ref, w_ref, b_ref, o_ref):
    a = _silu(cond_ref[...])
    o_ref[...] = jnp.dot(a, w_ref[...], precision=lax.Precision.HIGHEST,
                         preferred_element_type=F32) + b_ref[...]


def _modulation(cond, w_mod, b_mod):
    tn = D_MODEL
    out = pl.pallas_call(
        _mod_kernel,
        grid=(DEPTH, N_MOD),
        in_specs=[
            pl.BlockSpec((N_COND, D_MODEL), lambda l, j: (0, 0)),
            pl.BlockSpec((None, D_MODEL, tn), lambda l, j: (l, 0, j)),
            pl.BlockSpec((None, 1, tn), lambda l, j: (l, 0, j)),
        ],
        out_specs=pl.BlockSpec((None, N_COND, tn), lambda l, j: (l, 0, j)),
        out_shape=jax.ShapeDtypeStruct((DEPTH, N_COND, N_MOD * D_MODEL), F32),
        compiler_params=_params(2),
        name="modulation",
    )(cond, w_mod, b_mod.reshape(DEPTH, 1, N_MOD * D_MODEL))
    return out.reshape(DEPTH, N_COND, N_MOD, D_MODEL)


def _ffn_kernel(x_ref, mod_ref, w1_ref, w2_ref, g_ref, b_ref, o_ref, *, mod_off):
    x = x_ref[...]
    sh = mod_ref[mod_off:mod_off + 1, :]
    sc = mod_ref[mod_off + 1:mod_off + 2, :]
    gate = mod_ref[mod_off + 2:mod_off + 3, :]
    h = (x * (1.0 + sc) + sh).astype(BF16)
    gu = jnp.dot(h, w1_ref[...], preferred_element_type=F32)
    a = (_silu(gu[:, :D_FF]) * gu[:, D_FF:]).astype(BF16)
    f = jnp.dot(a, w2_ref[...], preferred_element_type=F32)
    y = ALPHA * x + (0.5 * gate) * f
    o_ref[...] = _layer_norm(y, g_ref[...], b_ref[...])


def _ffn_sublayer(x, mod_l, w1, w2, ln_g, ln_b, mod_off):
    return pl.pallas_call(
        functools.partial(_ffn_kernel, mod_off=mod_off),
        grid=(N_TILES,),
        in_specs=[
            pl.BlockSpec((TM, D_MODEL), lambda i: (i, 0)),
            pl.BlockSpec((None, N_MOD, D_MODEL), lambda i: (_cond_row(i), 0, 0)),
            _const_spec((D_MODEL, 2 * D_FF)),
            _const_spec((D_FF, D_MODEL)),
            _const_spec((1, D_MODEL)),
            _const_spec((1, D_MODEL)),
        ],
        out_specs=pl.BlockSpec((TM, D_MODEL), lambda i: (i, 0)),
        out_shape=jax.ShapeDtypeStruct((N_TOK, D_MODEL), F32),
        compiler_params=_params(1),
        name="ffn_sublayer",
    )(x, mod_l, w1, w2, ln_g, ln_b)


def _seg_rms(x, gain, n_lanes):
    parts = []
    for j in range(n_lanes // LANES):
        xs = x[:, j * LANES:(j + 1) * LANES]
        sq = xs * xs
        lo = lax.broadcasted_iota(jnp.int32, xs.shape, 1) < HEAD_DIM
        s_all = jnp.sum(sq, axis=-1, keepdims=True)
        s_lo = jnp.sum(jnp.where(lo, sq, 0.0), axis=-1, keepdims=True)
        ms = jnp.where(lo, s_lo, s_all - s_lo) * (1.0 / HEAD_DIM)
        parts.append(xs * lax.rsqrt(ms + RMS_EPS))
    y = parts[0] if len(parts) == 1 else jnp.concatenate(parts, axis=1)
    return y * gain


def _rope(x, cos_t, sin_t, n_lanes):
    parts = []
    for j in range(n_lanes // LANES):
        xs = x[:, j * LANES:(j + 1) * LANES]
        first = (lax.broadcasted_iota(jnp.int32, xs.shape, 1) % HEAD_DIM) < (HEAD_DIM // 2)
        partner = jnp.where(first, pltpu.roll(xs, LANES - HEAD_DIM // 2, axis=1),
                            pltpu.roll(xs, HEAD_DIM // 2, axis=1))
        cs = cos_t[:, j * LANES:(j + 1) * LANES]
        sn = sin_t[:, j * LANES:(j + 1) * LANES]
        parts.append(xs * cs + partner * sn)
    return parts[0] if len(parts) == 1 else jnp.concatenate(parts, axis=1)


def _inproj_kernel(x_ref, mod_ref, w_ref, cos_ref, sin_ref, qn_ref, kn_ref,
                   cg_ref, cb_ref, ws_ref, bs_ref,
                   pool_ref, ssm_ref, q_ref, k_ref, v_ref, chunk_ref):
    x = x_ref[...]
    sh = mod_ref[3:4, :]
    sc = mod_ref[4:5, :]
    h = (x * (1.0 + sc) + sh).astype(BF16)
    proj = jnp.dot(h, w_ref[...], preferred_element_type=F32)
    pool_ref[...] = proj[:, 0:256]
    ssm_ref[...] = proj[:, 256:512]
    cos_t = cos_ref[...]
    sin_t = sin_ref[...]
    q = _seg_rms(proj[:, 512:768], qn_ref[...], D_ATTN)
    q_ref[...] = _rope(q, cos_t, sin_t, D_ATTN)
    k = _seg_rms(proj[:, 768:896], kn_ref[...], D_KV)
    k_ref[...] = _rope(k, cos_t, sin_t, D_KV)
    v_ref[...] = proj[:, 896:1024]
    zu = _gelu(proj[:, 1024:1280])
    zv = _layer_norm(_gelu(proj[:, 1280:1536]), cg_ref[...], cb_ref[...])
    head = lax.broadcasted_iota(jnp.int32, (CHUNK, D_CHUNK), 1) // (D_CHUNK // N_CHUNK_HEADS)
    for c in range(TM // CHUNK):
        vb = zv[c * CHUNK:(c + 1) * CHUNK, :].astype(BF16)
        mixed = bs_ref[...]
        for hd in range(N_CHUNK_HEADS):
            r = jnp.dot(ws_ref[hd], vb, preferred_element_type=F32)
            mixed = mixed + jnp.where(head == hd, r, 0.0)
        chunk_ref[c * CHUNK:(c + 1) * CHUNK, :] = zu[c * CHUNK:(c + 1) * CHUNK, :] * mixed


def _rope_block(i):
    return jnp.where(i < CTX_TILES, LAT_TILES_PER_SEQ, (i - CTX_TILES) % LAT_TILES_PER_SEQ)


def _inproj(x, mod_l, w_in, cos_t, sin_t, qn, kn, cg, cb, ws, bs):
    tok = lambda i: (i, 0)
    out_shapes = [jax.ShapeDtypeStruct((N_TOK, w), F32) for w in (256, 256, 256, 128, 128, 256)]
    out_specs = [pl.BlockSpec((TM, w), tok) for w in (256, 256, 256, 128, 128, 256)]
    return pl.pallas_call(
        _inproj_kernel,
        grid=(N_TILES,),
        in_specs=[
            pl.BlockSpec((TM, D_MODEL), tok),
            pl.BlockSpec((None, N_MOD, D_MODEL), lambda i: (_cond_row(i), 0, 0)),
            _const_spec((D_MODEL, D_IN)),
            pl.BlockSpec((TM, D_ATTN), lambda i: (_rope_block(i), 0)),
            pl.BlockSpec((TM, D_ATTN), lambda i: (_rope_block(i), 0)),
            _const_spec((1, D_ATTN)),
            _const_spec((1, D_KV)),
            _const_spec((1, D_CHUNK)),
            _const_spec((1, D_CHUNK)),
            _const_spec((N_CHUNK_HEADS, CHUNK, CHUNK)),
            _const_spec((CHUNK, D_CHUNK)),
        ],
        out_specs=out_specs,
        out_shape=out_shapes,
        compiler_params=_params(1),
        name="mixer_inproj",
    )(x, mod_l, w_in, cos_t, sin_t, qn, kn, cg, cb, ws, bs)


def _pool_kernel(*refs, seq_len, aliased):
    if aliased:
        p_ref, w_ref, s_ref, _, o_ref = refs
    else:
        p_ref, w_ref, s_ref, o_ref = refs
    x = p_ref[...]
    n = seq_len + 2 * POOL_PAD
    zpad = jnp.zeros((POOL_PAD, D_POOL), F32)
    e = jnp.concatenate([zpad, x, zpad], axis=0)

    def prev(a, d):
        return pltpu.roll(a, d, axis=0)

    def nxt(a, d):
        return pltpu.roll(a, n - d, axis=0)

    s2 = e + prev(e, 1)
    s4 = prev(s2, 1) + nxt(s2, 1)
    s8 = prev(s4, 2) + nxt(s4, 2)
    s16 = prev(s8, 4) + nxt(s8, 4)
    lane = lax.broadcasted_iota(jnp.int32, (seq_len, D_POOL), 1)
    grp = lane // POOL_GROUP
    sl = slice(POOL_PAD, POOL_PAD + seq_len)
    s = jnp.where(grp == 0, s2[sl], jnp.where(grp == 1, s4[sl], jnp.where(grp == 2, s8[sl], s16[sl])))
    half = jnp.where(grp == 0, 1, jnp.where(grp == 1, 2, jnp.where(grp == 2, 4, 8)))
    t = lax.broadcasted_iota(jnp.int32, (seq_len, D_POOL), 0)
    cnt = jnp.minimum(t + half, seq_len) - jnp.maximum(t - half, 0)
    d = (s / cnt.astype(F32) - x).astype(BF16)
    o_ref[...] = jnp.dot(d, w_ref[...], preferred_element_type=F32) * s_ref[...]


def _pool(p, w_bd, scale):
    out_shape = jax.ShapeDtypeStruct((N_TOK, D_POOL), F32)
    ctx = pl.pallas_call(
        functools.partial(_pool_kernel, seq_len=SEQ, aliased=False),
        grid=(BATCH,),
        in_specs=[pl.BlockSpec((SEQ, D_POOL), lambda i: (i, 0)),
                  _const_spec((D_POOL, D_POOL)), _const_spec((1, D_POOL))],
        out_specs=pl.BlockSpec((SEQ, D_POOL), lambda i: (i, 0)),
        out_shape=out_shape,
        compiler_params=_params(1),
        name="pool_ctx",
    )(p, w_bd, scale)
    off = N_CTX // DEC_SEQ
    return pl.pallas_call(
        functools.partial(_pool_kernel, seq_len=DEC_SEQ, aliased=True),
        grid=(DEC_BATCH,),
        in_specs=[pl.BlockSpec((DEC_SEQ, D_POOL), lambda b: (off + b, 0)),
                  _const_spec((D_POOL, D_POOL)), _const_spec((1, D_POOL)),
                  pl.BlockSpec(memory_space=pl.ANY)],
        out_specs=pl.BlockSpec((DEC_SEQ, D_POOL), lambda b: (off + b, 0)),
        out_shape=out_shape,
        input_output_aliases={3: 0},
        compiler_params=_params(1),
        name="pool_lat",
    )(p, w_bd, scale, ctx)


def _ssm_prep_kernel(lam_ref, br_ref, bi_ref, bmat_ref, lamb_ref):
    lre = lam_ref[0:1, :]
    lim = lam_ref[1:2, :]
    dt = jnp.exp(lam_ref[2:3, :])
    mag = jnp.exp(lre * dt)
    ar = mag * jnp.cos(lim * dt)
    ai = mag * jnp.sin(lim * dt)
    lamb_ref[0:1, :] = ar
    lamb_ref[1:2, :] = ai
    den = lre * lre + lim * lim
    cr = ((ar - 1.0) * lre + ai * lim) / den
    ci = (ai * lre - (ar - 1.0) * lim) / den
    br = br_ref[...]
    bi = bi_ref[...]
    lane = lax.broadcasted_iota(jnp.int32, br.shape, 1)
    is_re = (lane % (2 * LANES)) < LANES
    bmat_ref[...] = jnp.where(is_re, cr * br - ci * bi, cr * bi + ci * br).astype(BF16)


def _ssm_prep(lam_rows, b_re_placed, b_im_placed):
    n = DEPTH * 2
    return pl.pallas_call(
        _ssm_prep_kernel,
        grid=(n,),
        in_specs=[pl.BlockSpec((None, 3, SSM_LANES), lambda i: (i, 0, 0)),
                  pl.BlockSpec((None, D_SSM, SSM_LANES), lambda i: (i, 0, 0)),
                  pl.BlockSpec((None, D_SSM, SSM_LANES), lambda i: (i, 0, 0))],
        out_specs=[pl.BlockSpec((None, D_SSM, SSM_LANES), lambda i: (i, 0, 0)),
                   pl.BlockSpec((None, 2, SSM_LANES), lambda i: (i, 0, 0))],
        out_shape=[jax.ShapeDtypeStruct((n, D_SSM, SSM_LANES), BF16),
                   jax.ShapeDtypeStruct((n, 2, SSM_LANES), F32)],
        compiler_params=_params(1),
        name="ssm_prep",
    )(lam_rows, b_re_placed, b_im_placed)


def _ssm_kernel(*refs, n_seq, seq_len, aliased):
    if aliased:
        u_ref, h0_ref, bmat_ref, cmat_ref, lamb_ref, d_ref, wg_ref, bg_ref, _, y_ref, fin_ref, buf_ref = refs
    else:
        u_ref, h0_ref, bmat_ref, cmat_ref, lamb_ref, d_ref, wg_ref, bg_ref, y_ref, fin_ref, buf_ref = refs
    n_rows = n_seq * seq_len
    n_cb = SSM_W // (2 * LANES)

    def row_chunks(fn):
        def body(c, carry):
            fn(pl.ds(pl.multiple_of(c * SSM_ROWS, SSM_ROWS), SSM_ROWS))
            return carry
        lax.fori_loop(0, n_rows // SSM_ROWS, body, 0)

    for dirn in range(2):
        for part in range(SSM_LANES // SSM_W):
            lo = part * SSM_W
            first = dirn == 0 and part == 0

            def fill(rows, dirn=dirn, lo=lo):
                ub = u_ref[rows, :].astype(BF16)
                for k in range(n_cb):
                    c0 = lo + k * 2 * LANES
                    bu = jnp.dot(ub, bmat_ref[dirn, :, c0:c0 + 2 * LANES], preferred_element_type=F32)
                    buf_ref[2 * k, rows, :] = bu[:, :LANES]
                    buf_ref[2 * k + 1, rows, :] = bu[:, LANES:]

            row_chunks(fill)
            a_re, a_im, h_init = [], [], []
            for k in range(n_cb):
                c0 = lo + k * 2 * LANES
                a_re.append(jnp.broadcast_to(lamb_ref[dirn, 0:1, c0:c0 + LANES], (n_seq, LANES)))
                a_im.append(jnp.broadcast_to(lamb_ref[dirn, 1:2, c0:c0 + LANES], (n_seq, LANES)))
                h_init.append(h0_ref[dirn, :, c0:c0 + LANES])
                h_init.append(h0_ref[dirn, :, c0 + LANES:c0 + 2 * LANES])

            def step(i, hs, dirn=dirn, a_re=a_re, a_im=a_im):
                t = i if dirn == 0 else seq_len - 1 - i
                rows = pl.ds(t, n_seq, stride=seq_len)
                new = []
                for k in range(n_cb):
                    hr, hi = hs[2 * k], hs[2 * k + 1]
                    nr = a_re[k] * hr - a_im[k] * hi + buf_ref[2 * k, rows, :]
                    ni = a_re[k] * hi + a_im[k] * hr + buf_ref[2 * k + 1, rows, :]
                    buf_ref[2 * k, rows, :] = nr
                    buf_ref[2 * k + 1, rows, :] = ni
                    new += [nr, ni]
                return tuple(new)

            hs = lax.fori_loop(0, seq_len, step, tuple(h_init), unroll=2)
            for k in range(n_cb):
                c0 = lo + k * 2 * LANES
                fin_ref[dirn, :, c0:c0 + LANES] = hs[2 * k]
                fin_ref[dirn, :, c0 + LANES:c0 + 2 * LANES] = hs[2 * k + 1]

            def readout(rows, dirn=dirn, lo=lo, first=first):
                hs_all = jnp.concatenate([buf_ref[j, rows, :].astype(BF16) for j in range(2 * n_cb)], axis=1)
                y = jnp.dot(hs_all, cmat_ref[dirn, lo:lo + SSM_W, :], preferred_element_type=F32)
                if first:
                    y_ref[rows, :] = y + d_ref[...] * u_ref[rows, :]
                else:
                    y_ref[rows, :] += y

            row_chunks(readout)

    def glu(rows):
        g = _gelu(y_ref[rows, :])
        z = jnp.dot(g.astype(BF16), wg_ref[...], preferred_element_type=F32) + bg_ref[...]
        y_ref[rows, :] = g * jax.nn.sigmoid(z)

    row_chunks(glu)


def _ssm(u, h0_lat, bmat, cmat, lamb, d, wg, bg):
    y_shape = jax.ShapeDtypeStruct((N_TOK, D_SSM), F32)
    weights = [_const_spec((2, D_SSM, SSM_LANES)), _const_spec((2, SSM_LANES, D_SSM)),
               _const_spec((2, 2, SSM_LANES)), _const_spec((1, D_SSM)),
               _const_spec((D_SSM, D_SSM)), _const_spec((1, D_SSM))]
    rows_ctx = SSM_CTX_SEQS * SEQ
    n_ctx_tiles = BATCH // SSM_CTX_SEQS
    h0_ctx = jnp.zeros((2, SSM_CTX_SEQS, SSM_LANES), F32)
    y_ctx, fin_ctx = pl.pallas_call(
        functools.partial(_ssm_kernel, n_seq=SSM_CTX_SEQS, seq_len=SEQ, aliased=False),
        grid=(n_ctx_tiles,),
        in_specs=[pl.BlockSpec((rows_ctx, D_SSM), lambda i: (i, 0)),
                  _const_spec((2, SSM_CTX_SEQS, SSM_LANES))] + weights,
        out_specs=[pl.BlockSpec((rows_ctx, D_SSM), lambda i: (i, 0)),
                   pl.BlockSpec((None, 2, SSM_CTX_SEQS, SSM_LANES), lambda i: (i, 0, 0, 0))],
        out_shape=[y_shape, jax.ShapeDtypeStruct((n_ctx_tiles, 2, SSM_CTX_SEQS, SSM_LANES), F32)],
        scratch_shapes=[pltpu.VMEM((SSM_W // LANES, rows_ctx, LANES), F32)],
        compiler_params=_params(1),
        name="ssm_ctx",
    )(u, h0_ctx, bmat, cmat, lamb, d, wg, bg)
    off = N_CTX // N_LAT
    y_all, _ = pl.pallas_call(
        functools.partial(_ssm_kernel, n_seq=DEC_BATCH, seq_len=DEC_SEQ, aliased=True),
        grid=(1,),
        in_specs=[pl.BlockSpec((N_LAT, D_SSM), lambda i: (off, 0)),
                  _const_spec((2, DEC_BATCH, SSM_LANES))] + weights
                 + [pl.BlockSpec(memory_space=pl.ANY)],
        out_specs=[pl.BlockSpec((N_LAT, D_SSM), lambda i: (off, 0)),
                   pl.BlockSpec((2, DEC_BATCH, SSM_LANES), lambda i: (0, 0, 0))],
        out_shape=[y_shape, jax.ShapeDtypeStruct((2, DEC_BATCH, SSM_LANES), F32)],
        scratch_shapes=[pltpu.VMEM((SSM_W // LANES, N_LAT, LANES), F32)],
        input_output_aliases={8: 0},
        compiler_params=_params(1),
        name="ssm_lat",
    )(u, h0_lat, bmat, cmat, lamb, d, wg, bg, y_ctx)
    return y_all, fin_ctx


def _attn_kernel(*refs, n_pieces, aliased):
    q_ref = refs[0]
    k_refs = refs[1:1 + n_pieces]
    v_refs = refs[1 + n_pieces:1 + 2 * n_pieces]
    o_ref = refs[-1]
    tq = q_ref.shape[0]
    q = q_ref[...] * (HEAD_DIM ** -0.5)
    if n_pieces == 1:
        k = k_refs[0][...]
        v = v_refs[0][...]
    else:
        k = jnp.concatenate([r[...] for r in k_refs], axis=0)
        v = jnp.concatenate([r[...] for r in v_refs], axis=0)
    k = k.astype(BF16)
    v = v.astype(BF16)
    lo = lax.broadcasted_iota(jnp.int32, (tq, LANES), 1) < HEAD_DIM
    q_lo = q[:, :LANES]
    q_hi = q[:, LANES:]
    qs = [jnp.where(lo, q_lo, 0.0), jnp.where(lo, pltpu.roll(q_lo, HEAD_DIM, axis=1), 0.0),
          jnp.where(lo, 0.0, pltpu.roll(q_hi, HEAD_DIM, axis=1)), jnp.where(lo, 0.0, q_hi)]
    outs = []
    for g in range(N_KV_HEADS):
        qq = jnp.concatenate([qs[2 * g], qs[2 * g + 1]], axis=0).astype(BF16)
        s = lax.dot_general(qq, k, (((1,), (1,)), ((), ())), preferred_element_type=F32)
        m = jnp.max(s, axis=-1, keepdims=True)
        p = jnp.exp(s - m)
        l = jnp.sum(p, axis=-1, keepdims=True)
        o = jnp.dot(p.astype(BF16), v, preferred_element_type=F32) / l
        outs += [o[:tq], o[tq:]]
    out_lo = jnp.where(lo, outs[0], pltpu.roll(outs[1], HEAD_DIM, axis=1))
    out_hi = jnp.where(lo, pltpu.roll(outs[2], HEAD_DIM, axis=1), outs[3])
    o_ref[...] = jnp.concatenate([out_lo, out_hi], axis=1)


def _attention(q, k, v, cache_k_l, cache_v_l):
    out_shape = jax.ShapeDtypeStruct((N_TOK, D_ATTN), F32)
    ctx = pl.pallas_call(
        functools.partial(_attn_kernel, n_pieces=1, aliased=False),
        grid=(BATCH,),
        in_specs=[pl.BlockSpec((SEQ, D_ATTN), lambda i: (i, 0)),
                  pl.BlockSpec((SEQ, D_KV), lambda i: (i, 0)),
                  pl.BlockSpec((SEQ, D_KV), lambda i: (i, 0))],
        out_specs=pl.BlockSpec((SEQ, D_ATTN), lambda i: (i, 0)),
        out_shape=out_shape,
        compiler_params=_params(1),
        name="attn_ctx",
    )(q, k, v)
    seq_off = N_CTX // DEC_SEQ
    return pl.pallas_call(
        functools.partial(_attn_kernel, n_pieces=2, aliased=True),
        grid=(DEC_BATCH, LAT_TILES_PER_SEQ),
        in_specs=[pl.BlockSpec((TM, D_ATTN), lambda b, j: (CTX_TILES + b * LAT_TILES_PER_SEQ + j, 0)),
                  pl.BlockSpec((None, PAST_LEN, D_KV), lambda b, j: (b, 0, 0)),
                  pl.BlockSpec((DEC_SEQ, D_KV), lambda b, j: (seq_off + b, 0)),
                  pl.BlockSpec((None, PAST_LEN, D_KV), lambda b, j: (b, 0, 0)),
                  pl.BlockSpec((DEC_SEQ, D_KV), lambda b, j: (seq_off + b, 0)),
                  pl.BlockSpec(memory_space=pl.ANY)],
        out_specs=pl.BlockSpec((TM, D_ATTN), lambda b, j: (CTX_TILES + b * LAT_TILES_PER_SEQ + j, 0)),
        out_shape=out_shape,
        input_output_aliases={5: 0},
        compiler_params=_params(2),
        name="attn_lat",
    )(q, cache_k_l, k, cache_v_l, v, ctx)


def _outproj_kernel(x_ref, mod_ref, yp_ref, ys_ref, ya_ref, yc_ref, w_ref, g_ref, b_ref, o_ref):
    x = x_ref[...]
    gate = mod_ref[5:6, :]
    y = jnp.dot(yp_ref[...].astype(BF16), w_ref[0:256, :], preferred_element_type=F32)
    y = y + jnp.dot(ys_ref[...].astype(BF16), w_ref[256:512, :], preferred_element_type=F32)
    y = y + jnp.dot(ya_ref[...].astype(BF16), w_ref[512:768, :], preferred_element_type=F32)
    y = y + jnp.dot(yc_ref[...].astype(BF16), w_ref[768:1024, :], preferred_element_type=F32)
    o_ref[...] = _layer_norm(ALPHA * x + gate * y, g_ref[...], b_ref[...])


def _outproj(x, mod_l, y_pool, y_ssm, y_attn, y_chunk, w_out, ln_g, ln_b):
    tok = lambda i: (i, 0)
    return pl.pallas_call(
        _outproj_kernel,
        grid=(N_TILES,),
        in_specs=[pl.BlockSpec((TM, D_MODEL), tok),
                  pl.BlockSpec((None, N_MOD, D_MODEL), lambda i: (_cond_row(i), 0, 0)),
                  pl.BlockSpec((TM, 256), tok), pl.BlockSpec((TM, 256), tok),
                  pl.BlockSpec((TM, 256), tok), pl.BlockSpec((TM, 256), tok),
                  _const_spec((D_MODEL, D_MODEL)), _const_spec((1, D_MODEL)), _const_spec((1, D_MODEL))],
        out_specs=pl.BlockSpec((TM, D_MODEL), tok),
        out_shape=jax.ShapeDtypeStruct((N_TOK, D_MODEL), F32),
        compiler_params=_params(1),
        name="mixer_outproj",
    )(x, mod_l, y_pool, y_ssm, y_attn, y_chunk, w_out, ln_g, ln_b)


def _rope_tables():
    rows = DEC_SEQ // GRID_W
    row_idx = jnp.repeat(jnp.arange(rows), GRID_W).astype(F32)
    col_idx = jnp.tile(jnp.arange(GRID_W), rows).astype(F32)
    n_freq = HEAD_DIM // 4
    inv = ROPE_THETA ** (-jnp.arange(n_freq, dtype=F32) / n_freq)
    ang = jnp.concatenate([row_idx[:, None] * inv, col_idx[:, None] * inv], axis=-1)
    cos = jnp.cos(ang)
    sin = jnp.sin(ang)
    cos_h = jnp.concatenate([cos, cos], axis=-1)
    sin_h = jnp.concatenate([-sin, sin], axis=-1)
    cos_t = jnp.tile(cos_h, (1, N_HEADS))
    sin_t = jnp.tile(sin_h, (1, N_HEADS))
    cos_t = jnp.concatenate([cos_t, jnp.ones((TM, D_ATTN), F32)], axis=0)
    sin_t = jnp.concatenate([sin_t, jnp.zeros((TM, D_ATTN), F32)], axis=0)
    return cos_t, sin_t


def _state_lanes(a):
    return a.reshape(a.shape[:-2] + (N_SSM_GROUPS // 2, 2 * SSM_STATE))


def _ssm_lane_rows(a):
    a = _state_lanes(a)
    return jnp.stack([a, a], axis=-2).reshape(a.shape[:-2] + (SSM_LANES,))


def _ssm_pack_state(re, im):
    return jnp.stack([_state_lanes(re), _state_lanes(im)], axis=-2).reshape(re.shape[:-2] + (SSM_LANES,))


def _ssm_unpack_state(s):
    s = s.reshape(s.shape[:-1] + (N_SSM_GROUPS // 2, 2, 2 * SSM_STATE))
    shp = s.shape[:-3] + (N_SSM_GROUPS, SSM_STATE)
    return s[..., 0, :].reshape(shp), s[..., 1, :].reshape(shp)


def _place_b(b):
    eye = jnp.eye(N_SSM_GROUPS, dtype=F32)
    full = jnp.einsum('...gnc,gh->...gchn', b, eye)
    full = full.reshape(b.shape[:-3] + (D_SSM, N_SSM_GROUPS, SSM_STATE))
    return _ssm_lane_rows(full)


def _place_c(c_re, c_im):
    eye = jnp.eye(N_SSM_GROUPS, dtype=F32)

    def place(c):
        full = jnp.einsum('...gcn,gh->...hngc', c, eye)
        full = full.reshape(c.shape[:-3] + (N_SSM_GROUPS // 2, 2 * SSM_STATE, D_SSM))
        return full

    both = jnp.stack([place(c_re), -place(c_im)], axis=-3)
    return both.reshape(c_re.shape[:-3] + (SSM_LANES, D_SSM))


def _block_diag_pool(pool_w):
    eye = jnp.eye(len(POOL_WINDOWS), dtype=F32)
    return jnp.einsum('gcd,gh->gchd', pool_w, eye).reshape(D_POOL, D_POOL)


def kernel(x_prompt, x_sample, cache_k, cache_v, state_ssm_re, state_ssm_im, c, c_ctx, w_mod, b_mod, ln_g, ln_b, ffn_w1, ffn_w2, w_in, w_out, pool_w, pool_scale, ssm_lam_re, ssm_lam_im, ssm_log_step, ssm_b_re, ssm_b_im, ssm_c_re, ssm_c_im, ssm_d, ssm_w_glu, ssm_b_glu, q_norm, k_norm, chunk_ln_g, chunk_ln_b, chunk_w_s, chunk_b_s):
    x = jnp.concatenate([x_prompt.reshape(N_CTX, D_MODEL), x_sample.reshape(N_LAT, D_MODEL)], axis=0)
    cond = jnp.concatenate([c_ctx[None, :], c, jnp.zeros((N_COND - 1 - DEC_BATCH, D_MODEL), F32)], axis=0)
    mod = _modulation(cond, w_mod, b_mod)
    cos_t, sin_t = _rope_tables()

    log_step = jnp.broadcast_to(ssm_log_step[..., None], ssm_lam_re.shape)
    lam_rows = jnp.stack([_ssm_lane_rows(ssm_lam_re), _ssm_lane_rows(ssm_lam_im),
                          _ssm_lane_rows(log_step)], axis=-2)
    bmat, lamb = _ssm_prep(lam_rows.reshape(DEPTH * 2, 3, SSM_LANES),
                           _place_b(ssm_b_re).reshape(DEPTH * 2, D_SSM, SSM_LANES),
                           _place_b(ssm_b_im).reshape(DEPTH * 2, D_SSM, SSM_LANES))
    bmat = bmat.reshape(DEPTH, 2, D_SSM, SSM_LANES)
    lamb = lamb.reshape(DEPTH, 2, 2, SSM_LANES)
    cmat = _place_c(ssm_c_re, ssm_c_im).astype(BF16)
    h0_lat = _ssm_pack_state(state_ssm_re, state_ssm_im)

    ks, vs, s_re, s_im = [], [], [], []
    for l in range(DEPTH):
        mod_l = mod[l]
        row = lambda a: a.reshape(1, -1)
        x = _ffn_sublayer(x, mod_l, ffn_w1[l, 0].astype(BF16), ffn_w2[l, 0].astype(BF16),
                          row(ln_g[l, 0]), row(ln_b[l, 0]), 0)
        bs = jnp.repeat(chunk_b_s[l].T, D_CHUNK // N_CHUNK_HEADS, axis=1)
        p_pool, u_ssm, q, k, v, y_chunk = _inproj(
            x, mod_l, w_in[l].astype(BF16), cos_t, sin_t,
            row(jnp.tile(q_norm[l], N_HEADS)), row(jnp.tile(k_norm[l], N_KV_HEADS)),
            row(chunk_ln_g[l]), row(chunk_ln_b[l]), chunk_w_s[l].astype(BF16), bs)
        y_pool = _pool(p_pool, _block_diag_pool(pool_w[l]).astype(BF16), row(pool_scale[l]))
        y_ssm, fin = _ssm(u_ssm, jnp.swapaxes(h0_lat[:, l], 0, 1), bmat[l], cmat[l], lamb[l],
                          row(ssm_d[l]), ssm_w_glu[l].astype(BF16), row(ssm_b_glu[l]))
        y_attn = _attention(q, k, v, cache_k[:, l].reshape(DEC_BATCH, PAST_LEN, D_KV),
                            cache_v[:, l].reshape(DEC_BATCH, PAST_LEN, D_KV))
        x = _outproj(x, mod_l, y_pool, y_ssm, y_attn, y_chunk, w_out[l].astype(BF16),
                     row(ln_g[l, 1]), row(ln_b[l, 1]))
        x = _ffn_sublayer(x, mod_l, ffn_w1[l, 1].astype(BF16), ffn_w2[l, 1].astype(BF16),
                          row(ln_g[l, 2]), row(ln_b[l, 2]), 6)
        ks.append(k[:N_CTX].reshape(BATCH, SEQ, N_KV_HEADS, HEAD_DIM))
        vs.append(v[:N_CTX].reshape(BATCH, SEQ, N_KV_HEADS, HEAD_DIM))
        f_re, f_im = _ssm_unpack_state(jnp.transpose(fin, (0, 2, 1, 3)).reshape(BATCH, 2, SSM_LANES))
        s_re.append(f_re)
        s_im.append(f_im)
    y_p = x[:N_CTX].reshape(BATCH, SEQ, D_MODEL)
    y_s = x[N_CTX:].reshape(DEC_BATCH, DEC_SEQ, D_MODEL)
    return (y_p, y_s, jnp.stack(ks, axis=1), jnp.stack(vs, axis=1),
            jnp.stack(s_re, axis=1), jnp.stack(s_im, axis=1))
```

```python
import functools
import math

import jax
import jax.numpy as jnp
from jax import lax
from jax.experimental import pallas as pl
from jax.experimental.pallas import tpu as pltpu

F32 = jnp.float32
BF16 = jnp.bfloat16

D_MODEL = 1024
BATCH = 32
SEQ = 256
DEPTH = 2
DEC_BATCH = 2
DEC_SEQ = 2048
PAST_LEN = 512
GRID_W = 64
D_POOL = 256
D_SSM = 256
D_ATTN = 256
D_CHUNK = 256
POOL_WINDOWS = (2, 4, 8, 16)
POOL_GROUP = 64
SSM_GROUP = 16
N_SSM_GROUPS = 16
SSM_STATE = 64
HEAD_DIM = 64
N_HEADS = 4
N_KV_HEADS = 2
D_KV = 128
CHUNK = 128
N_CHUNK_HEADS = 4
D_FF = 2816
N_MOD = 9
D_IN = 1536
ALPHA = (2 * DEPTH) ** 0.25
LN_EPS = 1e-5
RMS_EPS = 1e-6
ROPE_THETA = 10000.0

LANES = 128
SUBLANES = 8
TM = 256
N_CTX = BATCH * SEQ
N_LAT = DEC_BATCH * DEC_SEQ
N_TOK = N_CTX + N_LAT
CTX_TILES = N_CTX // TM
LAT_TILES_PER_SEQ = DEC_SEQ // TM
N_TILES = N_TOK // TM
N_COND = 8
POOL_PAD = 8
SSM_LANES = 2 * N_SSM_GROUPS * SSM_STATE
SSM_W = 1024
SSM_LAT_SEGS = SUBLANES // DEC_BATCH
SSM_LAT_SEG = DEC_SEQ // SSM_LAT_SEGS
SSM_ROWS = 512
VMEM_LIMIT = 56 * 1024 * 1024


def _params(n_grid):
    return pltpu.CompilerParams(dimension_semantics=("arbitrary",) * n_grid,
                                vmem_limit_bytes=VMEM_LIMIT)


def _const_spec(shape):
    zeros = (0,) * len(shape)
    return pl.BlockSpec(shape, lambda *_: zeros, pipeline_mode=pl.Buffered(1))


def _cond_row(i):
    return jnp.where(i < CTX_TILES, 0, 1 + (i - CTX_TILES) // LAT_TILES_PER_SEQ)


def _layer_norm(y, g, b):
    mu = jnp.mean(y, axis=-1, keepdims=True)
    d = y - mu
    var = jnp.mean(d * d, axis=-1, keepdims=True)
    return d * lax.rsqrt(var + LN_EPS) * g + b


def _gelu(x):
    return 0.5 * x * (1.0 + jnp.tanh(math.sqrt(2.0 / math.pi) * (x + 0.044715 * (x * x * x))))


def _silu(x):
    return x * jax.nn.sigmoid(x)


def _mod_kernel(cond_ref, w_ref, b_ref, o_ref):
    a = _silu(cond_ref[...])
    o_ref[...] = jnp.dot(a, w_ref[...], precision=lax.Precision.HIGHEST,
                         preferred_element_type=F32) + b_ref[...]


def _modulation(cond, w_mod, b_mod):
    tn = D_MODEL
    out = pl.pallas_call(
        _mod_kernel,
        grid=(DEPTH, N_MOD),
        in_specs=[
            pl.BlockSpec((N_COND, D_MODEL), lambda l, j: (0, 0)),
            pl.BlockSpec((None, D_MODEL, tn), lambda l, j: (l, 0, j)),
            pl.BlockSpec((None, 1, tn), lambda l, j: (l, 0, j)),
        ],
        out_specs=pl.BlockSpec((None, N_COND, tn), lambda l, j: (l, 0, j)),
        out_shape=jax.ShapeDtypeStruct((DEPTH, N_COND, N_MOD * D_MODEL), F32),
        compiler_params=_params(2),
        name="modulation",
    )(cond, w_mod, b_mod.reshape(DEPTH, 1, N_MOD * D_MODEL))
    return out.reshape(DEPTH, N_COND, N_MOD, D_MODEL)


def _ffn_kernel(x_ref, mod_ref, w1_ref, w2_ref, g_ref, b_ref, o_ref, *, mod_off):
    x = x_ref[...]
    sh = mod_ref[mod_off:mod_off + 1, :]
    sc = mod_ref[mod_off + 1:mod_off + 2, :]
    gate = mod_ref[mod_off + 2:mod_off + 3, :]
    h = (x * (1.0 + sc) + sh).astype(BF16)
    gu = jnp.dot(h, w1_ref[...], preferred_element_type=F32)
    a = (_silu(gu[:, :D_FF]) * gu[:, D_FF:]).astype(BF16)
    f = jnp.dot(a, w2_ref[...], preferred_element_type=F32)
    y = ALPHA * x + (0.5 * gate) * f
    o_ref[...] = _layer_norm(y, g_ref[...], b_ref[...])


def _ffn_sublayer(x, mod_l, w1, w2, ln_g, ln_b, mod_off):
    return pl.pallas_call(
        functools.partial(_ffn_kernel, mod_off=mod_off),
        grid=(N_TILES,),
        in_specs=[
            pl.BlockSpec((TM, D_MODEL), lambda i: (i, 0)),
            pl.BlockSpec((None, N_MOD, D_MODEL), lambda i: (_cond_row(i), 0, 0)),
            _const_spec((D_MODEL, 2 * D_FF)),
            _const_spec((D_FF, D_MODEL)),
            _const_spec((1, D_MODEL)),
            _const_spec((1, D_MODEL)),
        ],
        out_specs=pl.BlockSpec((TM, D_MODEL), lambda i: (i, 0)),
        out_shape=jax.ShapeDtypeStruct((N_TOK, D_MODEL), F32),
        compiler_params=_params(1),
        name="ffn_sublayer",
    )(x, mod_l, w1, w2, ln_g, ln_b)


def _seg_rms(x, gain, n_lanes):
    parts = []
    for j in range(n_lanes // LANES):
        xs = x[:, j * LANES:(j + 1) * LANES]
        sq = xs * xs
        lo = lax.broadcasted_iota(jnp.int32, xs.shape, 1) < HEAD_DIM
        s_all = jnp.sum(sq, axis=-1, keepdims=True)
        s_lo = jnp.sum(jnp.where(lo, sq, 0.0), axis=-1, keepdims=True)
        ms = jnp.where(lo, s_lo, s_all - s_lo) * (1.0 / HEAD_DIM)
        parts.append(xs * lax.rsqrt(ms + RMS_EPS))
    y = parts[0] if len(parts) == 1 else jnp.concatenate(parts, axis=1)
    return y * gain


def _rope(x, cos_t, sin_t, n_lanes):
    parts = []
    for j in range(n_lanes // LANES):
        xs = x[:, j * LANES:(j + 1) * LANES]
        first = (lax.broadcasted_iota(jnp.int32, xs.shape, 1) % HEAD_DIM) < (HEAD_DIM // 2)
        partner = jnp.where(first, pltpu.roll(xs, LANES - HEAD_DIM // 2, axis=1),
                            pltpu.roll(xs, HEAD_DIM // 2, axis=1))
        cs = cos_t[:, j * LANES:(j + 1) * LANES]
        sn = sin_t[:, j * LANES:(j + 1) * LANES]
        parts.append(xs * cs + partner * sn)
    return parts[0] if len(parts) == 1 else jnp.concatenate(parts, axis=1)


def _inproj_kernel(x_ref, mod_ref, w_ref, cos_ref, sin_ref, qn_ref, kn_ref,
                   cg_ref, cb_ref, ws_ref, bs_ref,
                   pool_ref, ssma_ref, ssmb_ref, q_ref, k_ref, v_ref, chunk_ref):
    x = x_ref[...]
    sh = mod_ref[3:4, :]
    sc = mod_ref[4:5, :]
    h = (x * (1.0 + sc) + sh).astype(BF16)
    proj = jnp.dot(h, w_ref[...], preferred_element_type=F32)
    pool_ref[...] = proj[:, 0:256]
    ssma_ref[...] = proj[:, 256:384]
    ssmb_ref[...] = proj[:, 384:512]
    cos_t = cos_ref[...]
    sin_t = sin_ref[...]
    q = _seg_rms(proj[:, 512:768], qn_ref[...], D_ATTN)
    q_ref[...] = _rope(q, cos_t, sin_t, D_ATTN)
    k = _seg_rms(proj[:, 768:896], kn_ref[...], D_KV)
    k_ref[...] = _rope(k, cos_t, sin_t, D_KV)
    v_ref[...] = proj[:, 896:1024]
    zu = _gelu(proj[:, 1024:1280])
    zv = _layer_norm(_gelu(proj[:, 1280:1536]), cg_ref[...], cb_ref[...])
    head = lax.broadcasted_iota(jnp.int32, (CHUNK, D_CHUNK), 1) // (D_CHUNK // N_CHUNK_HEADS)
    for c in range(TM // CHUNK):
        vb = zv[c * CHUNK:(c + 1) * CHUNK, :].astype(BF16)
        mixed = bs_ref[...]
        for hd in range(N_CHUNK_HEADS):
            r = jnp.dot(ws_ref[hd], vb, preferred_element_type=F32)
            mixed = mixed + jnp.where(head == hd, r, 0.0)
        chunk_ref[c * CHUNK:(c + 1) * CHUNK, :] = zu[c * CHUNK:(c + 1) * CHUNK, :] * mixed


def _rope_block(i):
    return jnp.where(i < CTX_TILES, LAT_TILES_PER_SEQ, (i - CTX_TILES) % LAT_TILES_PER_SEQ)


def _inproj(x, mod_l, w_in, cos_t, sin_t, qn, kn, cg, cb, ws, bs):
    tok = lambda i: (i, 0)
    widths = (256, 128, 128, 256, 128, 128, 256)
    out_shapes = [jax.ShapeDtypeStruct((N_TOK, w), F32) for w in widths]
    out_specs = [pl.BlockSpec((TM, w), tok) for w in widths]
    return pl.pallas_call(
        _inproj_kernel,
        grid=(N_TILES,),
        in_specs=[
            pl.BlockSpec((TM, D_MODEL), tok),
            pl.BlockSpec((None, N_MOD, D_MODEL), lambda i: (_cond_row(i), 0, 0)),
            _const_spec((D_MODEL, D_IN)),
            pl.BlockSpec((TM, D_ATTN), lambda i: (_rope_block(i), 0)),
            pl.BlockSpec((TM, D_ATTN), lambda i: (_rope_block(i), 0)),
            _const_spec((1, D_ATTN)),
            _const_spec((1, D_KV)),
            _const_spec((1, D_CHUNK)),
            _const_spec((1, D_CHUNK)),
            _const_spec((N_CHUNK_HEADS, CHUNK, CHUNK)),
            _const_spec((CHUNK, D_CHUNK)),
        ],
        out_specs=out_specs,
        out_shape=out_shapes,
        compiler_params=_params(1),
        name="mixer_inproj",
    )(x, mod_l, w_in, cos_t, sin_t, qn, kn, cg, cb, ws, bs)


def _pool_kernel(*refs, seq_len, aliased):
    if aliased:
        p_ref, w_ref, s_ref, _, o_ref = refs
    else:
        p_ref, w_ref, s_ref, o_ref = refs
    x = p_ref[...]
    n = seq_len + 2 * POOL_PAD
    zpad = jnp.zeros((POOL_PAD, D_POOL), F32)
    e = jnp.concatenate([zpad, x, zpad], axis=0)

    def prev(a, d):
        return pltpu.roll(a, d, axis=0)

    def nxt(a, d):
        return pltpu.roll(a, n - d, axis=0)

    s2 = e + prev(e, 1)
    s4 = prev(s2, 1) + nxt(s2, 1)
    s8 = prev(s4, 2) + nxt(s4, 2)
    s16 = prev(s8, 4) + nxt(s8, 4)
    lane = lax.broadcasted_iota(jnp.int32, (seq_len, D_POOL), 1)
    grp = lane // POOL_GROUP
    sl = slice(POOL_PAD, POOL_PAD + seq_len)
    s = jnp.where(grp == 0, s2[sl], jnp.where(grp == 1, s4[sl], jnp.where(grp == 2, s8[sl], s16[sl])))
    half = jnp.where(grp == 0, 1, jnp.where(grp == 1, 2, jnp.where(grp == 2, 4, 8)))
    t = lax.broadcasted_iota(jnp.int32, (seq_len, D_POOL), 0)
    cnt = jnp.minimum(t + half, seq_len) - jnp.maximum(t - half, 0)
    d = (s / cnt.astype(F32) - x).astype(BF16)
    o_ref[...] = jnp.dot(d, w_ref[...], preferred_element_type=F32) * s_ref[...]


def _pool(p, w_bd, scale):
    out_shape = jax.ShapeDtypeStruct((N_TOK, D_POOL), F32)
    ctx = pl.pallas_call(
        functools.partial(_pool_kernel, seq_len=SEQ, aliased=False),
        grid=(BATCH,),
        in_specs=[pl.BlockSpec((SEQ, D_POOL), lambda i: (i, 0)),
                  _const_spec((D_POOL, D_POOL)), _const_spec((1, D_POOL))],
        out_specs=pl.BlockSpec((SEQ, D_POOL), lambda i: (i, 0)),
        out_shape=out_shape,
        compiler_params=_params(1),
        name="pool_ctx",
    )(p, w_bd, scale)
    off = N_CTX // DEC_SEQ
    return pl.pallas_call(
        functools.partial(_pool_kernel, seq_len=DEC_SEQ, aliased=True),
        grid=(DEC_BATCH,),
        in_specs=[pl.BlockSpec((DEC_SEQ, D_POOL), lambda b: (off + b, 0)),
                  _const_spec((D_POOL, D_POOL)), _const_spec((1, D_POOL)),
                  pl.BlockSpec(memory_space=pl.ANY)],
        out_specs=pl.BlockSpec((DEC_SEQ, D_POOL), lambda b: (off + b, 0)),
        out_shape=out_shape,
        input_output_aliases={3: 0},
        compiler_params=_params(1),
        name="pool_lat",
    )(p, w_bd, scale, ctx)


def _ssm_prep_kernel(lam_ref, br_ref, bi_ref, bmat_ref, lamb_ref):
    lre = lam_ref[0:1, :]
    lim = lam_ref[1:2, :]
    dt = jnp.exp(lam_ref[2:3, :])
    mag = jnp.exp(lre * dt)
    ar = mag * jnp.cos(lim * dt)
    ai = mag * jnp.sin(lim * dt)
    lamb_ref[0:1, :] = ar
    lamb_ref[1:2, :] = ai
    mag_s = jnp.exp(lre * dt * SSM_LAT_SEG)
    lamb_ref[2:3, :] = mag_s * jnp.cos(lim * dt * SSM_LAT_SEG)
    lamb_ref[3:4, :] = mag_s * jnp.sin(lim * dt * SSM_LAT_SEG)
    den = lre * lre + lim * lim
    cr = ((ar - 1.0) * lre + ai * lim) / den
    ci = (ai * lre - (ar - 1.0) * lim) / den
    br = br_ref[...]
    bi = bi_ref[...]
    lane = lax.broadcasted_iota(jnp.int32, br.shape, 1)
    is_re = (lane % (2 * LANES)) < LANES
    bmat_ref[...] = jnp.where(is_re, cr * br - ci * bi, cr * bi + ci * br).astype(BF16)


def _ssm_prep(lam_rows, b_re_placed, b_im_placed):
    n = DEPTH * 2
    return pl.pallas_call(
        _ssm_prep_kernel,
        grid=(n,),
        in_specs=[pl.BlockSpec((None, 3, SSM_LANES), lambda i: (i, 0, 0)),
                  pl.BlockSpec((None, D_SSM, SSM_LANES), lambda i: (i, 0, 0)),
                  pl.BlockSpec((None, D_SSM, SSM_LANES), lambda i: (i, 0, 0))],
        out_specs=[pl.BlockSpec((None, D_SSM, SSM_LANES), lambda i: (i, 0, 0)),
                   pl.BlockSpec((None, 4, SSM_LANES), lambda i: (i, 0, 0))],
        out_shape=[jax.ShapeDtypeStruct((n, D_SSM, SSM_LANES), BF16),
                   jax.ShapeDtypeStruct((n, 4, SSM_LANES), F32)],
        compiler_params=_params(1),
        name="ssm_prep",
    )(lam_rows, b_re_placed, b_im_placed)


def _ssm_kernel(*refs, seg_len, n_per, aliased):
    if aliased:
        (ua_ref, ub_ref, h0_ref, bmat_ref, cmat_ref, lamb_ref, d_ref, wg_ref, bg_ref, _,
         y_ref, fin_ref, ui_ref, buf_ref, yi_ref) = refs
    else:
        (ua_ref, ub_ref, h0_ref, bmat_ref, cmat_ref, lamb_ref, d_ref, wg_ref, bg_ref,
         y_ref, fin_ref, ui_ref, buf_ref, yi_ref) = refs
    n_rows = SUBLANES * seg_len
    n_cb = SSM_W // (2 * LANES)

    def row_chunks(fn):
        def body(c, carry):
            fn(pl.ds(pl.multiple_of(c * SSM_ROWS, SSM_ROWS), SSM_ROWS))
            return carry
        lax.fori_loop(0, n_rows // SSM_ROWS, body, 0)

    def step_rows(t):
        return pl.ds(pl.multiple_of(t * SUBLANES, SUBLANES), SUBLANES)

    def interleave(t, carry):
        ui_ref[step_rows(t), 0:LANES] = ua_ref[pl.ds(t, SUBLANES, stride=seg_len), :]
        ui_ref[step_rows(t), LANES:2 * LANES] = ub_ref[pl.ds(t, SUBLANES, stride=seg_len), :]
        return carry

    lax.fori_loop(0, seg_len, interleave, 0, unroll=4)

    seg = lax.broadcasted_iota(jnp.int32, (SUBLANES, LANES), 0) % n_per

    for dirn in range(2):
        for part in range(SSM_LANES // SSM_W):
            lo = part * SSM_W
            first = dirn == 0 and part == 0

            def fill(rows, dirn=dirn, lo=lo):
                ub = ui_ref[rows, :].astype(BF16)
                for k in range(n_cb):
                    c0 = k * 2 * LANES
                    buf_ref[rows, c0:c0 + 2 * LANES] = jnp.dot(
                        ub, bmat_ref[dirn, :, lo + c0:lo + c0 + 2 * LANES], preferred_element_type=F32)

            row_chunks(fill)

            def lam_rows(r, lo=lo, dirn=dirn):
                out = []
                for k in range(n_cb):
                    c0 = lo + k * 2 * LANES
                    out.append(jnp.broadcast_to(lamb_ref[dirn, r:r + 1, c0:c0 + LANES], (SUBLANES, LANES)))
                return out

            a_re, a_im = lam_rows(0), lam_rows(1)
            h0 = []
            for k in range(n_cb):
                c0 = lo + k * 2 * LANES
                h0 += [h0_ref[dirn, :, c0:c0 + LANES], h0_ref[dirn, :, c0 + LANES:c0 + 2 * LANES]]

            def scan(init, store, dirn=dirn, a_re=a_re, a_im=a_im):
                def step(i, hs):
                    rows = step_rows(i if dirn == 0 else seg_len - 1 - i)
                    new = []
                    for k in range(n_cb):
                        re_sl = slice(k * 2 * LANES, k * 2 * LANES + LANES)
                        im_sl = slice(k * 2 * LANES + LANES, (k + 1) * 2 * LANES)
                        hr, hi = hs[2 * k], hs[2 * k + 1]
                        nr = a_re[k] * hr - a_im[k] * hi + buf_ref[rows, re_sl]
                        ni = a_re[k] * hi + a_im[k] * hr + buf_ref[rows, im_sl]
                        if store:
                            buf_ref[rows, re_sl] = nr
                            buf_ref[rows, im_sl] = ni
                        new += [nr, ni]
                    return tuple(new)
                return lax.fori_loop(0, seg_len, step, tuple(init), unroll=4)

            if n_per == 1:
                init = h0
            else:
                local = scan([jnp.zeros((SUBLANES, LANES), F32)] * (2 * n_cb), False)
                s_re, s_im = lam_rows(2), lam_rows(3)
                edge = seg == (0 if dirn == 0 else n_per - 1)
                shift = 1 if dirn == 0 else SUBLANES - 1
                init = []
                for k in range(n_cb):
                    cr, ci = h0[2 * k], h0[2 * k + 1]
                    lr = pltpu.roll(local[2 * k], shift, axis=0)
                    li = pltpu.roll(local[2 * k + 1], shift, axis=0)
                    for _ in range(n_per - 1):
                        pr = pltpu.roll(cr, shift, axis=0)
                        pi = pltpu.roll(ci, shift, axis=0)
                        cr = jnp.where(edge, h0[2 * k], s_re[k] * pr - s_im[k] * pi + lr)
                        ci = jnp.where(edge, h0[2 * k + 1], s_re[k] * pi + s_im[k] * pr + li)
                    init += [cr, ci]
            hs = scan(init, True)
            for k in range(n_cb):
                c0 = lo + k * 2 * LANES
                fin_ref[dirn, :, c0:c0 + LANES] = hs[2 * k]
                fin_ref[dirn, :, c0 + LANES:c0 + 2 * LANES] = hs[2 * k + 1]

            def readout(rows, dirn=dirn, lo=lo, first=first):
                y = jnp.dot(buf_ref[rows, :].astype(BF16), cmat_ref[dirn, lo:lo + SSM_W, :],
                            preferred_element_type=F32)
                if first:
                    y = y + d_ref[...] * ui_ref[rows, :]
                    yi_ref[0, rows, :] = y[:, :LANES]
                    yi_ref[1, rows, :] = y[:, LANES:]
                else:
                    yi_ref[0, rows, :] += y[:, :LANES]
                    yi_ref[1, rows, :] += y[:, LANES:]

            row_chunks(readout)

    def glu(rows):
        g = _gelu(jnp.concatenate([yi_ref[0, rows, :], yi_ref[1, rows, :]], axis=1))
        z = jnp.dot(g.astype(BF16), wg_ref[...], preferred_element_type=F32) + bg_ref[...]
        o = g * jax.nn.sigmoid(z)
        yi_ref[0, rows, :] = o[:, :LANES]
        yi_ref[1, rows, :] = o[:, LANES:]

    row_chunks(glu)

    def deinterleave(tb, carry):
        for s_idx in range(SUBLANES):
            dst = pl.ds(pl.multiple_of(s_idx * seg_len + tb * SUBLANES, SUBLANES), SUBLANES)
            src = pl.ds(tb * SUBLANES * SUBLANES + s_idx, SUBLANES, stride=SUBLANES)
            y_ref[dst, 0:LANES] = yi_ref[0, src, :]
            y_ref[dst, LANES:2 * LANES] = yi_ref[1, src, :]
        return carry

    lax.fori_loop(0, seg_len // SUBLANES, deinterleave, 0)


def _ssm(ua, ub, h0_lat, bmat, cmat, lamb, d, wg, bg):
    y_shape = jax.ShapeDtypeStruct((N_TOK, D_SSM), F32)
    weights = [_const_spec((2, SUBLANES, SSM_LANES)),
               _const_spec((2, D_SSM, SSM_LANES)), _const_spec((2, SSM_LANES, D_SSM)),
               _const_spec((2, 4, SSM_LANES)), _const_spec((1, D_SSM)),
               _const_spec((D_SSM, D_SSM)), _const_spec((1, D_SSM))]

    def scratch(rows):
        return [pltpu.VMEM((rows, D_SSM), F32), pltpu.VMEM((rows, SSM_W), F32),
                pltpu.VMEM((2, rows, LANES), F32)]

    rows_ctx = SUBLANES * SEQ
    n_ctx_tiles = N_CTX // rows_ctx
    h0_ctx = jnp.zeros((2, SUBLANES, SSM_LANES), F32)
    y_ctx, fin_ctx = pl.pallas_call(
        functools.partial(_ssm_kernel, seg_len=SEQ, n_per=1, aliased=False),
        grid=(n_ctx_tiles,),
        in_specs=[pl.BlockSpec((rows_ctx, LANES), lambda i: (i, 0)),
                  pl.BlockSpec((rows_ctx, LANES), lambda i: (i, 0))] + weights,
        out_specs=[pl.BlockSpec((rows_ctx, D_SSM), lambda i: (i, 0)),
                   pl.BlockSpec((None, 2, SUBLANES, SSM_LANES), lambda i: (i, 0, 0, 0))],
        out_shape=[y_shape, jax.ShapeDtypeStruct((n_ctx_tiles, 2, SUBLANES, SSM_LANES), F32)],
        scratch_shapes=scratch(rows_ctx),
        compiler_params=_params(1),
        name="ssm_ctx",
    )(ua, ub, h0_ctx, bmat, cmat, lamb, d, wg, bg)
    off = N_CTX // N_LAT
    y_all, _ = pl.pallas_call(
        functools.partial(_ssm_kernel, seg_len=SSM_LAT_SEG, n_per=SSM_LAT_SEGS, aliased=True),
        grid=(1,),
        in_specs=[pl.BlockSpec((N_LAT, LANES), lambda i: (off, 0)),
                  pl.BlockSpec((N_LAT, LANES), lambda i: (off, 0))] + weights
                 + [pl.BlockSpec(memory_space=pl.ANY)],
        out_specs=[pl.BlockSpec((N_LAT, D_SSM), lambda i: (off, 0)),
                   pl.BlockSpec((2, SUBLANES, SSM_LANES), lambda i: (0, 0, 0))],
        out_shape=[y_shape, jax.ShapeDtypeStruct((2, SUBLANES, SSM_LANES), F32)],
        scratch_shapes=scratch(N_LAT),
        input_output_aliases={9: 0},
        compiler_params=_params(1),
        name="ssm_lat",
    )(ua, ub, h0_lat, bmat, cmat, lamb, d, wg, bg, y_ctx)
    return y_all, fin_ctx


def _attn_kernel(*refs, n_pieces, aliased):
    q_ref = refs[0]
    k_refs = refs[1:1 + n_pieces]
    v_refs = refs[1 + n_pieces:1 + 2 * n_pieces]
    o_ref = refs[-1]
    tq = q_ref.shape[0]
    q = q_ref[...] * (HEAD_DIM ** -0.5)
    if n_pieces == 1:
        k = k_refs[0][...]
        v = v_refs[0][...]
    else:
        k = jnp.concatenate([r[...] for r in k_refs], axis=0)
        v = jnp.concatenate([r[...] for r in v_refs], axis=0)
    k = k.astype(BF16)
    v = v.astype(BF16)
    lo = lax.broadcasted_iota(jnp.int32, (tq, LANES), 1) < HEAD_DIM
    q_lo = q[:, :LANES]
    q_hi = q[:, LANES:]
    qs = [jnp.where(lo, q_lo, 0.0), jnp.where(lo, pltpu.roll(q_lo, HEAD_DIM, axis=1), 0.0),
          jnp.where(lo, 0.0, pltpu.roll(q_hi, HEAD_DIM, axis=1)), jnp.where(lo, 0.0, q_hi)]
    outs = []
    for g in range(N_KV_HEADS):
        qq = jnp.concatenate([qs[2 * g], qs[2 * g + 1]], axis=0).astype(BF16)
        s = lax.dot_general(qq, k, (((1,), (1,)), ((), ())), preferred_element_type=F32)
        m = jnp.max(s, axis=-1, keepdims=True)
        p = jnp.exp(s - m)
        l = jnp.sum(p, axis=-1, keepdims=True)
        o = jnp.dot(p.astype(BF16), v, preferred_element_type=F32) / l
        outs += [o[:tq], o[tq:]]
    out_lo = jnp.where(lo, outs[0], pltpu.roll(outs[1], HEAD_DIM, axis=1))
    out_hi = jnp.where(lo, pltpu.roll(outs[2], HEAD_DIM, axis=1), outs[3])
    o_ref[...] = jnp.concatenate([out_lo, out_hi], axis=1)


def _attention(q, k, v, cache_k_l, cache_v_l):
    out_shape = jax.ShapeDtypeStruct((N_TOK, D_ATTN), F32)
    ctx = pl.pallas_call(
        functools.partial(_attn_kernel, n_pieces=1, aliased=False),
        grid=(BATCH,),
        in_specs=[pl.BlockSpec((SEQ, D_ATTN), lambda i: (i, 0)),
                  pl.BlockSpec((SEQ, D_KV), lambda i: (i, 0)),
                  pl.BlockSpec((SEQ, D_KV), lambda i: (i, 0))],
        out_specs=pl.BlockSpec((SEQ, D_ATTN), lambda i: (i, 0)),
        out_shape=out_shape,
        compiler_params=_params(1),
        name="attn_ctx",
    )(q, k, v)
    seq_off = N_CTX // DEC_SEQ
    return pl.pallas_call(
        functools.partial(_attn_kernel, n_pieces=2, aliased=True),
        grid=(DEC_BATCH, LAT_TILES_PER_SEQ),
        in_specs=[pl.BlockSpec((TM, D_ATTN), lambda b, j: (CTX_TILES + b * LAT_TILES_PER_SEQ + j, 0)),
                  pl.BlockSpec((None, PAST_LEN, D_KV), lambda b, j: (b, 0, 0)),
                  pl.BlockSpec((DEC_SEQ, D_KV), lambda b, j: (seq_off + b, 0)),
                  pl.BlockSpec((None, PAST_LEN, D_KV), lambda b, j: (b, 0, 0)),
                  pl.BlockSpec((DEC_SEQ, D_KV), lambda b, j: (seq_off + b, 0)),
                  pl.BlockSpec(memory_space=pl.ANY)],
        out_specs=pl.BlockSpec((TM, D_ATTN), lambda b, j: (CTX_TILES + b * LAT_TILES_PER_SEQ + j, 0)),
        out_shape=out_shape,
        input_output_aliases={5: 0},
        compiler_params=_params(2),
        name="attn_lat",
    )(q, cache_k_l, k, cache_v_l, v, ctx)


def _outproj_kernel(x_ref, mod_ref, yp_ref, ys_ref, ya_ref, yc_ref, w_ref, g_ref, b_ref, o_ref):
    x = x_ref[...]
    gate = mod_ref[5:6, :]
    y = jnp.dot(yp_ref[...].astype(BF16), w_ref[0:256, :], preferred_element_type=F32)
    y = y + jnp.dot(ys_ref[...].astype(BF16), w_ref[256:512, :], preferred_element_type=F32)
    y = y + jnp.dot(ya_ref[...].astype(BF16), w_ref[512:768, :], preferred_element_type=F32)
    y = y + jnp.dot(yc_ref[...].astype(BF16), w_ref[768:1024, :], preferred_element_type=F32)
    o_ref[...] = _layer_norm(ALPHA * x + gate * y, g_ref[...], b_ref[...])


def _outproj(x, mod_l, y_pool, y_ssm, y_attn, y_chunk, w_out, ln_g, ln_b):
    tok = lambda i: (i, 0)
    return pl.pallas_call(
        _outproj_kernel,
        grid=(N_TILES,),
        in_specs=[pl.BlockSpec((TM, D_MODEL), tok),
                  pl.BlockSpec((None, N_MOD, D_MODEL), lambda i: (_cond_row(i), 0, 0)),
                  pl.BlockSpec((TM, 256), tok), pl.BlockSpec((TM, 256), tok),
                  pl.BlockSpec((TM, 256), tok), pl.BlockSpec((TM, 256), tok),
                  _const_spec((D_MODEL, D_MODEL)), _const_spec((1, D_MODEL)), _const_spec((1, D_MODEL))],
        out_specs=pl.BlockSpec((TM, D_MODEL), tok),
        out_shape=jax.ShapeDtypeStruct((N_TOK, D_MODEL), F32),
        compiler_params=_params(1),
        name="mixer_outproj",
    )(x, mod_l, y_pool, y_ssm, y_attn, y_chunk, w_out, ln_g, ln_b)


def _rope_tables():
    rows = DEC_SEQ // GRID_W
    row_idx = jnp.repeat(jnp.arange(rows), GRID_W).astype(F32)
    col_idx = jnp.tile(jnp.arange(GRID_W), rows).astype(F32)
    n_freq = HEAD_DIM // 4
    inv = ROPE_THETA ** (-jnp.arange(n_freq, dtype=F32) / n_freq)
    ang = jnp.concatenate([row_idx[:, None] * inv, col_idx[:, None] * inv], axis=-1)
    cos = jnp.cos(ang)
    sin = jnp.sin(ang)
    cos_h = jnp.concatenate([cos, cos], axis=-1)
    sin_h = jnp.concatenate([-sin, sin], axis=-1)
    cos_t = jnp.tile(cos_h, (1, N_HEADS))
    sin_t = jnp.tile(sin_h, (1, N_HEADS))
    cos_t = jnp.concatenate([cos_t, jnp.ones((TM, D_ATTN), F32)], axis=0)
    sin_t = jnp.concatenate([sin_t, jnp.zeros((TM, D_ATTN), F32)], axis=0)
    return cos_t, sin_t


def _state_lanes(a):
    return a.reshape(a.shape[:-2] + (N_SSM_GROUPS // 2, 2 * SSM_STATE))


def _ssm_lane_rows(a):
    a = _state_lanes(a)
    return jnp.stack([a, a], axis=-2).reshape(a.shape[:-2] + (SSM_LANES,))


def _ssm_pack_state(re, im):
    return jnp.stack([_state_lanes(re), _state_lanes(im)], axis=-2).reshape(re.shape[:-2] + (SSM_LANES,))


def _ssm_unpack_state(s):
    s = s.reshape(s.shape[:-1] + (N_SSM_GROUPS // 2, 2, 2 * SSM_STATE))
    shp = s.shape[:-3] + (N_SSM_GROUPS, SSM_STATE)
    return s[..., 0, :].reshape(shp), s[..., 1, :].reshape(shp)


def _place_b(b):
    eye = jnp.eye(N_SSM_GROUPS, dtype=F32)
    full = jnp.einsum('...gnc,gh->...gchn', b, eye)
    full = full.reshape(b.shape[:-3] + (D_SSM, N_SSM_GROUPS, SSM_STATE))
    return _ssm_lane_rows(full)


def _place_c(c_re, c_im):
    eye = jnp.eye(N_SSM_GROUPS, dtype=F32)

    def place(c):
        full = jnp.einsum('...gcn,gh->...hngc', c, eye)
        full = full.reshape(c.shape[:-3] + (N_SSM_GROUPS // 2, 2 * SSM_STATE, D_SSM))
        return full

    both = jnp.stack([place(c_re), -place(c_im)], axis=-3)
    return both.reshape(c_re.shape[:-3] + (SSM_LANES, D_SSM))


def _block_diag_pool(pool_w):
    eye = jnp.eye(len(POOL_WINDOWS), dtype=F32)
    return jnp.einsum('gcd,gh->gchd', pool_w, eye).reshape(D_POOL, D_POOL)


def kernel(x_prompt, x_sample, cache_k, cache_v, state_ssm_re, state_ssm_im, c, c_ctx, w_mod, b_mod, ln_g, ln_b, ffn_w1, ffn_w2, w_in, w_out, pool_w, pool_scale, ssm_lam_re, ssm_lam_im, ssm_log_step, ssm_b_re, ssm_b_im, ssm_c_re, ssm_c_im, ssm_d, ssm_w_glu, ssm_b_glu, q_norm, k_norm, chunk_ln_g, chunk_ln_b, chunk_w_s, chunk_b_s):
    x = jnp.concatenate([x_prompt.reshape(N_CTX, D_MODEL), x_sample.reshape(N_LAT, D_MODEL)], axis=0)
    cond = jnp.concatenate([c_ctx[None, :], c, jnp.zeros((N_COND - 1 - DEC_BATCH, D_MODEL), F32)], axis=0)
    mod = _modulation(cond, w_mod, b_mod)
    cos_t, sin_t = _rope_tables()

    log_step = jnp.broadcast_to(ssm_log_step[..., None], ssm_lam_re.shape)
    lam_rows = jnp.stack([_ssm_lane_rows(ssm_lam_re), _ssm_lane_rows(ssm_lam_im),
                          _ssm_lane_rows(log_step)], axis=-2)
    bmat, lamb = _ssm_prep(lam_rows.reshape(DEPTH * 2, 3, SSM_LANES),
                           _place_b(ssm_b_re).reshape(DEPTH * 2, D_SSM, SSM_LANES),
                           _place_b(ssm_b_im).reshape(DEPTH * 2, D_SSM, SSM_LANES))
    bmat = bmat.reshape(DEPTH, 2, D_SSM, SSM_LANES)
    lamb = lamb.reshape(DEPTH, 2, 4, SSM_LANES)
    cmat = _place_c(ssm_c_re, ssm_c_im).astype(BF16)
    h0 = _ssm_pack_state(state_ssm_re, state_ssm_im)
    h0_lat = jnp.zeros((DEPTH, 2, DEC_BATCH, SSM_LAT_SEGS, SSM_LANES), F32)
    h0_lat = h0_lat.at[:, 0, :, 0].set(jnp.swapaxes(h0[:, :, 0], 0, 1))
    h0_lat = h0_lat.at[:, 1, :, SSM_LAT_SEGS - 1].set(jnp.swapaxes(h0[:, :, 1], 0, 1))
    h0_lat = h0_lat.reshape(DEPTH, 2, SUBLANES, SSM_LANES)

    ks, vs, s_re, s_im = [], [], [], []
    for l in range(DEPTH):
        mod_l = mod[l]
        row = lambda a: a.reshape(1, -1)
        x = _ffn_sublayer(x, mod_l, ffn_w1[l, 0].astype(BF16), ffn_w2[l, 0].astype(BF16),
                          row(ln_g[l, 0]), row(ln_b[l, 0]), 0)
        bs = jnp.repeat(chunk_b_s[l].T, D_CHUNK // N_CHUNK_HEADS, axis=1)
        p_pool, u_a, u_b, q, k, v, y_chunk = _inproj(
            x, mod_l, w_in[l].astype(BF16), cos_t, sin_t,
            row(jnp.tile(q_norm[l], N_HEADS)), row(jnp.tile(k_norm[l], N_KV_HEADS)),
            row(chunk_ln_g[l]), row(chunk_ln_b[l]), chunk_w_s[l].astype(BF16), bs)
        y_pool = _pool(p_pool, _block_diag_pool(pool_w[l]).astype(BF16), row(pool_scale[l]))
        y_ssm, fin = _ssm(u_a, u_b, h0_lat[l], bmat[l], cmat[l], lamb[l],
                          row(ssm_d[l]), ssm_w_glu[l].astype(BF16), row(ssm_b_glu[l]))
        y_attn = _attention(q, k, v, cache_k[:, l].reshape(DEC_BATCH, PAST_LEN, D_KV),
                            cache_v[:, l].reshape(DEC_BATCH, PAST_LEN, D_KV))
        x = _outproj(x, mod_l, y_pool, y_ssm, y_attn, y_chunk, w_out[l].astype(BF16),
                     row(ln_g[l, 1]), row(ln_b[l, 1]))
        x = _ffn_sublayer(x, mod_l, ffn_w1[l, 1].astype(BF16), ffn_w2[l, 1].astype(BF16),
                          row(ln_g[l, 2]), row(ln_b[l, 2]), 6)
        ks.append(k[:N_CTX].reshape(BATCH, SEQ, N_KV_HEADS, HEAD_DIM))
        vs.append(v[:N_CTX].reshape(BATCH, SEQ, N_KV_HEADS, HEAD_DIM))
        f_re, f_im = _ssm_unpack_state(jnp.transpose(fin, (0, 2, 1, 3)).reshape(BATCH, 2, SSM_LANES))
        s_re.append(f_re)
        s_im.append(f_im)
    y_p = x[:N_CTX].reshape(BATCH, SEQ, D_MODEL)
    y_s = x[N_CTX:].reshape(DEC_BATCH, DEC_SEQ, D_MODEL)
    return (y_p, y_s, jnp.stack(ks, axis=1), jnp.stack(vs, axis=1),
            jnp.stack(s_re, axis=1), jnp.stack(s_im, axis=1))
```

```python
import functools
import math

import jax
import jax.numpy as jnp
from jax import lax
from jax.experimental import pallas as pl
from jax.experimental.pallas import tpu as pltpu

F32 = jnp.float32
BF16 = jnp.bfloat16

D_MODEL = 1024
BATCH = 32
SEQ = 256
DEPTH = 2
DEC_BATCH = 2
DEC_SEQ = 2048
PAST_LEN = 512
GRID_W = 64
D_POOL = 256
D_SSM = 256
D_ATTN = 256
D_CHUNK = 256
POOL_WINDOWS = (2, 4, 8, 16)
POOL_GROUP = 64
SSM_GROUP = 16
N_SSM_GROUPS = 16
SSM_STATE = 64
HEAD_DIM = 64
N_HEADS = 4
N_KV_HEADS = 2
D_KV = 128
CHUNK = 128
N_CHUNK_HEADS = 4
D_FF = 2816
N_MOD = 9
D_IN = 1536
ALPHA = (2 * DEPTH) ** 0.25
LN_EPS = 1e-5
RMS_EPS = 1e-6
ROPE_THETA = 10000.0

LANES = 128
SUBLANES = 8
TM = 512
ATT_TQ = 256
N_CTX = BATCH * SEQ
N_LAT = DEC_BATCH * DEC_SEQ
N_TOK = N_CTX + N_LAT
CTX_TILES = N_CTX // TM
LAT_TILES_PER_SEQ = DEC_SEQ // TM
N_TILES = N_TOK // TM
N_COND = 8
POOL_PAD = 8
SSM_LANES = 2 * N_SSM_GROUPS * SSM_STATE
SSM_W = 1024
SSM_LAT_SEGS = SUBLANES // DEC_BATCH
SSM_LAT_SEG = DEC_SEQ // SSM_LAT_SEGS
SSM_ROWS = 1024
VMEM_LIMIT = 56 * 1024 * 1024


def _params(n_grid):
    return pltpu.CompilerParams(dimension_semantics=("arbitrary",) * n_grid,
                                vmem_limit_bytes=VMEM_LIMIT)


def _const_spec(shape):
    zeros = (0,) * len(shape)
    return pl.BlockSpec(shape, lambda *_: zeros, pipeline_mode=pl.Buffered(1))


def _cond_row(i):
    return jnp.where(i < CTX_TILES, 0, 1 + (i - CTX_TILES) // LAT_TILES_PER_SEQ)


def _layer_norm(y, g, b):
    mu = jnp.mean(y, axis=-1, keepdims=True)
    d = y - mu
    var = jnp.mean(d * d, axis=-1, keepdims=True)
    return d * lax.rsqrt(var + LN_EPS) * g + b


def _gelu(x):
    return 0.5 * x * (1.0 + jnp.tanh(math.sqrt(2.0 / math.pi) * (x + 0.044715 * (x * x * x))))


def _silu(x):
    return x * jax.nn.sigmoid(x)


def _mod_kernel(cond_ref, w_ref, b_ref, o_ref):
    a = _silu(cond_ref[...])
    o_ref[...] = jnp.dot(a, w_ref[...], precision=lax.Precision.HIGHEST,
                         preferred_element_type=F32) + b_ref[...]


def _modulation(cond, w_mod, b_mod):
    tn = D_MODEL
    out = pl.pallas_call(
        _mod_kernel,
        grid=(DEPTH, N_MOD),
        in_specs=[
            pl.BlockSpec((N_COND, D_MODEL), lambda l, j: (0, 0)),
            pl.BlockSpec((None, D_MODEL, tn), lambda l, j: (l, 0, j)),
            pl.BlockSpec((None, 1, tn), lambda l, j: (l, 0, j)),
        ],
        out_specs=pl.BlockSpec((None, N_COND, tn), lambda l, j: (l, 0, j)),
        out_shape=jax.ShapeDtypeStruct((DEPTH, N_COND, N_MOD * D_MODEL), F32),
        compiler_params=_params(2),
        name="modulation",
    )(cond, w_mod, b_mod.reshape(DEPTH, 1, N_MOD * D_MODEL))
    return out.reshape(DEPTH, N_COND, N_MOD, D_MODEL)


def _ffn_kernel(*refs, mod_off, split_in, split_out):
    n_in = 2 if split_in else 1
    mod_ref, w1_ref, w2_ref, g_ref, b_ref = refs[n_in:n_in + 5]
    out_refs = refs[n_in + 5:]
    is_ctx = pl.program_id(0) < CTX_TILES
    if split_in:
        x = jnp.where(is_ctx, refs[0][...], refs[1][...])
    else:
        x = refs[0][...]
    sh = mod_ref[mod_off:mod_off + 1, :]
    sc = mod_ref[mod_off + 1:mod_off + 2, :]
    gate = mod_ref[mod_off + 2:mod_off + 3, :]
    h = (x * (1.0 + sc) + sh).astype(BF16)
    gu = jnp.dot(h, w1_ref[...], preferred_element_type=F32)
    a = (_silu(gu[:, :D_FF]) * gu[:, D_FF:]).astype(BF16)
    f = jnp.dot(a, w2_ref[...], preferred_element_type=F32)
    y = _layer_norm(ALPHA * x + (0.5 * gate) * f, g_ref[...], b_ref[...])
    if split_out:
        @pl.when(is_ctx)
        def _():
            out_refs[0][...] = y

        @pl.when(jnp.logical_not(is_ctx))
        def _():
            out_refs[1][...] = y
    else:
        out_refs[0][...] = y


def _ctx_tile(i):
    return (jnp.minimum(i, CTX_TILES - 1), 0)


def _lat_tile(i):
    return (jnp.maximum(i - CTX_TILES, 0), 0)


def _ffn_sublayer(xs, mod_l, w1, w2, ln_g, ln_b, mod_off, split_out=False):
    split_in = isinstance(xs, tuple)
    tok = lambda i: (i, 0)
    if split_in:
        x_specs = [pl.BlockSpec((TM, D_MODEL), _ctx_tile), pl.BlockSpec((TM, D_MODEL), _lat_tile)]
    else:
        xs = (xs,)
        x_specs = [pl.BlockSpec((TM, D_MODEL), tok)]
    if split_out:
        out_specs = [pl.BlockSpec((TM, D_MODEL), _ctx_tile), pl.BlockSpec((TM, D_MODEL), _lat_tile)]
        out_shape = [jax.ShapeDtypeStruct((N_CTX, D_MODEL), F32), jax.ShapeDtypeStruct((N_LAT, D_MODEL), F32)]
    else:
        out_specs = pl.BlockSpec((TM, D_MODEL), tok)
        out_shape = jax.ShapeDtypeStruct((N_TOK, D_MODEL), F32)
    return pl.pallas_call(
        functools.partial(_ffn_kernel, mod_off=mod_off, split_in=split_in, split_out=split_out),
        grid=(N_TILES,),
        in_specs=x_specs + [
            pl.BlockSpec((None, N_MOD, D_MODEL), lambda i: (_cond_row(i), 0, 0)),
            _const_spec((D_MODEL, 2 * D_FF)),
            _const_spec((D_FF, D_MODEL)),
            _const_spec((1, D_MODEL)),
            _const_spec((1, D_MODEL)),
        ],
        out_specs=out_specs,
        out_shape=out_shape,
        compiler_params=_params(1),
        name="ffn_sublayer",
    )(*xs, mod_l, w1, w2, ln_g, ln_b)


def _seg_rms(x, gain, n_lanes):
    parts = []
    for j in range(n_lanes // LANES):
        xs = x[:, j * LANES:(j + 1) * LANES]
        sq = xs * xs
        lo = lax.broadcasted_iota(jnp.int32, xs.shape, 1) < HEAD_DIM
        s_all = jnp.sum(sq, axis=-1, keepdims=True)
        s_lo = jnp.sum(jnp.where(lo, sq, 0.0), axis=-1, keepdims=True)
        ms = jnp.where(lo, s_lo, s_all - s_lo) * (1.0 / HEAD_DIM)
        parts.append(xs * lax.rsqrt(ms + RMS_EPS))
    y = parts[0] if len(parts) == 1 else jnp.concatenate(parts, axis=1)
    return y * gain


def _rope(x, cos_t, sin_t, n_lanes):
    parts = []
    for j in range(n_lanes // LANES):
        xs = x[:, j * LANES:(j + 1) * LANES]
        first = (lax.broadcasted_iota(jnp.int32, xs.shape, 1) % HEAD_DIM) < (HEAD_DIM // 2)
        partner = jnp.where(first, pltpu.roll(xs, LANES - HEAD_DIM // 2, axis=1),
                            pltpu.roll(xs, HEAD_DIM // 2, axis=1))
        cs = cos_t[:, j * LANES:(j + 1) * LANES]
        sn = sin_t[:, j * LANES:(j + 1) * LANES]
        parts.append(xs * cs + partner * sn)
    return parts[0] if len(parts) == 1 else jnp.concatenate(parts, axis=1)


def _inproj_kernel(x_ref, mod_ref, w_ref, cos_ref, sin_ref, qn_ref, kn_ref,
                   cg_ref, cb_ref, ws_ref, bs_ref,
                   pool_ref, ssma_ref, ssmb_ref, q_ref, k_ref, v_ref, chunk_ref):
    x = x_ref[...]
    sh = mod_ref[3:4, :]
    sc = mod_ref[4:5, :]
    h = (x * (1.0 + sc) + sh).astype(BF16)
    proj = jnp.dot(h, w_ref[...], preferred_element_type=F32)
    pool_ref[...] = proj[:, 0:256]
    ssma_ref[...] = proj[:, 256:384]
    ssmb_ref[...] = proj[:, 384:512]
    cos_t = cos_ref[...]
    sin_t = sin_ref[...]
    q = _seg_rms(proj[:, 512:768], qn_ref[...], D_ATTN)
    q_ref[...] = _rope(q, cos_t, sin_t, D_ATTN)
    k = _seg_rms(proj[:, 768:896], kn_ref[...], D_KV)
    k_ref[...] = _rope(k, cos_t, sin_t, D_KV)
    v_ref[...] = proj[:, 896:1024]
    zu = _gelu(proj[:, 1024:1280])
    zv = _layer_norm(_gelu(proj[:, 1280:1536]), cg_ref[...], cb_ref[...])
    head = lax.broadcasted_iota(jnp.int32, (CHUNK, D_CHUNK), 1) // (D_CHUNK // N_CHUNK_HEADS)
    for c in range(TM // CHUNK):
        vb = zv[c * CHUNK:(c + 1) * CHUNK, :].astype(BF16)
        mixed = bs_ref[...]
        for hd in range(N_CHUNK_HEADS):
            r = jnp.dot(ws_ref[hd], vb, preferred_element_type=F32)
            mixed = mixed + jnp.where(head == hd, r, 0.0)
        chunk_ref[c * CHUNK:(c + 1) * CHUNK, :] = zu[c * CHUNK:(c + 1) * CHUNK, :] * mixed


def _rope_block(i):
    return jnp.where(i < CTX_TILES, LAT_TILES_PER_SEQ, (i - CTX_TILES) % LAT_TILES_PER_SEQ)


def _inproj(x, mod_l, w_in, cos_t, sin_t, qn, kn, cg, cb, ws, bs):
    tok = lambda i: (i, 0)
    widths = (256, 128, 128, 256, 128, 128, 256)
    out_shapes = [jax.ShapeDtypeStruct((N_TOK, w), F32) for w in widths]
    out_specs = [pl.BlockSpec((TM, w), tok) for w in widths]
    return pl.pallas_call(
        _inproj_kernel,
        grid=(N_TILES,),
        in_specs=[
            pl.BlockSpec((TM, D_MODEL), tok),
            pl.BlockSpec((None, N_MOD, D_MODEL), lambda i: (_cond_row(i), 0, 0)),
            _const_spec((D_MODEL, D_IN)),
            pl.BlockSpec((TM, D_ATTN), lambda i: (_rope_block(i), 0)),
            pl.BlockSpec((TM, D_ATTN), lambda i: (_rope_block(i), 0)),
            _const_spec((1, D_ATTN)),
            _const_spec((1, D_KV)),
            _const_spec((1, D_CHUNK)),
            _const_spec((1, D_CHUNK)),
            _const_spec((N_CHUNK_HEADS, CHUNK, CHUNK)),
            _const_spec((CHUNK, D_CHUNK)),
        ],
        out_specs=out_specs,
        out_shape=out_shapes,
        compiler_params=_params(1),
        name="mixer_inproj",
    )(x, mod_l, w_in, cos_t, sin_t, qn, kn, cg, cb, ws, bs)


def _pool_kernel(*refs, seq_len, aliased):
    if aliased:
        p_ref, w_ref, s_ref, _, o_ref = refs
    else:
        p_ref, w_ref, s_ref, o_ref = refs
    x = p_ref[...]
    n = seq_len + 2 * POOL_PAD
    zpad = jnp.zeros((POOL_PAD, D_POOL), F32)
    e = jnp.concatenate([zpad, x, zpad], axis=0)

    def prev(a, d):
        return pltpu.roll(a, d, axis=0)

    def nxt(a, d):
        return pltpu.roll(a, n - d, axis=0)

    s2 = e + prev(e, 1)
    s4 = prev(s2, 1) + nxt(s2, 1)
    s8 = prev(s4, 2) + nxt(s4, 2)
    s16 = prev(s8, 4) + nxt(s8, 4)
    lane = lax.broadcasted_iota(jnp.int32, (seq_len, D_POOL), 1)
    grp = lane // POOL_GROUP
    sl = slice(POOL_PAD, POOL_PAD + seq_len)
    s = jnp.where(grp == 0, s2[sl], jnp.where(grp == 1, s4[sl], jnp.where(grp == 2, s8[sl], s16[sl])))
    half = jnp.where(grp == 0, 1, jnp.where(grp == 1, 2, jnp.where(grp == 2, 4, 8)))
    t = lax.broadcasted_iota(jnp.int32, (seq_len, D_POOL), 0)
    cnt = jnp.minimum(t + half, seq_len) - jnp.maximum(t - half, 0)
    d = (s / cnt.astype(F32) - x).astype(BF16)
    o_ref[...] = jnp.dot(d, w_ref[...], preferred_element_type=F32) * s_ref[...]


def _pool(p, w_bd, scale):
    out_shape = jax.ShapeDtypeStruct((N_TOK, D_POOL), F32)
    ctx = pl.pallas_call(
        functools.partial(_pool_kernel, seq_len=SEQ, aliased=False),
        grid=(BATCH,),
        in_specs=[pl.BlockSpec((SEQ, D_POOL), lambda i: (i, 0)),
                  _const_spec((D_POOL, D_POOL)), _const_spec((1, D_POOL))],
        out_specs=pl.BlockSpec((SEQ, D_POOL), lambda i: (i, 0)),
        out_shape=out_shape,
        compiler_params=_params(1),
        name="pool_ctx",
    )(p, w_bd, scale)
    off = N_CTX // DEC_SEQ
    return pl.pallas_call(
        functools.partial(_pool_kernel, seq_len=DEC_SEQ, aliased=True),
        grid=(DEC_BATCH,),
        in_specs=[pl.BlockSpec((DEC_SEQ, D_POOL), lambda b: (off + b, 0)),
                  _const_spec((D_POOL, D_POOL)), _const_spec((1, D_POOL)),
                  pl.BlockSpec(memory_space=pl.ANY)],
        out_specs=pl.BlockSpec((DEC_SEQ, D_POOL), lambda b: (off + b, 0)),
        out_shape=out_shape,
        input_output_aliases={3: 0},
        compiler_params=_params(1),
        name="pool_lat",
    )(p, w_bd, scale, ctx)


def _ssm_prep_kernel(lam_ref, br_ref, bi_ref, bmat_ref, lamb_ref):
    lre = lam_ref[0:1, :]
    lim = lam_ref[1:2, :]
    dt = jnp.exp(lam_ref[2:3, :])
    mag = jnp.exp(lre * dt)
    ar = mag * jnp.cos(lim * dt)
    ai = mag * jnp.sin(lim * dt)
    lamb_ref[0:1, :] = ar
    lamb_ref[1:2, :] = ai
    mag_s = jnp.exp(lre * dt * SSM_LAT_SEG)
    lamb_ref[2:3, :] = mag_s * jnp.cos(lim * dt * SSM_LAT_SEG)
    lamb_ref[3:4, :] = mag_s * jnp.sin(lim * dt * SSM_LAT_SEG)
    den = lre * lre + lim * lim
    cr = ((ar - 1.0) * lre + ai * lim) / den
    ci = (ai * lre - (ar - 1.0) * lim) / den
    br = br_ref[...]
    bi = bi_ref[...]
    lane = lax.broadcasted_iota(jnp.int32, br.shape, 1)
    is_re = (lane % (2 * LANES)) < LANES
    bmat_ref[...] = jnp.where(is_re, cr * br - ci * bi, cr * bi + ci * br).astype(BF16)


def _ssm_prep(lam_rows, b_re_placed, b_im_placed):
    n = DEPTH * 2
    return pl.pallas_call(
        _ssm_prep_kernel,
        grid=(n,),
        in_specs=[pl.BlockSpec((None, 3, SSM_LANES), lambda i: (i, 0, 0)),
                  pl.BlockSpec((None, D_SSM, SSM_LANES), lambda i: (i, 0, 0)),
                  pl.BlockSpec((None, D_SSM, SSM_LANES), lambda i: (i, 0, 0))],
        out_specs=[pl.BlockSpec((None, D_SSM, SSM_LANES), lambda i: (i, 0, 0)),
                   pl.BlockSpec((None, 4, SSM_LANES), lambda i: (i, 0, 0))],
        out_shape=[jax.ShapeDtypeStruct((n, D_SSM, SSM_LANES), BF16),
                   jax.ShapeDtypeStruct((n, 4, SSM_LANES), F32)],
        compiler_params=_params(1),
        name="ssm_prep",
    )(lam_rows, b_re_placed, b_im_placed)


def _ssm_kernel(*refs, seg_len, n_per, aliased):
    if aliased:
        (ua_ref, ub_ref, h0_ref, bmat_ref, cmat_ref, lamb_ref, d_ref, wg_ref, bg_ref, _,
         y_ref, fin_ref, ui_ref, buf_ref, yi_ref) = refs
    else:
        (ua_ref, ub_ref, h0_ref, bmat_ref, cmat_ref, lamb_ref, d_ref, wg_ref, bg_ref,
         y_ref, fin_ref, ui_ref, buf_ref, yi_ref) = refs
    n_rows = SUBLANES * seg_len
    n_cb = SSM_W // (2 * LANES)

    def row_chunks(fn):
        def body(c, carry):
            fn(pl.ds(pl.multiple_of(c * SSM_ROWS, SSM_ROWS), SSM_ROWS))
            return carry
        lax.fori_loop(0, n_rows // SSM_ROWS, body, 0)

    def step_rows(t):
        return pl.ds(pl.multiple_of(t * SUBLANES, SUBLANES), SUBLANES)

    def interleave(t, carry):
        ui_ref[step_rows(t), 0:LANES] = ua_ref[pl.ds(t, SUBLANES, stride=seg_len), :]
        ui_ref[step_rows(t), LANES:2 * LANES] = ub_ref[pl.ds(t, SUBLANES, stride=seg_len), :]
        return carry

    lax.fori_loop(0, seg_len, interleave, 0, unroll=4)

    seg = lax.broadcasted_iota(jnp.int32, (SUBLANES, LANES), 0) % n_per

    for dirn in range(2):
        for part in range(SSM_LANES // SSM_W):
            lo = part * SSM_W
            first = dirn == 0 and part == 0

            def fill(rows, dirn=dirn, lo=lo):
                ub = ui_ref[rows, :].astype(BF16)
                for k in range(n_cb):
                    c0 = k * 2 * LANES
                    buf_ref[rows, c0:c0 + 2 * LANES] = jnp.dot(
                        ub, bmat_ref[dirn, :, lo + c0:lo + c0 + 2 * LANES], preferred_element_type=F32)

            row_chunks(fill)

            def lam_rows(r, lo=lo, dirn=dirn):
                out = []
                for k in range(n_cb):
                    c0 = lo + k * 2 * LANES
                    out.append(jnp.broadcast_to(lamb_ref[dirn, r:r + 1, c0:c0 + LANES], (SUBLANES, LANES)))
                return out

            a_re, a_im = lam_rows(0), lam_rows(1)
            h0 = []
            for k in range(n_cb):
                c0 = lo + k * 2 * LANES
                h0 += [h0_ref[dirn, :, c0:c0 + LANES], h0_ref[dirn, :, c0 + LANES:c0 + 2 * LANES]]

            def scan(init, store, dirn=dirn, a_re=a_re, a_im=a_im):
                def step(i, hs):
                    rows = step_rows(i if dirn == 0 else seg_len - 1 - i)
                    new = []
                    for k in range(n_cb):
                        re_sl = slice(k * 2 * LANES, k * 2 * LANES + LANES)
                        im_sl = slice(k * 2 * LANES + LANES, (k + 1) * 2 * LANES)
                        hr, hi = hs[2 * k], hs[2 * k + 1]
                        nr = a_re[k] * hr - a_im[k] * hi + buf_ref[rows, re_sl]
                        ni = a_re[k] * hi + a_im[k] * hr + buf_ref[rows, im_sl]
                        if store:
                            buf_ref[rows, re_sl] = nr
                            buf_ref[rows, im_sl] = ni
                        new += [nr, ni]
                    return tuple(new)
                return lax.fori_loop(0, seg_len, step, tuple(init), unroll=4)

            if n_per == 1:
                init = h0
            else:
                local = scan([jnp.zeros((SUBLANES, LANES), F32)] * (2 * n_cb), False)
                s_re, s_im = lam_rows(2), lam_rows(3)
                edge = seg == (0 if dirn == 0 else n_per - 1)
                shift = 1 if dirn == 0 else SUBLANES - 1
                init = []
                for k in range(n_cb):
                    cr, ci = h0[2 * k], h0[2 * k + 1]
                    lr = pltpu.roll(local[2 * k], shift, axis=0)
                    li = pltpu.roll(local[2 * k + 1], shift, axis=0)
                    for _ in range(n_per - 1):
                        pr = pltpu.roll(cr, shift, axis=0)
                        pi = pltpu.roll(ci, shift, axis=0)
                        cr = jnp.where(edge, h0[2 * k], s_re[k] * pr - s_im[k] * pi + lr)
                        ci = jnp.where(edge, h0[2 * k + 1], s_re[k] * pi + s_im[k] * pr + li)
                    init += [cr, ci]
            hs = scan(init, True)
            for k in range(n_cb):
                c0 = lo + k * 2 * LANES
                fin_ref[dirn, :, c0:c0 + LANES] = hs[2 * k]
                fin_ref[dirn, :, c0 + LANES:c0 + 2 * LANES] = hs[2 * k + 1]

            def readout(rows, dirn=dirn, lo=lo, first=first):
                y = jnp.dot(buf_ref[rows, :].astype(BF16), cmat_ref[dirn, lo:lo + SSM_W, :],
                            preferred_element_type=F32)
                if first:
                    y = y + d_ref[...] * ui_ref[rows, :]
                    yi_ref[0, rows, :] = y[:, :LANES]
                    yi_ref[1, rows, :] = y[:, LANES:]
                else:
                    yi_ref[0, rows, :] += y[:, :LANES]
                    yi_ref[1, rows, :] += y[:, LANES:]

            row_chunks(readout)

    def glu(rows):
        g = _gelu(jnp.concatenate([yi_ref[0, rows, :], yi_ref[1, rows, :]], axis=1))
        z = jnp.dot(g.astype(BF16), wg_ref[...], preferred_element_type=F32) + bg_ref[...]
        o = g * jax.nn.sigmoid(z)
        yi_ref[0, rows, :] = o[:, :LANES]
        yi_ref[1, rows, :] = o[:, LANES:]

    row_chunks(glu)

    def deinterleave(tb, carry):
        for s_idx in range(SUBLANES):
            dst = pl.ds(pl.multiple_of(s_idx * seg_len + tb * SUBLANES, SUBLANES), SUBLANES)
            src = pl.ds(tb * SUBLANES * SUBLANES + s_idx, SUBLANES, stride=SUBLANES)
            y_ref[dst, 0:LANES] = yi_ref[0, src, :]
            y_ref[dst, LANES:2 * LANES] = yi_ref[1, src, :]
        return carry

    lax.fori_loop(0, seg_len // SUBLANES, deinterleave, 0)


def _ssm(ua, ub, h0_lat, bmat, cmat, lamb, d, wg, bg):
    y_shape = jax.ShapeDtypeStruct((N_TOK, D_SSM), F32)
    weights = [_const_spec((2, SUBLANES, SSM_LANES)),
               _const_spec((2, D_SSM, SSM_LANES)), _const_spec((2, SSM_LANES, D_SSM)),
               _const_spec((2, 4, SSM_LANES)), _const_spec((1, D_SSM)),
               _const_spec((D_SSM, D_SSM)), _const_spec((1, D_SSM))]

    def scratch(rows):
        return [pltpu.VMEM((rows, D_SSM), F32), pltpu.VMEM((rows, SSM_W), F32),
                pltpu.VMEM((2, rows, LANES), F32)]

    rows_ctx = SUBLANES * SEQ
    n_ctx_tiles = N_CTX // rows_ctx
    h0_ctx = jnp.zeros((2, SUBLANES, SSM_LANES), F32)
    y_ctx, fin_ctx = pl.pallas_call(
        functools.partial(_ssm_kernel, seg_len=SEQ, n_per=1, aliased=False),
        grid=(n_ctx_tiles,),
        in_specs=[pl.BlockSpec((rows_ctx, LANES), lambda i: (i, 0)),
                  pl.BlockSpec((rows_ctx, LANES), lambda i: (i, 0))] + weights,
        out_specs=[pl.BlockSpec((rows_ctx, D_SSM), lambda i: (i, 0)),
                   pl.BlockSpec((None, 2, SUBLANES, SSM_LANES), lambda i: (i, 0, 0, 0))],
        out_shape=[y_shape, jax.ShapeDtypeStruct((n_ctx_tiles, 2, SUBLANES, SSM_LANES), F32)],
        scratch_shapes=scratch(rows_ctx),
        compiler_params=_params(1),
        name="ssm_ctx",
    )(ua, ub, h0_ctx, bmat, cmat, lamb, d, wg, bg)
    off = N_CTX // N_LAT
    y_all, _ = pl.pallas_call(
        functools.partial(_ssm_kernel, seg_len=SSM_LAT_SEG, n_per=SSM_LAT_SEGS, aliased=True),
        grid=(1,),
        in_specs=[pl.BlockSpec((N_LAT, LANES), lambda i: (off, 0)),
                  pl.BlockSpec((N_LAT, LANES), lambda i: (off, 0))] + weights
                 + [pl.BlockSpec(memory_space=pl.ANY)],
        out_specs=[pl.BlockSpec((N_LAT, D_SSM), lambda i: (off, 0)),
                   pl.BlockSpec((2, SUBLANES, SSM_LANES), lambda i: (0, 0, 0))],
        out_shape=[y_shape, jax.ShapeDtypeStruct((2, SUBLANES, SSM_LANES), F32)],
        scratch_shapes=scratch(N_LAT),
        input_output_aliases={9: 0},
        compiler_params=_params(1),
        name="ssm_lat",
    )(ua, ub, h0_lat, bmat, cmat, lamb, d, wg, bg, y_ctx)
    return y_all, fin_ctx


def _attn_kernel(*refs, n_pieces, aliased):
    q_ref = refs[0]
    k_refs = refs[1:1 + n_pieces]
    v_refs = refs[1 + n_pieces:1 + 2 * n_pieces]
    o_ref = refs[-1]
    tq = q_ref.shape[0]
    q = q_ref[...] * (HEAD_DIM ** -0.5)
    if n_pieces == 1:
        k = k_refs[0][...]
        v = v_refs[0][...]
    else:
        k = jnp.concatenate([r[...] for r in k_refs], axis=0)
        v = jnp.concatenate([r[...] for r in v_refs], axis=0)
    k = k.astype(BF16)
    v = v.astype(BF16)
    lo = lax.broadcasted_iota(jnp.int32, (tq, LANES), 1) < HEAD_DIM
    q_lo = q[:, :LANES]
    q_hi = q[:, LANES:]
    qs = [jnp.where(lo, q_lo, 0.0), jnp.where(lo, pltpu.roll(q_lo, HEAD_DIM, axis=1), 0.0),
          jnp.where(lo, 0.0, pltpu.roll(q_hi, HEAD_DIM, axis=1)), jnp.where(lo, 0.0, q_hi)]
    outs = []
    for g in range(N_KV_HEADS):
        qq = jnp.concatenate([qs[2 * g], qs[2 * g + 1]], axis=0).astype(BF16)
        s = lax.dot_general(qq, k, (((1,), (1,)), ((), ())), preferred_element_type=F32)
        m = jnp.max(s, axis=-1, keepdims=True)
        p = jnp.exp(s - m)
        l = jnp.sum(p, axis=-1, keepdims=True)
        o = jnp.dot(p.astype(BF16), v, preferred_element_type=F32) / l
        outs += [o[:tq], o[tq:]]
    out_lo = jnp.where(lo, outs[0], pltpu.roll(outs[1], HEAD_DIM, axis=1))
    out_hi = jnp.where(lo, pltpu.roll(outs[2], HEAD_DIM, axis=1), outs[3])
    o_ref[...] = jnp.concatenate([out_lo, out_hi], axis=1)


def _attention(q, k, v, cache_k_l, cache_v_l):
    out_shape = jax.ShapeDtypeStruct((N_TOK, D_ATTN), F32)
    ctx = pl.pallas_call(
        functools.partial(_attn_kernel, n_pieces=1, aliased=False),
        grid=(BATCH,),
        in_specs=[pl.BlockSpec((SEQ, D_ATTN), lambda i: (i, 0)),
                  pl.BlockSpec((SEQ, D_KV), lambda i: (i, 0)),
                  pl.BlockSpec((SEQ, D_KV), lambda i: (i, 0))],
        out_specs=pl.BlockSpec((SEQ, D_ATTN), lambda i: (i, 0)),
        out_shape=out_shape,
        compiler_params=_params(1),
        name="attn_ctx",
    )(q, k, v)
    seq_off = N_CTX // DEC_SEQ
    q_off = N_CTX // ATT_TQ
    q_per_seq = DEC_SEQ // ATT_TQ
    return pl.pallas_call(
        functools.partial(_attn_kernel, n_pieces=2, aliased=True),
        grid=(DEC_BATCH, q_per_seq),
        in_specs=[pl.BlockSpec((ATT_TQ, D_ATTN), lambda b, j: (q_off + b * q_per_seq + j, 0)),
                  pl.BlockSpec((None, PAST_LEN, D_KV), lambda b, j: (b, 0, 0)),
                  pl.BlockSpec((DEC_SEQ, D_KV), lambda b, j: (seq_off + b, 0)),
                  pl.BlockSpec((None, PAST_LEN, D_KV), lambda b, j: (b, 0, 0)),
                  pl.BlockSpec((DEC_SEQ, D_KV), lambda b, j: (seq_off + b, 0)),
                  pl.BlockSpec(memory_space=pl.ANY)],
        out_specs=pl.BlockSpec((ATT_TQ, D_ATTN), lambda b, j: (q_off + b * q_per_seq + j, 0)),
        out_shape=out_shape,
        input_output_aliases={5: 0},
        compiler_params=_params(2),
        name="attn_lat",
    )(q, cache_k_l, k, cache_v_l, v, ctx)


def _outproj_kernel(x_ref, mod_ref, yp_ref, ys_ref, ya_ref, yc_ref, w_ref, g_ref, b_ref, o_ref):
    x = x_ref[...]
    gate = mod_ref[5:6, :]
    y = jnp.dot(yp_ref[...].astype(BF16), w_ref[0:256, :], preferred_element_type=F32)
    y = y + jnp.dot(ys_ref[...].astype(BF16), w_ref[256:512, :], preferred_element_type=F32)
    y = y + jnp.dot(ya_ref[...].astype(BF16), w_ref[512:768, :], preferred_element_type=F32)
    y = y + jnp.dot(yc_ref[...].astype(BF16), w_ref[768:1024, :], preferred_element_type=F32)
    o_ref[...] = _layer_norm(ALPHA * x + gate * y, g_ref[...], b_ref[...])


def _outproj(x, mod_l, y_pool, y_ssm, y_attn, y_chunk, w_out, ln_g, ln_b):
    tok = lambda i: (i, 0)
    return pl.pallas_call(
        _outproj_kernel,
        grid=(N_TILES,),
        in_specs=[pl.BlockSpec((TM, D_MODEL), tok),
                  pl.BlockSpec((None, N_MOD, D_MODEL), lambda i: (_cond_row(i), 0, 0)),
                  pl.BlockSpec((TM, 256), tok), pl.BlockSpec((TM, 256), tok),
                  pl.BlockSpec((TM, 256), tok), pl.BlockSpec((TM, 256), tok),
                  _const_spec((D_MODEL, D_MODEL)), _const_spec((1, D_MODEL)), _const_spec((1, D_MODEL))],
        out_specs=pl.BlockSpec((TM, D_MODEL), tok),
        out_shape=jax.ShapeDtypeStruct((N_TOK, D_MODEL), F32),
        compiler_params=_params(1),
        name="mixer_outproj",
    )(x, mod_l, y_pool, y_ssm, y_attn, y_chunk, w_out, ln_g, ln_b)


def _rope_tables():
    rows = DEC_SEQ // GRID_W
    row_idx = jnp.repeat(jnp.arange(rows), GRID_W).astype(F32)
    col_idx = jnp.tile(jnp.arange(GRID_W), rows).astype(F32)
    n_freq = HEAD_DIM // 4
    inv = ROPE_THETA ** (-jnp.arange(n_freq, dtype=F32) / n_freq)
    ang = jnp.concatenate([row_idx[:, None] * inv, col_idx[:, None] * inv], axis=-1)
    cos = jnp.cos(ang)
    sin = jnp.sin(ang)
    cos_h = jnp.concatenate([cos, cos], axis=-1)
    sin_h = jnp.concatenate([-sin, sin], axis=-1)
    cos_t = jnp.tile(cos_h, (1, N_HEADS))
    sin_t = jnp.tile(sin_h, (1, N_HEADS))
    cos_t = jnp.concatenate([cos_t, jnp.ones((TM, D_ATTN), F32)], axis=0)
    sin_t = jnp.concatenate([sin_t, jnp.zeros((TM, D_ATTN), F32)], axis=0)
    return cos_t, sin_t


def _state_lanes(a):
    return a.reshape(a.shape[:-2] + (N_SSM_GROUPS // 2, 2 * SSM_STATE))


def _ssm_lane_rows(a):
    a = _state_lanes(a)
    return jnp.stack([a, a], axis=-2).reshape(a.shape[:-2] + (SSM_LANES,))


def _ssm_pack_state(re, im):
    return jnp.stack([_state_lanes(re), _state_lanes(im)], axis=-2).reshape(re.shape[:-2] + (SSM_LANES,))


def _ssm_unpack_state(s):
    s = s.reshape(s.shape[:-1] + (N_SSM_GROUPS // 2, 2, 2 * SSM_STATE))
    shp = s.shape[:-3] + (N_SSM_GROUPS, SSM_STATE)
    return s[..., 0, :].reshape(shp), s[..., 1, :].reshape(shp)


def _ssm_group_mask():
    ch_g = jnp.arange(D_SSM) // SSM_GROUP
    lane = jnp.arange(SSM_LANES)
    lane_g = 2 * (lane // (2 * LANES)) + (lane % LANES) // SSM_STATE
    return ch_g[:, None] == lane_g[None, :]


def _place_b(b):
    bt = jnp.swapaxes(b, -1, -2).reshape(b.shape[:-3] + (D_SSM, SSM_STATE))
    tiled = jnp.tile(bt, (1,) * (bt.ndim - 1) + (SSM_LANES // SSM_STATE,))
    return jnp.where(_ssm_group_mask(), tiled, 0.0)


def _place_c(c_re, c_im):
    def rows(c):
        ct = jnp.moveaxis(c, -1, -3).reshape(c.shape[:-3] + (SSM_STATE, D_SSM))
        return jnp.tile(ct, (1,) * (ct.ndim - 2) + (SSM_LANES // SSM_STATE, 1))
    is_re = (jnp.arange(SSM_LANES) % (2 * LANES) < LANES)[:, None]
    return jnp.where(_ssm_group_mask().T, jnp.where(is_re, rows(c_re), -rows(c_im)), 0.0)


def _block_diag_pool(pool_w):
    eye = jnp.eye(len(POOL_WINDOWS), dtype=F32)
    return jnp.einsum('gcd,gh->gchd', pool_w, eye).reshape(D_POOL, D_POOL)


def kernel(x_prompt, x_sample, cache_k, cache_v, state_ssm_re, state_ssm_im, c, c_ctx, w_mod, b_mod, ln_g, ln_b, ffn_w1, ffn_w2, w_in, w_out, pool_w, pool_scale, ssm_lam_re, ssm_lam_im, ssm_log_step, ssm_b_re, ssm_b_im, ssm_c_re, ssm_c_im, ssm_d, ssm_w_glu, ssm_b_glu, q_norm, k_norm, chunk_ln_g, chunk_ln_b, chunk_w_s, chunk_b_s):
    x = (x_prompt.reshape(N_CTX, D_MODEL), x_sample.reshape(N_LAT, D_MODEL))
    cond =jnp.concatenate([c_ctx[None, :], c, jnp.zeros((N_COND - 1 - DEC_BATCH, D_MODEL), F32)], axis=0)
    mod = _modulation(cond, w_mod, b_mod)
    cos_t, sin_t = _rope_tables()

    log_step = jnp.broadcast_to(ssm_log_step[..., None], ssm_lam_re.shape)
    lam_rows = jnp.stack([_ssm_lane_rows(ssm_lam_re), _ssm_lane_rows(ssm_lam_im),
                          _ssm_lane_rows(log_step)], axis=-2)
    bmat, lamb = _ssm_prep(lam_rows.reshape(DEPTH * 2, 3, SSM_LANES),
                           _place_b(ssm_b_re).reshape(DEPTH * 2, D_SSM, SSM_LANES),
                           _place_b(ssm_b_im).reshape(DEPTH * 2, D_SSM, SSM_LANES))
    bmat = bmat.reshape(DEPTH, 2, D_SSM, SSM_LANES)
    lamb = lamb.reshape(DEPTH, 2, 4, SSM_LANES)
    cmat = _place_c(ssm_c_re, ssm_c_im).astype(BF16)
    h0 = _ssm_pack_state(state_ssm_re, state_ssm_im)
    h0_lat = jnp.zeros((DEPTH, 2, DEC_BATCH, SSM_LAT_SEGS, SSM_LANES), F32)
    h0_lat = h0_lat.at[:, 0, :, 0].set(jnp.swapaxes(h0[:, :, 0], 0, 1))
    h0_lat = h0_lat.at[:, 1, :, SSM_LAT_SEGS - 1].set(jnp.swapaxes(h0[:, :, 1], 0, 1))
    h0_lat = h0_lat.reshape(DEPTH, 2, SUBLANES, SSM_LANES)

    ks, vs, s_re, s_im = [], [], [], []
    for l in range(DEPTH):
        mod_l = mod[l]
        row = lambda a: a.reshape(1, -1)
        x = _ffn_sublayer(x, mod_l, ffn_w1[l, 0].astype(BF16), ffn_w2[l, 0].astype(BF16),
                          row(ln_g[l, 0]), row(ln_b[l, 0]), 0)
        bs = jnp.repeat(chunk_b_s[l].T, D_CHUNK // N_CHUNK_HEADS, axis=1)
        p_pool, u_a, u_b, q, k, v, y_chunk = _inproj(
            x, mod_l, w_in[l].astype(BF16), cos_t, sin_t,
            row(jnp.tile(q_norm[l], N_HEADS)), row(jnp.tile(k_norm[l], N_KV_HEADS)),
            row(chunk_ln_g[l]), row(chunk_ln_b[l]), chunk_w_s[l].astype(BF16), bs)
        y_pool = _pool(p_pool, _block_diag_pool(pool_w[l]).astype(BF16), row(pool_scale[l]))
        y_ssm, fin = _ssm(u_a, u_b, h0_lat[l], bmat[l], cmat[l], lamb[l],
                          row(ssm_d[l]), ssm_w_glu[l].astype(BF16), row(ssm_b_glu[l]))
        y_attn = _attention(q, k, v, cache_k[:, l].reshape(DEC_BATCH, PAST_LEN, D_KV),
                            cache_v[:, l].reshape(DEC_BATCH, PAST_LEN, D_KV))
        x = _outproj(x, mod_l, y_pool, y_ssm, y_attn, y_chunk, w_out[l].astype(BF16),
                     row(ln_g[l, 1]), row(ln_b[l, 1]))
        x = _ffn_sublayer(x, mod_l, ffn_w1[l, 1].astype(BF16), ffn_w2[l, 1].astype(BF16),
                          row(ln_g[l, 2]), row(ln_b[l, 2]), 6, split_out=(l == DEPTH - 1))
        ks.append(k[:N_CTX].reshape(BATCH, SEQ, N_KV_HEADS, HEAD_DIM))
        vs.append(v[:N_CTX].reshape(BATCH, SEQ, N_KV_HEADS, HEAD_DIM))
        f_re, f_im = _ssm_unpack_state(jnp.transpose(fin, (0, 2, 1, 3)).reshape(BATCH, 2, SSM_LANES))
        s_re.append(f_re)
        s_im.append(f_im)
    y_p = x[0].reshape(BATCH, SEQ, D_MODEL)
    y_s = x[1].reshape(DEC_BATCH, DEC_SEQ, D_MODEL)
    return (y_p, y_s, jnp.stack(ks, axis=1), jnp.stack(vs, axis=1),
            jnp.stack(s_re, axis=1), jnp.stack(s_im, axis=1))
```

```python
import functools
import math

import jax
import jax.numpy as jnp
from jax import lax
from jax.experimental import pallas as pl
from jax.experimental.pallas import tpu as pltpu

F32 = jnp.float32
BF16 = jnp.bfloat16

D_MODEL = 1024
BATCH = 32
SEQ = 256
DEPTH = 2
DEC_BATCH = 2
DEC_SEQ = 2048
PAST_LEN = 512
GRID_W = 64
D_POOL = 256
D_SSM = 256
D_ATTN = 256
D_CHUNK = 256
POOL_WINDOWS = (2, 4, 8, 16)
POOL_GROUP = 64
SSM_GROUP = 16
N_SSM_GROUPS = 16
SSM_STATE = 64
HEAD_DIM = 64
N_HEADS = 4
N_KV_HEADS = 2
D_KV = 128
CHUNK = 128
N_CHUNK_HEADS = 4
D_FF = 2816
N_MOD = 9
D_IN = 1536
ALPHA = (2 * DEPTH) ** 0.25
LN_EPS = 1e-5
RMS_EPS = 1e-6
ROPE_THETA = 10000.0

LANES = 128
SUBLANES = 8
TM = 512
ATT_TQ = 256
N_CTX = BATCH * SEQ
N_LAT = DEC_BATCH * DEC_SEQ
N_TOK = N_CTX + N_LAT
CTX_TILES = N_CTX // TM
LAT_TILES_PER_SEQ = DEC_SEQ // TM
N_TILES = N_TOK // TM
N_COND = 8
POOL_PAD = 8
SSM_LANES = 2 * N_SSM_GROUPS * SSM_STATE
SSM_W_CTX = 1024
SSM_W_LAT = 512
SSM_LAT_SEGS = SUBLANES // DEC_BATCH
SSM_LAT_SEG = DEC_SEQ // SSM_LAT_SEGS
SSM_ROWS = 1024
VMEM_LIMIT = 56 * 1024 * 1024


def _params(n_grid):
    return pltpu.CompilerParams(dimension_semantics=("arbitrary",) * n_grid,
                                vmem_limit_bytes=VMEM_LIMIT)


def _const_spec(shape):
    zeros = (0,) * len(shape)
    return pl.BlockSpec(shape, lambda *_: zeros, pipeline_mode=pl.Buffered(1))


def _cond_row(i):
    return jnp.where(i < CTX_TILES, 0, 1 + (i - CTX_TILES) // LAT_TILES_PER_SEQ)


def _layer_norm(y, g, b):
    mu = jnp.mean(y, axis=-1, keepdims=True)
    d = y - mu
    var = jnp.mean(d * d, axis=-1, keepdims=True)
    return d * lax.rsqrt(var + LN_EPS) * g + b


def _gelu(x):
    return 0.5 * x * (1.0 + jnp.tanh(math.sqrt(2.0 / math.pi) * (x + 0.044715 * (x * x * x))))


def _silu(x):
    return x * jax.nn.sigmoid(x)


def _split_bf16(a):
    hi = a.astype(BF16)
    return hi, (a - hi.astype(F32)).astype(BF16)


def _mod_kernel(cond_ref, w_ref, b_ref, o_ref):
    a_hi, a_lo = _split_bf16(_silu(cond_ref[...]))
    w_hi, w_lo = _split_bf16(w_ref[...])
    acc = jnp.dot(a_hi, w_lo, preferred_element_type=F32)
    acc = acc + jnp.dot(a_lo, w_hi, preferred_element_type=F32)
    acc = acc + jnp.dot(a_hi, w_hi, preferred_element_type=F32)
    o_ref[...] = acc + b_ref[...]


def _modulation(cond, w_mod, b_mod):
    tn = D_MODEL
    out = pl.pallas_call(
        _mod_kernel,
        grid=(DEPTH, N_MOD),
        in_specs=[
            pl.BlockSpec((N_COND, D_MODEL), lambda l, j: (0, 0)),
            pl.BlockSpec((None, D_MODEL, tn), lambda l, j: (l, 0, j)),
            pl.BlockSpec((None, 1, tn), lambda l, j: (l, 0, j)),
        ],
        out_specs=pl.BlockSpec((None, N_COND, tn), lambda l, j: (l, 0, j)),
        out_shape=jax.ShapeDtypeStruct((DEPTH, N_COND, N_MOD * D_MODEL), F32),
        compiler_params=_params(2),
        name="modulation",
    )(cond, w_mod, b_mod.reshape(DEPTH, 1, N_MOD * D_MODEL))
    return out.reshape(DEPTH, N_COND, N_MOD, D_MODEL)


def _ffn_kernel(*refs, mod_off, split_in, split_out, mixer_out):
    n_in = 2 if split_in else 1
    n_mix = 10 if mixer_out else 0
    mix_refs = refs[n_in:n_in + n_mix]
    mod_ref, w1_ref, w2_ref, g_ref, b_ref = refs[n_in + n_mix:n_in + n_mix + 5]
    out_refs = refs[n_in + n_mix + 5:]
    is_ctx = pl.program_id(0) < CTX_TILES
    if split_in:
        x = jnp.where(is_ctx, refs[0][...], refs[1][...])
    else:
        x = refs[0][...]
    if mixer_out:
        yc_ref, wo_ref, go_ref, bo_ref = mix_refs[6:]
        parts = [jnp.where(is_ctx, mix_refs[2 * j][...], mix_refs[2 * j + 1][...]) for j in range(3)]
        parts.append(yc_ref[...])
        y = None
        for j, part in enumerate(parts):
            r = jnp.dot(part.astype(BF16), wo_ref[256 * j:256 * (j + 1), :], preferred_element_type=F32)
            y = r if y is None else y + r
        x = _layer_norm(ALPHA * x + mod_ref[5:6, :] * y, go_ref[...], bo_ref[...])
    sh = mod_ref[mod_off:mod_off + 1, :]
    sc = mod_ref[mod_off + 1:mod_off + 2, :]
    gate = mod_ref[mod_off + 2:mod_off + 3, :]
    h = (x * (1.0 + sc) + sh).astype(BF16)
    gu = jnp.dot(h, w1_ref[...], preferred_element_type=F32)
    a = (_silu(gu[:, :D_FF]) * gu[:, D_FF:]).astype(BF16)
    f = jnp.dot(a, w2_ref[...], preferred_element_type=F32)
    y = _layer_norm(ALPHA * x + (0.5 * gate) * f, g_ref[...], b_ref[...])
    if split_out:
        @pl.when(is_ctx)
        def _():
            out_refs[0][...] = y

        @pl.when(jnp.logical_not(is_ctx))
        def _():
            out_refs[1][...] = y
    else:
        out_refs[0][...] = y


def _ctx_tile(i):
    return (jnp.minimum(i, CTX_TILES - 1), 0)


def _lat_tile(i):
    return (jnp.maximum(i - CTX_TILES, 0), 0)


def _ffn_sublayer(xs, mod_l, w1, w2, ln_g, ln_b, mod_off, split_out=False, mixer=None):
    mix_args, mix_specs = (), []
    if mixer is not None:
        y_pool, y_ssm, y_attn, y_chunk, w_out, g_out, b_out = mixer
        mix_args = (*y_pool, *y_ssm, *y_attn, y_chunk, w_out, g_out, b_out)
        mix_specs = [pl.BlockSpec((TM, 256), _ctx_tile), pl.BlockSpec((TM, 256), _lat_tile)] * 3 + [
            pl.BlockSpec((TM, 256), lambda i: (i, 0)),
            _const_spec((D_MODEL, D_MODEL)), _const_spec((1, D_MODEL)), _const_spec((1, D_MODEL))]
    split_in = isinstance(xs, tuple)
    tok = lambda i: (i, 0)
    if split_in:
        x_specs = [pl.BlockSpec((TM, D_MODEL), _ctx_tile), pl.BlockSpec((TM, D_MODEL), _lat_tile)]
    else:
        xs = (xs,)
        x_specs = [pl.BlockSpec((TM, D_MODEL), tok)]
    if split_out:
        out_specs = [pl.BlockSpec((TM, D_MODEL), _ctx_tile), pl.BlockSpec((TM, D_MODEL), _lat_tile)]
        out_shape = [jax.ShapeDtypeStruct((N_CTX, D_MODEL), F32), jax.ShapeDtypeStruct((N_LAT, D_MODEL), F32)]
    else:
        out_specs = pl.BlockSpec((TM, D_MODEL), tok)
        out_shape = jax.ShapeDtypeStruct((N_TOK, D_MODEL), F32)
    return pl.pallas_call(
        functools.partial(_ffn_kernel, mod_off=mod_off, split_in=split_in, split_out=split_out,
                          mixer_out=mixer is not None),
        grid=(N_TILES,),
        in_specs=x_specs + mix_specs + [
            pl.BlockSpec((None, N_MOD, D_MODEL), lambda i: (_cond_row(i), 0, 0)),
            _const_spec((D_MODEL, 2 * D_FF)),
            _const_spec((D_FF, D_MODEL)),
            _const_spec((1, D_MODEL)),
            _const_spec((1, D_MODEL)),
        ],
        out_specs=out_specs,
        out_shape=out_shape,
        compiler_params=_params(1),
        name="ffn_sublayer",
    )(*xs, *mix_args, mod_l, w1, w2, ln_g, ln_b)


def _seg_rms(x, gain, n_lanes):
    parts = []
    for j in range(n_lanes // LANES):
        xs = x[:, j * LANES:(j + 1) * LANES]
        sq = xs * xs
        lo = lax.broadcasted_iota(jnp.int32, xs.shape, 1) < HEAD_DIM
        s_all = jnp.sum(sq, axis=-1, keepdims=True)
        s_lo = jnp.sum(jnp.where(lo, sq, 0.0), axis=-1, keepdims=True)
        ms = jnp.where(lo, s_lo, s_all - s_lo) * (1.0 / HEAD_DIM)
        parts.append(xs * lax.rsqrt(ms + RMS_EPS))
    y = parts[0] if len(parts) == 1 else jnp.concatenate(parts, axis=1)
    return y * gain


def _rope(x, cos_t, sin_t, n_lanes):
    parts = []
    for j in range(n_lanes // LANES):
        xs = x[:, j * LANES:(j + 1) * LANES]
        first = (lax.broadcasted_iota(jnp.int32, xs.shape, 1) % HEAD_DIM) < (HEAD_DIM // 2)
        partner = jnp.where(first, pltpu.roll(xs, LANES - HEAD_DIM // 2, axis=1),
                            pltpu.roll(xs, HEAD_DIM // 2, axis=1))
        cs = cos_t[:, j * LANES:(j + 1) * LANES]
        sn = sin_t[:, j * LANES:(j + 1) * LANES]
        parts.append(xs * cs + partner * sn)
    return parts[0] if len(parts) == 1 else jnp.concatenate(parts, axis=1)


def _inproj_kernel(x_ref, mod_ref, w_ref, cos_ref, sin_ref, qn_ref, kn_ref,
                   cg_ref, cb_ref, ws_ref, bs_ref,
                   pool_ref, ssma_ref, ssmb_ref, q_ref, k_ref, v_ref, chunk_ref):
    x = x_ref[...]
    sh = mod_ref[3:4, :]
    sc = mod_ref[4:5, :]
    h = (x * (1.0 + sc) + sh).astype(BF16)
    proj = jnp.dot(h, w_ref[...], preferred_element_type=F32)
    pool_ref[...] = proj[:, 0:256]
    ssma_ref[...] = proj[:, 256:384]
    ssmb_ref[...] = proj[:, 384:512]
    cos_t = cos_ref[...]
    sin_t = sin_ref[...]
    q = _seg_rms(proj[:, 512:768], qn_ref[...], D_ATTN)
    q_ref[...] = _rope(q, cos_t, sin_t, D_ATTN)
    k = _seg_rms(proj[:, 768:896], kn_ref[...], D_KV)
    k_ref[...] = _rope(k, cos_t, sin_t, D_KV)
    v_ref[...] = proj[:, 896:1024]
    zu = _gelu(proj[:, 1024:1280])
    zv = _layer_norm(_gelu(proj[:, 1280:1536]), cg_ref[...], cb_ref[...])
    head = lax.broadcasted_iota(jnp.int32, (CHUNK, D_CHUNK), 1) // (D_CHUNK // N_CHUNK_HEADS)
    for c in range(TM // CHUNK):
        vb = zv[c * CHUNK:(c + 1) * CHUNK, :].astype(BF16)
        mixed = bs_ref[...]
        for hd in range(N_CHUNK_HEADS):
            r = jnp.dot(ws_ref[hd], vb, preferred_element_type=F32)
            mixed = mixed + jnp.where(head == hd, r, 0.0)
        chunk_ref[c * CHUNK:(c + 1) * CHUNK, :] = zu[c * CHUNK:(c + 1) * CHUNK, :] * mixed


def _rope_block(i):
    return jnp.where(i < CTX_TILES, LAT_TILES_PER_SEQ, (i - CTX_TILES) % LAT_TILES_PER_SEQ)


def _inproj(x, mod_l, w_in, cos_t, sin_t, qn, kn, cg, cb, ws, bs):
    tok = lambda i: (i, 0)
    widths = (256, 128, 128, 256, 128, 128, 256)
    out_shapes = [jax.ShapeDtypeStruct((N_TOK, w), F32) for w in widths]
    out_specs = [pl.BlockSpec((TM, w), tok) for w in widths]
    return pl.pallas_call(
        _inproj_kernel,
        grid=(N_TILES,),
        in_specs=[
            pl.BlockSpec((TM, D_MODEL), tok),
            pl.BlockSpec((None, N_MOD, D_MODEL), lambda i: (_cond_row(i), 0, 0)),
            _const_spec((D_MODEL, D_IN)),
            pl.BlockSpec((TM, D_ATTN), lambda i: (_rope_block(i), 0)),
            pl.BlockSpec((TM, D_ATTN), lambda i: (_rope_block(i), 0)),
            _const_spec((1, D_ATTN)),
            _const_spec((1, D_KV)),
            _const_spec((1, D_CHUNK)),
            _const_spec((1, D_CHUNK)),
            _const_spec((N_CHUNK_HEADS, CHUNK, CHUNK)),
            _const_spec((CHUNK, D_CHUNK)),
        ],
        out_specs=out_specs,
        out_shape=out_shapes,
        compiler_params=_params(1),
        name="mixer_inproj",
    )(x, mod_l, w_in, cos_t, sin_t, qn, kn, cg, cb, ws, bs)


def _pool_kernel(p_ref, w_ref, s_ref, o_ref, *, seq_len):
    x = p_ref[...]
    n = seq_len + 2 * POOL_PAD
    zpad = jnp.zeros((POOL_PAD, D_POOL), F32)
    e = jnp.concatenate([zpad, x, zpad], axis=0)

    def prev(a, d):
        return pltpu.roll(a, d, axis=0)

    def nxt(a, d):
        return pltpu.roll(a, n - d, axis=0)

    s2 = e + prev(e, 1)
    s4 = prev(s2, 1) + nxt(s2, 1)
    s8 = prev(s4, 2) + nxt(s4, 2)
    s16 = prev(s8, 4) + nxt(s8, 4)
    lane = lax.broadcasted_iota(jnp.int32, (seq_len, D_POOL), 1)
    grp = lane // POOL_GROUP
    sl = slice(POOL_PAD, POOL_PAD + seq_len)
    s = jnp.where(grp == 0, s2[sl], jnp.where(grp == 1, s4[sl], jnp.where(grp == 2, s8[sl], s16[sl])))
    half = jnp.where(grp == 0, 1, jnp.where(grp == 1, 2, jnp.where(grp == 2, 4, 8)))
    t = lax.broadcasted_iota(jnp.int32, (seq_len, D_POOL), 0)
    cnt = jnp.minimum(t + half, seq_len) - jnp.maximum(t - half, 0)
    d = (s / cnt.astype(F32) - x).astype(BF16)
    o_ref[...] = jnp.dot(d, w_ref[...], preferred_element_type=F32) * s_ref[...]


def _pool(p, w_bd, scale):
    ctx = pl.pallas_call(
        functools.partial(_pool_kernel, seq_len=SEQ),
        grid=(BATCH,),
        in_specs=[pl.BlockSpec((SEQ, D_POOL), lambda i: (i, 0)),
                  _const_spec((D_POOL, D_POOL)), _const_spec((1, D_POOL))],
        out_specs=pl.BlockSpec((SEQ, D_POOL), lambda i: (i, 0)),
        out_shape=jax.ShapeDtypeStruct((N_CTX, D_POOL), F32),
        compiler_params=_params(1),
        name="pool_ctx",
    )(p, w_bd, scale)
    off = N_CTX // DEC_SEQ
    lat = pl.pallas_call(
        functools.partial(_pool_kernel, seq_len=DEC_SEQ),
        grid=(DEC_BATCH,),
        in_specs=[pl.BlockSpec((DEC_SEQ, D_POOL), lambda b: (off + b, 0)),
                  _const_spec((D_POOL, D_POOL)), _const_spec((1, D_POOL))],
        out_specs=pl.BlockSpec((DEC_SEQ, D_POOL), lambda b: (b, 0)),
        out_shape=jax.ShapeDtypeStruct((N_LAT, D_POOL), F32),
        compiler_params=_params(1),
        name="pool_lat",
    )(p, w_bd, scale)
    return ctx, lat


def _ssm_prep_kernel(lam_ref, br_ref, bi_ref, bmat_ref, lamb_ref):
    lre = lam_ref[0:1, :]
    lim = lam_ref[1:2, :]
    dt = jnp.exp(lam_ref[2:3, :])
    mag = jnp.exp(lre * dt)
    ar = mag * jnp.cos(lim * dt)
    ai = mag * jnp.sin(lim * dt)
    lamb_ref[0:1, :] = ar
    lamb_ref[1:2, :] = ai
    mag_s = jnp.exp(lre * dt * SSM_LAT_SEG)
    lamb_ref[2:3, :] = mag_s * jnp.cos(lim * dt * SSM_LAT_SEG)
    lamb_ref[3:4, :] = mag_s * jnp.sin(lim * dt * SSM_LAT_SEG)
    den = lre * lre + lim * lim
    cr = ((ar - 1.0) * lre + ai * lim) / den
    ci = (ai * lre - (ar - 1.0) * lim) / den
    br = br_ref[...]
    bi = bi_ref[...]
    lane = lax.broadcasted_iota(jnp.int32, br.shape, 1)
    is_re = (lane % (2 * LANES)) < LANES
    bmat_ref[...] = jnp.where(is_re, cr * br - ci * bi, cr * bi + ci * br).astype(BF16)


def _ssm_prep(lam_rows, b_re_placed, b_im_placed):
    n = DEPTH * 2
    return pl.pallas_call(
        _ssm_prep_kernel,
        grid=(n,),
        in_specs=[pl.BlockSpec((None, 3, SSM_LANES), lambda i: (i, 0, 0)),
                  pl.BlockSpec((None, D_SSM, SSM_LANES), lambda i: (i, 0, 0)),
                  pl.BlockSpec((None, D_SSM, SSM_LANES), lambda i: (i, 0, 0))],
        out_specs=[pl.BlockSpec((None, D_SSM, SSM_LANES), lambda i: (i, 0, 0)),
                   pl.BlockSpec((None, 4, SSM_LANES), lambda i: (i, 0, 0))],
        out_shape=[jax.ShapeDtypeStruct((n, D_SSM, SSM_LANES), BF16),
                   jax.ShapeDtypeStruct((n, 4, SSM_LANES), F32)],
        compiler_params=_params(1),
        name="ssm_prep",
    )(lam_rows, b_re_placed, b_im_placed)


def _ssm_kernel(ua_ref, ub_ref, h0_ref, bmat_ref, cmat_ref, lamb_ref, d_ref, wg_ref, bg_ref,
                y_ref, fin_ref, ui_ref, buf_ref, yi_ref, *, seg_len, n_per, ssm_w):
    n_rows = SUBLANES * seg_len
    n_cb = ssm_w // (2 * LANES)

    def row_chunks(fn):
        def body(c, carry):
            fn(pl.ds(pl.multiple_of(c * SSM_ROWS, SSM_ROWS), SSM_ROWS))
            return carry
        lax.fori_loop(0, n_rows // SSM_ROWS, body, 0)

    def step_rows(t):
        return pl.ds(pl.multiple_of(t * SUBLANES, SUBLANES), SUBLANES)

    def interleave(t, carry):
        ui_ref[step_rows(t), 0:LANES] = ua_ref[pl.ds(t, SUBLANES, stride=seg_len), :]
        ui_ref[step_rows(t), LANES:2 * LANES] = ub_ref[pl.ds(t, SUBLANES, stride=seg_len), :]
        return carry

    lax.fori_loop(0, seg_len, interleave, 0, unroll=4)

    seg = lax.broadcasted_iota(jnp.int32, (SUBLANES, LANES), 0) % n_per

    chains = [(dirn, k) for dirn in range(2) for k in range(n_cb)]
    for part in range(SSM_LANES // ssm_w):
        lo = part * ssm_w

        def fill(rows, lo=lo):
            ub = ui_ref[rows, :].astype(BF16)
            for dirn, k in chains:
                c0 = k * 2 * LANES
                buf_ref[dirn, rows, c0:c0 + 2 * LANES] = jnp.dot(
                    ub, bmat_ref[dirn, :, lo + c0:lo + c0 + 2 * LANES], preferred_element_type=F32)

        row_chunks(fill)

        def lam_rows(r, lo=lo):
            out = []
            for dirn, k in chains:
                c0 = lo + k * 2 * LANES
                out.append(jnp.broadcast_to(lamb_ref[dirn, r:r + 1, c0:c0 + LANES], (SUBLANES, LANES)))
            return out

        a_re, a_im = lam_rows(0), lam_rows(1)
        h0 = []
        for dirn, k in chains:
            c0 = lo + k * 2 * LANES
            h0 += [h0_ref[dirn, :, c0:c0 + LANES], h0_ref[dirn, :, c0 + LANES:c0 + 2 * LANES]]

        def scan(init, store, a_re=a_re, a_im=a_im):
            def step(i, hs):
                rows = (step_rows(i), step_rows(seg_len - 1 - i))
                new = []
                for c, (dirn, k) in enumerate(chains):
                    re_sl = slice(k * 2 * LANES, k * 2 * LANES + LANES)
                    im_sl = slice(k * 2 * LANES + LANES, (k + 1) * 2 * LANES)
                    hr, hi = hs[2 * c], hs[2 * c + 1]
                    nr = a_re[c] * hr - a_im[c] * hi + buf_ref[dirn, rows[dirn], re_sl]
                    ni = a_re[c] * hi + a_im[c] * hr + buf_ref[dirn, rows[dirn], im_sl]
                    if store:
                        buf_ref[dirn, rows[dirn], re_sl] = nr
                        buf_ref[dirn, rows[dirn], im_sl] = ni
                    new += [nr, ni]
                return tuple(new)
            return lax.fori_loop(0, seg_len, step, tuple(init), unroll=2)

        if n_per == 1:
            init = h0
        else:
            local = scan([jnp.zeros((SUBLANES, LANES), F32)] * (2 * len(chains)), False)
            s_re, s_im = lam_rows(2), lam_rows(3)
            init = []
            for c, (dirn, k) in enumerate(chains):
                edge = seg == (0 if dirn == 0 else n_per - 1)
                shift = 1 if dirn == 0 else SUBLANES - 1
                cr, ci = h0[2 * c], h0[2 * c + 1]
                lr = pltpu.roll(local[2 * c], shift, axis=0)
                li = pltpu.roll(local[2 * c + 1], shift, axis=0)
                for _ in range(n_per - 1):
                    pr = pltpu.roll(cr, shift, axis=0)
                    pi = pltpu.roll(ci, shift, axis=0)
                    cr = jnp.where(edge, h0[2 * c], s_re[c] * pr - s_im[c] * pi + lr)
                    ci = jnp.where(edge, h0[2 * c + 1], s_re[c] * pi + s_im[c] * pr + li)
                init += [cr, ci]
        hs = scan(init, True)
        for c, (dirn, k) in enumerate(chains):
            c0 = lo + k * 2 * LANES
            fin_ref[dirn, :, c0:c0 + LANES] = hs[2 * c]
            fin_ref[dirn, :, c0 + LANES:c0 + 2 * LANES] = hs[2 * c + 1]

        def readout(rows, lo=lo, first=(part == 0)):
            y = jnp.dot(buf_ref[0, rows, :].astype(BF16), cmat_ref[0, lo:lo + ssm_w, :],
                        preferred_element_type=F32)
            y = y + jnp.dot(buf_ref[1, rows, :].astype(BF16), cmat_ref[1, lo:lo + ssm_w, :],
                            preferred_element_type=F32)
            if first:
                y = y + d_ref[...] * ui_ref[rows, :]
                yi_ref[0, rows, :] = y[:, :LANES]
                yi_ref[1, rows, :] = y[:, LANES:]
            else:
                yi_ref[0, rows, :] += y[:, :LANES]
                yi_ref[1, rows, :] += y[:, LANES:]

        row_chunks(readout)

    def glu(rows):
        g = _gelu(jnp.concatenate([yi_ref[0, rows, :], yi_ref[1, rows, :]], axis=1))
        z = jnp.dot(g.astype(BF16), wg_ref[...], preferred_element_type=F32) + bg_ref[...]
        o = g * jax.nn.sigmoid(z)
        yi_ref[0, rows, :] = o[:, :LANES]
        yi_ref[1, rows, :] = o[:, LANES:]

    row_chunks(glu)

    def deinterleave(tb, carry):
        for s_idx in range(SUBLANES):
            dst = pl.ds(pl.multiple_of(s_idx * seg_len + tb * SUBLANES, SUBLANES), SUBLANES)
            src = pl.ds(tb * SUBLANES * SUBLANES + s_idx, SUBLANES, stride=SUBLANES)
            y_ref[dst, 0:LANES] = yi_ref[0, src, :]
            y_ref[dst, LANES:2 * LANES] = yi_ref[1, src, :]
        return carry

    lax.fori_loop(0, seg_len // SUBLANES, deinterleave, 0)


def _ssm(ua, ub, h0_lat, bmat, cmat, lamb, d, wg, bg):
    weights = [_const_spec((2, SUBLANES, SSM_LANES)),
               _const_spec((2, D_SSM, SSM_LANES)), _const_spec((2, SSM_LANES, D_SSM)),
               _const_spec((2, 4, SSM_LANES)), _const_spec((1, D_SSM)),
               _const_spec((D_SSM, D_SSM)), _const_spec((1, D_SSM))]

    def scratch(rows, ssm_w):
        return [pltpu.VMEM((rows, D_SSM), F32), pltpu.VMEM((2, rows, ssm_w), F32),
                pltpu.VMEM((2, rows, LANES), F32)]

    rows_ctx = SUBLANES * SEQ
    n_ctx_tiles = N_CTX // rows_ctx
    h0_ctx = jnp.zeros((2, SUBLANES, SSM_LANES), F32)
    y_ctx, fin_ctx = pl.pallas_call(
        functools.partial(_ssm_kernel, seg_len=SEQ, n_per=1, ssm_w=SSM_W_CTX),
        grid=(n_ctx_tiles,),
        in_specs=[pl.BlockSpec((rows_ctx, LANES), lambda i: (i, 0)),
                  pl.BlockSpec((rows_ctx, LANES), lambda i: (i, 0))] + weights,
        out_specs=[pl.BlockSpec((rows_ctx, D_SSM), lambda i: (i, 0)),
                   pl.BlockSpec((None, 2, SUBLANES, SSM_LANES), lambda i: (i, 0, 0, 0))],
        out_shape=[jax.ShapeDtypeStruct((N_CTX, D_SSM), F32),
                   jax.ShapeDtypeStruct((n_ctx_tiles, 2, SUBLANES, SSM_LANES), F32)],
        scratch_shapes=scratch(rows_ctx, SSM_W_CTX),
        compiler_params=_params(1),
        name="ssm_ctx",
    )(ua, ub, h0_ctx, bmat, cmat, lamb, d, wg, bg)
    off = N_CTX // N_LAT
    y_lat, _ = pl.pallas_call(
        functools.partial(_ssm_kernel, seg_len=SSM_LAT_SEG, n_per=SSM_LAT_SEGS, ssm_w=SSM_W_LAT),
        grid=(1,),
        in_specs=[pl.BlockSpec((N_LAT, LANES), lambda i: (off, 0)),
                  pl.BlockSpec((N_LAT, LANES), lambda i: (off, 0))] + weights,
        out_specs=[pl.BlockSpec((N_LAT, D_SSM), lambda i: (0, 0)),
                   pl.BlockSpec((2, SUBLANES, SSM_LANES), lambda i: (0, 0, 0))],
        out_shape=[jax.ShapeDtypeStruct((N_LAT, D_SSM), F32),
                   jax.ShapeDtypeStruct((2, SUBLANES, SSM_LANES), F32)],
        scratch_shapes=scratch(N_LAT, SSM_W_LAT),
        compiler_params=_params(1),
        name="ssm_lat",
    )(ua, ub, h0_lat, bmat, cmat, lamb, d, wg, bg)
    return (y_ctx, y_lat), fin_ctx


def _attn_kernel(*refs, n_pieces):
    q_ref = refs[0]
    k_refs = refs[1:1 + n_pieces]
    v_refs = refs[1 + n_pieces:1 + 2 * n_pieces]
    o_ref = refs[-1]
    tq = q_ref.shape[0]
    q = q_ref[...] * (HEAD_DIM ** -0.5)
    if n_pieces == 1:
        k = k_refs[0][...]
        v = v_refs[0][...]
    else:
        k = jnp.concatenate([r[...] for r in k_refs], axis=0)
        v = jnp.concatenate([r[...] for r in v_refs], axis=0)
    k = k.astype(BF16)
    v = v.astype(BF16)
    lo = lax.broadcasted_iota(jnp.int32, (tq, LANES), 1) < HEAD_DIM
    q_lo = q[:, :LANES]
    q_hi = q[:, LANES:]
    qs = [jnp.where(lo, q_lo, 0.0), jnp.where(lo, pltpu.roll(q_lo, HEAD_DIM, axis=1), 0.0),
          jnp.where(lo, 0.0, pltpu.roll(q_hi, HEAD_DIM, axis=1)), jnp.where(lo, 0.0, q_hi)]
    outs = []
    for g in range(N_KV_HEADS):
        qq = jnp.concatenate([qs[2 * g], qs[2 * g + 1]], axis=0).astype(BF16)
        s = lax.dot_general(qq, k, (((1,), (1,)), ((), ())), preferred_element_type=F32)
        m = jnp.max(s, axis=-1, keepdims=True)
        p = jnp.exp(s - m)
        l = jnp.sum(p, axis=-1, keepdims=True)
        o = jnp.dot(p.astype(BF16), v, preferred_element_type=F32) / l
        outs += [o[:tq], o[tq:]]
    out_lo = jnp.where(lo, outs[0], pltpu.roll(outs[1], HEAD_DIM, axis=1))
    out_hi = jnp.where(lo, pltpu.roll(outs[2], HEAD_DIM, axis=1), outs[3])
    o_ref[...] = jnp.concatenate([out_lo, out_hi], axis=1)


def _attention(q, k, v, cache_k_l, cache_v_l):
    ctx = pl.pallas_call(
        functools.partial(_attn_kernel, n_pieces=1),
        grid=(BATCH,),
        in_specs=[pl.BlockSpec((SEQ, D_ATTN), lambda i: (i, 0)),
                  pl.BlockSpec((SEQ, D_KV), lambda i: (i, 0)),
                  pl.BlockSpec((SEQ, D_KV), lambda i: (i, 0))],
        out_specs=pl.BlockSpec((SEQ, D_ATTN), lambda i: (i, 0)),
        out_shape=jax.ShapeDtypeStruct((N_CTX, D_ATTN), F32),
        compiler_params=_params(1),
        name="attn_ctx",
    )(q, k, v)
    seq_off = N_CTX // DEC_SEQ
    q_off = N_CTX // ATT_TQ
    q_per_seq = DEC_SEQ // ATT_TQ
    lat = pl.pallas_call(
        functools.partial(_attn_kernel, n_pieces=2),
        grid=(DEC_BATCH, q_per_seq),
        in_specs=[pl.BlockSpec((ATT_TQ, D_ATTN), lambda b, j: (q_off + b * q_per_seq + j, 0)),
                  pl.BlockSpec((None, PAST_LEN, D_KV), lambda b, j: (b, 0, 0)),
                  pl.BlockSpec((DEC_SEQ, D_KV), lambda b, j: (seq_off + b, 0)),
                  pl.BlockSpec((None, PAST_LEN, D_KV), lambda b, j: (b, 0, 0)),
                  pl.BlockSpec((DEC_SEQ, D_KV), lambda b, j: (seq_off + b, 0))],
        out_specs=pl.BlockSpec((ATT_TQ, D_ATTN), lambda b, j: (b * q_per_seq + j, 0)),
        out_shape=jax.ShapeDtypeStruct((N_LAT, D_ATTN), F32),
        compiler_params=_params(2),
        name="attn_lat",
    )(q, cache_k_l, k, cache_v_l, v)
    return ctx, lat


def _rope_tables():
    rows = DEC_SEQ // GRID_W
    row_idx = jnp.repeat(jnp.arange(rows), GRID_W).astype(F32)
    col_idx = jnp.tile(jnp.arange(GRID_W), rows).astype(F32)
    n_freq = HEAD_DIM // 4
    inv = ROPE_THETA ** (-jnp.arange(n_freq, dtype=F32) / n_freq)
    ang = jnp.concatenate([row_idx[:, None] * inv, col_idx[:, None] * inv], axis=-1)
    cos = jnp.cos(ang)
    sin = jnp.sin(ang)
    cos_h = jnp.concatenate([cos, cos], axis=-1)
    sin_h = jnp.concatenate([-sin, sin], axis=-1)
    cos_t = jnp.tile(cos_h, (1, N_HEADS))
    sin_t = jnp.tile(sin_h, (1, N_HEADS))
    cos_t = jnp.concatenate([cos_t, jnp.ones((TM, D_ATTN), F32)], axis=0)
    sin_t = jnp.concatenate([sin_t, jnp.zeros((TM, D_ATTN), F32)], axis=0)
    return cos_t, sin_t


def _state_lanes(a):
    return a.reshape(a.shape[:-2] + (N_SSM_GROUPS // 2, 2 * SSM_STATE))


def _ssm_lane_rows(a):
    a = _state_lanes(a)
    return jnp.stack([a, a], axis=-2).reshape(a.shape[:-2] + (SSM_LANES,))


def _ssm_pack_state(re, im):
    return jnp.stack([_state_lanes(re), _state_lanes(im)], axis=-2).reshape(re.shape[:-2] + (SSM_LANES,))


def _ssm_unpack_state(s):
    s = s.reshape(s.shape[:-1] + (N_SSM_GROUPS // 2, 2, 2 * SSM_STATE))
    shp = s.shape[:-3] + (N_SSM_GROUPS, SSM_STATE)
    return s[..., 0, :].reshape(shp), s[..., 1, :].reshape(shp)


def _ssm_group_mask():
    ch_g = jnp.arange(D_SSM) // SSM_GROUP
    lane = jnp.arange(SSM_LANES)
    lane_g = 2 * (lane // (2 * LANES)) + (lane % LANES) // SSM_STATE
    return ch_g[:, None] == lane_g[None, :]


def _place_b(b):
    bt = jnp.swapaxes(b, -1, -2).reshape(b.shape[:-3] + (D_SSM, SSM_STATE))
    tiled = jnp.tile(bt, (1,) * (bt.ndim - 1) + (SSM_LANES // SSM_STATE,))
    return jnp.where(_ssm_group_mask(), tiled, 0.0)


def _place_c(c_re, c_im):
    def rows(c):
        ct = jnp.moveaxis(c, -1, -3).reshape(c.shape[:-3] + (SSM_STATE, D_SSM))
        return jnp.tile(ct, (1,) * (ct.ndim - 2) + (SSM_LANES // SSM_STATE, 1))
    is_re = (jnp.arange(SSM_LANES) % (2 * LANES) < LANES)[:, None]
    return jnp.where(_ssm_group_mask().T, jnp.where(is_re, rows(c_re), -rows(c_im)), 0.0)


def _block_diag_pool(pool_w):
    eye = jnp.eye(len(POOL_WINDOWS), dtype=F32)
    return jnp.einsum('gcd,gh->gchd', pool_w, eye).reshape(D_POOL, D_POOL)


def kernel(x_prompt, x_sample, cache_k, cache_v, state_ssm_re, state_ssm_im, c, c_ctx, w_mod, b_mod, ln_g, ln_b, ffn_w1, ffn_w2, w_in, w_out, pool_w, pool_scale, ssm_lam_re, ssm_lam_im, ssm_log_step, ssm_b_re, ssm_b_im, ssm_c_re, ssm_c_im, ssm_d, ssm_w_glu, ssm_b_glu, q_norm, k_norm, chunk_ln_g, chunk_ln_b, chunk_w_s, chunk_b_s):
    x = (x_prompt.reshape(N_CTX, D_MODEL), x_sample.reshape(N_LAT, D_MODEL))
    cond =jnp.concatenate([c_ctx[None, :], c, jnp.zeros((N_COND - 1 - DEC_BATCH, D_MODEL), F32)], axis=0)
    mod = _modulation(cond, w_mod, b_mod)
    cos_t, sin_t = _rope_tables()

    log_step = jnp.broadcast_to(ssm_log_step[..., None], ssm_lam_re.shape)
    lam_rows = jnp.stack([_ssm_lane_rows(ssm_lam_re), _ssm_lane_rows(ssm_lam_im),
                          _ssm_lane_rows(log_step)], axis=-2)
    bmat, lamb = _ssm_prep(lam_rows.reshape(DEPTH * 2, 3, SSM_LANES),
                           _place_b(ssm_b_re).reshape(DEPTH * 2, D_SSM, SSM_LANES),
                           _place_b(ssm_b_im).reshape(DEPTH * 2, D_SSM, SSM_LANES))
    bmat = bmat.reshape(DEPTH, 2, D_SSM, SSM_LANES)
    lamb = lamb.reshape(DEPTH, 2, 4, SSM_LANES)
    cmat = _place_c(ssm_c_re, ssm_c_im).astype(BF16)
    h0 = _ssm_pack_state(state_ssm_re, state_ssm_im)
    h0_lat = jnp.zeros((DEPTH, 2, DEC_BATCH, SSM_LAT_SEGS, SSM_LANES), F32)
    h0_lat = h0_lat.at[:, 0, :, 0].set(jnp.swapaxes(h0[:, :, 0], 0, 1))
    h0_lat = h0_lat.at[:, 1, :, SSM_LAT_SEGS - 1].set(jnp.swapaxes(h0[:, :, 1], 0, 1))
    h0_lat = h0_lat.reshape(DEPTH, 2, SUBLANES, SSM_LANES)

    ks, vs, s_re, s_im = [], [], [], []
    for l in range(DEPTH):
        mod_l = mod[l]
        row = lambda a: a.reshape(1, -1)
        x = _ffn_sublayer(x, mod_l, ffn_w1[l, 0].astype(BF16), ffn_w2[l, 0].astype(BF16),
                          row(ln_g[l, 0]), row(ln_b[l, 0]), 0)
        bs = jnp.repeat(chunk_b_s[l].T, D_CHUNK // N_CHUNK_HEADS, axis=1)
        p_pool, u_a, u_b, q, k, v, y_chunk = _inproj(
            x, mod_l, w_in[l].astype(BF16), cos_t, sin_t,
            row(jnp.tile(q_norm[l], N_HEADS)), row(jnp.tile(k_norm[l], N_KV_HEADS)),
            row(chunk_ln_g[l]), row(chunk_ln_b[l]), chunk_w_s[l].astype(BF16), bs)
        y_pool = _pool(p_pool, _block_diag_pool(pool_w[l]).astype(BF16), row(pool_scale[l]))
        y_ssm, fin = _ssm(u_a, u_b, h0_lat[l], bmat[l], cmat[l], lamb[l],
                          row(ssm_d[l]), ssm_w_glu[l].astype(BF16), row(ssm_b_glu[l]))
        y_attn = _attention(q, k, v, cache_k[:, l].reshape(DEC_BATCH, PAST_LEN, D_KV),
                            cache_v[:, l].reshape(DEC_BATCH, PAST_LEN, D_KV))
        x = _ffn_sublayer(x, mod_l, ffn_w1[l, 1].astype(BF16), ffn_w2[l, 1].astype(BF16),
                          row(ln_g[l, 2]), row(ln_b[l, 2]), 6, split_out=(l == DEPTH - 1),
                          mixer=(y_pool, y_ssm, y_attn, y_chunk, w_out[l].astype(BF16),
                                 row(ln_g[l, 1]), row(ln_b[l, 1])))
        ks.append(k[:N_CTX].reshape(BATCH, SEQ, N_KV_HEADS, HEAD_DIM))
        vs.append(v[:N_CTX].reshape(BATCH, SEQ, N_KV_HEADS, HEAD_DIM))
        f_re, f_im = _ssm_unpack_state(jnp.transpose(fin, (0, 2, 1, 3)).reshape(BATCH, 2, SSM_LANES))
        s_re.append(f_re)
        s_im.append(f_im)
    y_p = x[0].reshape(BATCH, SEQ, D_MODEL)
    y_s = x[1].reshape(DEC_BATCH, DEC_SEQ, D_MODEL)
    return (y_p, y_s, jnp.stack(ks, axis=1), jnp.stack(vs, axis=1),
            jnp.stack(s_re, axis=1), jnp.stack(s_im, axis=1))
```

```python
import functools
import math

import jax
import jax.numpy as jnp
from jax import lax
from jax.experimental import pallas as pl
from jax.experimental.pallas import tpu as pltpu

F32 = jnp.float32
BF16 = jnp.bfloat16

D_MODEL = 1024
BATCH = 32
SEQ = 256
DEPTH = 2
DEC_BATCH = 2
DEC_SEQ = 2048
PAST_LEN = 512
GRID_W = 64
D_POOL = 256
D_SSM = 256
D_ATTN = 256
D_CHUNK = 256
D_HEADS = 256
POOL_WINDOWS = (2, 4, 8, 16)
POOL_GROUP = 64
SSM_GROUP = 16
N_SSM_GROUPS = 16
SSM_STATE = 64
HEAD_DIM = 64
N_HEADS = 4
N_KV_HEADS = 2
D_KV = 128
CHUNK = 128
N_CHUNK_HEADS = 4
D_FF = 2816
N_MOD = 9
D_IN = 1536
ALPHA = (2 * DEPTH) ** 0.25
LN_EPS = 1e-5
RMS_EPS = 1e-6
ROPE_THETA = 10000.0

LANES = 128
SUBLANES = 8
TM = 512
ATT_TQ = 256
N_CTX = BATCH * SEQ
N_LAT = DEC_BATCH * DEC_SEQ
N_TOK = N_CTX + N_LAT
CTX_TILES = N_CTX // TM
LAT_TILES_PER_SEQ = DEC_SEQ // TM
N_TILES = N_TOK // TM
N_COND = 8
POOL_PAD = 8
SSM_LANES = 2 * N_SSM_GROUPS * SSM_STATE
SSM_W_CTX = 1024
SSM_W_LAT = 512
SSM_LAT_SEGS = SUBLANES // DEC_BATCH
SSM_LAT_SEG = DEC_SEQ // SSM_LAT_SEGS
SSM_ROWS = 1024
VMEM_LIMIT = 56 * 1024 * 1024


def _params(n_grid):
    return pltpu.CompilerParams(dimension_semantics=("arbitrary",) * n_grid,
                                vmem_limit_bytes=VMEM_LIMIT)


def _fixed_spec(tail, *lead):
    idx = tuple(lead) + (0,) * len(tail)
    return pl.BlockSpec((None,) * len(lead) + tuple(tail), lambda *_: idx, pipeline_mode=pl.Buffered(1))


def _cond_row(i):
    return jnp.where(i < CTX_TILES, 0, 1 + (i - CTX_TILES) // LAT_TILES_PER_SEQ)


def _mod_spec(l):
    return pl.BlockSpec((None, None, N_MOD, D_MODEL), lambda i: (l, _cond_row(i), 0, 0))


def _tok_tile(i):
    return (i, 0)


def _ctx_tile(i):
    return (jnp.minimum(i, CTX_TILES - 1), 0)


def _lat_tile(i):
    return (jnp.maximum(i - CTX_TILES, 0), 0)


def _layer_norm(y, g, b):
    mu = jnp.mean(y, axis=-1, keepdims=True)
    d = y - mu
    var = jnp.mean(d * d, axis=-1, keepdims=True)
    return d * lax.rsqrt(var + LN_EPS) * g + b


def _gelu(x):
    return 0.5 * x * (1.0 + jnp.tanh(math.sqrt(2.0 / math.pi) * (x + 0.044715 * (x * x * x))))


def _silu(x):
    return x * jax.nn.sigmoid(x)


def _split_bf16(a):
    hi = a.astype(BF16)
    return hi, (a - hi.astype(F32)).astype(BF16)


def _mod_kernel(cond_ref, w_ref, b_ref, o_ref):
    a_hi, a_lo = _split_bf16(_silu(cond_ref[...]))
    w_hi, w_lo = _split_bf16(w_ref[...])
    acc = jnp.dot(a_hi, w_lo, preferred_element_type=F32)
    acc = acc + jnp.dot(a_lo, w_hi, preferred_element_type=F32)
    acc = acc + jnp.dot(a_hi, w_hi, preferred_element_type=F32)
    o_ref[...] = acc + b_ref[...]


def _modulation(cond, w_mod, b_mod):
    tn = D_MODEL
    out = pl.pallas_call(
        _mod_kernel,
        grid=(DEPTH, N_MOD),
        in_specs=[
            pl.BlockSpec((N_COND, D_MODEL), lambda l, j: (0, 0)),
            pl.BlockSpec((None, D_MODEL, tn), lambda l, j: (l, 0, j)),
            pl.BlockSpec((None, 1, tn), lambda l, j: (l, 0, j)),
        ],
        out_specs=pl.BlockSpec((None, N_COND, tn), lambda l, j: (l, 0, j)),
        out_shape=jax.ShapeDtypeStruct((DEPTH, N_COND, N_MOD * D_MODEL), F32),
        compiler_params=_params(2),
        name="modulation",
    )(cond, w_mod, b_mod.reshape(DEPTH, 1, N_MOD * D_MODEL))
    return out.reshape(DEPTH, N_COND, N_MOD, D_MODEL)


def _ffn_kernel(*refs, mod_off, split_in, split_out, mixer_out):
    n_in = 2 if split_in else 1
    n_mix = 10 if mixer_out else 0
    mix_refs = refs[n_in:n_in + n_mix]
    mod_ref, w1_ref, w2_ref, g_ref, b_ref = refs[n_in + n_mix:n_in + n_mix + 5]
    out_refs = refs[n_in + n_mix + 5:]
    is_ctx = pl.program_id(0) < CTX_TILES
    if split_in:
        x = jnp.where(is_ctx, refs[0][...], refs[1][...])
    else:
        x = refs[0][...]
    if mixer_out:
        yc_ref, wo_ref, go_ref, bo_ref = mix_refs[6:]
        parts = [jnp.where(is_ctx, mix_refs[2 * j][...], mix_refs[2 * j + 1][...]) for j in range(3)]
        parts.append(yc_ref[...])
        y = None
        for j, part in enumerate(parts):
            r = jnp.dot(part.astype(BF16), wo_ref[D_HEADS * j:D_HEADS * (j + 1), :],
                        preferred_element_type=F32)
            y = r if y is None else y + r
        x = _layer_norm(ALPHA * x + mod_ref[5:6, :] * y, go_ref[...], bo_ref[...])
    sh = mod_ref[mod_off:mod_off + 1, :]
    sc = mod_ref[mod_off + 1:mod_off + 2, :]
    gate = mod_ref[mod_off + 2:mod_off + 3, :]
    h = (x * (1.0 + sc) + sh).astype(BF16)
    gu = jnp.dot(h, w1_ref[...], preferred_element_type=F32)
    a = (_silu(gu[:, :D_FF]) * gu[:, D_FF:]).astype(BF16)
    f = jnp.dot(a, w2_ref[...], preferred_element_type=F32)
    y = _layer_norm(ALPHA * x + (0.5 * gate) * f, g_ref[...], b_ref[...])
    if split_out:
        @pl.when(is_ctx)
        def _():
            out_refs[0][...] = y

        @pl.when(jnp.logical_not(is_ctx))
        def _():
            out_refs[1][...] = y
    else:
        out_refs[0][...] = y


def _ffn_sublayer(xs, mod, prm, l, sub, split_out=False, mixer=None):
    split_in = isinstance(xs, tuple)
    if split_in:
        x_specs = [pl.BlockSpec((TM, D_MODEL), _ctx_tile), pl.BlockSpec((TM, D_MODEL), _lat_tile)]
    else:
        xs = (xs,)
        x_specs = [pl.BlockSpec((TM, D_MODEL), _tok_tile)]
    mix_args, mix_specs = (), []
    if mixer is not None:
        y_pool, y_ssm, y_attn, y_chunk = mixer
        mix_args = (*y_pool, *y_ssm, *y_attn, y_chunk, prm['w_out'], prm['ln_g'], prm['ln_b'])
        mix_specs = [pl.BlockSpec((TM, D_HEADS), _ctx_tile), pl.BlockSpec((TM, D_HEADS), _lat_tile)] * 3 + [
            pl.BlockSpec((TM, D_HEADS), _tok_tile),
            _fixed_spec((D_MODEL, D_MODEL), l), _fixed_spec((1, D_MODEL), l, 1), _fixed_spec((1, D_MODEL), l, 1)]
    if split_out:
        out_specs = [pl.BlockSpec((TM, D_MODEL), _ctx_tile), pl.BlockSpec((TM, D_MODEL), _lat_tile)]
        out_shape = [jax.ShapeDtypeStruct((N_CTX, D_MODEL), F32), jax.ShapeDtypeStruct((N_LAT, D_MODEL), F32)]
    else:
        out_specs = pl.BlockSpec((TM, D_MODEL), _tok_tile)
        out_shape = jax.ShapeDtypeStruct((N_TOK, D_MODEL), F32)
    ln_idx = 2 * sub
    return pl.pallas_call(
        functools.partial(_ffn_kernel, mod_off=6 * sub, split_in=split_in, split_out=split_out,
                          mixer_out=mixer is not None),
        grid=(N_TILES,),
        in_specs=x_specs + mix_specs + [
            _mod_spec(l),
            _fixed_spec((D_MODEL, 2 * D_FF), l, sub),
            _fixed_spec((D_FF, D_MODEL), l, sub),
            _fixed_spec((1, D_MODEL), l, ln_idx),
            _fixed_spec((1, D_MODEL), l, ln_idx),
        ],
        out_specs=out_specs,
        out_shape=out_shape,
        compiler_params=_params(1),
        name="ffn_sublayer",
    )(*xs, *mix_args, mod, prm['ffn_w1'], prm['ffn_w2'], prm['ln_g'], prm['ln_b'])


def _pool_math(x, w_bd, scale, seq_len):
    n = seq_len + 2 * POOL_PAD
    zpad = jnp.zeros((POOL_PAD, D_POOL), F32)
    e = jnp.concatenate([zpad, x, zpad], axis=0)

    def prev(a, d):
        return pltpu.roll(a, d, axis=0)

    def nxt(a, d):
        return pltpu.roll(a, n - d, axis=0)

    s2 = e + prev(e, 1)
    s4 = prev(s2, 1) + nxt(s2, 1)
    s8 = prev(s4, 2) + nxt(s4, 2)
    s16 = prev(s8, 4) + nxt(s8, 4)
    lane = lax.broadcasted_iota(jnp.int32, (seq_len, D_POOL), 1)
    grp = lane // POOL_GROUP
    sl = slice(POOL_PAD, POOL_PAD + seq_len)
    s = jnp.where(grp == 0, s2[sl], jnp.where(grp == 1, s4[sl], jnp.where(grp == 2, s8[sl], s16[sl])))
    half = jnp.where(grp == 0, 1, jnp.where(grp == 1, 2, jnp.where(grp == 2, 4, 8)))
    t = lax.broadcasted_iota(jnp.int32, (seq_len, D_POOL), 0)
    cnt = jnp.minimum(t + half, seq_len) - jnp.maximum(t - half, 0)
    d = (s / cnt.astype(F32) - x).astype(BF16)
    return jnp.dot(d, w_bd, preferred_element_type=F32) * scale


def _attn_math(q, k, v):
    tq = q.shape[0]
    q = q * (HEAD_DIM ** -0.5)
    k = k.astype(BF16)
    v = v.astype(BF16)
    lo = lax.broadcasted_iota(jnp.int32, (tq, LANES), 1) < HEAD_DIM
    q_lo = q[:, :LANES]
    q_hi = q[:, LANES:]
    qs = [jnp.where(lo, q_lo, 0.0), jnp.where(lo, pltpu.roll(q_lo, HEAD_DIM, axis=1), 0.0),
          jnp.where(lo, 0.0, pltpu.roll(q_hi, HEAD_DIM, axis=1)), jnp.where(lo, 0.0, q_hi)]
    outs = []
    for g in range(N_KV_HEADS):
        qq = jnp.concatenate([qs[2 * g], qs[2 * g + 1]], axis=0).astype(BF16)
        s = lax.dot_general(qq, k, (((1,), (1,)), ((), ())), preferred_element_type=F32)
        m = jnp.max(s, axis=-1, keepdims=True)
        p = jnp.exp(s - m)
        l = jnp.sum(p, axis=-1, keepdims=True)
        o = jnp.dot(p.astype(BF16), v, preferred_element_type=F32) / l
        outs += [o[:tq], o[tq:]]
    out_lo = jnp.where(lo, outs[0], pltpu.roll(outs[1], HEAD_DIM, axis=1))
    out_hi = jnp.where(lo, pltpu.roll(outs[2], HEAD_DIM, axis=1), outs[3])
    return jnp.concatenate([out_lo, out_hi], axis=1)


def _seg_rms(x, gain, n_lanes):
    parts = []
    for j in range(n_lanes // LANES):
        xs = x[:, j * LANES:(j + 1) * LANES]
        sq = xs * xs
        lo = lax.broadcasted_iota(jnp.int32, xs.shape, 1) < HEAD_DIM
        s_all = jnp.sum(sq, axis=-1, keepdims=True)
        s_lo = jnp.sum(jnp.where(lo, sq, 0.0), axis=-1, keepdims=True)
        ms = jnp.where(lo, s_lo, s_all - s_lo) * (1.0 / HEAD_DIM)
        parts.append(xs * lax.rsqrt(ms + RMS_EPS))
    y = parts[0] if len(parts) == 1 else jnp.concatenate(parts, axis=1)
    return y * gain


def _rope(x, cos_t, sin_t, n_lanes):
    parts = []
    for j in range(n_lanes // LANES):
        xs = x[:, j * LANES:(j + 1) * LANES]
        first = (lax.broadcasted_iota(jnp.int32, xs.shape, 1) % HEAD_DIM) < (HEAD_DIM // 2)
        partner = jnp.where(first, pltpu.roll(xs, LANES - HEAD_DIM // 2, axis=1),
                            pltpu.roll(xs, HEAD_DIM // 2, axis=1))
        cs = cos_t[:, j * LANES:(j + 1) * LANES]
        sn = sin_t[:, j * LANES:(j + 1) * LANES]
        parts.append(xs * cs + partner * sn)
    return parts[0] if len(parts) == 1 else jnp.concatenate(parts, axis=1)


def _inproj_kernel(x_ref, mod_ref, w_ref, cos_ref, sin_ref, qn_ref, kn_ref,
                   cg_ref, cb_ref, ws_ref, bs_ref, pw_ref, ps_ref,
                   ua_ref, ub_ref, chunk_ref, kc_ref, vc_ref, yp_ref, ya_ref,
                   kl_ref, vl_ref, pl_ref, ql_ref):
    is_ctx = pl.program_id(0) < CTX_TILES
    x = x_ref[...]
    sh = mod_ref[3:4, :]
    sc = mod_ref[4:5, :]
    h = (x * (1.0 + sc) + sh).astype(BF16)
    proj = jnp.dot(h, w_ref[...], preferred_element_type=F32)
    p = proj[:, 0:256]
    ua_ref[...] = proj[:, 256:384]
    ub_ref[...] = proj[:, 384:512]
    cos_t = cos_ref[...]
    sin_t = sin_ref[...]
    q = _rope(_seg_rms(proj[:, 512:768], qn_ref[...], D_ATTN), cos_t, sin_t, D_ATTN)
    k = _rope(_seg_rms(proj[:, 768:896], kn_ref[...], D_KV), cos_t, sin_t, D_KV)
    v = proj[:, 896:1024]
    zu = _gelu(proj[:, 1024:1280])
    zv = _layer_norm(_gelu(proj[:, 1280:1536]), cg_ref[...], cb_ref[...])
    head = lax.broadcasted_iota(jnp.int32, (CHUNK, D_CHUNK), 1) // (D_CHUNK // N_CHUNK_HEADS)
    for c in range(TM // CHUNK):
        vb = zv[c * CHUNK:(c + 1) * CHUNK, :].astype(BF16)
        mixed = bs_ref[...]
        for hd in range(N_CHUNK_HEADS):
            r = jnp.dot(ws_ref[hd], vb, preferred_element_type=F32)
            mixed = mixed + jnp.where(head == hd, r, 0.0)
        chunk_ref[c * CHUNK:(c + 1) * CHUNK, :] = zu[c * CHUNK:(c + 1) * CHUNK, :] * mixed

    @pl.when(is_ctx)
    def _():
        kc_ref[...] = k
        vc_ref[...] = v
        for s in range(TM // SEQ):
            r = slice(s * SEQ, (s + 1) * SEQ)
            yp_ref[r, :] = _pool_math(p[r], pw_ref[...], ps_ref[...], SEQ)
            ya_ref[r, :] = _attn_math(q[r], k[r], v[r])

    @pl.when(jnp.logical_not(is_ctx))
    def _():
        kl_ref[...] = k
        vl_ref[...] = v
        pl_ref[...] = p
        ql_ref[...] = q


def _rope_block(i):
    return jnp.where(i < CTX_TILES, LAT_TILES_PER_SEQ, (i - CTX_TILES) % LAT_TILES_PER_SEQ)


def _inproj(x, mod, prm, l, cos_t, sin_t):
    def out(rows, width, index_map):
        return jax.ShapeDtypeStruct((rows, width), F32), pl.BlockSpec((TM, width), index_map)

    outs = [out(N_TOK, LANES, _tok_tile), out(N_TOK, LANES, _tok_tile), out(N_TOK, D_CHUNK, _tok_tile),
            out(N_CTX, D_KV, _ctx_tile), out(N_CTX, D_KV, _ctx_tile),
            out(N_CTX, D_POOL, _ctx_tile), out(N_CTX, D_ATTN, _ctx_tile),
            out(N_LAT, D_KV, _lat_tile), out(N_LAT, D_KV, _lat_tile),
            out(N_LAT, D_POOL, _lat_tile), out(N_LAT, D_ATTN, _lat_tile)]
    return pl.pallas_call(
        _inproj_kernel,
        grid=(N_TILES,),
        in_specs=[
            pl.BlockSpec((TM, D_MODEL), _tok_tile),
            _mod_spec(l),
            _fixed_spec((D_MODEL, D_IN), l),
            pl.BlockSpec((TM, D_ATTN), lambda i: (_rope_block(i), 0)),
            pl.BlockSpec((TM, D_ATTN), lambda i: (_rope_block(i), 0)),
            _fixed_spec((1, D_ATTN), l),
            _fixed_spec((1, D_KV), l),
            _fixed_spec((1, D_CHUNK), l),
            _fixed_spec((1, D_CHUNK), l),
            _fixed_spec((N_CHUNK_HEADS, CHUNK, CHUNK), l),
            _fixed_spec((CHUNK, D_CHUNK), l),
            _fixed_spec((D_POOL, D_POOL), l),
            _fixed_spec((1, D_POOL), l),
        ],
        out_specs=[o[1] for o in outs],
        out_shape=[o[0] for o in outs],
        compiler_params=_params(1),
        name="mixer_inproj",
    )(x, mod, prm['w_in'], cos_t, sin_t, prm['q_norm'], prm['k_norm'], prm['chunk_ln_g'],
      prm['chunk_ln_b'], prm['chunk_w_s'], prm['chunk_b_s'], prm['pool_w'], prm['pool_scale'])


def _pool_kernel(p_ref, w_ref, s_ref, o_ref):
    o_ref[...] = _pool_math(p_ref[...], w_ref[...], s_ref[...], DEC_SEQ)


def _pool_lat(p_lat, prm, l):
    return pl.pallas_call(
        _pool_kernel,
        grid=(DEC_BATCH,),
        in_specs=[pl.BlockSpec((DEC_SEQ, D_POOL), lambda b: (b, 0)),
                  _fixed_spec((D_POOL, D_POOL), l), _fixed_spec((1, D_POOL), l)],
        out_specs=pl.BlockSpec((DEC_SEQ, D_POOL), lambda b: (b, 0)),
        out_shape=jax.ShapeDtypeStruct((N_LAT, D_POOL), F32),
        compiler_params=_params(1),
        name="pool_lat",
    )(p_lat, prm['pool_w'], prm['pool_scale'])


def _attn_kernel(q_ref, kc_ref, kl_ref, vc_ref, vl_ref, o_ref):
    k = jnp.concatenate([kc_ref[...], kl_ref[...]], axis=0)
    v = jnp.concatenate([vc_ref[...], vl_ref[...]], axis=0)
    o_ref[...] = _attn_math(q_ref[...], k, v)


def _attn_lat(q_lat, k_lat, v_lat, cache_k, cache_v, l):
    q_per_seq = DEC_SEQ // ATT_TQ
    cache_spec = pl.BlockSpec((None, None, PAST_LEN, D_KV), lambda b, j: (b, l, 0, 0))
    seq_spec = pl.BlockSpec((DEC_SEQ, D_KV), lambda b, j: (b, 0))
    return pl.pallas_call(
        _attn_kernel,
        grid=(DEC_BATCH, q_per_seq),
        in_specs=[pl.BlockSpec((ATT_TQ, D_ATTN), lambda b, j: (b * q_per_seq + j, 0)),
                  cache_spec, seq_spec, cache_spec, seq_spec],
        out_specs=pl.BlockSpec((ATT_TQ, D_ATTN), lambda b, j: (b * q_per_seq + j, 0)),
        out_shape=jax.ShapeDtypeStruct((N_LAT, D_ATTN), F32),
        compiler_params=_params(2),
        name="attn_lat",
    )(q_lat, cache_k, k_lat, cache_v, v_lat)


def _ssm_prep_kernel(lam_ref, br_ref, bi_ref, bmat_ref, lamb_ref):
    lre = lam_ref[0:1, :]
    lim = lam_ref[1:2, :]
    dt = jnp.exp(lam_ref[2:3, :])
    mag = jnp.exp(lre * dt)
    ar = mag * jnp.cos(lim * dt)
    ai = mag * jnp.sin(lim * dt)
    lamb_ref[0:1, :] = ar
    lamb_ref[1:2, :] = ai
    mag_s = jnp.exp(lre * dt * SSM_LAT_SEG)
    lamb_ref[2:3, :] = mag_s * jnp.cos(lim * dt * SSM_LAT_SEG)
    lamb_ref[3:4, :] = mag_s * jnp.sin(lim * dt * SSM_LAT_SEG)
    den = lre * lre + lim * lim
    cr = ((ar - 1.0) * lre + ai * lim) / den
    ci = (ai * lre - (ar - 1.0) * lim) / den
    br = br_ref[...]
    bi = bi_ref[...]
    lane = lax.broadcasted_iota(jnp.int32, br.shape, 1)
    is_re = (lane % (2 * LANES)) < LANES
    bmat_ref[...] = jnp.where(is_re, cr * br - ci * bi, cr * bi + ci * br).astype(BF16)


def _ssm_prep(lam_rows, b_re_placed, b_im_placed):
    n = DEPTH * 2
    return pl.pallas_call(
        _ssm_prep_kernel,
        grid=(n,),
        in_specs=[pl.BlockSpec((None, 3, SSM_LANES), lambda i: (i, 0, 0)),
                  pl.BlockSpec((None, D_SSM, SSM_LANES), lambda i: (i, 0, 0)),
                  pl.BlockSpec((None, D_SSM, SSM_LANES), lambda i: (i, 0, 0))],
        out_specs=[pl.BlockSpec((None, D_SSM, SSM_LANES), lambda i: (i, 0, 0)),
                   pl.BlockSpec((None, 4, SSM_LANES), lambda i: (i, 0, 0))],
        out_shape=[jax.ShapeDtypeStruct((n, D_SSM, SSM_LANES), BF16),
                   jax.ShapeDtypeStruct((n, 4, SSM_LANES), F32)],
        compiler_params=_params(1),
        name="ssm_prep",
    )(lam_rows, b_re_placed, b_im_placed)


def _ssm_kernel(ua_ref, ub_ref, h0_ref, bmat_ref, cmat_ref, lamb_ref, d_ref, wg_ref, bg_ref,
                y_ref, fin_ref, ui_ref, buf_ref, yi_ref, *, seg_len, n_per, ssm_w):
    n_rows = SUBLANES * seg_len
    n_cb = ssm_w // (2 * LANES)

    def row_chunks(fn):
        def body(c, carry):
            fn(pl.ds(pl.multiple_of(c * SSM_ROWS, SSM_ROWS), SSM_ROWS))
            return carry
        lax.fori_loop(0, n_rows // SSM_ROWS, body, 0)

    def step_rows(t):
        return pl.ds(pl.multiple_of(t * SUBLANES, SUBLANES), SUBLANES)

    def interleave(t, carry):
        ui_ref[step_rows(t), 0:LANES] = ua_ref[pl.ds(t, SUBLANES, stride=seg_len), :]
        ui_ref[step_rows(t), LANES:2 * LANES] = ub_ref[pl.ds(t, SUBLANES, stride=seg_len), :]
        return carry

    lax.fori_loop(0, seg_len, interleave, 0, unroll=4)

    seg = lax.broadcasted_iota(jnp.int32, (SUBLANES, LANES), 0) % n_per

    chains = [(dirn, k) for dirn in range(2) for k in range(n_cb)]
    for part in range(SSM_LANES // ssm_w):
        lo = part * ssm_w

        def fill(rows, lo=lo):
            ub = ui_ref[rows, :].astype(BF16)
            for dirn, k in chains:
                c0 = k * 2 * LANES
                buf_ref[dirn, rows, c0:c0 + 2 * LANES] = jnp.dot(
                    ub, bmat_ref[dirn, :, lo + c0:lo + c0 + 2 * LANES], preferred_element_type=F32)

        row_chunks(fill)

        def lam_rows(r, lo=lo):
            out = []
            for dirn, k in chains:
                c0 = lo + k * 2 * LANES
                out.append(jnp.broadcast_to(lamb_ref[dirn, r:r + 1, c0:c0 + LANES], (SUBLANES, LANES)))
            return out

        a_re, a_im = lam_rows(0), lam_rows(1)
        h0 = []
        for dirn, k in chains:
            c0 = lo + k * 2 * LANES
            h0 += [h0_ref[dirn, :, c0:c0 + LANES], h0_ref[dirn, :, c0 + LANES:c0 + 2 * LANES]]

        def scan(init, store, a_re=a_re, a_im=a_im):
            def step(i, hs):
                rows = (step_rows(i), step_rows(seg_len - 1 - i))
                new = []
                for c, (dirn, k) in enumerate(chains):
                    re_sl = slice(k * 2 * LANES, k * 2 * LANES + LANES)
                    im_sl = slice(k * 2 * LANES + LANES, (k + 1) * 2 * LANES)
                    hr, hi = hs[2 * c], hs[2 * c + 1]
                    nr = a_re[c] * hr - a_im[c] * hi + buf_ref[dirn, rows[dirn], re_sl]
                    ni = a_re[c] * hi + a_im[c] * hr + buf_ref[dirn, rows[dirn], im_sl]
                    if store:
                        buf_ref[dirn, rows[dirn], re_sl] = nr
                        buf_ref[dirn, rows[dirn], im_sl] = ni
                    new += [nr, ni]
                return tuple(new)
            return lax.fori_loop(0, seg_len, step, tuple(init), unroll=2)

        if n_per == 1:
            init = h0
        else:
            local = scan([jnp.zeros((SUBLANES, LANES), F32)] * (2 * len(chains)), False)
            s_re, s_im = lam_rows(2), lam_rows(3)
            init = []
            for c, (dirn, k) in enumerate(chains):
                edge = seg == (0 if dirn == 0 else n_per - 1)
                shift = 1 if dirn == 0 else SUBLANES - 1
                cr, ci = h0[2 * c], h0[2 * c + 1]
                lr = pltpu.roll(local[2 * c], shift, axis=0)
                li = pltpu.roll(local[2 * c + 1], shift, axis=0)
                for _ in range(n_per - 1):
                    pr = pltpu.roll(cr, shift, axis=0)
                    pi = pltpu.roll(ci, shift, axis=0)
                    cr = jnp.where(edge, h0[2 * c], s_re[c] * pr - s_im[c] * pi + lr)
                    ci = jnp.where(edge, h0[2 * c + 1], s_re[c] * pi + s_im[c] * pr + li)
                init += [cr, ci]
        hs = scan(init, True)
        for c, (dirn, k) in enumerate(chains):
            c0 = lo + k * 2 * LANES
            fin_ref[dirn, :, c0:c0 + LANES] = hs[2 * c]
            fin_ref[dirn, :, c0 + LANES:c0 + 2 * LANES] = hs[2 * c + 1]

        def readout(rows, lo=lo, first=(part == 0)):
            y = jnp.dot(buf_ref[0, rows, :].astype(BF16), cmat_ref[0, lo:lo + ssm_w, :],
                        preferred_element_type=F32)
            y = y + jnp.dot(buf_ref[1, rows, :].astype(BF16), cmat_ref[1, lo:lo + ssm_w, :],
                            preferred_element_type=F32)
            if first:
                y = y + d_ref[...] * ui_ref[rows, :]
                yi_ref[0, rows, :] = y[:, :LANES]
                yi_ref[1, rows, :] = y[:, LANES:]
            else:
                yi_ref[0, rows, :] += y[:, :LANES]
                yi_ref[1, rows, :] += y[:, LANES:]

        row_chunks(readout)

    def glu(rows):
        g = _gelu(jnp.concatenate([yi_ref[0, rows, :], yi_ref[1, rows, :]], axis=1))
        z = jnp.dot(g.astype(BF16), wg_ref[...], preferred_element_type=F32) + bg_ref[...]
        o = g * jax.nn.sigmoid(z)
        yi_ref[0, rows, :] = o[:, :LANES]
        yi_ref[1, rows, :] = o[:, LANES:]

    row_chunks(glu)

    def deinterleave(tb, carry):
        for s_idx in range(SUBLANES):
            dst = pl.ds(pl.multiple_of(s_idx * seg_len + tb * SUBLANES, SUBLANES), SUBLANES)
            src = pl.ds(tb * SUBLANES * SUBLANES + s_idx, SUBLANES, stride=SUBLANES)
            y_ref[dst, 0:LANES] = yi_ref[0, src, :]
            y_ref[dst, LANES:2 * LANES] = yi_ref[1, src, :]
        return carry

    lax.fori_loop(0, seg_len // SUBLANES, deinterleave, 0)


def _ssm(ua, ub, prm, l):
    def weights(h0_lead):
        return [_fixed_spec((2, SUBLANES, SSM_LANES), *h0_lead),
                _fixed_spec((2, D_SSM, SSM_LANES), l), _fixed_spec((2, SSM_LANES, D_SSM), l),
                _fixed_spec((2, 4, SSM_LANES), l), _fixed_spec((1, D_SSM), l),
                _fixed_spec((D_SSM, D_SSM), l), _fixed_spec((1, D_SSM), l)]

    def scratch(rows, ssm_w):
        return [pltpu.VMEM((rows, D_SSM), F32), pltpu.VMEM((2, rows, ssm_w), F32),
                pltpu.VMEM((2, rows, LANES), F32)]

    tail = (prm['ssm_b'], prm['ssm_c'], prm['ssm_lam'], prm['ssm_d'], prm['ssm_w_glu'], prm['ssm_b_glu'])
    rows_ctx = SUBLANES * SEQ
    n_ctx_tiles = N_CTX // rows_ctx
    h0_ctx = jnp.zeros((2, SUBLANES, SSM_LANES), F32)
    y_ctx, fin_ctx = pl.pallas_call(
        functools.partial(_ssm_kernel, seg_len=SEQ, n_per=1, ssm_w=SSM_W_CTX),
        grid=(n_ctx_tiles,),
        in_specs=[pl.BlockSpec((rows_ctx, LANES), lambda i: (i, 0)),
                  pl.BlockSpec((rows_ctx, LANES), lambda i: (i, 0))] + weights(()),
        out_specs=[pl.BlockSpec((rows_ctx, D_SSM), lambda i: (i, 0)),
                   pl.BlockSpec((None, 2, SUBLANES, SSM_LANES), lambda i: (i, 0, 0, 0))],
        out_shape=[jax.ShapeDtypeStruct((N_CTX, D_SSM), F32),
                   jax.ShapeDtypeStruct((n_ctx_tiles, 2, SUBLANES, SSM_LANES), F32)],
        scratch_shapes=scratch(rows_ctx, SSM_W_CTX),
        compiler_params=_params(1),
        name="ssm_ctx",
    )(ua, ub, h0_ctx, *tail)
    off = N_CTX // N_LAT
    y_lat, _ = pl.pallas_call(
        functools.partial(_ssm_kernel, seg_len=SSM_LAT_SEG, n_per=SSM_LAT_SEGS, ssm_w=SSM_W_LAT),
        grid=(1,),
        in_specs=[pl.BlockSpec((N_LAT, LANES), lambda i: (off, 0)),
                  pl.BlockSpec((N_LAT, LANES), lambda i: (off, 0))] + weights((l,)),
        out_specs=[pl.BlockSpec((N_LAT, D_SSM), lambda i: (0, 0)),
                   pl.BlockSpec((2, SUBLANES, SSM_LANES), lambda i: (0, 0, 0))],
        out_shape=[jax.ShapeDtypeStruct((N_LAT, D_SSM), F32),
                   jax.ShapeDtypeStruct((2, SUBLANES, SSM_LANES), F32)],
        scratch_shapes=scratch(N_LAT, SSM_W_LAT),
        compiler_params=_params(1),
        name="ssm_lat",
    )(ua, ub, prm['ssm_h0'], *tail)
    return (y_ctx, y_lat), fin_ctx


def _rope_tables():
    rows = DEC_SEQ // GRID_W
    row_idx = jnp.repeat(jnp.arange(rows), GRID_W).astype(F32)
    col_idx = jnp.tile(jnp.arange(GRID_W), rows).astype(F32)
    n_freq = HEAD_DIM // 4
    inv = ROPE_THETA ** (-jnp.arange(n_freq, dtype=F32) / n_freq)
    ang = jnp.concatenate([row_idx[:, None] * inv, col_idx[:, None] * inv], axis=-1)
    cos = jnp.cos(ang)
    sin = jnp.sin(ang)
    cos_h = jnp.concatenate([cos, cos], axis=-1)
    sin_h = jnp.concatenate([-sin, sin], axis=-1)
    cos_t = jnp.tile(cos_h, (1, N_HEADS))
    sin_t = jnp.tile(sin_h, (1, N_HEADS))
    cos_t = jnp.concatenate([cos_t, jnp.ones((TM, D_ATTN), F32)], axis=0)
    sin_t = jnp.concatenate([sin_t, jnp.zeros((TM, D_ATTN), F32)], axis=0)
    return cos_t, sin_t


def _state_lanes(a):
    return a.reshape(a.shape[:-2] + (N_SSM_GROUPS // 2, 2 * SSM_STATE))


def _ssm_lane_rows(a):
    a = _state_lanes(a)
    return jnp.stack([a, a], axis=-2).reshape(a.shape[:-2] + (SSM_LANES,))


def _ssm_pack_state(re, im):
    return jnp.stack([_state_lanes(re), _state_lanes(im)], axis=-2).reshape(re.shape[:-2] + (SSM_LANES,))


def _ssm_unpack_state(s):
    s = s.reshape(s.shape[:-1] + (N_SSM_GROUPS // 2, 2, 2 * SSM_STATE))
    shp = s.shape[:-3] + (N_SSM_GROUPS, SSM_STATE)
    return s[..., 0, :].reshape(shp), s[..., 1, :].reshape(shp)


def _ssm_group_mask():
    ch_g = jnp.arange(D_SSM) // SSM_GROUP
    lane = jnp.arange(SSM_LANES)
    lane_g = 2 * (lane // (2 * LANES)) + (lane % LANES) // SSM_STATE
    return ch_g[:, None] == lane_g[None, :]


def _place_b(b):
    bt = jnp.swapaxes(b, -1, -2).reshape(b.shape[:-3] + (D_SSM, SSM_STATE))
    tiled = jnp.tile(bt, (1,) * (bt.ndim - 1) + (SSM_LANES // SSM_STATE,))
    return jnp.where(_ssm_group_mask(), tiled, 0.0)


def _place_c(c_re, c_im):
    def rows(c):
        ct = jnp.moveaxis(c, -1, -3).reshape(c.shape[:-3] + (SSM_STATE, D_SSM))
        return jnp.tile(ct, (1,) * (ct.ndim - 2) + (SSM_LANES // SSM_STATE, 1))
    is_re = (jnp.arange(SSM_LANES) % (2 * LANES) < LANES)[:, None]
    return jnp.where(_ssm_group_mask().T, jnp.where(is_re, rows(c_re), -rows(c_im)), 0.0)


def _block_diag_pool(pool_w):
    tiled = jnp.tile(pool_w.reshape(DEPTH, D_POOL, POOL_GROUP), (1, 1, len(POOL_WINDOWS)))
    grp = jnp.arange(D_POOL) // POOL_GROUP
    return jnp.where(grp[:, None] == grp[None, :], tiled, 0.0)


def kernel(x_prompt, x_sample, cache_k, cache_v, state_ssm_re, state_ssm_im, c, c_ctx, w_mod, b_mod, ln_g, ln_b, ffn_w1, ffn_w2, w_in, w_out, pool_w, pool_scale, ssm_lam_re, ssm_lam_im, ssm_log_step, ssm_b_re, ssm_b_im, ssm_c_re, ssm_c_im, ssm_d, ssm_w_glu, ssm_b_glu, q_norm, k_norm, chunk_ln_g, chunk_ln_b, chunk_w_s, chunk_b_s):
    cond = jnp.concatenate([c_ctx[None, :], c, jnp.zeros((N_COND - 1 - DEC_BATCH, D_MODEL), F32)], axis=0)
    mod = _modulation(cond, w_mod, b_mod)
    cos_t, sin_t = _rope_tables()

    log_step = jnp.broadcast_to(ssm_log_step[..., None], ssm_lam_re.shape)
    lam_rows = jnp.stack([_ssm_lane_rows(ssm_lam_re), _ssm_lane_rows(ssm_lam_im),
                          _ssm_lane_rows(log_step)], axis=-2)
    bmat, lamb = _ssm_prep(lam_rows.reshape(DEPTH * 2, 3, SSM_LANES),
                           _place_b(ssm_b_re).reshape(DEPTH * 2, D_SSM, SSM_LANES),
                           _place_b(ssm_b_im).reshape(DEPTH * 2, D_SSM, SSM_LANES))
    h0 = _ssm_pack_state(state_ssm_re, state_ssm_im)
    h0_lat = jnp.zeros((DEPTH, 2, DEC_BATCH, SSM_LAT_SEGS, SSM_LANES), F32)
    h0_lat = h0_lat.at[:, 0, :, 0].set(jnp.swapaxes(h0[:, :, 0], 0, 1))
    h0_lat = h0_lat.at[:, 1, :, SSM_LAT_SEGS - 1].set(jnp.swapaxes(h0[:, :, 1], 0, 1))

    vec = lambda a: a[..., None, :]
    prm = {
        'ln_g': vec(ln_g), 'ln_b': vec(ln_b),
        'ffn_w1': ffn_w1.astype(BF16), 'ffn_w2': ffn_w2.astype(BF16),
        'w_in': w_in.astype(BF16), 'w_out': w_out.astype(BF16),
        'q_norm': vec(jnp.tile(q_norm, (1, N_HEADS))), 'k_norm': vec(jnp.tile(k_norm, (1, N_KV_HEADS))),
        'chunk_ln_g': vec(chunk_ln_g), 'chunk_ln_b': vec(chunk_ln_b),
        'chunk_w_s': chunk_w_s.astype(BF16),
        'chunk_b_s': jnp.repeat(jnp.swapaxes(chunk_b_s, 1, 2), D_CHUNK // N_CHUNK_HEADS, axis=2),
        'pool_w': _block_diag_pool(pool_w).astype(BF16), 'pool_scale': vec(pool_scale),
        'ssm_b': bmat.reshape(DEPTH, 2, D_SSM, SSM_LANES),
        'ssm_lam': lamb.reshape(DEPTH, 2, 4, SSM_LANES),
        'ssm_c': _place_c(ssm_c_re, ssm_c_im).astype(BF16),
        'ssm_h0': h0_lat.reshape(DEPTH, 2, SUBLANES, SSM_LANES),
        'ssm_d': vec(ssm_d), 'ssm_w_glu': ssm_w_glu.astype(BF16), 'ssm_b_glu': vec(ssm_b_glu),
    }
    cache_k = cache_k.reshape(DEC_BATCH, DEPTH, PAST_LEN, D_KV)
    cache_v = cache_v.reshape(DEC_BATCH, DEPTH, PAST_LEN, D_KV)

    x = (x_prompt.reshape(N_CTX, D_MODEL), x_sample.reshape(N_LAT, D_MODEL))
    ks, vs, s_re, s_im = [], [], [], []
    for l in range(DEPTH):
        x = _ffn_sublayer(x, mod, prm, l, 0)
        u_a, u_b, y_chunk, k_ctx, v_ctx, yp_ctx, ya_ctx, k_lat, v_lat, p_lat, q_lat = _inproj(
            x, mod, prm, l, cos_t, sin_t)
        yp_lat = _pool_lat(p_lat, prm, l)
        y_ssm, fin = _ssm(u_a, u_b, prm, l)
        ya_lat = _attn_lat(q_lat, k_lat, v_lat, cache_k, cache_v, l)
        x = _ffn_sublayer(x, mod, prm, l, 1, split_out=(l == DEPTH - 1),
                          mixer=((yp_ctx, yp_lat), y_ssm, (ya_ctx, ya_lat), y_chunk))
        ks.append(k_ctx.reshape(BATCH, SEQ, N_KV_HEADS, HEAD_DIM))
        vs.append(v_ctx.reshape(BATCH, SEQ, N_KV_HEADS, HEAD_DIM))
        f_re, f_im = _ssm_unpack_state(jnp.transpose(fin, (0, 2, 1, 3)).reshape(BATCH, 2, SSM_LANES))
        s_re.append(f_re)
        s_im.append(f_im)
    y_p = x[0].reshape(BATCH, SEQ, D_MODEL)
    y_s = x[1].reshape(DEC_BATCH, DEC_SEQ, D_MODEL)
    return (y_p, y_s, jnp.stack(ks, axis=1), jnp.stack(vs, axis=1),
            jnp.stack(s_re, axis=1), jnp.stack(s_im, axis=1))
```

```python
import functools
import math

import jax
import jax.numpy as jnp
from jax import lax
from jax.experimental import pallas as pl
from jax.experimental.pallas import tpu as pltpu

F32 = jnp.float32
BF16 = jnp.bfloat16

D_MODEL = 1024
BATCH = 32
SEQ = 256
DEPTH = 2
DEC_BATCH = 2
DEC_SEQ = 2048
PAST_LEN = 512
GRID_W = 64
D_POOL = 256
D_SSM = 256
D_ATTN = 256
D_CHUNK = 256
D_HEADS = 256
POOL_WINDOWS = (2, 4, 8, 16)
POOL_GROUP = 64
SSM_GROUP = 16
N_SSM_GROUPS = 16
SSM_STATE = 64
HEAD_DIM = 64
N_HEADS = 4
N_KV_HEADS = 2
D_KV = 128
CHUNK = 128
N_CHUNK_HEADS = 4
D_FF = 2816
N_MOD = 9
D_IN = 1536
ALPHA = (2 * DEPTH) ** 0.25
LN_EPS = 1e-5
RMS_EPS = 1e-6
ROPE_THETA = 10000.0

LANES = 128
SUBLANES = 8
TM = 512
ATT_TQ = 512
N_CTX = BATCH * SEQ
N_LAT = DEC_BATCH * DEC_SEQ
N_TOK = N_CTX + N_LAT
CTX_TILES = N_CTX // TM
LAT_TILES_PER_SEQ = DEC_SEQ // TM
N_TILES = N_TOK // TM
N_COND = 8
POOL_PAD = 8
SSM_LANES = 2 * N_SSM_GROUPS * SSM_STATE
SSM_W_CTX = 1024
SSM_W_LAT = 512
SSM_LAT_SEGS = SUBLANES // DEC_BATCH
SSM_LAT_SEG = DEC_SEQ // SSM_LAT_SEGS
SSM_ROWS = 1024
VMEM_LIMIT = 56 * 1024 * 1024


def _params(n_grid):
    return pltpu.CompilerParams(dimension_semantics=("arbitrary",) * n_grid,
                                vmem_limit_bytes=VMEM_LIMIT)


def _fixed_spec(tail, *lead):
    idx = tuple(lead) + (0,) * len(tail)
    return pl.BlockSpec((None,) * len(lead) + tuple(tail), lambda *_: idx, pipeline_mode=pl.Buffered(1))


def _cond_row(i):
    return jnp.where(i < CTX_TILES, 0, 1 + (i - CTX_TILES) // LAT_TILES_PER_SEQ)


def _mod_spec(l):
    return pl.BlockSpec((None, None, N_MOD, D_MODEL), lambda i: (l, _cond_row(i), 0, 0))


def _tok_tile(i):
    return (i, 0)


def _ctx_tile(i):
    return (jnp.minimum(i, CTX_TILES - 1), 0)


def _lat_tile(i):
    return (jnp.maximum(i - CTX_TILES, 0), 0)


def _layer_norm(y, g, b):
    mu = jnp.mean(y, axis=-1, keepdims=True)
    d = y - mu
    var = jnp.mean(d * d, axis=-1, keepdims=True)
    return d * lax.rsqrt(var + LN_EPS) * g + b


def _gelu(x):
    return 0.5 * x * (1.0 + jnp.tanh(math.sqrt(2.0 / math.pi) * (x + 0.044715 * (x * x * x))))


def _silu(x):
    return x * jax.nn.sigmoid(x)


def _split_bf16(a):
    hi = a.astype(BF16)
    return hi, (a - hi.astype(F32)).astype(BF16)


def _mod_kernel(cond_ref, w_ref, b_ref, o_ref):
    a_hi, a_lo = _split_bf16(_silu(cond_ref[...]))
    w_hi, w_lo = _split_bf16(w_ref[...])
    acc = jnp.dot(a_hi, w_lo, preferred_element_type=F32)
    acc = acc + jnp.dot(a_lo, w_hi, preferred_element_type=F32)
    acc = acc + jnp.dot(a_hi, w_hi, preferred_element_type=F32)
    o_ref[...] = acc + b_ref[...]


def _modulation(cond, w_mod, b_mod):
    tn = D_MODEL
    out = pl.pallas_call(
        _mod_kernel,
        grid=(DEPTH, N_MOD),
        in_specs=[
            pl.BlockSpec((N_COND, D_MODEL), lambda l, j: (0, 0)),
            pl.BlockSpec((None, D_MODEL, tn), lambda l, j: (l, 0, j)),
            pl.BlockSpec((None, 1, tn), lambda l, j: (l, 0, j)),
        ],
        out_specs=pl.BlockSpec((None, N_COND, tn), lambda l, j: (l, 0, j)),
        out_shape=jax.ShapeDtypeStruct((DEPTH, N_COND, N_MOD * D_MODEL), F32),
        compiler_params=_params(2),
        name="modulation",
    )(cond, w_mod, b_mod.reshape(DEPTH, 1, N_MOD * D_MODEL))
    return out.reshape(DEPTH, N_COND, N_MOD, D_MODEL)


def _ffn_kernel(*refs, mod_off, split_in, split_out, mixer_out):
    n_in = 2 if split_in else 1
    n_mix = 10 if mixer_out else 0
    mix_refs = refs[n_in:n_in + n_mix]
    mod_ref, w1_ref, w2_ref, g_ref, b_ref = refs[n_in + n_mix:n_in + n_mix + 5]
    out_refs = refs[n_in + n_mix + 5:]
    is_ctx = pl.program_id(0) < CTX_TILES
    if split_in:
        x = jnp.where(is_ctx, refs[0][...], refs[1][...])
    else:
        x = refs[0][...]
    if mixer_out:
        yc_ref, wo_ref, go_ref, bo_ref = mix_refs[6:]
        parts = [jnp.where(is_ctx, mix_refs[2 * j][...], mix_refs[2 * j + 1][...]) for j in range(3)]
        parts.append(yc_ref[...])
        y = None
        for j, part in enumerate(parts):
            r = jnp.dot(part.astype(BF16), wo_ref[D_HEADS * j:D_HEADS * (j + 1), :],
                        preferred_element_type=F32)
            y = r if y is None else y + r
        x = _layer_norm(ALPHA * x + mod_ref[5:6, :] * y, go_ref[...], bo_ref[...])
    sh = mod_ref[mod_off:mod_off + 1, :]
    sc = mod_ref[mod_off + 1:mod_off + 2, :]
    gate = mod_ref[mod_off + 2:mod_off + 3, :]
    h = (x * (1.0 + sc) + sh).astype(BF16)
    gu = jnp.dot(h, w1_ref[...], preferred_element_type=F32)
    a = (_silu(gu[:, :D_FF]) * gu[:, D_FF:]).astype(BF16)
    f = jnp.dot(a, w2_ref[...], preferred_element_type=F32)
    y = _layer_norm(ALPHA * x + (0.5 * gate) * f, g_ref[...], b_ref[...])
    if split_out:
        @pl.when(is_ctx)
        def _():
            out_refs[0][...] = y

        @pl.when(jnp.logical_not(is_ctx))
        def _():
            out_refs[1][...] = y
    else:
        out_refs[0][...] = y


def _ffn_sublayer(xs, mod, prm, l, sub, split_out=False, mixer=None):
    split_in = isinstance(xs, tuple)
    if split_in:
        x_specs = [pl.BlockSpec((TM, D_MODEL), _ctx_tile), pl.BlockSpec((TM, D_MODEL), _lat_tile)]
    else:
        xs = (xs,)
        x_specs = [pl.BlockSpec((TM, D_MODEL), _tok_tile)]
    mix_args, mix_specs = (), []
    if mixer is not None:
        y_pool, y_ssm, y_attn, y_chunk = mixer
        mix_args = (*y_pool, *y_ssm, *y_attn, y_chunk, prm['w_out'], prm['ln_g'], prm['ln_b'])
        mix_specs = [pl.BlockSpec((TM, D_HEADS), _ctx_tile), pl.BlockSpec((TM, D_HEADS), _lat_tile)] * 3 + [
            pl.BlockSpec((TM, D_HEADS), _tok_tile),
            _fixed_spec((D_MODEL, D_MODEL), l), _fixed_spec((1, D_MODEL), l, 1), _fixed_spec((1, D_MODEL), l, 1)]
    if split_out:
        out_specs = [pl.BlockSpec((TM, D_MODEL), _ctx_tile), pl.BlockSpec((TM, D_MODEL), _lat_tile)]
        out_shape = [jax.ShapeDtypeStruct((N_CTX, D_MODEL), F32), jax.ShapeDtypeStruct((N_LAT, D_MODEL), F32)]
    else:
        out_specs = pl.BlockSpec((TM, D_MODEL), _tok_tile)
        out_shape = jax.ShapeDtypeStruct((N_TOK, D_MODEL), F32)
    ln_idx = 2 * sub
    return pl.pallas_call(
        functools.partial(_ffn_kernel, mod_off=6 * sub, split_in=split_in, split_out=split_out,
                          mixer_out=mixer is not None),
        grid=(N_TILES,),
        in_specs=x_specs + mix_specs + [
            _mod_spec(l),
            _fixed_spec((D_MODEL, 2 * D_FF), l, sub),
            _fixed_spec((D_FF, D_MODEL), l, sub),
            _fixed_spec((1, D_MODEL), l, ln_idx),
            _fixed_spec((1, D_MODEL), l, ln_idx),
        ],
        out_specs=out_specs,
        out_shape=out_shape,
        compiler_params=_params(1),
        name="ffn_sublayer",
    )(*xs, *mix_args, mod, prm['ffn_w1'], prm['ffn_w2'], prm['ln_g'], prm['ln_b'])


def _pool_math(x, w_bd, scale, seq_len):
    n = seq_len + 2 * POOL_PAD
    zpad = jnp.zeros((POOL_PAD, D_POOL), F32)
    e = jnp.concatenate([zpad, x, zpad], axis=0)

    def prev(a, d):
        return pltpu.roll(a, d, axis=0)

    def nxt(a, d):
        return pltpu.roll(a, n - d, axis=0)

    s2 = e + prev(e, 1)
    s4 = prev(s2, 1) + nxt(s2, 1)
    s8 = prev(s4, 2) + nxt(s4, 2)
    s16 = prev(s8, 4) + nxt(s8, 4)
    lane = lax.broadcasted_iota(jnp.int32, (seq_len, D_POOL), 1)
    grp = lane // POOL_GROUP
    sl = slice(POOL_PAD, POOL_PAD + seq_len)
    s = jnp.where(grp == 0, s2[sl], jnp.where(grp == 1, s4[sl], jnp.where(grp == 2, s8[sl], s16[sl])))
    half = jnp.where(grp == 0, 1, jnp.where(grp == 1, 2, jnp.where(grp == 2, 4, 8)))
    t = lax.broadcasted_iota(jnp.int32, (seq_len, D_POOL), 0)
    cnt = jnp.minimum(t + half, seq_len) - jnp.maximum(t - half, 0)
    d = (s / cnt.astype(F32) - x).astype(BF16)
    return jnp.dot(d, w_bd, preferred_element_type=F32) * scale


def _attn_math(problems, mxu_sums):
    nt = (((1,), (1,)), ((), ()))
    tq = problems[0][0].shape[0]
    lo = lax.broadcasted_iota(jnp.int32, (tq, LANES), 1) < HEAD_DIM
    scores = []
    for q, k, _ in problems:
        q = q * (HEAD_DIM ** -0.5 * math.log2(math.e))
        q_lo = q[:, :LANES]
        q_hi = q[:, LANES:]
        qs = [jnp.where(lo, q_lo, 0.0), jnp.where(lo, pltpu.roll(q_lo, HEAD_DIM, axis=1), 0.0),
              jnp.where(lo, 0.0, pltpu.roll(q_hi, HEAD_DIM, axis=1)), jnp.where(lo, 0.0, q_hi)]
        for g in range(N_KV_HEADS):
            qq = jnp.concatenate([qs[2 * g], qs[2 * g + 1]], axis=0).astype(BF16)
            scores.append(lax.dot_general(qq, k, nt, preferred_element_type=F32))
    results = []
    for i, (_, _, v) in enumerate(problems):
        if mxu_sums:
            lo_k = lax.broadcasted_iota(jnp.int32, v.shape, 1) < HEAD_DIM
            one = jnp.ones_like(v)
            v_g = [jnp.where(lo_k, v, one), jnp.where(lo_k, one, v)]
        outs = []
        for g in range(N_KV_HEADS):
            s = scores[N_KV_HEADS * i + g]
            p = jnp.exp2(s - jnp.max(s, axis=-1, keepdims=True))
            if mxu_sums:
                o = jnp.dot(p.astype(BF16), v_g[g], preferred_element_type=F32)
            else:
                l = jnp.sum(p, axis=-1, keepdims=True)
                o = jnp.dot(p.astype(BF16), v, preferred_element_type=F32) / l
            outs += [o[:tq], o[tq:]]
        if mxu_sums:
            r = [pltpu.roll(o, HEAD_DIM, axis=1) for o in outs]
            out_lo = jnp.where(lo, outs[0], r[1]) / jnp.where(lo, r[0], outs[1])
            out_hi = jnp.where(lo, r[2], outs[3]) / jnp.where(lo, outs[2], r[3])
        else:
            out_lo = jnp.where(lo, outs[0], pltpu.roll(outs[1], HEAD_DIM, axis=1))
            out_hi = jnp.where(lo, pltpu.roll(outs[2], HEAD_DIM, axis=1), outs[3])
        results.append(jnp.concatenate([out_lo, out_hi], axis=1))
    return results


def _seg_rms(x, gain, n_lanes):
    parts = []
    for j in range(n_lanes // LANES):
        xs = x[:, j * LANES:(j + 1) * LANES]
        sq = xs * xs
        lo = lax.broadcasted_iota(jnp.int32, xs.shape, 1) < HEAD_DIM
        s_all = jnp.sum(sq, axis=-1, keepdims=True)
        s_lo = jnp.sum(jnp.where(lo, sq, 0.0), axis=-1, keepdims=True)
        ms = jnp.where(lo, s_lo, s_all - s_lo) * (1.0 / HEAD_DIM)
        parts.append(xs * lax.rsqrt(ms + RMS_EPS))
    y = parts[0] if len(parts) == 1 else jnp.concatenate(parts, axis=1)
    return y * gain


def _rope(x, cos_t, sin_t, n_lanes):
    parts = []
    for j in range(n_lanes // LANES):
        xs = x[:, j * LANES:(j + 1) * LANES]
        first = (lax.broadcasted_iota(jnp.int32, xs.shape, 1) % HEAD_DIM) < (HEAD_DIM // 2)
        partner = jnp.where(first, pltpu.roll(xs, LANES - HEAD_DIM // 2, axis=1),
                            pltpu.roll(xs, HEAD_DIM // 2, axis=1))
        cs = cos_t[:, j * LANES:(j + 1) * LANES]
        sn = sin_t[:, j * LANES:(j + 1) * LANES]
        parts.append(xs * cs + partner * sn)
    return parts[0] if len(parts) == 1 else jnp.concatenate(parts, axis=1)


def _inproj_kernel(x_ref, mod_ref, w_ref, cos_ref, sin_ref, qn_ref, kn_ref,
                   cg_ref, cb_ref, ws_ref, bs_ref, pw_ref, ps_ref,
                   ua_ref, ub_ref, chunk_ref, kc_ref, vc_ref, yp_ref, ya_ref,
                   kl_ref, vl_ref, pl_ref, ql_ref):
    is_ctx = pl.program_id(0) < CTX_TILES
    x = x_ref[...]
    sh = mod_ref[3:4, :]
    sc = mod_ref[4:5, :]
    h = (x * (1.0 + sc) + sh).astype(BF16)
    proj = jnp.dot(h, w_ref[...], preferred_element_type=F32)
    p = proj[:, 0:256]
    ua_ref[...] = proj[:, 256:384]
    ub_ref[...] = proj[:, 384:512]
    cos_t = cos_ref[...]
    sin_t = sin_ref[...]
    q = _rope(_seg_rms(proj[:, 512:768], qn_ref[...], D_ATTN), cos_t, sin_t, D_ATTN)
    k = _rope(_seg_rms(proj[:, 768:896], kn_ref[...], D_KV), cos_t, sin_t, D_KV)
    v = proj[:, 896:1024]
    zu = _gelu(proj[:, 1024:1280])
    zv = _layer_norm(_gelu(proj[:, 1280:1536]), cg_ref[...], cb_ref[...])
    head = lax.broadcasted_iota(jnp.int32, (CHUNK, D_CHUNK), 1) // (D_CHUNK // N_CHUNK_HEADS)
    for c in range(TM // CHUNK):
        vb = zv[c * CHUNK:(c + 1) * CHUNK, :].astype(BF16)
        mixed = bs_ref[...]
        for hd in range(N_CHUNK_HEADS):
            r = jnp.dot(ws_ref[hd], vb, preferred_element_type=F32)
            mixed = mixed + jnp.where(head == hd, r, 0.0)
        chunk_ref[c * CHUNK:(c + 1) * CHUNK, :] = zu[c * CHUNK:(c + 1) * CHUNK, :] * mixed

    @pl.when(is_ctx)
    def _():
        kc_ref[...] = k
        vc_ref[...] = v
        for s in range(TM // SEQ):
            r = slice(s * SEQ, (s + 1) * SEQ)
            yp_ref[r, :] = _pool_math(p[r], pw_ref[...], ps_ref[...], SEQ)
            ya_ref[r, :] = _attn_math([(q[r], k[r].astype(BF16), v[r].astype(BF16))], False)[0]

    @pl.when(jnp.logical_not(is_ctx))
    def _():
        kl_ref[...] = k.astype(BF16)
        vl_ref[...] = v.astype(BF16)
        pl_ref[...] = p
        ql_ref[...] = q


def _rope_block(i):
    return jnp.where(i < CTX_TILES, LAT_TILES_PER_SEQ, (i - CTX_TILES) % LAT_TILES_PER_SEQ)


def _inproj(x, mod, prm, l, cos_t, sin_t):
    def out(rows, width, index_map, dtype=F32):
        return jax.ShapeDtypeStruct((rows, width), dtype), pl.BlockSpec((TM, width), index_map)

    outs = [out(N_TOK, LANES, _tok_tile), out(N_TOK, LANES, _tok_tile), out(N_TOK, D_CHUNK, _tok_tile),
            out(N_CTX, D_KV, _ctx_tile), out(N_CTX, D_KV, _ctx_tile),
            out(N_CTX, D_POOL, _ctx_tile), out(N_CTX, D_ATTN, _ctx_tile),
            out(N_LAT, D_KV, _lat_tile, BF16), out(N_LAT, D_KV, _lat_tile, BF16),
            out(N_LAT, D_POOL, _lat_tile), out(N_LAT, D_ATTN, _lat_tile)]
    return pl.pallas_call(
        _inproj_kernel,
        grid=(N_TILES,),
        in_specs=[
            pl.BlockSpec((TM, D_MODEL), _tok_tile),
            _mod_spec(l),
            _fixed_spec((D_MODEL, D_IN), l),
            pl.BlockSpec((TM, D_ATTN), lambda i: (_rope_block(i), 0)),
            pl.BlockSpec((TM, D_ATTN), lambda i: (_rope_block(i), 0)),
            _fixed_spec((1, D_ATTN), l),
            _fixed_spec((1, D_KV), l),
            _fixed_spec((1, D_CHUNK), l),
            _fixed_spec((1, D_CHUNK), l),
            _fixed_spec((N_CHUNK_HEADS, CHUNK, CHUNK), l),
            _fixed_spec((CHUNK, D_CHUNK), l),
            _fixed_spec((D_POOL, D_POOL), l),
            _fixed_spec((1, D_POOL), l),
        ],
        out_specs=[o[1] for o in outs],
        out_shape=[o[0] for o in outs],
        compiler_params=_params(1),
        name="mixer_inproj",
    )(x, mod, prm['w_in'], cos_t, sin_t, prm['q_norm'], prm['k_norm'], prm['chunk_ln_g'],
      prm['chunk_ln_b'], prm['chunk_w_s'], prm['chunk_b_s'], prm['pool_w'], prm['pool_scale'])


def _pool_kernel(p_ref, w_ref, s_ref, o_ref):
    o_ref[...] = _pool_math(p_ref[...], w_ref[...], s_ref[...], DEC_SEQ)


def _pool_lat(p_lat, prm, l):
    return pl.pallas_call(
        _pool_kernel,
        grid=(DEC_BATCH,),
        in_specs=[pl.BlockSpec((DEC_SEQ, D_POOL), lambda b: (b, 0)),
                  _fixed_spec((D_POOL, D_POOL), l), _fixed_spec((1, D_POOL), l)],
        out_specs=pl.BlockSpec((DEC_SEQ, D_POOL), lambda b: (b, 0)),
        out_shape=jax.ShapeDtypeStruct((N_LAT, D_POOL), F32),
        compiler_params=_params(1),
        name="pool_lat",
    )(p_lat, prm['pool_w'], prm['pool_scale'])


def _attn_kernel(q_ref, kc_ref, kl_ref, vc_ref, vl_ref, o_ref):
    k = jnp.concatenate([kc_ref[...], kl_ref[...]], axis=0)
    v = jnp.concatenate([vc_ref[...], vl_ref[...]], axis=0)
    o_ref[...] = _attn_math([(q_ref[...], k, v)], True)[0]


def _attn_lat(q_lat, k_lat, v_lat, cache_k, cache_v, l):
    q_per_seq = DEC_SEQ // ATT_TQ
    cache_spec = pl.BlockSpec((None, None, PAST_LEN, D_KV), lambda b, j: (b, l, 0, 0))
    seq_spec = pl.BlockSpec((DEC_SEQ, D_KV), lambda b, j: (b, 0))
    return pl.pallas_call(
        _attn_kernel,
        grid=(DEC_BATCH, q_per_seq),
        in_specs=[pl.BlockSpec((ATT_TQ, D_ATTN), lambda b, j: (b * q_per_seq + j, 0)),
                  cache_spec, seq_spec, cache_spec, seq_spec],
        out_specs=pl.BlockSpec((ATT_TQ, D_ATTN), lambda b, j: (b * q_per_seq + j, 0)),
        out_shape=jax.ShapeDtypeStruct((N_LAT, D_ATTN), F32),
        compiler_params=_params(2),
        name="attn_lat",
    )(q_lat, cache_k, k_lat, cache_v, v_lat)


def _ssm_prep_kernel(lam_ref, br_ref, bi_ref, bmat_ref, lamb_ref):
    lre = lam_ref[0:1, :]
    lim = lam_ref[1:2, :]
    dt = jnp.exp(lam_ref[2:3, :])
    mag = jnp.exp(lre * dt)
    ar = mag * jnp.cos(lim * dt)
    ai = mag * jnp.sin(lim * dt)
    lamb_ref[0:1, :] = ar
    lamb_ref[1:2, :] = ai
    mag_s = jnp.exp(lre * dt * SSM_LAT_SEG)
    lamb_ref[2:3, :] = mag_s * jnp.cos(lim * dt * SSM_LAT_SEG)
    lamb_ref[3:4, :] = mag_s * jnp.sin(lim * dt * SSM_LAT_SEG)
    den = lre * lre + lim * lim
    cr = ((ar - 1.0) * lre + ai * lim) / den
    ci = (ai * lre - (ar - 1.0) * lim) / den
    br = br_ref[...]
    bi = bi_ref[...]
    lane = lax.broadcasted_iota(jnp.int32, br.shape, 1)
    is_re = (lane % (2 * LANES)) < LANES
    bmat_ref[...] = jnp.where(is_re, cr * br - ci * bi, cr * bi + ci * br).astype(BF16)


def _ssm_prep(lam_rows, b_re_placed, b_im_placed):
    n = DEPTH * 2
    return pl.pallas_call(
        _ssm_prep_kernel,
        grid=(n,),
        in_specs=[pl.BlockSpec((None, 3, SSM_LANES), lambda i: (i, 0, 0)),
                  pl.BlockSpec((None, D_SSM, SSM_LANES), lambda i: (i, 0, 0)),
                  pl.BlockSpec((None, D_SSM, SSM_LANES), lambda i: (i, 0, 0))],
        out_specs=[pl.BlockSpec((None, D_SSM, SSM_LANES), lambda i: (i, 0, 0)),
                   pl.BlockSpec((None, 4, SSM_LANES), lambda i: (i, 0, 0))],
        out_shape=[jax.ShapeDtypeStruct((n, D_SSM, SSM_LANES), BF16),
                   jax.ShapeDtypeStruct((n, 4, SSM_LANES), F32)],
        compiler_params=_params(1),
        name="ssm_prep",
    )(lam_rows, b_re_placed, b_im_placed)


def _ssm_kernel(ua_ref, ub_ref, h0_ref, bmat_ref, cmat_ref, lamb_ref, d_ref, wg_ref, bg_ref,
                y_ref, fin_ref, ui_ref, buf_ref, yi_ref, *, seg_len, n_per, ssm_w):
    n_rows = SUBLANES * seg_len
    n_cb = ssm_w // (2 * LANES)

    def row_chunks(fn):
        def body(c, carry):
            fn(pl.ds(pl.multiple_of(c * SSM_ROWS, SSM_ROWS), SSM_ROWS))
            return carry
        lax.fori_loop(0, n_rows // SSM_ROWS, body, 0)

    def step_rows(t):
        return pl.ds(pl.multiple_of(t * SUBLANES, SUBLANES), SUBLANES)

    def interleave(t, carry):
        ui_ref[step_rows(t), 0:LANES] = ua_ref[pl.ds(t, SUBLANES, stride=seg_len), :]
        ui_ref[step_rows(t), LANES:2 * LANES] = ub_ref[pl.ds(t, SUBLANES, stride=seg_len), :]
        return carry

    lax.fori_loop(0, seg_len, interleave, 0, unroll=4)

    seg = lax.broadcasted_iota(jnp.int32, (SUBLANES, LANES), 0) % n_per

    chains = [(dirn, k) for dirn in range(2) for k in range(n_cb)]
    for part in range(SSM_LANES // ssm_w):
        lo = part * ssm_w

        def fill(rows, lo=lo):
            ub = ui_ref[rows, :].astype(BF16)
            for dirn, k in chains:
                c0 = k * 2 * LANES
                buf_ref[dirn, rows, c0:c0 + 2 * LANES] = jnp.dot(
                    ub, bmat_ref[dirn, :, lo + c0:lo + c0 + 2 * LANES], preferred_element_type=F32)

        row_chunks(fill)

        def lam_rows(r, lo=lo):
            out = []
            for dirn, k in chains:
                c0 = lo + k * 2 * LANES
                out.append(jnp.broadcast_to(lamb_ref[dirn, r:r + 1, c0:c0 + LANES], (SUBLANES, LANES)))
            return out

        a_re, a_im = lam_rows(0), lam_rows(1)
        h0 = []
        for dirn, k in chains:
            c0 = lo + k * 2 * LANES
            h0 += [h0_ref[dirn, :, c0:c0 + LANES], h0_ref[dirn, :, c0 + LANES:c0 + 2 * LANES]]

        def scan(init, store, a_re=a_re, a_im=a_im):
            def step(i, hs):
                rows = (step_rows(i), step_rows(seg_len - 1 - i))
                new = []
                for c, (dirn, k) in enumerate(chains):
                    re_sl = slice(k * 2 * LANES, k * 2 * LANES + LANES)
                    im_sl = slice(k * 2 * LANES + LANES, (k + 1) * 2 * LANES)
                    hr, hi = hs[2 * c], hs[2 * c + 1]
                    nr = a_re[c] * hr - a_im[c] * hi + buf_ref[dirn, rows[dirn], re_sl]
                    ni = a_re[c] * hi + a_im[c] * hr + buf_ref[dirn, rows[dirn], im_sl]
                    if store:
                        buf_ref[dirn, rows[dirn], re_sl] = nr
                        buf_ref[dirn, rows[dirn], im_sl] = ni
                    new += [nr, ni]
                return tuple(new)
            return lax.fori_loop(0, seg_len, step, tuple(init), unroll=2)

        if n_per == 1:
            init = h0
        else:
            local = scan([jnp.zeros((SUBLANES, LANES), F32)] * (2 * len(chains)), False)
            s_re, s_im = lam_rows(2), lam_rows(3)
            init = []
            for c, (dirn, k) in enumerate(chains):
                edge = seg == (0 if dirn == 0 else n_per - 1)
                shift = 1 if dirn == 0 else SUBLANES - 1
                cr, ci = h0[2 * c], h0[2 * c + 1]
                lr = pltpu.roll(local[2 * c], shift, axis=0)
                li = pltpu.roll(local[2 * c + 1], shift, axis=0)
                for _ in range(n_per - 1):
                    pr = pltpu.roll(cr, shift, axis=0)
                    pi = pltpu.roll(ci, shift, axis=0)
                    cr = jnp.where(edge, h0[2 * c], s_re[c] * pr - s_im[c] * pi + lr)
                    ci = jnp.where(edge, h0[2 * c + 1], s_re[c] * pi + s_im[c] * pr + li)
                init += [cr, ci]
        hs = scan(init, True)
        for c, (dirn, k) in enumerate(chains):
            c0 = lo + k * 2 * LANES
            fin_ref[dirn, :, c0:c0 + LANES] = hs[2 * c]
            fin_ref[dirn, :, c0 + LANES:c0 + 2 * LANES] = hs[2 * c + 1]

        def readout(rows, lo=lo, first=(part == 0)):
            y = jnp.dot(buf_ref[0, rows, :].astype(BF16), cmat_ref[0, lo:lo + ssm_w, :],
                        preferred_element_type=F32)
            y = y + jnp.dot(buf_ref[1, rows, :].astype(BF16), cmat_ref[1, lo:lo + ssm_w, :],
                            preferred_element_type=F32)
            if first:
                y = y + d_ref[...] * ui_ref[rows, :]
                yi_ref[0, rows, :] = y[:, :LANES]
                yi_ref[1, rows, :] = y[:, LANES:]
            else:
                yi_ref[0, rows, :] += y[:, :LANES]
                yi_ref[1, rows, :] += y[:, LANES:]

        row_chunks(readout)

    def glu(rows):
        g = _gelu(jnp.concatenate([yi_ref[0, rows, :], yi_ref[1, rows, :]], axis=1))
        z = jnp.dot(g.astype(BF16), wg_ref[...], preferred_element_type=F32) + bg_ref[...]
        o = g * jax.nn.sigmoid(z)
        yi_ref[0, rows, :] = o[:, :LANES]
        yi_ref[1, rows, :] = o[:, LANES:]

    row_chunks(glu)

    def deinterleave(tb, carry):
        for s_idx in range(SUBLANES):
            dst = pl.ds(pl.multiple_of(s_idx * seg_len + tb * SUBLANES, SUBLANES), SUBLANES)
            src = pl.ds(tb * SUBLANES * SUBLANES + s_idx, SUBLANES, stride=SUBLANES)
            y_ref[dst, 0:LANES] = yi_ref[0, src, :]
            y_ref[dst, LANES:2 * LANES] = yi_ref[1, src, :]
        return carry

    lax.fori_loop(0, seg_len // SUBLANES, deinterleave, 0)


def _ssm(ua, ub, prm, l):
    def weights(h0_lead):
        return [_fixed_spec((2, SUBLANES, SSM_LANES), *h0_lead),
                _fixed_spec((2, D_SSM, SSM_LANES), l), _fixed_spec((2, SSM_LANES, D_SSM), l),
                _fixed_spec((2, 4, SSM_LANES), l), _fixed_spec((1, D_SSM), l),
                _fixed_spec((D_SSM, D_SSM), l), _fixed_spec((1, D_SSM), l)]

    def scratch(rows, ssm_w):
        return [pltpu.VMEM((rows, D_SSM), F32), pltpu.VMEM((2, rows, ssm_w), F32),
                pltpu.VMEM((2, rows, LANES), F32)]

    tail = (prm['ssm_b'], prm['ssm_c'], prm['ssm_lam'], prm['ssm_d'], prm['ssm_w_glu'], prm['ssm_b_glu'])
    rows_ctx = SUBLANES * SEQ
    n_ctx_tiles = N_CTX // rows_ctx
    h0_ctx = jnp.zeros((2, SUBLANES, SSM_LANES), F32)
    y_ctx, fin_ctx = pl.pallas_call(
        functools.partial(_ssm_kernel, seg_len=SEQ, n_per=1, ssm_w=SSM_W_CTX),
        grid=(n_ctx_tiles,),
        in_specs=[pl.BlockSpec((rows_ctx, LANES), lambda i: (i, 0)),
                  pl.BlockSpec((rows_ctx, LANES), lambda i: (i, 0))] + weights(()),
        out_specs=[pl.BlockSpec((rows_ctx, D_SSM), lambda i: (i, 0)),
                   pl.BlockSpec((None, 2, SUBLANES, SSM_LANES), lambda i: (i, 0, 0, 0))],
        out_shape=[jax.ShapeDtypeStruct((N_CTX, D_SSM), F32),
                   jax.ShapeDtypeStruct((n_ctx_tiles, 2, SUBLANES, SSM_LANES), F32)],
        scratch_shapes=scratch(rows_ctx, SSM_W_CTX),
        compiler_params=_params(1),
        name="ssm_ctx",
    )(ua, ub, h0_ctx, *tail)
    off = N_CTX // N_LAT
    y_lat, _ = pl.pallas_call(
        functools.partial(_ssm_kernel, seg_len=SSM_LAT_SEG, n_per=SSM_LAT_SEGS, ssm_w=SSM_W_LAT),
        grid=(1,),
        in_specs=[pl.BlockSpec((N_LAT, LANES), lambda i: (off, 0)),
                  pl.BlockSpec((N_LAT, LANES), lambda i: (off, 0))] + weights((l,)),
        out_specs=[pl.BlockSpec((N_LAT, D_SSM), lambda i: (0, 0)),
                   pl.BlockSpec((2, SUBLANES, SSM_LANES), lambda i: (0, 0, 0))],
        out_shape=[jax.ShapeDtypeStruct((N_LAT, D_SSM), F32),
                   jax.ShapeDtypeStruct((2, SUBLANES, SSM_LANES), F32)],
        scratch_shapes=scratch(N_LAT, SSM_W_LAT),
        compiler_params=_params(1),
        name="ssm_lat",
    )(ua, ub, prm['ssm_h0'], *tail)
    return (y_ctx, y_lat), fin_ctx


def _rope_tables():
    rows = DEC_SEQ // GRID_W
    row_idx = jnp.repeat(jnp.arange(rows), GRID_W).astype(F32)
    col_idx = jnp.tile(jnp.arange(GRID_W), rows).astype(F32)
    n_freq = HEAD_DIM // 4
    inv = ROPE_THETA ** (-jnp.arange(n_freq, dtype=F32) / n_freq)
    ang = jnp.concatenate([row_idx[:, None] * inv, col_idx[:, None] * inv], axis=-1)
    cos = jnp.cos(ang)
    sin = jnp.sin(ang)
    cos_h = jnp.concatenate([cos, cos], axis=-1)
    sin_h = jnp.concatenate([-sin, sin], axis=-1)
    cos_t = jnp.tile(cos_h, (1, N_HEADS))
    sin_t = jnp.tile(sin_h, (1, N_HEADS))
    cos_t = jnp.concatenate([cos_t, jnp.ones((TM, D_ATTN), F32)], axis=0)
    sin_t = jnp.concatenate([sin_t, jnp.zeros((TM, D_ATTN), F32)], axis=0)
    return cos_t, sin_t


def _state_lanes(a):
    return a.reshape(a.shape[:-2] + (N_SSM_GROUPS // 2, 2 * SSM_STATE))


def _ssm_lane_rows(a):
    a = _state_lanes(a)
    return jnp.stack([a, a], axis=-2).reshape(a.shape[:-2] + (SSM_LANES,))


def _ssm_pack_state(re, im):
    return jnp.stack([_state_lanes(re), _state_lanes(im)], axis=-2).reshape(re.shape[:-2] + (SSM_LANES,))


def _ssm_unpack_state(s):
    s = s.reshape(s.shape[:-1] + (N_SSM_GROUPS // 2, 2, 2 * SSM_STATE))
    shp = s.shape[:-3] + (N_SSM_GROUPS, SSM_STATE)
    return s[..., 0, :].reshape(shp), s[..., 1, :].reshape(shp)


def _ssm_group_mask():
    ch_g = jnp.arange(D_SSM) // SSM_GROUP
    lane = jnp.arange(SSM_LANES)
    lane_g = 2 * (lane // (2 * LANES)) + (lane % LANES) // SSM_STATE
    return ch_g[:, None] == lane_g[None, :]


def _place_b(b):
    bt = jnp.swapaxes(b, -1, -2).reshape(b.shape[:-3] + (D_SSM, SSM_STATE))
    tiled = jnp.tile(bt, (1,) * (bt.ndim - 1) + (SSM_LANES // SSM_STATE,))
    return jnp.where(_ssm_group_mask(), tiled, 0.0)


def _place_c(c_re, c_im):
    def rows(c):
        ct = jnp.moveaxis(c, -1, -3).reshape(c.shape[:-3] + (SSM_STATE, D_SSM))
        return jnp.tile(ct, (1,) * (ct.ndim - 2) + (SSM_LANES // SSM_STATE, 1))
    is_re = (jnp.arange(SSM_LANES) % (2 * LANES) < LANES)[:, None]
    return jnp.where(_ssm_group_mask().T, jnp.where(is_re, rows(c_re), -rows(c_im)), 0.0)


def _block_diag_pool(pool_w):
    tiled = jnp.tile(pool_w.reshape(DEPTH, D_POOL, POOL_GROUP), (1, 1, len(POOL_WINDOWS)))
    grp = jnp.arange(D_POOL) // POOL_GROUP
    return jnp.where(grp[:, None] == grp[None, :], tiled, 0.0)


def kernel(x_prompt, x_sample, cache_k, cache_v, state_ssm_re, state_ssm_im, c, c_ctx, w_mod, b_mod, ln_g, ln_b, ffn_w1, ffn_w2, w_in, w_out, pool_w, pool_scale, ssm_lam_re, ssm_lam_im, ssm_log_step, ssm_b_re, ssm_b_im, ssm_c_re, ssm_c_im, ssm_d, ssm_w_glu, ssm_b_glu, q_norm, k_norm, chunk_ln_g, chunk_ln_b, chunk_w_s, chunk_b_s):
    cond = jnp.concatenate([c_ctx[None, :], c, jnp.zeros((N_COND - 1 - DEC_BATCH, D_MODEL), F32)], axis=0)
    mod = _modulation(cond, w_mod, b_mod)
    cos_t, sin_t = _rope_tables()

    log_step = jnp.broadcast_to(ssm_log_step[..., None], ssm_lam_re.shape)
    lam_rows = jnp.stack([_ssm_lane_rows(ssm_lam_re), _ssm_lane_rows(ssm_lam_im),
                          _ssm_lane_rows(log_step)], axis=-2)
    bmat, lamb = _ssm_prep(lam_rows.reshape(DEPTH * 2, 3, SSM_LANES),
                           _place_b(ssm_b_re).reshape(DEPTH * 2, D_SSM, SSM_LANES),
                           _place_b(ssm_b_im).reshape(DEPTH * 2, D_SSM, SSM_LANES))
    h0 = _ssm_pack_state(state_ssm_re, state_ssm_im)
    h0_lat = jnp.zeros((DEPTH, 2, DEC_BATCH, SSM_LAT_SEGS, SSM_LANES), F32)
    h0_lat = h0_lat.at[:, 0, :, 0].set(jnp.swapaxes(h0[:, :, 0], 0, 1))
    h0_lat = h0_lat.at[:, 1, :, SSM_LAT_SEGS - 1].set(jnp.swapaxes(h0[:, :, 1], 0, 1))

    vec = lambda a: a[..., None, :]
    prm = {
        'ln_g': vec(ln_g), 'ln_b': vec(ln_b),
        'ffn_w1': ffn_w1.astype(BF16), 'ffn_w2': ffn_w2.astype(BF16),
        'w_in': w_in.astype(BF16), 'w_out': w_out.astype(BF16),
        'q_norm': vec(jnp.tile(q_norm, (1, N_HEADS))), 'k_norm': vec(jnp.tile(k_norm, (1, N_KV_HEADS))),
        'chunk_ln_g': vec(chunk_ln_g), 'chunk_ln_b': vec(chunk_ln_b),
        'chunk_w_s': chunk_w_s.astype(BF16),
        'chunk_b_s': jnp.repeat(jnp.swapaxes(chunk_b_s, 1, 2), D_CHUNK // N_CHUNK_HEADS, axis=2),
        'pool_w': _block_diag_pool(pool_w).astype(BF16), 'pool_scale': vec(pool_scale),
        'ssm_b': bmat.reshape(DEPTH, 2, D_SSM, SSM_LANES),
        'ssm_lam': lamb.reshape(DEPTH, 2, 4, SSM_LANES),
        'ssm_c': _place_c(ssm_c_re, ssm_c_im).astype(BF16),
        'ssm_h0': h0_lat.reshape(DEPTH, 2, SUBLANES, SSM_LANES),
        'ssm_d': vec(ssm_d), 'ssm_w_glu': ssm_w_glu.astype(BF16), 'ssm_b_glu': vec(ssm_b_glu),
    }
    cache_k = cache_k.reshape(DEC_BATCH, DEPTH, PAST_LEN, D_KV).astype(BF16)
    cache_v = cache_v.reshape(DEC_BATCH, DEPTH, PAST_LEN, D_KV).astype(BF16)

    x = (x_prompt.reshape(N_CTX, D_MODEL), x_sample.reshape(N_LAT, D_MODEL))
    ks, vs, s_re, s_im = [], [], [], []
    for l in range(DEPTH):
        x = _ffn_sublayer(x, mod, prm, l, 0)
        u_a, u_b, y_chunk, k_ctx, v_ctx, yp_ctx, ya_ctx, k_lat, v_lat, p_lat, q_lat = _inproj(
            x, mod, prm, l, cos_t, sin_t)
        yp_lat = _pool_lat(p_lat, prm, l)
        y_ssm, fin = _ssm(u_a, u_b, prm, l)
        ya_lat = _attn_lat(q_lat, k_lat, v_lat, cache_k, cache_v, l)
        x = _ffn_sublayer(x, mod, prm, l, 1, split_out=(l == DEPTH - 1),
                          mixer=((yp_ctx, yp_lat), y_ssm, (ya_ctx, ya_lat), y_chunk))
        ks.append(k_ctx.reshape(BATCH, SEQ, N_KV_HEADS, HEAD_DIM))
        vs.append(v_ctx.reshape(BATCH, SEQ, N_KV_HEADS, HEAD_DIM))
        f_re, f_im = _ssm_unpack_state(jnp.transpose(fin, (0, 2, 1, 3)).reshape(BATCH, 2, SSM_LANES))
        s_re.append(f_re)
        s_im.append(f_im)
    y_p = x[0].reshape(BATCH, SEQ, D_MODEL)
    y_s = x[1].reshape(DEC_BATCH, DEC_SEQ, D_MODEL)
    return (y_p, y_s, jnp.stack(ks, axis=1), jnp.stack(vs, axis=1),
            jnp.stack(s_re, axis=1), jnp.stack(s_im, axis=1))
```

```python
import functools
import math

import jax
import jax.numpy as jnp
from jax import lax
from jax.experimental import pallas as pl
from jax.experimental.pallas import tpu as pltpu

F32 = jnp.float32
BF16 = jnp.bfloat16

D_MODEL = 1024
BATCH = 32
SEQ = 256
DEPTH = 2
DEC_BATCH = 2
DEC_SEQ = 2048
PAST_LEN = 512
GRID_W = 64
D_POOL = 256
D_SSM = 256
D_ATTN = 256
D_CHUNK = 256
D_HEADS = 256
POOL_WINDOWS = (2, 4, 8, 16)
POOL_GROUP = 64
SSM_GROUP = 16
N_SSM_GROUPS = 16
SSM_STATE = 64
HEAD_DIM = 64
N_HEADS = 4
N_KV_HEADS = 2
D_KV = 128
CHUNK = 128
N_CHUNK_HEADS = 4
D_FF = 2816
N_MOD = 9
D_IN = 1536
ALPHA = (2 * DEPTH) ** 0.25
LN_EPS = 1e-5
RMS_EPS = 1e-6
ROPE_THETA = 10000.0

LANES = 128
SUBLANES = 8
TM = 512
ATT_TQ = 512
N_CTX = BATCH * SEQ
N_LAT = DEC_BATCH * DEC_SEQ
N_TOK = N_CTX + N_LAT
CTX_TILES = N_CTX // TM
LAT_TILES_PER_SEQ = DEC_SEQ // TM
N_TILES = N_TOK // TM
N_COND = 8
POOL_PAD = 8
SSM_LANES = 2 * N_SSM_GROUPS * SSM_STATE
SSM_W = 1024
SSM_LAT_SEGS = SUBLANES
SSM_LAT_SEG = DEC_SEQ // SSM_LAT_SEGS
SSM_ROWS = 1024
SSM_BLOCK = 512
VMEM_LIMIT = 56 * 1024 * 1024


def _params(n_grid):
    return pltpu.CompilerParams(dimension_semantics=("arbitrary",) * n_grid,
                                vmem_limit_bytes=VMEM_LIMIT)


def _fixed_spec(tail, *lead):
    idx = tuple(lead) + (0,) * len(tail)
    return pl.BlockSpec((None,) * len(lead) + tuple(tail), lambda *_: idx, pipeline_mode=pl.Buffered(1))


def _cond_row(i):
    return jnp.where(i < CTX_TILES, 0, 1 + (i - CTX_TILES) // LAT_TILES_PER_SEQ)


def _mod_spec(l):
    return pl.BlockSpec((None, None, N_MOD, D_MODEL), lambda i: (l, _cond_row(i), 0, 0))


def _tok_tile(i):
    return (i, 0)


def _ctx_tile(i):
    return (jnp.minimum(i, CTX_TILES - 1), 0)


def _lat_tile(i):
    return (jnp.maximum(i - CTX_TILES, 0), 0)


def _layer_norm(y, g, b):
    mu = jnp.mean(y, axis=-1, keepdims=True)
    d = y - mu
    var = jnp.mean(d * d, axis=-1, keepdims=True)
    return d * lax.rsqrt(var + LN_EPS) * g + b


def _gelu(x):
    return 0.5 * x * (1.0 + jnp.tanh(math.sqrt(2.0 / math.pi) * (x + 0.044715 * (x * x * x))))


def _silu(x):
    return x * jax.nn.sigmoid(x)


def _split_bf16(a):
    hi = a.astype(BF16)
    return hi, (a - hi.astype(F32)).astype(BF16)


def _mod_kernel(cond_ref, w_ref, b_ref, o_ref):
    a_hi, a_lo = _split_bf16(_silu(cond_ref[...]))
    w_hi, w_lo = _split_bf16(w_ref[...])
    acc = jnp.dot(a_hi, w_lo, preferred_element_type=F32)
    acc = acc + jnp.dot(a_lo, w_hi, preferred_element_type=F32)
    acc = acc + jnp.dot(a_hi, w_hi, preferred_element_type=F32)
    o_ref[...] = acc + b_ref[...]


def _modulation(cond, w_mod, b_mod):
    tn = D_MODEL
    out = pl.pallas_call(
        _mod_kernel,
        grid=(DEPTH, N_MOD),
        in_specs=[
            pl.BlockSpec((N_COND, D_MODEL), lambda l, j: (0, 0)),
            pl.BlockSpec((None, D_MODEL, tn), lambda l, j: (l, 0, j)),
            pl.BlockSpec((None, 1, tn), lambda l, j: (l, 0, j)),
        ],
        out_specs=pl.BlockSpec((None, N_COND, tn), lambda l, j: (l, 0, j)),
        out_shape=jax.ShapeDtypeStruct((DEPTH, N_COND, N_MOD * D_MODEL), F32),
        compiler_params=_params(2),
        name="modulation",
    )(cond, w_mod, b_mod.reshape(DEPTH, 1, N_MOD * D_MODEL))
    return out.reshape(DEPTH, N_COND, N_MOD, D_MODEL)


def _ffn_kernel(*refs, mod_off, split_in, split_out, mixer_out):
    n_in = 2 if split_in else 1
    n_mix = 10 if mixer_out else 0
    mix_refs = refs[n_in:n_in + n_mix]
    mod_ref, w1_ref, w2_ref, g_ref, b_ref = refs[n_in + n_mix:n_in + n_mix + 5]
    out_refs = refs[n_in + n_mix + 5:]
    is_ctx = pl.program_id(0) < CTX_TILES
    if split_in:
        x = jnp.where(is_ctx, refs[0][...], refs[1][...])
    else:
        x = refs[0][...]
    if mixer_out:
        yc_ref, wo_ref, go_ref, bo_ref = mix_refs[6:]
        parts = [jnp.where(is_ctx, mix_refs[2 * j][...], mix_refs[2 * j + 1][...]) for j in range(3)]
        parts.append(yc_ref[...])
        y = None
        for j, part in enumerate(parts):
            r = jnp.dot(part.astype(BF16), wo_ref[D_HEADS * j:D_HEADS * (j + 1), :],
                        preferred_element_type=F32)
            y = r if y is None else y + r
        x = _layer_norm(ALPHA * x + mod_ref[5:6, :] * y, go_ref[...], bo_ref[...])
    sh = mod_ref[mod_off:mod_off + 1, :]
    sc = mod_ref[mod_off + 1:mod_off + 2, :]
    gate = mod_ref[mod_off + 2:mod_off + 3, :]
    h = (x * (1.0 + sc) + sh).astype(BF16)
    gu = jnp.dot(h, w1_ref[...], preferred_element_type=F32)
    a = (_silu(gu[:, :D_FF]) * gu[:, D_FF:]).astype(BF16)
    f = jnp.dot(a, w2_ref[...], preferred_element_type=F32)
    y = _layer_norm(ALPHA * x + (0.5 * gate) * f, g_ref[...], b_ref[...])
    if split_out:
        @pl.when(is_ctx)
        def _():
            out_refs[0][...] = y

        @pl.when(jnp.logical_not(is_ctx))
        def _():
            out_refs[1][...] = y
    else:
        out_refs[0][...] = y


def _ffn_sublayer(xs, mod, prm, l, sub, split_out=False, mixer=None):
    split_in = isinstance(xs, tuple)
    if split_in:
        x_specs = [pl.BlockSpec((TM, D_MODEL), _ctx_tile), pl.BlockSpec((TM, D_MODEL), _lat_tile)]
    else:
        xs = (xs,)
        x_specs = [pl.BlockSpec((TM, D_MODEL), _tok_tile)]
    mix_args, mix_specs = (), []
    if mixer is not None:
        y_pool, y_ssm, y_attn, y_chunk = mixer
        mix_args = (*y_pool, *y_ssm, *y_attn, y_chunk, prm['w_out'], prm['ln_g'], prm['ln_b'])
        mix_specs = [pl.BlockSpec((TM, D_HEADS), _ctx_tile), pl.BlockSpec((TM, D_HEADS), _lat_tile)] * 3 + [
            pl.BlockSpec((TM, D_HEADS), _tok_tile),
            _fixed_spec((D_MODEL, D_MODEL), l), _fixed_spec((1, D_MODEL), l, 1), _fixed_spec((1, D_MODEL), l, 1)]
    if split_out:
        out_specs = [pl.BlockSpec((TM, D_MODEL), _ctx_tile), pl.BlockSpec((TM, D_MODEL), _lat_tile)]
        out_shape = [jax.ShapeDtypeStruct((N_CTX, D_MODEL), F32), jax.ShapeDtypeStruct((N_LAT, D_MODEL), F32)]
    else:
        out_specs = pl.BlockSpec((TM, D_MODEL), _tok_tile)
        out_shape = jax.ShapeDtypeStruct((N_TOK, D_MODEL), F32)
    ln_idx = 2 * sub
    return pl.pallas_call(
        functools.partial(_ffn_kernel, mod_off=6 * sub, split_in=split_in, split_out=split_out,
                          mixer_out=mixer is not None),
        grid=(N_TILES,),
        in_specs=x_specs + mix_specs + [
            _mod_spec(l),
            _fixed_spec((D_MODEL, 2 * D_FF), l, sub),
            _fixed_spec((D_FF, D_MODEL), l, sub),
            _fixed_spec((1, D_MODEL), l, ln_idx),
            _fixed_spec((1, D_MODEL), l, ln_idx),
        ],
        out_specs=out_specs,
        out_shape=out_shape,
        compiler_params=_params(1),
        name="ffn_sublayer",
    )(*xs, *mix_args, mod, prm['ffn_w1'], prm['ffn_w2'], prm['ln_g'], prm['ln_b'])


def _pool_math(x, w_bd, scale, seq_len):
    n = seq_len + 2 * POOL_PAD
    zpad = jnp.zeros((POOL_PAD, D_POOL), F32)
    e = jnp.concatenate([zpad, x, zpad], axis=0)

    def prev(a, d):
        return pltpu.roll(a, d, axis=0)

    def nxt(a, d):
        return pltpu.roll(a, n - d, axis=0)

    s2 = e + prev(e, 1)
    s4 = prev(s2, 1) + nxt(s2, 1)
    s8 = prev(s4, 2) + nxt(s4, 2)
    s16 = prev(s8, 4) + nxt(s8, 4)
    lane = lax.broadcasted_iota(jnp.int32, (seq_len, D_POOL), 1)
    grp = lane // POOL_GROUP
    sl = slice(POOL_PAD, POOL_PAD + seq_len)
    s = jnp.where(grp == 0, s2[sl], jnp.where(grp == 1, s4[sl], jnp.where(grp == 2, s8[sl], s16[sl])))
    half = jnp.where(grp == 0, 1, jnp.where(grp == 1, 2, jnp.where(grp == 2, 4, 8)))
    t = lax.broadcasted_iota(jnp.int32, (seq_len, D_POOL), 0)
    cnt = jnp.minimum(t + half, seq_len) - jnp.maximum(t - half, 0)
    d = (s / cnt.astype(F32) - x).astype(BF16)
    return jnp.dot(d, w_bd, preferred_element_type=F32) * scale


def _attn_math(problems, mxu_sums):
    nt = (((1,), (1,)), ((), ()))
    tq = problems[0][0].shape[0]
    lo = lax.broadcasted_iota(jnp.int32, (tq, LANES), 1) < HEAD_DIM
    scores = []
    for q, k, _ in problems:
        q = q * (HEAD_DIM ** -0.5 * math.log2(math.e))
        q_lo = q[:, :LANES]
        q_hi = q[:, LANES:]
        qs = [jnp.where(lo, q_lo, 0.0), jnp.where(lo, pltpu.roll(q_lo, HEAD_DIM, axis=1), 0.0),
              jnp.where(lo, 0.0, pltpu.roll(q_hi, HEAD_DIM, axis=1)), jnp.where(lo, 0.0, q_hi)]
        for g in range(N_KV_HEADS):
            qq = jnp.concatenate([qs[2 * g], qs[2 * g + 1]], axis=0).astype(BF16)
            scores.append(lax.dot_general(qq, k, nt, preferred_element_type=F32))
    results = []
    for i, (_, _, v) in enumerate(problems):
        if mxu_sums:
            lo_k = lax.broadcasted_iota(jnp.int32, v.shape, 1) < HEAD_DIM
            one = jnp.ones_like(v)
            v_g = [jnp.where(lo_k, v, one), jnp.where(lo_k, one, v)]
        outs = []
        for g in range(N_KV_HEADS):
            s = scores[N_KV_HEADS * i + g]
            p = jnp.exp2(s - jnp.max(s, axis=-1, keepdims=True))
            if mxu_sums:
                o = jnp.dot(p.astype(BF16), v_g[g], preferred_element_type=F32)
            else:
                l = jnp.sum(p, axis=-1, keepdims=True)
                o = jnp.dot(p.astype(BF16), v, preferred_element_type=F32) / l
            outs += [o[:tq], o[tq:]]
        if mxu_sums:
            r = [pltpu.roll(o, HEAD_DIM, axis=1) for o in outs]
            out_lo = jnp.where(lo, outs[0], r[1]) / jnp.where(lo, r[0], outs[1])
            out_hi = jnp.where(lo, r[2], outs[3]) / jnp.where(lo, outs[2], r[3])
        else:
            out_lo = jnp.where(lo, outs[0], pltpu.roll(outs[1], HEAD_DIM, axis=1))
            out_hi = jnp.where(lo, pltpu.roll(outs[2], HEAD_DIM, axis=1), outs[3])
        results.append(jnp.concatenate([out_lo, out_hi], axis=1))
    return results


def _seg_rms(x, gain, n_lanes):
    parts = []
    for j in range(n_lanes // LANES):
        xs = x[:, j * LANES:(j + 1) * LANES]
        sq = xs * xs
        lo = lax.broadcasted_iota(jnp.int32, xs.shape, 1) < HEAD_DIM
        s_all = jnp.sum(sq, axis=-1, keepdims=True)
        s_lo = jnp.sum(jnp.where(lo, sq, 0.0), axis=-1, keepdims=True)
        ms = jnp.where(lo, s_lo, s_all - s_lo) * (1.0 / HEAD_DIM)
        parts.append(xs * lax.rsqrt(ms + RMS_EPS))
    y = parts[0] if len(parts) == 1 else jnp.concatenate(parts, axis=1)
    return y * gain


def _rope(x, cos_t, sin_t, n_lanes):
    parts = []
    for j in range(n_lanes // LANES):
        xs = x[:, j * LANES:(j + 1) * LANES]
        first = (lax.broadcasted_iota(jnp.int32, xs.shape, 1) % HEAD_DIM) < (HEAD_DIM // 2)
        partner = jnp.where(first, pltpu.roll(xs, LANES - HEAD_DIM // 2, axis=1),
                            pltpu.roll(xs, HEAD_DIM // 2, axis=1))
        cs = cos_t[:, j * LANES:(j + 1) * LANES]
        sn = sin_t[:, j * LANES:(j + 1) * LANES]
        parts.append(xs * cs + partner * sn)
    return parts[0] if len(parts) == 1 else jnp.concatenate(parts, axis=1)


def _inproj_kernel(x_ref, mod_ref, w_ref, cos_ref, sin_ref, qn_ref, kn_ref,
                   cg_ref, cb_ref, ws_ref, bs_ref, pw_ref, ps_ref,
                   ua_ref, ub_ref, chunk_ref, kc_ref, vc_ref, yp_ref, ya_ref,
                   kl_ref, vl_ref, pl_ref, ql_ref):
    is_ctx = pl.program_id(0) < CTX_TILES
    x = x_ref[...]
    sh = mod_ref[3:4, :]
    sc = mod_ref[4:5, :]
    h = (x * (1.0 + sc) + sh).astype(BF16)
    proj = jnp.dot(h, w_ref[...], preferred_element_type=F32)
    p = proj[:, 0:256]
    ua_ref[...] = proj[:, 256:384]
    ub_ref[...] = proj[:, 384:512]
    cos_t = cos_ref[...]
    sin_t = sin_ref[...]
    q = _rope(_seg_rms(proj[:, 512:768], qn_ref[...], D_ATTN), cos_t, sin_t, D_ATTN)
    k = _rope(_seg_rms(proj[:, 768:896], kn_ref[...], D_KV), cos_t, sin_t, D_KV)
    v = proj[:, 896:1024]
    zu = _gelu(proj[:, 1024:1280])
    zv = _layer_norm(_gelu(proj[:, 1280:1536]), cg_ref[...], cb_ref[...])
    head = lax.broadcasted_iota(jnp.int32, (CHUNK, D_CHUNK), 1) // (D_CHUNK // N_CHUNK_HEADS)
    for c in range(TM // CHUNK):
        vb = zv[c * CHUNK:(c + 1) * CHUNK, :].astype(BF16)
        mixed = bs_ref[...]
        for hd in range(N_CHUNK_HEADS):
            r = jnp.dot(ws_ref[hd], vb, preferred_element_type=F32)
            mixed = mixed + jnp.where(head == hd, r, 0.0)
        chunk_ref[c * CHUNK:(c + 1) * CHUNK, :] = zu[c * CHUNK:(c + 1) * CHUNK, :] * mixed

    @pl.when(is_ctx)
    def _():
        kc_ref[...] = k
        vc_ref[...] = v
        for s in range(TM // SEQ):
            r = slice(s * SEQ, (s + 1) * SEQ)
            yp_ref[r, :] = _pool_math(p[r], pw_ref[...], ps_ref[...], SEQ)
            ya_ref[r, :] = _attn_math([(q[r], k[r].astype(BF16), v[r].astype(BF16))], False)[0]

    @pl.when(jnp.logical_not(is_ctx))
    def _():
        kl_ref[...] = k.astype(BF16)
        vl_ref[...] = v.astype(BF16)
        pl_ref[...] = p
        ql_ref[...] = q


def _rope_block(i):
    return jnp.where(i < CTX_TILES, LAT_TILES_PER_SEQ, (i - CTX_TILES) % LAT_TILES_PER_SEQ)


def _inproj(x, mod, prm, l, cos_t, sin_t):
    def out(rows, width, index_map, dtype=F32):
        return jax.ShapeDtypeStruct((rows, width), dtype), pl.BlockSpec((TM, width), index_map)

    outs = [out(N_TOK, LANES, _tok_tile), out(N_TOK, LANES, _tok_tile), out(N_TOK, D_CHUNK, _tok_tile),
            out(N_CTX, D_KV, _ctx_tile), out(N_CTX, D_KV, _ctx_tile),
            out(N_CTX, D_POOL, _ctx_tile), out(N_CTX, D_ATTN, _ctx_tile),
            out(N_LAT, D_KV, _lat_tile, BF16), out(N_LAT, D_KV, _lat_tile, BF16),
            out(N_LAT, D_POOL, _lat_tile), out(N_LAT, D_ATTN, _lat_tile)]
    return pl.pallas_call(
        _inproj_kernel,
        grid=(N_TILES,),
        in_specs=[
            pl.BlockSpec((TM, D_MODEL), _tok_tile),
            _mod_spec(l),
            _fixed_spec((D_MODEL, D_IN), l),
            pl.BlockSpec((TM, D_ATTN), lambda i: (_rope_block(i), 0)),
            pl.BlockSpec((TM, D_ATTN), lambda i: (_rope_block(i), 0)),
            _fixed_spec((1, D_ATTN), l),
            _fixed_spec((1, D_KV), l),
            _fixed_spec((1, D_CHUNK), l),
            _fixed_spec((1, D_CHUNK), l),
            _fixed_spec((N_CHUNK_HEADS, CHUNK, CHUNK), l),
            _fixed_spec((CHUNK, D_CHUNK), l),
            _fixed_spec((D_POOL, D_POOL), l),
            _fixed_spec((1, D_POOL), l),
        ],
        out_specs=[o[1] for o in outs],
        out_shape=[o[0] for o in outs],
        compiler_params=_params(1),
        name="mixer_inproj",
    )(x, mod, prm['w_in'], cos_t, sin_t, prm['q_norm'], prm['k_norm'], prm['chunk_ln_g'],
      prm['chunk_ln_b'], prm['chunk_w_s'], prm['chunk_b_s'], prm['pool_w'], prm['pool_scale'])


def _pool_kernel(p_ref, w_ref, s_ref, o_ref):
    o_ref[...] = _pool_math(p_ref[...], w_ref[...], s_ref[...], DEC_SEQ)


def _pool_lat(p_lat, prm, l):
    return pl.pallas_call(
        _pool_kernel,
        grid=(DEC_BATCH,),
        in_specs=[pl.BlockSpec((DEC_SEQ, D_POOL), lambda b: (b, 0)),
                  _fixed_spec((D_POOL, D_POOL), l), _fixed_spec((1, D_POOL), l)],
        out_specs=pl.BlockSpec((DEC_SEQ, D_POOL), lambda b: (b, 0)),
        out_shape=jax.ShapeDtypeStruct((N_LAT, D_POOL), F32),
        compiler_params=_params(1),
        name="pool_lat",
    )(p_lat, prm['pool_w'], prm['pool_scale'])


def _attn_kernel(q_ref, kc_ref, kl_ref, vc_ref, vl_ref, o_ref):
    k = jnp.concatenate([kc_ref[...], kl_ref[...]], axis=0)
    v = jnp.concatenate([vc_ref[...], vl_ref[...]], axis=0)
    o_ref[...] = _attn_math([(q_ref[...], k, v)], True)[0]


def _attn_lat(q_lat, k_lat, v_lat, cache_k, cache_v, l):
    q_per_seq = DEC_SEQ // ATT_TQ
    cache_spec = pl.BlockSpec((None, None, PAST_LEN, D_KV), lambda b, j: (b, l, 0, 0))
    seq_spec = pl.BlockSpec((DEC_SEQ, D_KV), lambda b, j: (b, 0))
    return pl.pallas_call(
        _attn_kernel,
        grid=(DEC_BATCH, q_per_seq),
        in_specs=[pl.BlockSpec((ATT_TQ, D_ATTN), lambda b, j: (b * q_per_seq + j, 0)),
                  cache_spec, seq_spec, cache_spec, seq_spec],
        out_specs=pl.BlockSpec((ATT_TQ, D_ATTN), lambda b, j: (b * q_per_seq + j, 0)),
        out_shape=jax.ShapeDtypeStruct((N_LAT, D_ATTN), F32),
        compiler_params=_params(2),
        name="attn_lat",
    )(q_lat, cache_k, k_lat, cache_v, v_lat)


def _ssm_prep_kernel(lam_ref, br_ref, bi_ref, bmat_ref, lamb_ref):
    lre = lam_ref[0:1, :]
    lim = lam_ref[1:2, :]
    dt = jnp.exp(lam_ref[2:3, :])
    mag = jnp.exp(lre * dt)
    ar = mag * jnp.cos(lim * dt)
    ai = mag * jnp.sin(lim * dt)
    lamb_ref[0:1, :] = ar
    lamb_ref[1:2, :] = ai
    mag_s = jnp.exp(lre * dt * SSM_LAT_SEG)
    lamb_ref[2:3, :] = mag_s * jnp.cos(lim * dt * SSM_LAT_SEG)
    lamb_ref[3:4, :] = mag_s * jnp.sin(lim * dt * SSM_LAT_SEG)
    den = lre * lre + lim * lim
    cr = ((ar - 1.0) * lre + ai * lim) / den
    ci = (ai * lre - (ar - 1.0) * lim) / den
    n_rep = SSM_LANES // LANES
    br = jnp.concatenate([br_ref[...]] * n_rep, axis=1)
    bi = jnp.concatenate([bi_ref[...]] * n_rep, axis=1)
    lane = lax.broadcasted_iota(jnp.int32, br.shape, 1)
    row_g = lax.broadcasted_iota(jnp.int32, br.shape, 0) // SSM_GROUP
    lane_g = 2 * (lane // (2 * LANES)) + (lane % LANES) // SSM_STATE
    is_re = (lane % (2 * LANES)) < LANES
    b_bar = jnp.where(is_re, cr * br - ci * bi, cr * bi + ci * br)
    bmat_ref[...] = jnp.where(row_g == lane_g, b_bar, 0.0).astype(BF16)


def _ssm_prep(lam_rows, b_re_placed, b_im_placed):
    n = DEPTH * 2
    return pl.pallas_call(
        _ssm_prep_kernel,
        grid=(n,),
        in_specs=[pl.BlockSpec((None, 3, SSM_LANES), lambda i: (i, 0, 0)),
                  pl.BlockSpec((None, D_SSM, LANES), lambda i: (i, 0, 0)),
                  pl.BlockSpec((None, D_SSM, LANES), lambda i: (i, 0, 0))],
        out_specs=[pl.BlockSpec((None, D_SSM, SSM_LANES), lambda i: (i, 0, 0)),
                   pl.BlockSpec((None, 4, SSM_LANES), lambda i: (i, 0, 0))],
        out_shape=[jax.ShapeDtypeStruct((n, D_SSM, SSM_LANES), BF16),
                   jax.ShapeDtypeStruct((n, 4, SSM_LANES), F32)],
        compiler_params=_params(1),
        name="ssm_prep",
    )(lam_rows, b_re_placed, b_im_placed)


def _ssm_kernel(ua_ref, ub_ref, h0_ref, bmat_ref, cmat_ref, lamb_ref, d_ref, wg_ref, bg_ref,
                y_ref, fin_ref, ui_ref, buf0_ref, buf1_ref, yi_ref, *, n_per):
    seg_len = SEQ
    ssm_w = SSM_W
    n_rows = SUBLANES * seg_len
    n_cb = ssm_w // (2 * LANES)

    def row_chunks(fn):
        def body(c, carry):
            fn(pl.ds(pl.multiple_of(c * SSM_ROWS, SSM_ROWS), SSM_ROWS))
            return carry
        lax.fori_loop(0, n_rows // SSM_ROWS, body, 0)

    def step_rows(t):
        return pl.ds(pl.multiple_of(t * SUBLANES, SUBLANES), SUBLANES)

    def interleave(t, carry):
        ui_ref[step_rows(t), 0:LANES] = ua_ref[pl.ds(t, SUBLANES, stride=seg_len), :]
        ui_ref[step_rows(t), LANES:2 * LANES] = ub_ref[pl.ds(t, SUBLANES, stride=seg_len), :]
        return carry

    lax.fori_loop(0, seg_len, interleave, 0, unroll=4)

    seg = lax.broadcasted_iota(jnp.int32, (SUBLANES, LANES), 0) % n_per

    chains = [(dirn, k) for dirn in range(2) for k in range(n_cb)]
    n_parts = SSM_LANES // ssm_w
    bufs = (buf0_ref, buf1_ref)
    assert n_parts == len(bufs)

    def fill(part, rows):
        ub = ui_ref[rows, :].astype(BF16)
        for dirn, k in chains:
            c0 = k * 2 * LANES
            lane0 = part * ssm_w + c0
            bufs[part][dirn, rows, c0:c0 + 2 * LANES] = jnp.dot(
                ub, bmat_ref[dirn, :, lane0:lane0 + 2 * LANES], preferred_element_type=F32)

    def readout(part, rows):
        lo = part * ssm_w
        y = jnp.dot(bufs[part][0, rows, :].astype(BF16), cmat_ref[0, lo:lo + ssm_w, :],
                    preferred_element_type=F32)
        y = y + jnp.dot(bufs[part][1, rows, :].astype(BF16), cmat_ref[1, lo:lo + ssm_w, :],
                        preferred_element_type=F32)
        if part == 0:
            y = y + d_ref[...] * ui_ref[rows, :]
            yi_ref[0, rows, :] = y[:, :LANES]
            yi_ref[1, rows, :] = y[:, LANES:]
        else:
            yi_ref[0, rows, :] += y[:, :LANES]
            yi_ref[1, rows, :] += y[:, LANES:]

    def lam_rows(part, r):
        out = []
        for dirn, k in chains:
            c0 = part * ssm_w + k * 2 * LANES
            out.append(jnp.broadcast_to(lamb_ref[dirn, r:r + 1, c0:c0 + LANES], (SUBLANES, LANES)))
        return out

    def scan(part, init, store, side_work=None):
        buf = bufs[part]
        a_re, a_im = lam_rows(part, 0), lam_rows(part, 1)

        def step(i, hs):
            rows = (step_rows(i), step_rows(seg_len - 1 - i))
            new = []
            for c, (dirn, k) in enumerate(chains):
                re_sl = slice(k * 2 * LANES, k * 2 * LANES + LANES)
                im_sl = slice(k * 2 * LANES + LANES, (k + 1) * 2 * LANES)
                hr, hi = hs[2 * c], hs[2 * c + 1]
                nr = a_re[c] * hr - a_im[c] * hi + buf[dirn, rows[dirn], re_sl]
                ni = a_re[c] * hi + a_im[c] * hr + buf[dirn, rows[dirn], im_sl]
                if store:
                    buf[dirn, rows[dirn], re_sl] = nr
                    buf[dirn, rows[dirn], im_sl] = ni
                new += [nr, ni]
            return tuple(new)

        if side_work is None:
            return lax.fori_loop(0, seg_len, step, tuple(init), unroll=2)

        steps = SSM_BLOCK // SUBLANES

        def block(blk, hs):
            side_work(pl.ds(pl.multiple_of(blk * SSM_BLOCK, SSM_BLOCK), SSM_BLOCK))
            for j in range(steps):
                hs = step(blk * steps + j, hs)
            return hs

        return lax.fori_loop(0, n_rows // SSM_BLOCK, block, tuple(init))

    def initial_states(part, local):
        h0 = []
        for dirn, k in chains:
            c0 = part * ssm_w + k * 2 * LANES
            h0 += [h0_ref[dirn, :, c0:c0 + LANES], h0_ref[dirn, :, c0 + LANES:c0 + 2 * LANES]]
        if local is None:
            return h0
        s_re, s_im = lam_rows(part, 2), lam_rows(part, 3)
        init = []
        for c, (dirn, k) in enumerate(chains):
            edge = seg == (0 if dirn == 0 else n_per - 1)
            shift = 1 if dirn == 0 else SUBLANES - 1
            cr, ci = h0[2 * c], h0[2 * c + 1]
            lr = pltpu.roll(local[2 * c], shift, axis=0)
            li = pltpu.roll(local[2 * c + 1], shift, axis=0)
            for _ in range(n_per - 1):
                pr = pltpu.roll(cr, shift, axis=0)
                pi = pltpu.roll(ci, shift, axis=0)
                cr = jnp.where(edge, h0[2 * c], s_re[c] * pr - s_im[c] * pi + lr)
                ci = jnp.where(edge, h0[2 * c + 1], s_re[c] * pi + s_im[c] * pr + li)
            init += [cr, ci]
        return init

    zeros = [jnp.zeros((SUBLANES, LANES), F32)] * (2 * len(chains))
    row_chunks(functools.partial(fill, 0))
    for part in range(n_parts):
        fill_next = functools.partial(fill, part + 1) if part + 1 < n_parts else None
        read_prev = functools.partial(readout, part - 1) if part > 0 else None
        if n_per == 1:
            if part > 0 and fill_next is not None:
                row_chunks(fill_next)
            hs = scan(part, initial_states(part, None), True, fill_next if part == 0 else read_prev)
        else:
            local = scan(part, zeros, False, fill_next)
            hs = scan(part, initial_states(part, local), True, read_prev)
        for c, (dirn, k) in enumerate(chains):
            c0 = part * ssm_w + k * 2 * LANES
            fin_ref[dirn, :, c0:c0 + LANES] = hs[2 * c]
            fin_ref[dirn, :, c0 + LANES:c0 + 2 * LANES] = hs[2 * c + 1]
    row_chunks(functools.partial(readout, n_parts - 1))

    def glu(rows):
        g = _gelu(jnp.concatenate([yi_ref[0, rows, :], yi_ref[1, rows, :]], axis=1))
        z = jnp.dot(g.astype(BF16), wg_ref[...], preferred_element_type=F32) + bg_ref[...]
        o = g * jax.nn.sigmoid(z)
        yi_ref[0, rows, :] = o[:, :LANES]
        yi_ref[1, rows, :] = o[:, LANES:]

    row_chunks(glu)

    def deinterleave(tb, carry):
        for s_idx in range(SUBLANES):
            dst = pl.ds(pl.multiple_of(s_idx * seg_len + tb * SUBLANES, SUBLANES), SUBLANES)
            src = pl.ds(tb * SUBLANES * SUBLANES + s_idx, SUBLANES, stride=SUBLANES)
            y_ref[dst, 0:LANES] = yi_ref[0, src, :]
            y_ref[dst, LANES:2 * LANES] = yi_ref[1, src, :]
        return carry

    lax.fori_loop(0, seg_len // SUBLANES, deinterleave, 0)


def _ssm(ua, ub, prm, l):
    weights = [_fixed_spec((2, D_SSM, SSM_LANES), l), _fixed_spec((2, SSM_LANES, D_SSM), l),
               _fixed_spec((2, 4, SSM_LANES), l), _fixed_spec((1, D_SSM), l),
               _fixed_spec((D_SSM, D_SSM), l), _fixed_spec((1, D_SSM), l)]
    rows = SUBLANES * SEQ
    scratch = [pltpu.VMEM((rows, D_SSM), F32), pltpu.VMEM((2, rows, SSM_W), F32),
               pltpu.VMEM((2, rows, SSM_W), F32), pltpu.VMEM((2, rows, LANES), F32)]
    tile_a = pl.BlockSpec((rows, LANES), lambda i: (i, 0))
    tile_y = pl.BlockSpec((rows, D_SSM), lambda i: (i, 0))
    fin_spec = pl.BlockSpec((None, 2, SUBLANES, SSM_LANES), lambda i: (i, 0, 0, 0))

    tail = (prm['ssm_b'], prm['ssm_c'], prm['ssm_lam'], prm['ssm_d'], prm['ssm_w_glu'], prm['ssm_b_glu'])
    n_ctx_tiles = N_CTX // rows
    h0_ctx = jnp.zeros((2, SUBLANES, SSM_LANES), F32)
    y_ctx, fin_ctx = pl.pallas_call(
        functools.partial(_ssm_kernel, n_per=1),
        grid=(n_ctx_tiles,),
        in_specs=[tile_a, tile_a, _fixed_spec((2, SUBLANES, SSM_LANES))] + weights,
        out_specs=[tile_y, fin_spec],
        out_shape=[jax.ShapeDtypeStruct((N_CTX, D_SSM), F32),
                   jax.ShapeDtypeStruct((n_ctx_tiles, 2, SUBLANES, SSM_LANES), F32)],
        scratch_shapes=scratch,
        compiler_params=_params(1),
        name="ssm_ctx",
    )(ua, ub, h0_ctx, *tail)
    lat_a = pl.BlockSpec((rows, LANES), lambda b: (n_ctx_tiles + b, 0))
    h0_spec = pl.BlockSpec((None, None, 2, SUBLANES, SSM_LANES), lambda b: (l, b, 0, 0, 0))
    y_lat, _ = pl.pallas_call(
        functools.partial(_ssm_kernel, n_per=SSM_LAT_SEGS),
        grid=(DEC_BATCH,),
        in_specs=[lat_a, lat_a, h0_spec] + weights,
        out_specs=[tile_y, fin_spec],
        out_shape=[jax.ShapeDtypeStruct((N_LAT, D_SSM), F32),
                   jax.ShapeDtypeStruct((DEC_BATCH, 2, SUBLANES, SSM_LANES), F32)],
        scratch_shapes=scratch,
        compiler_params=_params(1),
        name="ssm_lat",
    )(ua, ub, prm['ssm_h0'], *tail)
    return (y_ctx, y_lat), fin_ctx


def _rope_tables():
    rows = DEC_SEQ // GRID_W
    row_idx = jnp.repeat(jnp.arange(rows), GRID_W).astype(F32)
    col_idx = jnp.tile(jnp.arange(GRID_W), rows).astype(F32)
    n_freq = HEAD_DIM // 4
    inv = ROPE_THETA ** (-jnp.arange(n_freq, dtype=F32) / n_freq)
    ang = jnp.concatenate([row_idx[:, None] * inv, col_idx[:, None] * inv], axis=-1)
    cos = jnp.cos(ang)
    sin = jnp.sin(ang)
    cos_h = jnp.concatenate([cos, cos], axis=-1)
    sin_h = jnp.concatenate([-sin, sin], axis=-1)
    cos_t = jnp.tile(cos_h, (1, N_HEADS))
    sin_t = jnp.tile(sin_h, (1, N_HEADS))
    cos_t = jnp.concatenate([cos_t, jnp.ones((TM, D_ATTN), F32)], axis=0)
    sin_t = jnp.concatenate([sin_t, jnp.zeros((TM, D_ATTN), F32)], axis=0)
    return cos_t, sin_t


def _state_lanes(a):
    return a.reshape(a.shape[:-2] + (N_SSM_GROUPS // 2, 2 * SSM_STATE))


def _ssm_lane_rows(a):
    a = _state_lanes(a)
    return jnp.stack([a, a], axis=-2).reshape(a.shape[:-2] + (SSM_LANES,))


def _ssm_pack_state(re, im):
    return jnp.stack([_state_lanes(re), _state_lanes(im)], axis=-2).reshape(re.shape[:-2] + (SSM_LANES,))


def _ssm_unpack_state(s):
    s = s.reshape(s.shape[:-1] + (N_SSM_GROUPS // 2, 2, 2 * SSM_STATE))
    shp = s.shape[:-3] + (N_SSM_GROUPS, SSM_STATE)
    return s[..., 0, :].reshape(shp), s[..., 1, :].reshape(shp)


def _ssm_group_mask():
    ch_g = jnp.arange(D_SSM) // SSM_GROUP
    lane = jnp.arange(SSM_LANES)
    lane_g = 2 * (lane // (2 * LANES)) + (lane % LANES) // SSM_STATE
    return ch_g[:, None] == lane_g[None, :]


def _b_rows(b):
    bt = jnp.swapaxes(b, -1, -2).reshape(DEPTH * 2, D_SSM, SSM_STATE)
    return jnp.concatenate([bt, bt], axis=-1)


def _place_c(c_re, c_im):
    def rows(c):
        ct = jnp.moveaxis(c, -1, -3).reshape(c.shape[:-3] + (SSM_STATE, D_SSM))
        return jnp.tile(ct, (1,) * (ct.ndim - 2) + (SSM_LANES // SSM_STATE, 1))
    is_re = (jnp.arange(SSM_LANES) % (2 * LANES) < LANES)[:, None]
    return jnp.where(_ssm_group_mask().T, jnp.where(is_re, rows(c_re), -rows(c_im)), 0.0)


def _block_diag_pool(pool_w):
    tiled = jnp.tile(pool_w.reshape(DEPTH, D_POOL, POOL_GROUP), (1, 1, len(POOL_WINDOWS)))
    grp = jnp.arange(D_POOL) // POOL_GROUP
    return jnp.where(grp[:, None] == grp[None, :], tiled, 0.0)


def kernel(x_prompt, x_sample, cache_k, cache_v, state_ssm_re, state_ssm_im, c, c_ctx, w_mod, b_mod, ln_g, ln_b, ffn_w1, ffn_w2, w_in, w_out, pool_w, pool_scale, ssm_lam_re, ssm_lam_im, ssm_log_step, ssm_b_re, ssm_b_im, ssm_c_re, ssm_c_im, ssm_d, ssm_w_glu, ssm_b_glu, q_norm, k_norm, chunk_ln_g, chunk_ln_b, chunk_w_s, chunk_b_s):
    cond = jnp.concatenate([c_ctx[None, :], c, jnp.zeros((N_COND - 1 - DEC_BATCH, D_MODEL), F32)], axis=0)
    mod = _modulation(cond, w_mod, b_mod)
    cos_t, sin_t = _rope_tables()

    log_step = jnp.broadcast_to(ssm_log_step[..., None], ssm_lam_re.shape)
    lam_rows = jnp.stack([_ssm_lane_rows(ssm_lam_re), _ssm_lane_rows(ssm_lam_im),
                          _ssm_lane_rows(log_step)], axis=-2)
    bmat, lamb = _ssm_prep(lam_rows.reshape(DEPTH * 2, 3, SSM_LANES),
                           _b_rows(ssm_b_re), _b_rows(ssm_b_im))
    h0 = _ssm_pack_state(state_ssm_re, state_ssm_im)
    h0_lat = jnp.zeros((DEPTH, DEC_BATCH, 2, SSM_LAT_SEGS, SSM_LANES), F32)
    h0_lat = h0_lat.at[:, :, 0, 0].set(jnp.swapaxes(h0[:, :, 0], 0, 1))
    h0_lat = h0_lat.at[:, :, 1, SSM_LAT_SEGS - 1].set(jnp.swapaxes(h0[:, :, 1], 0, 1))

    vec = lambda a: a[..., None, :]
    prm = {
        'ln_g': vec(ln_g), 'ln_b': vec(ln_b),
        'ffn_w1': ffn_w1.astype(BF16), 'ffn_w2': ffn_w2.astype(BF16),
        'w_in': w_in.astype(BF16), 'w_out': w_out.astype(BF16),
        'q_norm': vec(jnp.tile(q_norm, (1, N_HEADS))), 'k_norm': vec(jnp.tile(k_norm, (1, N_KV_HEADS))),
        'chunk_ln_g': vec(chunk_ln_g), 'chunk_ln_b': vec(chunk_ln_b),
        'chunk_w_s': chunk_w_s.astype(BF16),
        'chunk_b_s': jnp.repeat(jnp.swapaxes(chunk_b_s, 1, 2), D_CHUNK // N_CHUNK_HEADS, axis=2),
        'pool_w': _block_diag_pool(pool_w).astype(BF16), 'pool_scale': vec(pool_scale),
        'ssm_b': bmat.reshape(DEPTH, 2, D_SSM, SSM_LANES),
        'ssm_lam': lamb.reshape(DEPTH, 2, 4, SSM_LANES),
        'ssm_c': _place_c(ssm_c_re, ssm_c_im).astype(BF16),
        'ssm_h0': h0_lat,
        'ssm_d': vec(ssm_d), 'ssm_w_glu': ssm_w_glu.astype(BF16), 'ssm_b_glu': vec(ssm_b_glu),
    }
    cache_k = cache_k.reshape(DEC_BATCH, DEPTH, PAST_LEN, D_KV).astype(BF16)
    cache_v = cache_v.reshape(DEC_BATCH, DEPTH, PAST_LEN, D_KV).astype(BF16)

    x = (x_prompt.reshape(N_CTX, D_MODEL), x_sample.reshape(N_LAT, D_MODEL))
    ks, vs, s_re, s_im = [], [], [], []
    for l in range(DEPTH):
        x = _ffn_sublayer(x, mod, prm, l, 0)
        u_a, u_b, y_chunk, k_ctx, v_ctx, yp_ctx, ya_ctx, k_lat, v_lat, p_lat, q_lat = _inproj(
            x, mod, prm, l, cos_t, sin_t)
        yp_lat = _pool_lat(p_lat, prm, l)
        y_ssm, fin = _ssm(u_a, u_b, prm, l)
        ya_lat = _attn_lat(q_lat, k_lat, v_lat, cache_k, cache_v, l)
        x = _ffn_sublayer(x, mod, prm, l, 1, split_out=(l == DEPTH - 1),
                          mixer=((yp_ctx, yp_lat), y_ssm, (ya_ctx, ya_lat), y_chunk))
        ks.append(k_ctx.reshape(BATCH, SEQ, N_KV_HEADS, HEAD_DIM))
        vs.append(v_ctx.reshape(BATCH, SEQ, N_KV_HEADS, HEAD_DIM))
        f_re, f_im = _ssm_unpack_state(jnp.transpose(fin, (0, 2, 1, 3)).reshape(BATCH, 2, SSM_LANES))
        s_re.append(f_re)
        s_im.append(f_im)
    y_p = x[0].reshape(BATCH, SEQ, D_MODEL)
    y_s = x[1].reshape(DEC_BATCH, DEC_SEQ, D_MODEL)
    return (y_p, y_s, jnp.stack(ks, axis=1), jnp.stack(vs, axis=1),
            jnp.stack(s_re, axis=1), jnp.stack(s_im, axis=1))
```

```python
import functools
import math

import jax
import jax.numpy as jnp
from jax import lax
from jax.experimental import pallas as pl
from jax.experimental.pallas import tpu as pltpu

F32 = jnp.float32
BF16 = jnp.bfloat16

D_MODEL = 1024
BATCH = 32
SEQ = 256
DEPTH = 2
DEC_BATCH = 2
DEC_SEQ = 2048
PAST_LEN = 512
GRID_W = 64
D_POOL = 256
D_SSM = 256
D_ATTN = 256
D_CHUNK = 256
D_HEADS = 256
POOL_WINDOWS = (2, 4, 8, 16)
POOL_GROUP = 64
SSM_GROUP = 16
N_SSM_GROUPS = 16
SSM_STATE = 64
HEAD_DIM = 64
N_HEADS = 4
N_KV_HEADS = 2
D_KV = 128
CHUNK = 128
N_CHUNK_HEADS = 4
D_FF = 2816
N_MOD = 9
D_IN = 1536
ALPHA = (2 * DEPTH) ** 0.25
LN_EPS = 1e-5
RMS_EPS = 1e-6
ROPE_THETA = 10000.0

LANES = 128
SUBLANES = 8
TM = 512
FFN_SUB = 256
ATT_TQ = 512
N_CTX = BATCH * SEQ
N_LAT = DEC_BATCH * DEC_SEQ
N_TOK = N_CTX + N_LAT
CTX_TILES = N_CTX // TM
LAT_TILES_PER_SEQ = DEC_SEQ // TM
N_TILES = N_TOK // TM
N_COND = 8
POOL_PAD = 8
SSM_LANES = 2 * N_SSM_GROUPS * SSM_STATE
SSM_W = 1024
SSM_LAT_SEGS = SUBLANES
SSM_LAT_SEG = DEC_SEQ // SSM_LAT_SEGS
SSM_ROWS = 1024
SSM_BLOCK = 512
VMEM_LIMIT = 56 * 1024 * 1024


def _params(n_grid):
    return pltpu.CompilerParams(dimension_semantics=("arbitrary",) * n_grid,
                                vmem_limit_bytes=VMEM_LIMIT)


def _fixed_spec(tail, *lead):
    idx = tuple(lead) + (0,) * len(tail)
    return pl.BlockSpec((None,) * len(lead) + tuple(tail), lambda *_: idx, pipeline_mode=pl.Buffered(1))


def _cond_row(i):
    return jnp.where(i < CTX_TILES, 0, 1 + (i - CTX_TILES) // LAT_TILES_PER_SEQ)


def _mod_spec(l):
    return pl.BlockSpec((None, None, N_MOD, D_MODEL), lambda i: (l, _cond_row(i), 0, 0))


def _tok_tile(i):
    return (i, 0)


def _ctx_tile(i):
    return (jnp.minimum(i, CTX_TILES - 1), 0)


def _lat_tile(i):
    return (jnp.maximum(i - CTX_TILES, 0), 0)


def _layer_norm(y, g, b):
    mu = jnp.mean(y, axis=-1, keepdims=True)
    d = y - mu
    var = jnp.mean(d * d, axis=-1, keepdims=True)
    return d * lax.rsqrt(var + LN_EPS) * g + b


def _gelu(x):
    return 0.5 * x * (1.0 + jnp.tanh(math.sqrt(2.0 / math.pi) * (x + 0.044715 * (x * x * x))))


def _silu(x):
    return x * jax.nn.sigmoid(x)


def _split_bf16(a):
    hi = a.astype(BF16)
    return hi, (a - hi.astype(F32)).astype(BF16)


def _mod_kernel(cond_ref, w_ref, b_ref, o_ref):
    a_hi, a_lo = _split_bf16(_silu(cond_ref[...]))
    w_hi, w_lo = _split_bf16(w_ref[...])
    acc = jnp.dot(a_hi, w_lo, preferred_element_type=F32)
    acc = acc + jnp.dot(a_lo, w_hi, preferred_element_type=F32)
    acc = acc + jnp.dot(a_hi, w_hi, preferred_element_type=F32)
    o_ref[...] = acc + b_ref[...]


def _modulation(cond, w_mod, b_mod):
    tn = D_MODEL
    out = pl.pallas_call(
        _mod_kernel,
        grid=(DEPTH, N_MOD),
        in_specs=[
            pl.BlockSpec((N_COND, D_MODEL), lambda l, j: (0, 0)),
            pl.BlockSpec((None, D_MODEL, tn), lambda l, j: (l, 0, j)),
            pl.BlockSpec((None, 1, tn), lambda l, j: (l, 0, j)),
        ],
        out_specs=pl.BlockSpec((None, N_COND, tn), lambda l, j: (l, 0, j)),
        out_shape=jax.ShapeDtypeStruct((DEPTH, N_COND, N_MOD * D_MODEL), F32),
        compiler_params=_params(2),
        name="modulation",
    )(cond, w_mod, b_mod.reshape(DEPTH, 1, N_MOD * D_MODEL))
    return out.reshape(DEPTH, N_COND, N_MOD, D_MODEL)


def _ffn_kernel(*refs, mod_off, split_in, split_out, mixer_out):
    n_in = 2 if split_in else 1
    n_mix = 11 if mixer_out else 0
    mix_refs = refs[n_in:n_in + n_mix]
    mod_ref, w1_ref, w2_ref, g_ref, b_ref = refs[n_in + n_mix:n_in + n_mix + 5]
    out_refs = refs[n_in + n_mix + 5:]
    is_ctx = pl.program_id(0) < CTX_TILES
    sh = mod_ref[mod_off:mod_off + 1, :]
    sc = mod_ref[mod_off + 1:mod_off + 2, :]
    gate = mod_ref[mod_off + 2:mod_off + 3, :]

    def load_x(rows):
        if split_in:
            x = jnp.where(is_ctx, refs[0][rows, :], refs[1][rows, :])
        else:
            x = refs[0][rows, :]
        if mixer_out:
            wo_ref, go_ref, bo_ref = mix_refs[8:]
            parts = [jnp.where(is_ctx, mix_refs[2 * j][rows, :], mix_refs[2 * j + 1][rows, :])
                     for j in range(4)]
            y = None
            for j, part in enumerate(parts):
                r = jnp.dot(part.astype(BF16), wo_ref[D_HEADS * j:D_HEADS * (j + 1), :],
                            preferred_element_type=F32)
                y = r if y is None else y + r
            x = _layer_norm(ALPHA * x + mod_ref[5:6, :] * y, go_ref[...], bo_ref[...])
        return x

    def up(x):
        h = (x * (1.0 + sc) + sh).astype(BF16)
        return jnp.dot(h, w1_ref[...], preferred_element_type=F32)

    def down(gu):
        a = (_silu(gu[:, :D_FF]) * gu[:, D_FF:]).astype(BF16)
        return jnp.dot(a, w2_ref[...], preferred_element_type=F32)

    def finish(x, f):
        return _layer_norm(ALPHA * x + (0.5 * gate) * f, g_ref[...], b_ref[...])

    row_slices = [slice(j * FFN_SUB, (j + 1) * FFN_SUB) for j in range(TM // FFN_SUB)]
    xs, gus, fs, ys = {}, {}, {}, {}
    n = len(row_slices)
    for j in range(n + 2):
        if j < n:
            xs[j] = load_x(row_slices[j])
            gus[j] = up(xs[j])
        if 0 <= j - 1 < n:
            fs[j - 1] = down(gus.pop(j - 1))
        if 0 <= j - 2 < n:
            ys[j - 2] = finish(xs.pop(j - 2), fs.pop(j - 2))

    def store(o_ref):
        for j, rows in enumerate(row_slices):
            o_ref[rows, :] = ys[j]

    if split_out:
        @pl.when(is_ctx)
        def _():
            store(out_refs[0])

        @pl.when(jnp.logical_not(is_ctx))
        def _():
            store(out_refs[1])
    else:
        store(out_refs[0])


def _ffn_sublayer(xs, mod, prm, l, sub, split_out=False, mixer=None):
    split_in = isinstance(xs, tuple)
    if split_in:
        x_specs = [pl.BlockSpec((TM, D_MODEL), _ctx_tile), pl.BlockSpec((TM, D_MODEL), _lat_tile)]
    else:
        xs = (xs,)
        x_specs = [pl.BlockSpec((TM, D_MODEL), _tok_tile)]
    mix_args, mix_specs = (), []
    if mixer is not None:
        y_pool, y_ssm, y_attn, y_chunk = mixer
        mix_args = (*y_pool, *y_ssm, *y_attn, *y_chunk, prm['w_out'], prm['ln_g'], prm['ln_b'])
        mix_specs = [pl.BlockSpec((TM, D_HEADS), _ctx_tile), pl.BlockSpec((TM, D_HEADS), _lat_tile)] * 4 + [
            _fixed_spec((D_MODEL, D_MODEL), l), _fixed_spec((1, D_MODEL), l, 1), _fixed_spec((1, D_MODEL), l, 1)]
    if split_out:
        out_specs = [pl.BlockSpec((TM, D_MODEL), _ctx_tile), pl.BlockSpec((TM, D_MODEL), _lat_tile)]
        out_shape = [jax.ShapeDtypeStruct((N_CTX, D_MODEL), F32), jax.ShapeDtypeStruct((N_LAT, D_MODEL), F32)]
    else:
        out_specs = pl.BlockSpec((TM, D_MODEL), _tok_tile)
        out_shape = jax.ShapeDtypeStruct((N_TOK, D_MODEL), F32)
    ln_idx = 2 * sub
    return pl.pallas_call(
        functools.partial(_ffn_kernel, mod_off=6 * sub, split_in=split_in, split_out=split_out,
                          mixer_out=mixer is not None),
        grid=(N_TILES,),
        in_specs=x_specs + mix_specs + [
            _mod_spec(l),
            _fixed_spec((D_MODEL, 2 * D_FF), l, sub),
            _fixed_spec((D_FF, D_MODEL), l, sub),
            _fixed_spec((1, D_MODEL), l, ln_idx),
            _fixed_spec((1, D_MODEL), l, ln_idx),
        ],
        out_specs=out_specs,
        out_shape=out_shape,
        compiler_params=_params(1),
        name="ffn_sublayer",
    )(*xs, *mix_args, mod, prm['ffn_w1'], prm['ffn_w2'], prm['ln_g'], prm['ln_b'])


def _pool_math(x, w_bd, scale, seq_len):
    n = seq_len + 2 * POOL_PAD
    zpad = jnp.zeros((POOL_PAD, D_POOL), F32)
    e = jnp.concatenate([zpad, x, zpad], axis=0)

    def prev(a, d):
        return pltpu.roll(a, d, axis=0)

    def nxt(a, d):
        return pltpu.roll(a, n - d, axis=0)

    s2 = e + prev(e, 1)
    s4 = prev(s2, 1) + nxt(s2, 1)
    s8 = prev(s4, 2) + nxt(s4, 2)
    s16 = prev(s8, 4) + nxt(s8, 4)
    lane = lax.broadcasted_iota(jnp.int32, (seq_len, D_POOL), 1)
    grp = lane // POOL_GROUP
    sl = slice(POOL_PAD, POOL_PAD + seq_len)
    s = jnp.where(grp == 0, s2[sl], jnp.where(grp == 1, s4[sl], jnp.where(grp == 2, s8[sl], s16[sl])))
    half = jnp.where(grp == 0, 1, jnp.where(grp == 1, 2, jnp.where(grp == 2, 4, 8)))
    t = lax.broadcasted_iota(jnp.int32, (seq_len, D_POOL), 0)
    cnt = jnp.minimum(t + half, seq_len) - jnp.maximum(t - half, 0)
    d = (s / cnt.astype(F32) - x).astype(BF16)
    return jnp.dot(d, w_bd, preferred_element_type=F32) * scale


def _attn_math(problems, mxu_sums):
    nt = (((1,), (1,)), ((), ()))
    tq = problems[0][0].shape[0]
    lo = lax.broadcasted_iota(jnp.int32, (tq, LANES), 1) < HEAD_DIM
    scores = []
    for q, k, _ in problems:
        q = q * (HEAD_DIM ** -0.5 * math.log2(math.e))
        q_lo = q[:, :LANES]
        q_hi = q[:, LANES:]
        qs = [jnp.where(lo, q_lo, 0.0), jnp.where(lo, pltpu.roll(q_lo, HEAD_DIM, axis=1), 0.0),
              jnp.where(lo, 0.0, pltpu.roll(q_hi, HEAD_DIM, axis=1)), jnp.where(lo, 0.0, q_hi)]
        for g in range(N_KV_HEADS):
            qq = jnp.concatenate([qs[2 * g], qs[2 * g + 1]], axis=0).astype(BF16)
            scores.append(lax.dot_general(qq, k, nt, preferred_element_type=F32))
    results = []
    for i, (_, _, v) in enumerate(problems):
        if mxu_sums:
            lo_k = lax.broadcasted_iota(jnp.int32, v.shape, 1) < HEAD_DIM
            one = jnp.ones_like(v)
            v_g = [jnp.where(lo_k, v, one), jnp.where(lo_k, one, v)]
        outs = []
        for g in range(N_KV_HEADS):
            s = scores[N_KV_HEADS * i + g]
            p = jnp.exp2(s - jnp.max(s, axis=-1, keepdims=True))
            if mxu_sums:
                o = jnp.dot(p.astype(BF16), v_g[g], preferred_element_type=F32)
            else:
                l = jnp.sum(p, axis=-1, keepdims=True)
                o = jnp.dot(p.astype(BF16), v, preferred_element_type=F32) / l
            outs += [o[:tq], o[tq:]]
        if mxu_sums:
            r = [pltpu.roll(o, HEAD_DIM, axis=1) for o in outs]
            out_lo = jnp.where(lo, outs[0], r[1]) / jnp.where(lo, r[0], outs[1])
            out_hi = jnp.where(lo, r[2], outs[3]) / jnp.where(lo, outs[2], r[3])
        else:
            out_lo = jnp.where(lo, outs[0], pltpu.roll(outs[1], HEAD_DIM, axis=1))
            out_hi = jnp.where(lo, pltpu.roll(outs[2], HEAD_DIM, axis=1), outs[3])
        results.append(jnp.concatenate([out_lo, out_hi], axis=1))
    return results


def _seg_rms(x, gain, n_lanes):
    parts = []
    for j in range(n_lanes // LANES):
        xs = x[:, j * LANES:(j + 1) * LANES]
        sq = xs * xs
        lo = lax.broadcasted_iota(jnp.int32, xs.shape, 1) < HEAD_DIM
        s_all = jnp.sum(sq, axis=-1, keepdims=True)
        s_lo = jnp.sum(jnp.where(lo, sq, 0.0), axis=-1, keepdims=True)
        ms = jnp.where(lo, s_lo, s_all - s_lo) * (1.0 / HEAD_DIM)
        parts.append(xs * lax.rsqrt(ms + RMS_EPS))
    y = parts[0] if len(parts) == 1 else jnp.concatenate(parts, axis=1)
    return y * gain


def _rope(x, cos_t, sin_t, n_lanes):
    parts = []
    for j in range(n_lanes // LANES):
        xs = x[:, j * LANES:(j + 1) * LANES]
        first = (lax.broadcasted_iota(jnp.int32, xs.shape, 1) % HEAD_DIM) < (HEAD_DIM // 2)
        partner = jnp.where(first, pltpu.roll(xs, LANES - HEAD_DIM // 2, axis=1),
                            pltpu.roll(xs, HEAD_DIM // 2, axis=1))
        cs = cos_t[:, j * LANES:(j + 1) * LANES]
        sn = sin_t[:, j * LANES:(j + 1) * LANES]
        parts.append(xs * cs + partner * sn)
    return parts[0] if len(parts) == 1 else jnp.concatenate(parts, axis=1)


def _inproj_kernel(*refs, ctx):
    if ctx:
        (x_ref, mod_ref, w_ref, qn_ref, kn_ref, cg_ref, cb_ref, ws_ref, bs_ref, pw_ref, ps_ref,
         ua_ref, ub_ref, chunk_ref, k_ref, v_ref, yp_ref, ya_ref) = refs
    else:
        (x_ref, mod_ref, w_ref, qn_ref, kn_ref, cg_ref, cb_ref, ws_ref, bs_ref, cos_ref, sin_ref,
         ua_ref, ub_ref, chunk_ref, k_ref, v_ref, p_ref, q_ref) = refs
    x = x_ref[...]
    sh = mod_ref[3:4, :]
    sc = mod_ref[4:5, :]
    h = (x * (1.0 + sc) + sh).astype(BF16)
    proj = jnp.dot(h, w_ref[...], preferred_element_type=F32)
    p = proj[:, 0:256]
    ua_ref[...] = proj[:, 256:384]
    ub_ref[...] = proj[:, 384:512]
    q = _seg_rms(proj[:, 512:768], qn_ref[...], D_ATTN)
    k = _seg_rms(proj[:, 768:896], kn_ref[...], D_KV)
    v = proj[:, 896:1024]
    if not ctx:
        cos_t = cos_ref[...]
        sin_t = sin_ref[...]
        q = _rope(q, cos_t, sin_t, D_ATTN)
        k = _rope(k, cos_t, sin_t, D_KV)
    zu = _gelu(proj[:, 1024:1280])
    zv = _layer_norm(_gelu(proj[:, 1280:1536]), cg_ref[...], cb_ref[...])
    head = lax.broadcasted_iota(jnp.int32, (CHUNK, D_CHUNK), 1) // (D_CHUNK // N_CHUNK_HEADS)
    for c in range(TM // CHUNK):
        vb = zv[c * CHUNK:(c + 1) * CHUNK, :].astype(BF16)
        mixed = bs_ref[...]
        for hd in range(N_CHUNK_HEADS):
            r = jnp.dot(ws_ref[hd], vb, preferred_element_type=F32)
            mixed = mixed + jnp.where(head == hd, r, 0.0)
        chunk_ref[c * CHUNK:(c + 1) * CHUNK, :] = zu[c * CHUNK:(c + 1) * CHUNK, :] * mixed
    if ctx:
        k_ref[...] = k
        v_ref[...] = v
        for s in range(TM // SEQ):
            r = slice(s * SEQ, (s + 1) * SEQ)
            yp_ref[r, :] = _pool_math(p[r], pw_ref[...], ps_ref[...], SEQ)
            ya_ref[r, :] = _attn_math([(q[r], k[r].astype(BF16), v[r].astype(BF16))], False)[0]
    else:
        k_ref[...] = k.astype(BF16)
        v_ref[...] = v.astype(BF16)
        p_ref[...] = p
        q_ref[...] = q


def _inproj(x, mod, prm, l, cos_t, sin_t):
    tok = lambda i: (i, 0)
    common = [_fixed_spec((D_MODEL, D_IN), l), _fixed_spec((1, D_ATTN), l), _fixed_spec((1, D_KV), l),
              _fixed_spec((1, D_CHUNK), l), _fixed_spec((1, D_CHUNK), l),
              _fixed_spec((N_CHUNK_HEADS, CHUNK, CHUNK), l), _fixed_spec((CHUNK, D_CHUNK), l)]
    common_args = (prm['w_in'], prm['q_norm'], prm['k_norm'], prm['chunk_ln_g'], prm['chunk_ln_b'],
                   prm['chunk_w_s'], prm['chunk_b_s'])
    mod_block = (None, None, N_MOD, D_MODEL)

    def outs(rows, kv_dtype):
        widths = [(LANES, F32), (LANES, F32), (D_CHUNK, F32), (D_KV, kv_dtype), (D_KV, kv_dtype),
                  (D_HEADS, F32), (D_HEADS, F32)]
        return ([jax.ShapeDtypeStruct((rows, w), dt) for w, dt in widths],
                [pl.BlockSpec((TM, w), tok) for w, _ in widths])

    ctx_shapes, ctx_specs = outs(N_CTX, F32)
    ctx = pl.pallas_call(
        functools.partial(_inproj_kernel, ctx=True),
        grid=(CTX_TILES,),
        in_specs=[pl.BlockSpec((TM, D_MODEL), tok), pl.BlockSpec(mod_block, lambda i: (l, 0, 0, 0))]
                 + common + [_fixed_spec((D_POOL, D_POOL), l), _fixed_spec((1, D_POOL), l)],
        out_specs=ctx_specs,
        out_shape=ctx_shapes,
        compiler_params=_params(1),
        name="mixer_inproj_ctx",
    )(x, mod, *common_args, prm['pool_w'], prm['pool_scale'])
    lat_shapes, lat_specs = outs(N_LAT, BF16)
    rope_spec = pl.BlockSpec((TM, D_ATTN), lambda i: (i % LAT_TILES_PER_SEQ, 0))
    lat = pl.pallas_call(
        functools.partial(_inproj_kernel, ctx=False),
        grid=(N_TILES - CTX_TILES,),
        in_specs=[pl.BlockSpec((TM, D_MODEL), lambda i: (CTX_TILES + i, 0)),
                  pl.BlockSpec(mod_block, lambda i: (l, 1 + i // LAT_TILES_PER_SEQ, 0, 0))]
                 + common + [rope_spec, rope_spec],
        out_specs=lat_specs,
        out_shape=lat_shapes,
        compiler_params=_params(1),
        name="mixer_inproj_lat",
    )(x, mod, *common_args, cos_t, sin_t)
    return ctx, lat


def _pool_kernel(p_ref, w_ref, s_ref, o_ref):
    o_ref[...] = _pool_math(p_ref[...], w_ref[...], s_ref[...], DEC_SEQ)


def _pool_lat(p_lat, prm, l):
    return pl.pallas_call(
        _pool_kernel,
        grid=(DEC_BATCH,),
        in_specs=[pl.BlockSpec((DEC_SEQ, D_POOL), lambda b: (b, 0)),
                  _fixed_spec((D_POOL, D_POOL), l), _fixed_spec((1, D_POOL), l)],
        out_specs=pl.BlockSpec((DEC_SEQ, D_POOL), lambda b: (b, 0)),
        out_shape=jax.ShapeDtypeStruct((N_LAT, D_POOL), F32),
        compiler_params=_params(1),
        name="pool_lat",
    )(p_lat, prm['pool_w'], prm['pool_scale'])


def _attn_kernel(q_ref, kc_ref, kl_ref, vc_ref, vl_ref, o_ref):
    k = jnp.concatenate([kc_ref[...], kl_ref[...]], axis=0)
    v = jnp.concatenate([vc_ref[...], vl_ref[...]], axis=0)
    o_ref[...] = _attn_math([(q_ref[...], k, v)], True)[0]


def _attn_lat(q_lat, k_lat, v_lat, cache_k, cache_v, l):
    q_per_seq = DEC_SEQ // ATT_TQ
    cache_spec = pl.BlockSpec((None, None, PAST_LEN, D_KV), lambda b, j: (b, l, 0, 0))
    seq_spec = pl.BlockSpec((DEC_SEQ, D_KV), lambda b, j: (b, 0))
    return pl.pallas_call(
        _attn_kernel,
        grid=(DEC_BATCH, q_per_seq),
        in_specs=[pl.BlockSpec((ATT_TQ, D_ATTN), lambda b, j: (b * q_per_seq + j, 0)),
                  cache_spec, seq_spec, cache_spec, seq_spec],
        out_specs=pl.BlockSpec((ATT_TQ, D_ATTN), lambda b, j: (b * q_per_seq + j, 0)),
        out_shape=jax.ShapeDtypeStruct((N_LAT, D_ATTN), F32),
        compiler_params=_params(2),
        name="attn_lat",
    )(q_lat, cache_k, k_lat, cache_v, v_lat)


def _ssm_prep_kernel(lam_ref, br_ref, bi_ref, bmat_ref, lamb_ref):
    lre = lam_ref[0:1, :]
    lim = lam_ref[1:2, :]
    dt = jnp.exp(lam_ref[2:3, :])
    mag = jnp.exp(lre * dt)
    ar = mag * jnp.cos(lim * dt)
    ai = mag * jnp.sin(lim * dt)
    lamb_ref[0:1, :] = ar
    lamb_ref[1:2, :] = ai
    mag_s = jnp.exp(lre * dt * SSM_LAT_SEG)
    lamb_ref[2:3, :] = mag_s * jnp.cos(lim * dt * SSM_LAT_SEG)
    lamb_ref[3:4, :] = mag_s * jnp.sin(lim * dt * SSM_LAT_SEG)
    den = lre * lre + lim * lim
    cr = ((ar - 1.0) * lre + ai * lim) / den
    ci = (ai * lre - (ar - 1.0) * lim) / den
    n_rep = SSM_LANES // LANES
    br = jnp.concatenate([br_ref[...]] * n_rep, axis=1)
    bi = jnp.concatenate([bi_ref[...]] * n_rep, axis=1)
    lane = lax.broadcasted_iota(jnp.int32, br.shape, 1)
    row_g = lax.broadcasted_iota(jnp.int32, br.shape, 0) // SSM_GROUP
    lane_g = 2 * (lane // (2 * LANES)) + (lane % LANES) // SSM_STATE
    is_re = (lane % (2 * LANES)) < LANES
    b_bar = jnp.where(is_re, cr * br - ci * bi, cr * bi + ci * br)
    bmat_ref[...] = jnp.where(row_g == lane_g, b_bar, 0.0).astype(BF16)


def _ssm_prep(lam_rows, b_re_placed, b_im_placed):
    n = DEPTH * 2
    return pl.pallas_call(
        _ssm_prep_kernel,
        grid=(n,),
        in_specs=[pl.BlockSpec((None, 3, SSM_LANES), lambda i: (i, 0, 0)),
                  pl.BlockSpec((None, D_SSM, LANES), lambda i: (i, 0, 0)),
                  pl.BlockSpec((None, D_SSM, LANES), lambda i: (i, 0, 0))],
        out_specs=[pl.BlockSpec((None, D_SSM, SSM_LANES), lambda i: (i, 0, 0)),
                   pl.BlockSpec((None, 4, SSM_LANES), lambda i: (i, 0, 0))],
        out_shape=[jax.ShapeDtypeStruct((n, D_SSM, SSM_LANES), BF16),
                   jax.ShapeDtypeStruct((n, 4, SSM_LANES), F32)],
        compiler_params=_params(1),
        name="ssm_prep",
    )(lam_rows, b_re_placed, b_im_placed)


def _ssm_kernel(ua_ref, ub_ref, h0_ref, bmat_ref, cmat_ref, lamb_ref, d_ref, wg_ref, bg_ref,
                y_ref, fin_ref, ui_ref, buf0_ref, buf1_ref, yi_ref, *, n_per):
    seg_len = SEQ
    ssm_w = SSM_W
    n_rows = SUBLANES * seg_len
    n_cb = ssm_w // (2 * LANES)

    def row_chunks(fn):
        def body(c, carry):
            fn(pl.ds(pl.multiple_of(c * SSM_ROWS, SSM_ROWS), SSM_ROWS))
            return carry
        lax.fori_loop(0, n_rows // SSM_ROWS, body, 0)

    def step_rows(t):
        return pl.ds(pl.multiple_of(t * SUBLANES, SUBLANES), SUBLANES)

    def interleave(t, carry):
        ui_ref[step_rows(t), 0:LANES] = ua_ref[pl.ds(t, SUBLANES, stride=seg_len), :]
        ui_ref[step_rows(t), LANES:2 * LANES] = ub_ref[pl.ds(t, SUBLANES, stride=seg_len), :]
        return carry

    lax.fori_loop(0, seg_len, interleave, 0, unroll=4)

    seg = lax.broadcasted_iota(jnp.int32, (SUBLANES, LANES), 0) % n_per

    chains = [(dirn, k) for dirn in range(2) for k in range(n_cb)]
    n_parts = SSM_LANES // ssm_w
    bufs = (buf0_ref, buf1_ref)
    assert n_parts == len(bufs)

    def fill(part, rows):
        ub = ui_ref[rows, :].astype(BF16)
        for dirn, k in chains:
            c0 = k * 2 * LANES
            lane0 = part * ssm_w + c0
            bufs[part][dirn, rows, c0:c0 + 2 * LANES] = jnp.dot(
                ub, bmat_ref[dirn, :, lane0:lane0 + 2 * LANES], preferred_element_type=F32)

    def readout(part, rows):
        lo = part * ssm_w
        y = jnp.dot(bufs[part][0, rows, :].astype(BF16), cmat_ref[0, lo:lo + ssm_w, :],
                    preferred_element_type=F32)
        y = y + jnp.dot(bufs[part][1, rows, :].astype(BF16), cmat_ref[1, lo:lo + ssm_w, :],
                        preferred_element_type=F32)
        if part == 0:
            y = y + d_ref[...] * ui_ref[rows, :]
            yi_ref[0, rows, :] = y[:, :LANES]
            yi_ref[1, rows, :] = y[:, LANES:]
        else:
            yi_ref[0, rows, :] += y[:, :LANES]
            yi_ref[1, rows, :] += y[:, LANES:]

    def lam_rows(part, r):
        out = []
        for dirn, k in chains:
            c0 = part * ssm_w + k * 2 * LANES
            out.append(jnp.broadcast_to(lamb_ref[dirn, r:r + 1, c0:c0 + LANES], (SUBLANES, LANES)))
        return out

    def scan(part, init, store, side_work=None):
        buf = bufs[part]
        a_re, a_im = lam_rows(part, 0), lam_rows(part, 1)

        def step(i, hs):
            rows = (step_rows(i), step_rows(seg_len - 1 - i))
            new = []
            for c, (dirn, k) in enumerate(chains):
                re_sl = slice(k * 2 * LANES, k * 2 * LANES + LANES)
                im_sl = slice(k * 2 * LANES + LANES, (k + 1) * 2 * LANES)
                hr, hi = hs[2 * c], hs[2 * c + 1]
                nr = a_re[c] * hr - a_im[c] * hi + buf[dirn, rows[dirn], re_sl]
                ni = a_re[c] * hi + a_im[c] * hr + buf[dirn, rows[dirn], im_sl]
                if store:
                    buf[dirn, rows[dirn], re_sl] = nr
                    buf[dirn, rows[dirn], im_sl] = ni
                new += [nr, ni]
            return tuple(new)

        if side_work is None:
            return lax.fori_loop(0, seg_len, step, tuple(init), unroll=2)

        steps = SSM_BLOCK // SUBLANES

        def block(blk, hs):
            side_work(pl.ds(pl.multiple_of(blk * SSM_BLOCK, SSM_BLOCK), SSM_BLOCK))
            for j in range(steps):
                hs = step(blk * steps + j, hs)
            return hs

        return lax.fori_loop(0, n_rows // SSM_BLOCK, block, tuple(init))

    def initial_states(part, local):
        h0 = []
        for dirn, k in chains:
            c0 = part * ssm_w + k * 2 * LANES
            h0 += [h0_ref[dirn, :, c0:c0 + LANES], h0_ref[dirn, :, c0 + LANES:c0 + 2 * LANES]]
        if local is None:
            return h0
        s_re, s_im = lam_rows(part, 2), lam_rows(part, 3)
        init = []
        for c, (dirn, k) in enumerate(chains):
            edge = seg == (0 if dirn == 0 else n_per - 1)
            shift = 1 if dirn == 0 else SUBLANES - 1
            cr, ci = h0[2 * c], h0[2 * c + 1]
            lr = pltpu.roll(local[2 * c], shift, axis=0)
            li = pltpu.roll(local[2 * c + 1], shift, axis=0)
            for _ in range(n_per - 1):
                pr = pltpu.roll(cr, shift, axis=0)
                pi = pltpu.roll(ci, shift, axis=0)
                cr = jnp.where(edge, h0[2 * c], s_re[c] * pr - s_im[c] * pi + lr)
                ci = jnp.where(edge, h0[2 * c + 1], s_re[c] * pi + s_im[c] * pr + li)
            init += [cr, ci]
        return init

    zeros = [jnp.zeros((SUBLANES, LANES), F32)] * (2 * len(chains))
    row_chunks(functools.partial(fill, 0))
    for part in range(n_parts):
        fill_next = functools.partial(fill, part + 1) if part + 1 < n_parts else None
        read_prev = functools.partial(readout, part - 1) if part > 0 else None
        if n_per == 1:
            if part > 0 and fill_next is not None:
                row_chunks(fill_next)
            hs = scan(part, initial_states(part, None), True, fill_next if part == 0 else read_prev)
        else:
            local = scan(part, zeros, False, fill_next)
            hs = scan(part, initial_states(part, local), True, read_prev)
        for c, (dirn, k) in enumerate(chains):
            c0 = part * ssm_w + k * 2 * LANES
            fin_ref[dirn, :, c0:c0 + LANES] = hs[2 * c]
            fin_ref[dirn, :, c0 + LANES:c0 + 2 * LANES] = hs[2 * c + 1]
    row_chunks(functools.partial(readout, n_parts - 1))

    def glu(rows):
        g = _gelu(jnp.concatenate([yi_ref[0, rows, :], yi_ref[1, rows, :]], axis=1))
        z = jnp.dot(g.astype(BF16), wg_ref[...], preferred_element_type=F32) + bg_ref[...]
        o = g * jax.nn.sigmoid(z)
        yi_ref[0, rows, :] = o[:, :LANES]
        yi_ref[1, rows, :] = o[:, LANES:]

    row_chunks(glu)

    def deinterleave(tb, carry):
        for s_idx in range(SUBLANES):
            dst = pl.ds(pl.multiple_of(s_idx * seg_len + tb * SUBLANES, SUBLANES), SUBLANES)
            src = pl.ds(tb * SUBLANES * SUBLANES + s_idx, SUBLANES, stride=SUBLANES)
            y_ref[dst, 0:LANES] = yi_ref[0, src, :]
            y_ref[dst, LANES:2 * LANES] = yi_ref[1, src, :]
        return carry

    lax.fori_loop(0, seg_len // SUBLANES, deinterleave, 0)


def _ssm(u_ctx, u_lat, prm, l):
    weights = [_fixed_spec((2, D_SSM, SSM_LANES), l), _fixed_spec((2, SSM_LANES, D_SSM), l),
               _fixed_spec((2, 4, SSM_LANES), l), _fixed_spec((1, D_SSM), l),
               _fixed_spec((D_SSM, D_SSM), l), _fixed_spec((1, D_SSM), l)]
    rows = SUBLANES * SEQ
    scratch = [pltpu.VMEM((rows, D_SSM), F32), pltpu.VMEM((2, rows, SSM_W), F32),
               pltpu.VMEM((2, rows, SSM_W), F32), pltpu.VMEM((2, rows, LANES), F32)]
    tile_a = pl.BlockSpec((rows, LANES), lambda i: (i, 0))
    tile_y = pl.BlockSpec((rows, D_SSM), lambda i: (i, 0))
    fin_spec = pl.BlockSpec((None, 2, SUBLANES, SSM_LANES), lambda i: (i, 0, 0, 0))

    tail = (prm['ssm_b'], prm['ssm_c'], prm['ssm_lam'], prm['ssm_d'], prm['ssm_w_glu'], prm['ssm_b_glu'])
    n_ctx_tiles = N_CTX // rows
    h0_ctx = jnp.zeros((2, SUBLANES, SSM_LANES), F32)
    y_ctx, fin_ctx = pl.pallas_call(
        functools.partial(_ssm_kernel, n_per=1),
        grid=(n_ctx_tiles,),
        in_specs=[tile_a, tile_a, _fixed_spec((2, SUBLANES, SSM_LANES))] + weights,
        out_specs=[tile_y, fin_spec],
        out_shape=[jax.ShapeDtypeStruct((N_CTX, D_SSM), F32),
                   jax.ShapeDtypeStruct((n_ctx_tiles, 2, SUBLANES, SSM_LANES), F32)],
        scratch_shapes=scratch,
        compiler_params=_params(1),
        name="ssm_ctx",
    )(*u_ctx, h0_ctx, *tail)
    h0_spec = pl.BlockSpec((None, None, 2, SUBLANES, SSM_LANES), lambda b: (l, b, 0, 0, 0))
    y_lat, _ = pl.pallas_call(
        functools.partial(_ssm_kernel, n_per=SSM_LAT_SEGS),
        grid=(DEC_BATCH,),
        in_specs=[tile_a, tile_a, h0_spec] + weights,
        out_specs=[tile_y, fin_spec],
        out_shape=[jax.ShapeDtypeStruct((N_LAT, D_SSM), F32),
                   jax.ShapeDtypeStruct((DEC_BATCH, 2, SUBLANES, SSM_LANES), F32)],
        scratch_shapes=scratch,
        compiler_params=_params(1),
        name="ssm_lat",
    )(*u_lat, prm['ssm_h0'], *tail)
    return (y_ctx, y_lat), fin_ctx


def _rope_tables():
    rows = DEC_SEQ // GRID_W
    row_idx = jnp.repeat(jnp.arange(rows), GRID_W).astype(F32)
    col_idx = jnp.tile(jnp.arange(GRID_W), rows).astype(F32)
    n_freq = HEAD_DIM // 4
    inv = ROPE_THETA ** (-jnp.arange(n_freq, dtype=F32) / n_freq)
    ang = jnp.concatenate([row_idx[:, None] * inv, col_idx[:, None] * inv], axis=-1)
    cos = jnp.cos(ang)
    sin = jnp.sin(ang)
    cos_h = jnp.concatenate([cos, cos], axis=-1)
    sin_h = jnp.concatenate([-sin, sin], axis=-1)
    return jnp.tile(cos_h, (1, N_HEADS)), jnp.tile(sin_h, (1, N_HEADS))


def _state_lanes(a):
    return a.reshape(a.shape[:-2] + (N_SSM_GROUPS // 2, 2 * SSM_STATE))


def _ssm_lane_rows(a):
    a = _state_lanes(a)
    return jnp.stack([a, a], axis=-2).reshape(a.shape[:-2] + (SSM_LANES,))


def _ssm_pack_state(re, im):
    return jnp.stack([_state_lanes(re), _state_lanes(im)], axis=-2).reshape(re.shape[:-2] + (SSM_LANES,))


def _ssm_unpack_state(s):
    s = s.reshape(s.shape[:-1] + (N_SSM_GROUPS // 2, 2, 2 * SSM_STATE))
    shp = s.shape[:-3] + (N_SSM_GROUPS, SSM_STATE)
    return s[..., 0, :].reshape(shp), s[..., 1, :].reshape(shp)


def _ssm_group_mask():
    ch_g = jnp.arange(D_SSM) // SSM_GROUP
    lane = jnp.arange(SSM_LANES)
    lane_g = 2 * (lane // (2 * LANES)) + (lane % LANES) // SSM_STATE
    return ch_g[:, None] == lane_g[None, :]


def _b_rows(b):
    bt = jnp.swapaxes(b, -1, -2).reshape(DEPTH * 2, D_SSM, SSM_STATE)
    return jnp.concatenate([bt, bt], axis=-1)


def _place_c(c_re, c_im):
    def rows(c):
        ct = jnp.moveaxis(c, -1, -3).reshape(c.shape[:-3] + (SSM_STATE, D_SSM))
        return jnp.tile(ct, (1,) * (ct.ndim - 2) + (SSM_LANES // SSM_STATE, 1))
    is_re = (jnp.arange(SSM_LANES) % (2 * LANES) < LANES)[:, None]
    return jnp.where(_ssm_group_mask().T, jnp.where(is_re, rows(c_re), -rows(c_im)), 0.0)


def _block_diag_pool(pool_w):
    tiled = jnp.tile(pool_w.reshape(DEPTH, D_POOL, POOL_GROUP), (1, 1, len(POOL_WINDOWS)))
    grp = jnp.arange(D_POOL) // POOL_GROUP
    return jnp.where(grp[:, None] == grp[None, :], tiled, 0.0)


def kernel(x_prompt, x_sample, cache_k, cache_v, state_ssm_re, state_ssm_im, c, c_ctx, w_mod, b_mod, ln_g, ln_b, ffn_w1, ffn_w2, w_in, w_out, pool_w, pool_scale, ssm_lam_re, ssm_lam_im, ssm_log_step, ssm_b_re, ssm_b_im, ssm_c_re, ssm_c_im, ssm_d, ssm_w_glu, ssm_b_glu, q_norm, k_norm, chunk_ln_g, chunk_ln_b, chunk_w_s, chunk_b_s):
    cond = jnp.concatenate([c_ctx[None, :], c, jnp.zeros((N_COND - 1 - DEC_BATCH, D_MODEL), F32)], axis=0)
    mod = _modulation(cond, w_mod, b_mod)
    cos_t, sin_t = _rope_tables()

    log_step = jnp.broadcast_to(ssm_log_step[..., None], ssm_lam_re.shape)
    lam_rows = jnp.stack([_ssm_lane_rows(ssm_lam_re), _ssm_lane_rows(ssm_lam_im),
                          _ssm_lane_rows(log_step)], axis=-2)
    bmat, lamb = _ssm_prep(lam_rows.reshape(DEPTH * 2, 3, SSM_LANES),
                           _b_rows(ssm_b_re), _b_rows(ssm_b_im))
    h0 = _ssm_pack_state(state_ssm_re, state_ssm_im)
    h0_lat = jnp.zeros((DEPTH, DEC_BATCH, 2, SSM_LAT_SEGS, SSM_LANES), F32)
    h0_lat = h0_lat.at[:, :, 0, 0].set(jnp.swapaxes(h0[:, :, 0], 0, 1))
    h0_lat = h0_lat.at[:, :, 1, SSM_LAT_SEGS - 1].set(jnp.swapaxes(h0[:, :, 1], 0, 1))

    vec = lambda a: a[..., None, :]
    prm = {
        'ln_g': vec(ln_g), 'ln_b': vec(ln_b),
        'ffn_w1': ffn_w1.astype(BF16), 'ffn_w2': ffn_w2.astype(BF16),
        'w_in': w_in.astype(BF16), 'w_out': w_out.astype(BF16),
        'q_norm': vec(jnp.tile(q_norm, (1, N_HEADS))), 'k_norm': vec(jnp.tile(k_norm, (1, N_KV_HEADS))),
        'chunk_ln_g': vec(chunk_ln_g), 'chunk_ln_b': vec(chunk_ln_b),
        'chunk_w_s': chunk_w_s.astype(BF16),
        'chunk_b_s': jnp.repeat(jnp.swapaxes(chunk_b_s, 1, 2), D_CHUNK // N_CHUNK_HEADS, axis=2),
        'pool_w': _block_diag_pool(pool_w).astype(BF16), 'pool_scale': vec(pool_scale),
        'ssm_b': bmat.reshape(DEPTH, 2, D_SSM, SSM_LANES),
        'ssm_lam': lamb.reshape(DEPTH, 2, 4, SSM_LANES),
        'ssm_c': _place_c(ssm_c_re, ssm_c_im).astype(BF16),
        'ssm_h0': h0_lat,
        'ssm_d': vec(ssm_d), 'ssm_w_glu': ssm_w_glu.astype(BF16), 'ssm_b_glu': vec(ssm_b_glu),
    }
    cache_k = cache_k.reshape(DEC_BATCH, DEPTH, PAST_LEN, D_KV).astype(BF16)
    cache_v = cache_v.reshape(DEC_BATCH, DEPTH, PAST_LEN, D_KV).astype(BF16)

    x = (x_prompt.reshape(N_CTX, D_MODEL), x_sample.reshape(N_LAT, D_MODEL))
    ks, vs, s_re, s_im = [], [], [], []
    for l in range(DEPTH):
        x = _ffn_sublayer(x, mod, prm, l, 0)
        ctx_out, lat_out = _inproj(x, mod, prm, l, cos_t, sin_t)
        ua_ctx, ub_ctx, yc_ctx, k_ctx, v_ctx, yp_ctx, ya_ctx = ctx_out
        ua_lat, ub_lat, yc_lat, k_lat, v_lat, p_lat, q_lat = lat_out
        yp_lat = _pool_lat(p_lat, prm, l)
        y_ssm, fin = _ssm((ua_ctx, ub_ctx), (ua_lat, ub_lat), prm, l)
        ya_lat = _attn_lat(q_lat, k_lat, v_lat, cache_k, cache_v, l)
        x = _ffn_sublayer(x, mod, prm, l, 1, split_out=(l == DEPTH - 1),
                          mixer=((yp_ctx, yp_lat), y_ssm, (ya_ctx, ya_lat), (yc_ctx, yc_lat)))
        ks.append(k_ctx.reshape(BATCH, SEQ, N_KV_HEADS, HEAD_DIM))
        vs.append(v_ctx.reshape(BATCH, SEQ, N_KV_HEADS, HEAD_DIM))
        f_re, f_im = _ssm_unpack_state(jnp.transpose(fin, (0, 2, 1, 3)).reshape(BATCH, 2, SSM_LANES))
        s_re.append(f_re)
        s_im.append(f_im)
    y_p = x[0].reshape(BATCH, SEQ, D_MODEL)
    y_s = x[1].reshape(DEC_BATCH, DEC_SEQ, D_MODEL)
    return (y_p, y_s, jnp.stack(ks, axis=1), jnp.stack(vs, axis=1),
            jnp.stack(s_re, axis=1), jnp.stack(s_im, axis=1))
```

```python
import functools
import math

import jax
import jax.numpy as jnp
from jax import lax
from jax.experimental import pallas as pl
from jax.experimental.pallas import tpu as pltpu

F32 = jnp.float32
BF16 = jnp.bfloat16

D_MODEL = 1024
BATCH = 32
SEQ = 256
DEPTH = 2
DEC_BATCH = 2
DEC_SEQ = 2048
PAST_LEN = 512
GRID_W = 64
D_POOL = 256
D_SSM = 256
D_ATTN = 256
D_CHUNK = 256
D_HEADS = 256
POOL_WINDOWS = (2, 4, 8, 16)
POOL_GROUP = 64
SSM_GROUP = 16
N_SSM_GROUPS = 16
SSM_STATE = 64
HEAD_DIM = 64
N_HEADS = 4
N_KV_HEADS = 2
D_KV = 128
CHUNK = 128
N_CHUNK_HEADS = 4
D_FF = 2816
N_MOD = 9
D_IN = 1536
ALPHA = (2 * DEPTH) ** 0.25
LN_EPS = 1e-5
RMS_EPS = 1e-6
ROPE_THETA = 10000.0

LANES = 128
SUBLANES = 8
TM = 512
FFN_SUB = 256
ATT_TQ = 512
N_CTX = BATCH * SEQ
N_LAT = DEC_BATCH * DEC_SEQ
N_TOK = N_CTX + N_LAT
CTX_TILES = N_CTX // TM
LAT_TILES_PER_SEQ = DEC_SEQ // TM
N_TILES = N_TOK // TM
N_COND = 8
MOD_COLS = 2304
POOL_PAD = 8
SSM_LANES = 2 * N_SSM_GROUPS * SSM_STATE
SSM_W = 1024
SSM_LAT_SEGS = SUBLANES
SSM_LAT_SEG = DEC_SEQ // SSM_LAT_SEGS
SSM_ROWS = 1024
SSM_BLOCK = 512
VMEM_LIMIT = 56 * 1024 * 1024


def _params(n_grid):
    return pltpu.CompilerParams(dimension_semantics=("arbitrary",) * n_grid,
                                vmem_limit_bytes=VMEM_LIMIT)


def _fixed_spec(tail, *lead):
    idx = tuple(lead) + (0,) * len(tail)
    return pl.BlockSpec((None,) * len(lead) + tuple(tail), lambda *_: idx, pipeline_mode=pl.Buffered(1))


def _cond_row(i):
    return jnp.where(i < CTX_TILES, 0, 1 + (i - CTX_TILES) // LAT_TILES_PER_SEQ)


def _mod_spec(l):
    return pl.BlockSpec((None, None, N_MOD, D_MODEL), lambda i: (l, _cond_row(i), 0, 0))


def _tok_tile(i):
    return (i, 0)


def _ctx_tile(i):
    return (jnp.minimum(i, CTX_TILES - 1), 0)


def _lat_tile(i):
    return (jnp.maximum(i - CTX_TILES, 0), 0)


def _layer_norm(y, g, b):
    mu = jnp.mean(y, axis=-1, keepdims=True)
    d = y - mu
    var = jnp.mean(d * d, axis=-1, keepdims=True)
    return d * lax.rsqrt(var + LN_EPS) * g + b


def _gelu(x):
    return 0.5 * x * (1.0 + jnp.tanh(math.sqrt(2.0 / math.pi) * (x + 0.044715 * (x * x * x))))


def _silu(x):
    return x * jax.nn.sigmoid(x)


def _split_bf16(a):
    hi = a.astype(BF16)
    return hi, (a - hi.astype(F32)).astype(BF16)


def _mod_kernel(cond_ref, w_ref, b_ref, o_ref):
    a_hi, a_lo = _split_bf16(_silu(cond_ref[...]))
    w_hi, w_lo = _split_bf16(w_ref[...])
    acc = jnp.dot(a_hi, w_lo, preferred_element_type=F32)
    acc = acc + jnp.dot(a_lo, w_hi, preferred_element_type=F32)
    acc = acc + jnp.dot(a_hi, w_hi, preferred_element_type=F32)
    o_ref[...] = acc + b_ref[...]


def _modulation(cond, w_mod, b_mod):
    tn = MOD_COLS
    out = pl.pallas_call(
        _mod_kernel,
        grid=(DEPTH, N_MOD * D_MODEL // tn),
        in_specs=[
            pl.BlockSpec((N_COND, D_MODEL), lambda l, j: (0, 0)),
            pl.BlockSpec((None, D_MODEL, tn), lambda l, j: (l, 0, j)),
            pl.BlockSpec((None, 1, tn), lambda l, j: (l, 0, j)),
        ],
        out_specs=pl.BlockSpec((None, N_COND, tn), lambda l, j: (l, 0, j)),
        out_shape=jax.ShapeDtypeStruct((DEPTH, N_COND, N_MOD * D_MODEL), F32),
        compiler_params=_params(2),
        name="modulation",
    )(cond, w_mod, b_mod.reshape(DEPTH, 1, N_MOD * D_MODEL))
    return out.reshape(DEPTH, N_COND, N_MOD, D_MODEL)


def _ffn_kernel(*refs, mod_off, split_in, split_out, mixer_out):
    n_in = 2 if split_in else 1
    n_mix = 11 if mixer_out else 0
    mix_refs = refs[n_in:n_in + n_mix]
    mod_ref, w1_ref, w2_ref, g_ref, b_ref = refs[n_in + n_mix:n_in + n_mix + 5]
    out_refs = refs[n_in + n_mix + 5:]
    is_ctx = pl.program_id(0) < CTX_TILES
    sh = mod_ref[mod_off:mod_off + 1, :]
    sc = mod_ref[mod_off + 1:mod_off + 2, :]
    gate = mod_ref[mod_off + 2:mod_off + 3, :]

    def load_x(rows):
        if split_in:
            return jnp.where(is_ctx, refs[0][rows, :], refs[1][rows, :])
        return refs[0][rows, :]

    def project(rows):
        if not mixer_out:
            return None
        wo_ref = mix_refs[8]
        parts = [jnp.where(is_ctx, mix_refs[2 * j][rows, :], mix_refs[2 * j + 1][rows, :])
                 for j in range(4)]
        y = None
        for j, part in enumerate(parts):
            r = jnp.dot(part.astype(BF16), wo_ref[D_HEADS * j:D_HEADS * (j + 1), :],
                        preferred_element_type=F32)
            y = r if y is None else y + r
        return y

    def mixer_norm(x, y):
        if not mixer_out:
            return x
        go_ref, bo_ref = mix_refs[9:]
        return _layer_norm(ALPHA * x + mod_ref[5:6, :] * y, go_ref[...], bo_ref[...])

    def up(x):
        h = (x * (1.0 + sc) + sh).astype(BF16)
        return jnp.dot(h, w1_ref[...], preferred_element_type=F32)

    def down(gu):
        a = (_silu(gu[:, :D_FF]) * gu[:, D_FF:]).astype(BF16)
        return jnp.dot(a, w2_ref[...], preferred_element_type=F32)

    def finish(x, f):
        return _layer_norm(ALPHA * x + (0.5 * gate) * f, g_ref[...], b_ref[...])

    row_slices = [slice(j * FFN_SUB, (j + 1) * FFN_SUB) for j in range(TM // FFN_SUB)]
    n = len(row_slices)
    ys = []
    x = mixer_norm(load_x(row_slices[0]), project(row_slices[0]))
    gu = up(x)
    for j in range(n):
        y_next = project(row_slices[j + 1]) if j + 1 < n else None
        f = down(gu)
        if j + 1 < n:
            x_next = mixer_norm(load_x(row_slices[j + 1]), y_next)
            gu = up(x_next)
        ys.append(finish(x, f))
        if j + 1 < n:
            x = x_next

    def store(o_ref):
        for j, rows in enumerate(row_slices):
            o_ref[rows, :] = ys[j]

    if split_out:
        @pl.when(is_ctx)
        def _():
            store(out_refs[0])

        @pl.when(jnp.logical_not(is_ctx))
        def _():
            store(out_refs[1])
    else:
        store(out_refs[0])


def _ffn_sublayer(xs, mod, prm, l, sub, split_out=False, mixer=None):
    split_in = isinstance(xs, tuple)
    if split_in:
        x_specs = [pl.BlockSpec((TM, D_MODEL), _ctx_tile), pl.BlockSpec((TM, D_MODEL), _lat_tile)]
    else:
        xs = (xs,)
        x_specs = [pl.BlockSpec((TM, D_MODEL), _tok_tile)]
    mix_args, mix_specs = (), []
    if mixer is not None:
        y_pool, y_ssm, y_attn, y_chunk = mixer
        mix_args = (*y_pool, *y_ssm, *y_attn, *y_chunk, prm['w_out'], prm['ln_g'], prm['ln_b'])
        mix_specs = [pl.BlockSpec((TM, D_HEADS), _ctx_tile), pl.BlockSpec((TM, D_HEADS), _lat_tile)] * 4 + [
            _fixed_spec((D_MODEL, D_MODEL), l), _fixed_spec((1, D_MODEL), l, 1), _fixed_spec((1, D_MODEL), l, 1)]
    if split_out:
        out_specs = [pl.BlockSpec((TM, D_MODEL), _ctx_tile), pl.BlockSpec((TM, D_MODEL), _lat_tile)]
        out_shape = [jax.ShapeDtypeStruct((N_CTX, D_MODEL), F32), jax.ShapeDtypeStruct((N_LAT, D_MODEL), F32)]
    else:
        out_specs = pl.BlockSpec((TM, D_MODEL), _tok_tile)
        out_shape = jax.ShapeDtypeStruct((N_TOK, D_MODEL), F32)
    ln_idx = 2 * sub
    return pl.pallas_call(
        functools.partial(_ffn_kernel, mod_off=6 * sub, split_in=split_in, split_out=split_out,
                          mixer_out=mixer is not None),
        grid=(N_TILES,),
        in_specs=x_specs + mix_specs + [
            _mod_spec(l),
            _fixed_spec((D_MODEL, 2 * D_FF), l, sub),
            _fixed_spec((D_FF, D_MODEL), l, sub),
            _fixed_spec((1, D_MODEL), l, ln_idx),
            _fixed_spec((1, D_MODEL), l, ln_idx),
        ],
        out_specs=out_specs,
        out_shape=out_shape,
        compiler_params=_params(1),
        name="ffn_sublayer",
    )(*xs, *mix_args, mod, prm['ffn_w1'], prm['ffn_w2'], prm['ln_g'], prm['ln_b'])


def _pool_math(x, w_bd, scale, seq_len):
    n = seq_len + 2 * POOL_PAD
    zpad = jnp.zeros((POOL_PAD, D_POOL), F32)
    e = jnp.concatenate([zpad, x, zpad], axis=0)

    def prev(a, d):
        return pltpu.roll(a, d, axis=0)

    def nxt(a, d):
        return pltpu.roll(a, n - d, axis=0)

    s2 = e + prev(e, 1)
    s4 = prev(s2, 1) + nxt(s2, 1)
    s8 = prev(s4, 2) + nxt(s4, 2)
    s16 = prev(s8, 4) + nxt(s8, 4)
    lane = lax.broadcasted_iota(jnp.int32, (seq_len, D_POOL), 1)
    grp = lane // POOL_GROUP
    sl = slice(POOL_PAD, POOL_PAD + seq_len)
    s = jnp.where(grp == 0, s2[sl], jnp.where(grp == 1, s4[sl], jnp.where(grp == 2, s8[sl], s16[sl])))
    half = jnp.where(grp == 0, 1, jnp.where(grp == 1, 2, jnp.where(grp == 2, 4, 8)))
    t = lax.broadcasted_iota(jnp.int32, (seq_len, D_POOL), 0)
    cnt = jnp.minimum(t + half, seq_len) - jnp.maximum(t - half, 0)
    d = (s / cnt.astype(F32) - x).astype(BF16)
    return jnp.dot(d, w_bd, preferred_element_type=F32) * scale


def _attn_math(problems, mxu_sums):
    nt = (((1,), (1,)), ((), ()))
    tq = problems[0][0].shape[0]
    lo = lax.broadcasted_iota(jnp.int32, (tq, LANES), 1) < HEAD_DIM
    scores = []
    for q, k, _ in problems:
        q = q * (HEAD_DIM ** -0.5 * math.log2(math.e))
        q_lo = q[:, :LANES]
        q_hi = q[:, LANES:]
        qs = [jnp.where(lo, q_lo, 0.0), jnp.where(lo, pltpu.roll(q_lo, HEAD_DIM, axis=1), 0.0),
              jnp.where(lo, 0.0, pltpu.roll(q_hi, HEAD_DIM, axis=1)), jnp.where(lo, 0.0, q_hi)]
        for g in range(N_KV_HEADS):
            qq = jnp.concatenate([qs[2 * g], qs[2 * g + 1]], axis=0).astype(BF16)
            scores.append(lax.dot_general(qq, k, nt, preferred_element_type=F32))
    results = []
    for i, (_, _, v) in enumerate(problems):
        if mxu_sums:
            lo_k = lax.broadcasted_iota(jnp.int32, v.shape, 1) < HEAD_DIM
            one = jnp.ones_like(v)
            v_g = [jnp.where(lo_k, v, one), jnp.where(lo_k, one, v)]
        outs = []
        for g in range(N_KV_HEADS):
            s = scores[N_KV_HEADS * i + g]
            p = jnp.exp2(s - jnp.max(s, axis=-1, keepdims=True))
            if mxu_sums:
                o = jnp.dot(p.astype(BF16), v_g[g], preferred_element_type=F32)
            else:
                l = jnp.sum(p, axis=-1, keepdims=True)
                o = jnp.dot(p.astype(BF16), v, preferred_element_type=F32) / l
            outs += [o[:tq], o[tq:]]
        if mxu_sums:
            r = [pltpu.roll(o, HEAD_DIM, axis=1) for o in outs]
            out_lo = jnp.where(lo, outs[0], r[1]) / jnp.where(lo, r[0], outs[1])
            out_hi = jnp.where(lo, r[2], outs[3]) / jnp.where(lo, outs[2], r[3])
        else:
            out_lo = jnp.where(lo, outs[0], pltpu.roll(outs[1], HEAD_DIM, axis=1))
            out_hi = jnp.where(lo, pltpu.roll(outs[2], HEAD_DIM, axis=1), outs[3])
        results.append(jnp.concatenate([out_lo, out_hi], axis=1))
    return results


def _seg_rms(x, gain, n_lanes):
    parts = []
    for j in range(n_lanes // LANES):
        xs = x[:, j * LANES:(j + 1) * LANES]
        sq = xs * xs
        lo = lax.broadcasted_iota(jnp.int32, xs.shape, 1) < HEAD_DIM
        s_all = jnp.sum(sq, axis=-1, keepdims=True)
        s_lo = jnp.sum(jnp.where(lo, sq, 0.0), axis=-1, keepdims=True)
        ms = jnp.where(lo, s_lo, s_all - s_lo) * (1.0 / HEAD_DIM)
        parts.append(xs * lax.rsqrt(ms + RMS_EPS))
    y = parts[0] if len(parts) == 1 else jnp.concatenate(parts, axis=1)
    return y * gain


def _rope(x, cos_t, sin_t, n_lanes):
    parts = []
    for j in range(n_lanes // LANES):
        xs = x[:, j * LANES:(j + 1) * LANES]
        first = (lax.broadcasted_iota(jnp.int32, xs.shape, 1) % HEAD_DIM) < (HEAD_DIM // 2)
        partner = jnp.where(first, pltpu.roll(xs, LANES - HEAD_DIM // 2, axis=1),
                            pltpu.roll(xs, HEAD_DIM // 2, axis=1))
        cs = cos_t[:, j * LANES:(j + 1) * LANES]
        sn = sin_t[:, j * LANES:(j + 1) * LANES]
        parts.append(xs * cs + partner * sn)
    return parts[0] if len(parts) == 1 else jnp.concatenate(parts, axis=1)


def _inproj_kernel(*refs, ctx):
    if ctx:
        (x_ref, mod_ref, w_ref, qn_ref, kn_ref, cg_ref, cb_ref, ws_ref, bs_ref, pw_ref, ps_ref,
         ua_ref, ub_ref, chunk_ref, k_ref, v_ref, yp_ref, ya_ref) = refs
    else:
        (x_ref, mod_ref, w_ref, qn_ref, kn_ref, cg_ref, cb_ref, ws_ref, bs_ref, cos_ref, sin_ref,
         ua_ref, ub_ref, chunk_ref, k_ref, v_ref, p_ref, q_ref) = refs
    x = x_ref[...]
    sh = mod_ref[3:4, :]
    sc = mod_ref[4:5, :]
    h = (x * (1.0 + sc) + sh).astype(BF16)
    proj = jnp.dot(h, w_ref[...], preferred_element_type=F32)
    p = proj[:, 0:256]
    ua_ref[...] = proj[:, 256:384]
    ub_ref[...] = proj[:, 384:512]
    q = _seg_rms(proj[:, 512:768], qn_ref[...], D_ATTN)
    k = _seg_rms(proj[:, 768:896], kn_ref[...], D_KV)
    v = proj[:, 896:1024]
    if not ctx:
        cos_t = cos_ref[...]
        sin_t = sin_ref[...]
        q = _rope(q, cos_t, sin_t, D_ATTN)
        k = _rope(k, cos_t, sin_t, D_KV)
    zu = _gelu(proj[:, 1024:1280])
    zv = _layer_norm(_gelu(proj[:, 1280:1536]), cg_ref[...], cb_ref[...])
    head = lax.broadcasted_iota(jnp.int32, (CHUNK, D_CHUNK), 1) // (D_CHUNK // N_CHUNK_HEADS)
    for c in range(TM // CHUNK):
        vb = zv[c * CHUNK:(c + 1) * CHUNK, :].astype(BF16)
        mixed = bs_ref[...]
        for hd in range(N_CHUNK_HEADS):
            r = jnp.dot(ws_ref[hd], vb, preferred_element_type=F32)
            mixed = mixed + jnp.where(head == hd, r, 0.0)
        chunk_ref[c * CHUNK:(c + 1) * CHUNK, :] = zu[c * CHUNK:(c + 1) * CHUNK, :] * mixed
    if ctx:
        k_ref[...] = k.T
        v_ref[...] = v.T
        for s in range(TM // SEQ):
            r = slice(s * SEQ, (s + 1) * SEQ)
            yp_ref[r, :] = _pool_math(p[r], pw_ref[...], ps_ref[...], SEQ)
            ya_ref[r, :] = _attn_math([(q[r], k[r].astype(BF16), v[r].astype(BF16))], False)[0]
    else:
        k_ref[...] = k.astype(BF16)
        v_ref[...] = v.astype(BF16)
        p_ref[...] = p
        q_ref[...] = q


def _inproj(x, mod, prm, l, cos_t, sin_t):
    tok = lambda i: (i, 0)
    common = [_fixed_spec((D_MODEL, D_IN), l), _fixed_spec((1, D_ATTN), l), _fixed_spec((1, D_KV), l),
              _fixed_spec((1, D_CHUNK), l), _fixed_spec((1, D_CHUNK), l),
              _fixed_spec((N_CHUNK_HEADS, CHUNK, CHUNK), l), _fixed_spec((CHUNK, D_CHUNK), l)]
    common_args = (prm['w_in'], prm['q_norm'], prm['k_norm'], prm['chunk_ln_g'], prm['chunk_ln_b'],
                   prm['chunk_w_s'], prm['chunk_b_s'])
    mod_block = (None, None, N_MOD, D_MODEL)

    def outs(rows, kv_dtype):
        widths = [(LANES, F32), (LANES, F32), (D_CHUNK, F32), (D_KV, kv_dtype), (D_KV, kv_dtype),
                  (D_HEADS, F32), (D_HEADS, F32)]
        return ([jax.ShapeDtypeStruct((rows, w), dt) for w, dt in widths],
                [pl.BlockSpec((TM, w), tok) for w, _ in widths])

    ctx_shapes, ctx_specs = outs(N_CTX, F32)
    for j in (3, 4):
        ctx_shapes[j] = jax.ShapeDtypeStruct((D_KV, N_CTX), F32)
        ctx_specs[j] = pl.BlockSpec((D_KV, TM), lambda i: (0, i))
    ctx = pl.pallas_call(
        functools.partial(_inproj_kernel, ctx=True),
        grid=(CTX_TILES,),
        in_specs=[pl.BlockSpec((TM, D_MODEL), tok), pl.BlockSpec(mod_block, lambda i: (l, 0, 0, 0))]
                 + common + [_fixed_spec((D_POOL, D_POOL), l), _fixed_spec((1, D_POOL), l)],
        out_specs=ctx_specs,
        out_shape=ctx_shapes,
        compiler_params=_params(1),
        name="mixer_inproj_ctx",
    )(x, mod, *common_args, prm['pool_w'], prm['pool_scale'])
    lat_shapes, lat_specs = outs(N_LAT, BF16)
    rope_spec = pl.BlockSpec((TM, D_ATTN), lambda i: (i % LAT_TILES_PER_SEQ, 0))
    lat = pl.pallas_call(
        functools.partial(_inproj_kernel, ctx=False),
        grid=(N_TILES - CTX_TILES,),
        in_specs=[pl.BlockSpec((TM, D_MODEL), lambda i: (CTX_TILES + i, 0)),
                  pl.BlockSpec(mod_block, lambda i: (l, 1 + i // LAT_TILES_PER_SEQ, 0, 0))]
                 + common + [rope_spec, rope_spec],
        out_specs=lat_specs,
        out_shape=lat_shapes,
        compiler_params=_params(1),
        name="mixer_inproj_lat",
    )(x, mod, *common_args, cos_t, sin_t)
    return ctx, lat


def _pool_kernel(p_ref, w_ref, s_ref, o_ref):
    o_ref[...] = _pool_math(p_ref[...], w_ref[...], s_ref[...], DEC_SEQ)


def _pool_lat(p_lat, prm, l):
    return pl.pallas_call(
        _pool_kernel,
        grid=(DEC_BATCH,),
        in_specs=[pl.BlockSpec((DEC_SEQ, D_POOL), lambda b: (b, 0)),
                  _fixed_spec((D_POOL, D_POOL), l), _fixed_spec((1, D_POOL), l)],
        out_specs=pl.BlockSpec((DEC_SEQ, D_POOL), lambda b: (b, 0)),
        out_shape=jax.ShapeDtypeStruct((N_LAT, D_POOL), F32),
        compiler_params=_params(1),
        name="pool_lat",
    )(p_lat, prm['pool_w'], prm['pool_scale'])


def _attn_kernel(q_ref, kc_ref, kl_ref, vc_ref, vl_ref, o_ref):
    k = jnp.concatenate([kc_ref[...], kl_ref[...]], axis=0)
    v = jnp.concatenate([vc_ref[...], vl_ref[...]], axis=0)
    o_ref[...] = _attn_math([(q_ref[...], k, v)], True)[0]


def _attn_lat(q_lat, k_lat, v_lat, cache_k, cache_v, l):
    q_per_seq = DEC_SEQ // ATT_TQ
    cache_spec = pl.BlockSpec((None, None, PAST_LEN, D_KV), lambda b, j: (b, l, 0, 0))
    seq_spec = pl.BlockSpec((DEC_SEQ, D_KV), lambda b, j: (b, 0))
    return pl.pallas_call(
        _attn_kernel,
        grid=(DEC_BATCH, q_per_seq),
        in_specs=[pl.BlockSpec((ATT_TQ, D_ATTN), lambda b, j: (b * q_per_seq + j, 0)),
                  cache_spec, seq_spec, cache_spec, seq_spec],
        out_specs=pl.BlockSpec((ATT_TQ, D_ATTN), lambda b, j: (b * q_per_seq + j, 0)),
        out_shape=jax.ShapeDtypeStruct((N_LAT, D_ATTN), F32),
        compiler_params=_params(2),
        name="attn_lat",
    )(q_lat, cache_k, k_lat, cache_v, v_lat)


def _ssm_prep_kernel(lam_ref, br_ref, bi_ref, bmat_ref, lamb_ref):
    lre = lam_ref[0:1, :]
    lim = lam_ref[1:2, :]
    dt = jnp.exp(lam_ref[2:3, :])
    mag = jnp.exp(lre * dt)
    ar = mag * jnp.cos(lim * dt)
    ai = mag * jnp.sin(lim * dt)
    lamb_ref[0:1, :] = ar
    lamb_ref[1:2, :] = ai
    mag_s = jnp.exp(lre * dt * SSM_LAT_SEG)
    lamb_ref[2:3, :] = mag_s * jnp.cos(lim * dt * SSM_LAT_SEG)
    lamb_ref[3:4, :] = mag_s * jnp.sin(lim * dt * SSM_LAT_SEG)
    den = lre * lre + lim * lim
    cr = ((ar - 1.0) * lre + ai * lim) / den
    ci = (ai * lre - (ar - 1.0) * lim) / den
    n_rep = SSM_LANES // LANES
    br = jnp.concatenate([br_ref[...]] * n_rep, axis=1)
    bi = jnp.concatenate([bi_ref[...]] * n_rep, axis=1)
    lane = lax.broadcasted_iota(jnp.int32, br.shape, 1)
    row_g = lax.broadcasted_iota(jnp.int32, br.shape, 0) // SSM_GROUP
    lane_g = 2 * (lane // (2 * LANES)) + (lane % LANES) // SSM_STATE
    is_re = (lane % (2 * LANES)) < LANES
    b_bar = jnp.where(is_re, cr * br - ci * bi, cr * bi + ci * br)
    bmat_ref[...] = jnp.where(row_g == lane_g, b_bar, 0.0).astype(BF16)


def _ssm_prep(lam_rows, b_re_placed, b_im_placed):
    n = DEPTH * 2
    return pl.pallas_call(
        _ssm_prep_kernel,
        grid=(n,),
        in_specs=[pl.BlockSpec((None, 3, SSM_LANES), lambda i: (i, 0, 0)),
                  pl.BlockSpec((None, D_SSM, LANES), lambda i: (i, 0, 0)),
                  pl.BlockSpec((None, D_SSM, LANES), lambda i: (i, 0, 0))],
        out_specs=[pl.BlockSpec((None, D_SSM, SSM_LANES), lambda i: (i, 0, 0)),
                   pl.BlockSpec((None, 4, SSM_LANES), lambda i: (i, 0, 0))],
        out_shape=[jax.ShapeDtypeStruct((n, D_SSM, SSM_LANES), BF16),
                   jax.ShapeDtypeStruct((n, 4, SSM_LANES), F32)],
        compiler_params=_params(1),
        name="ssm_prep",
    )(lam_rows, b_re_placed, b_im_placed)


def _ssm_kernel(ua_ref, ub_ref, h0_ref, bmat_ref, cmat_ref, lamb_ref, d_ref, wg_ref, bg_ref,
                y_ref, fin_ref, ui_ref, buf0_ref, buf1_ref, yi_ref, *, n_per):
    seg_len = SEQ
    ssm_w = SSM_W
    n_rows = SUBLANES * seg_len
    n_cb = ssm_w // (2 * LANES)

    def row_chunks(fn):
        def body(c, carry):
            fn(pl.ds(pl.multiple_of(c * SSM_ROWS, SSM_ROWS), SSM_ROWS))
            return carry
        lax.fori_loop(0, n_rows // SSM_ROWS, body, 0)

    def step_rows(t):
        return pl.ds(pl.multiple_of(t * SUBLANES, SUBLANES), SUBLANES)

    def interleave(t, carry):
        ui_ref[step_rows(t), 0:LANES] = ua_ref[pl.ds(t, SUBLANES, stride=seg_len), :]
        ui_ref[step_rows(t), LANES:2 * LANES] = ub_ref[pl.ds(t, SUBLANES, stride=seg_len), :]
        return carry

    lax.fori_loop(0, seg_len, interleave, 0, unroll=4)

    seg = lax.broadcasted_iota(jnp.int32, (SUBLANES, LANES), 0) % n_per

    chains = [(dirn, k) for dirn in range(2) for k in range(n_cb)]
    n_parts = SSM_LANES // ssm_w
    bufs = (buf0_ref, buf1_ref)
    assert n_parts == len(bufs)

    def fill(part, rows):
        ub = ui_ref[rows, :].astype(BF16)
        for dirn, k in chains:
            c0 = k * 2 * LANES
            lane0 = part * ssm_w + c0
            bufs[part][dirn, rows, c0:c0 + 2 * LANES] = jnp.dot(
                ub, bmat_ref[dirn, :, lane0:lane0 + 2 * LANES], preferred_element_type=F32)

    def readout(part, rows):
        lo = part * ssm_w
        y = jnp.dot(bufs[part][0, rows, :].astype(BF16), cmat_ref[0, lo:lo + ssm_w, :],
                    preferred_element_type=F32)
        y = y + jnp.dot(bufs[part][1, rows, :].astype(BF16), cmat_ref[1, lo:lo + ssm_w, :],
                        preferred_element_type=F32)
        if part == 0:
            y = y + d_ref[...] * ui_ref[rows, :]
            yi_ref[0, rows, :] = y[:, :LANES]
            yi_ref[1, rows, :] = y[:, LANES:]
        else:
            yi_ref[0, rows, :] += y[:, :LANES]
            yi_ref[1, rows, :] += y[:, LANES:]

    def lam_rows(part, r):
        out = []
        for dirn, k in chains:
            c0 = part * ssm_w + k * 2 * LANES
            out.append(jnp.broadcast_to(lamb_ref[dirn, r:r + 1, c0:c0 + LANES], (SUBLANES, LANES)))
        return out

    def scan(part, init, store, side_work=None):
        buf = bufs[part]
        a_re, a_im = lam_rows(part, 0), lam_rows(part, 1)

        def step(i, hs):
            rows = (step_rows(i), step_rows(seg_len - 1 - i))
            new = []
            for c, (dirn, k) in enumerate(chains):
                re_sl = slice(k * 2 * LANES, k * 2 * LANES + LANES)
                im_sl = slice(k * 2 * LANES + LANES, (k + 1) * 2 * LANES)
                hr, hi = hs[2 * c], hs[2 * c + 1]
                nr = a_re[c] * hr - a_im[c] * hi + buf[dirn, rows[dirn], re_sl]
                ni = a_re[c] * hi + a_im[c] * hr + buf[dirn, rows[dirn], im_sl]
                if store:
                    buf[dirn, rows[dirn], re_sl] = nr
                    buf[dirn, rows[dirn], im_sl] = ni
                new += [nr, ni]
            return tuple(new)

        if side_work is None:
            return lax.fori_loop(0, seg_len, step, tuple(init), unroll=2)

        steps = SSM_BLOCK // SUBLANES

        def block(blk, hs):
            side_work(pl.ds(pl.multiple_of(blk * SSM_BLOCK, SSM_BLOCK), SSM_BLOCK))
            for j in range(steps):
                hs = step(blk * steps + j, hs)
            return hs

        return lax.fori_loop(0, n_rows // SSM_BLOCK, block, tuple(init))

    def initial_states(part, local):
        h0 = []
        for dirn, k in chains:
            c0 = part * ssm_w + k * 2 * LANES
            h0 += [h0_ref[dirn, :, c0:c0 + LANES], h0_ref[dirn, :, c0 + LANES:c0 + 2 * LANES]]
        if local is None:
            return h0
        s_re, s_im = lam_rows(part, 2), lam_rows(part, 3)
        init = []
        for c, (dirn, k) in enumerate(chains):
            edge = seg == (0 if dirn == 0 else n_per - 1)
            shift = 1 if dirn == 0 else SUBLANES - 1
            cr, ci = h0[2 * c], h0[2 * c + 1]
            lr = pltpu.roll(local[2 * c], shift, axis=0)
            li = pltpu.roll(local[2 * c + 1], shift, axis=0)
            for _ in range(n_per - 1):
                pr = pltpu.roll(cr, shift, axis=0)
                pi = pltpu.roll(ci, shift, axis=0)
                cr = jnp.where(edge, h0[2 * c], s_re[c] * pr - s_im[c] * pi + lr)
                ci = jnp.where(edge, h0[2 * c + 1], s_re[c] * pi + s_im[c] * pr + li)
            init += [cr, ci]
        return init

    zeros = [jnp.zeros((SUBLANES, LANES), F32)] * (2 * len(chains))
    row_chunks(functools.partial(fill, 0))
    for part in range(n_parts):
        fill_next = functools.partial(fill, part + 1) if part + 1 < n_parts else None
        read_prev = functools.partial(readout, part - 1) if part > 0 else None
        if n_per == 1:
            if part > 0 and fill_next is not None:
                row_chunks(fill_next)
            hs = scan(part, initial_states(part, None), True, fill_next if part == 0 else read_prev)
        else:
            local = scan(part, zeros, False, fill_next)
            hs = scan(part, initial_states(part, local), True, read_prev)
        for c, (dirn, k) in enumerate(chains):
            c0 = part * ssm_w + k * 2 * LANES
            fin_ref[dirn, :, c0:c0 + LANES] = hs[2 * c]
            fin_ref[dirn, :, c0 + LANES:c0 + 2 * LANES] = hs[2 * c + 1]
    row_chunks(functools.partial(readout, n_parts - 1))

    def glu(rows):
        g = _gelu(jnp.concatenate([yi_ref[0, rows, :], yi_ref[1, rows, :]], axis=1))
        z = jnp.dot(g.astype(BF16), wg_ref[...], preferred_element_type=F32) + bg_ref[...]
        o = g * jax.nn.sigmoid(z)
        yi_ref[0, rows, :] = o[:, :LANES]
        yi_ref[1, rows, :] = o[:, LANES:]

    row_chunks(glu)

    def deinterleave(tb, carry):
        for s_idx in range(SUBLANES):
            dst = pl.ds(pl.multiple_of(s_idx * seg_len + tb * SUBLANES, SUBLANES), SUBLANES)
            src = pl.ds(tb * SUBLANES * SUBLANES + s_idx, SUBLANES, stride=SUBLANES)
            y_ref[dst, 0:LANES] = yi_ref[0, src, :]
            y_ref[dst, LANES:2 * LANES] = yi_ref[1, src, :]
        return carry

    lax.fori_loop(0, seg_len // SUBLANES, deinterleave, 0)


def _ssm(u_ctx, u_lat, prm, l):
    weights = [_fixed_spec((2, D_SSM, SSM_LANES), l), _fixed_spec((2, SSM_LANES, D_SSM), l),
               _fixed_spec((2, 4, SSM_LANES), l), _fixed_spec((1, D_SSM), l),
               _fixed_spec((D_SSM, D_SSM), l), _fixed_spec((1, D_SSM), l)]
    rows = SUBLANES * SEQ
    scratch = [pltpu.VMEM((rows, D_SSM), F32), pltpu.VMEM((2, rows, SSM_W), F32),
               pltpu.VMEM((2, rows, SSM_W), F32), pltpu.VMEM((2, rows, LANES), F32)]
    tile_a = pl.BlockSpec((rows, LANES), lambda i: (i, 0))
    tile_y = pl.BlockSpec((rows, D_SSM), lambda i: (i, 0))
    fin_spec = pl.BlockSpec((None, 2, SUBLANES, SSM_LANES), lambda i: (i, 0, 0, 0))

    tail = (prm['ssm_b'], prm['ssm_c'], prm['ssm_lam'], prm['ssm_d'], prm['ssm_w_glu'], prm['ssm_b_glu'])
    n_ctx_tiles = N_CTX // rows
    h0_ctx = jnp.zeros((2, SUBLANES, SSM_LANES), F32)
    y_ctx, fin_ctx = pl.pallas_call(
        functools.partial(_ssm_kernel, n_per=1),
        grid=(n_ctx_tiles,),
        in_specs=[tile_a, tile_a, _fixed_spec((2, SUBLANES, SSM_LANES))] + weights,
        out_specs=[tile_y, fin_spec],
        out_shape=[jax.ShapeDtypeStruct((N_CTX, D_SSM), F32),
                   jax.ShapeDtypeStruct((n_ctx_tiles, 2, SUBLANES, SSM_LANES), F32)],
        scratch_shapes=scratch,
        compiler_params=_params(1),
        name="ssm_ctx",
    )(*u_ctx, h0_ctx, *tail)
    h0_spec = pl.BlockSpec((None, None, 2, SUBLANES, SSM_LANES), lambda b: (l, b, 0, 0, 0))
    y_lat, _ = pl.pallas_call(
        functools.partial(_ssm_kernel, n_per=SSM_LAT_SEGS),
        grid=(DEC_BATCH,),
        in_specs=[tile_a, tile_a, h0_spec] + weights,
        out_specs=[tile_y, fin_spec],
        out_shape=[jax.ShapeDtypeStruct((N_LAT, D_SSM), F32),
                   jax.ShapeDtypeStruct((DEC_BATCH, 2, SUBLANES, SSM_LANES), F32)],
        scratch_shapes=scratch,
        compiler_params=_params(1),
        name="ssm_lat",
    )(*u_lat, prm['ssm_h0'], *tail)
    return (y_ctx, y_lat), fin_ctx


def _rope_tables():
    rows = DEC_SEQ // GRID_W
    row_idx = jnp.repeat(jnp.arange(rows), GRID_W).astype(F32)
    col_idx = jnp.tile(jnp.arange(GRID_W), rows).astype(F32)
    n_freq = HEAD_DIM // 4
    inv = ROPE_THETA ** (-jnp.arange(n_freq, dtype=F32) / n_freq)
    ang = jnp.concatenate([row_idx[:, None] * inv, col_idx[:, None] * inv], axis=-1)
    cos = jnp.cos(ang)
    sin = jnp.sin(ang)
    cos_h = jnp.concatenate([cos, cos], axis=-1)
    sin_h = jnp.concatenate([-sin, sin], axis=-1)
    return jnp.tile(cos_h, (1, N_HEADS)), jnp.tile(sin_h, (1, N_HEADS))


def _state_lanes(a):
    return a.reshape(a.shape[:-2] + (N_SSM_GROUPS // 2, 2 * SSM_STATE))


def _ssm_lane_rows(a):
    a = _state_lanes(a)
    return jnp.stack([a, a], axis=-2).reshape(a.shape[:-2] + (SSM_LANES,))


def _ssm_pack_state(re, im):
    return jnp.stack([_state_lanes(re), _state_lanes(im)], axis=-2).reshape(re.shape[:-2] + (SSM_LANES,))


def _ssm_unpack_state(s):
    s = s.reshape(s.shape[:-1] + (N_SSM_GROUPS // 2, 2, 2 * SSM_STATE))
    shp = s.shape[:-3] + (N_SSM_GROUPS, SSM_STATE)
    return s[..., 0, :].reshape(shp), s[..., 1, :].reshape(shp)


def _ssm_group_mask():
    ch_g = jnp.arange(D_SSM) // SSM_GROUP
    lane = jnp.arange(SSM_LANES)
    lane_g = 2 * (lane // (2 * LANES)) + (lane % LANES) // SSM_STATE
    return ch_g[:, None] == lane_g[None, :]


def _b_rows(b):
    bt = jnp.swapaxes(b, -1, -2).reshape(DEPTH * 2, D_SSM, SSM_STATE)
    return jnp.concatenate([bt, bt], axis=-1)


def _place_c(c_re, c_im):
    def rows(c):
        ct = jnp.moveaxis(c, -1, -3).reshape(c.shape[:-3] + (SSM_STATE, D_SSM))
        return jnp.tile(ct, (1,) * (ct.ndim - 2) + (SSM_LANES // SSM_STATE, 1))
    is_re = (jnp.arange(SSM_LANES) % (2 * LANES) < LANES)[:, None]
    return jnp.where(_ssm_group_mask().T, jnp.where(is_re, rows(c_re), -rows(c_im)), 0.0)


def _block_diag_pool(pool_w):
    tiled = jnp.tile(pool_w.reshape(DEPTH, D_POOL, POOL_GROUP), (1, 1, len(POOL_WINDOWS)))
    grp = jnp.arange(D_POOL) // POOL_GROUP
    return jnp.where(grp[:, None] == grp[None, :], tiled, 0.0)


def kernel(x_prompt, x_sample, cache_k, cache_v, state_ssm_re, state_ssm_im, c, c_ctx, w_mod, b_mod, ln_g, ln_b, ffn_w1, ffn_w2, w_in, w_out, pool_w, pool_scale, ssm_lam_re, ssm_lam_im, ssm_log_step, ssm_b_re, ssm_b_im, ssm_c_re, ssm_c_im, ssm_d, ssm_w_glu, ssm_b_glu, q_norm, k_norm, chunk_ln_g, chunk_ln_b, chunk_w_s, chunk_b_s):
    cond = jnp.concatenate([c_ctx[None, :], c, jnp.zeros((N_COND - 1 - DEC_BATCH, D_MODEL), F32)], axis=0)
    mod = _modulation(cond, w_mod, b_mod)
    cos_t, sin_t = _rope_tables()

    log_step = jnp.broadcast_to(ssm_log_step[..., None], ssm_lam_re.shape)
    lam_rows = jnp.stack([_ssm_lane_rows(ssm_lam_re), _ssm_lane_rows(ssm_lam_im),
                          _ssm_lane_rows(log_step)], axis=-2)
    bmat, lamb = _ssm_prep(lam_rows.reshape(DEPTH * 2, 3, SSM_LANES),
                           _b_rows(ssm_b_re), _b_rows(ssm_b_im))
    h0 = _ssm_pack_state(state_ssm_re, state_ssm_im)
    h0_lat = jnp.zeros((DEPTH, DEC_BATCH, 2, SSM_LAT_SEGS, SSM_LANES), F32)
    h0_lat = h0_lat.at[:, :, 0, 0].set(jnp.swapaxes(h0[:, :, 0], 0, 1))
    h0_lat = h0_lat.at[:, :, 1, SSM_LAT_SEGS - 1].set(jnp.swapaxes(h0[:, :, 1], 0, 1))

    vec = lambda a: a[..., None, :]
    prm = {
        'ln_g': vec(ln_g), 'ln_b': vec(ln_b),
        'ffn_w1': ffn_w1.astype(BF16), 'ffn_w2': ffn_w2.astype(BF16),
        'w_in': w_in.astype(BF16), 'w_out': w_out.astype(BF16),
        'q_norm': vec(jnp.tile(q_norm, (1, N_HEADS))), 'k_norm': vec(jnp.tile(k_norm, (1, N_KV_HEADS))),
        'chunk_ln_g': vec(chunk_ln_g), 'chunk_ln_b': vec(chunk_ln_b),
        'chunk_w_s': chunk_w_s.astype(BF16),
        'chunk_b_s': jnp.repeat(jnp.swapaxes(chunk_b_s, 1, 2), D_CHUNK // N_CHUNK_HEADS, axis=2),
        'pool_w': _block_diag_pool(pool_w).astype(BF16), 'pool_scale': vec(pool_scale),
        'ssm_b': bmat.reshape(DEPTH, 2, D_SSM, SSM_LANES),
        'ssm_lam': lamb.reshape(DEPTH, 2, 4, SSM_LANES),
        'ssm_c': _place_c(ssm_c_re, ssm_c_im).astype(BF16),
        'ssm_h0': h0_lat,
        'ssm_d': vec(ssm_d), 'ssm_w_glu': ssm_w_glu.astype(BF16), 'ssm_b_glu': vec(ssm_b_glu),
    }
    cache_k = cache_k.reshape(DEC_BATCH, DEPTH, PAST_LEN, D_KV).astype(BF16)
    cache_v = cache_v.reshape(DEC_BATCH, DEPTH, PAST_LEN, D_KV).astype(BF16)

    x = (x_prompt.reshape(N_CTX, D_MODEL), x_sample.reshape(N_LAT, D_MODEL))
    ks, vs, s_re, s_im = [], [], [], []
    for l in range(DEPTH):
        x = _ffn_sublayer(x, mod, prm, l, 0)
        ctx_out, lat_out = _inproj(x, mod, prm, l, cos_t, sin_t)
        ua_ctx, ub_ctx, yc_ctx, k_ctx, v_ctx, yp_ctx, ya_ctx = ctx_out
        ua_lat, ub_lat, yc_lat, k_lat, v_lat, p_lat, q_lat = lat_out
        yp_lat = _pool_lat(p_lat, prm, l)
        y_ssm, fin = _ssm((ua_ctx, ub_ctx), (ua_lat, ub_lat), prm, l)
        ya_lat = _attn_lat(q_lat, k_lat, v_lat, cache_k, cache_v, l)
        x = _ffn_sublayer(x, mod, prm, l, 1, split_out=(l == DEPTH - 1),
                          mixer=((yp_ctx, yp_lat), y_ssm, (ya_ctx, ya_lat), (yc_ctx, yc_lat)))
        ks.append(k_ctx)
        vs.append(v_ctx)
        f_re, f_im = _ssm_unpack_state(jnp.transpose(fin, (0, 2, 1, 3)).reshape(BATCH, 2, SSM_LANES))
        s_re.append(f_re)
        s_im.append(f_im)
    y_p = x[0].reshape(BATCH, SEQ, D_MODEL)
    y_s = x[1].reshape(DEC_BATCH, DEC_SEQ, D_MODEL)
    def cache(ts):
        t = jnp.stack(ts).reshape(DEPTH, N_KV_HEADS, HEAD_DIM, BATCH, SEQ)
        return jnp.transpose(t, (3, 0, 4, 1, 2))

    return (y_p, y_s, cache(ks), cache(vs), jnp.stack(s_re, axis=1), jnp.stack(s_im, axis=1))
```

```python
import functools
import math

import jax
import jax.numpy as jnp
from jax import lax
from jax.experimental import pallas as pl
from jax.experimental.pallas import tpu as pltpu

F32 = jnp.float32
BF16 = jnp.bfloat16

D_MODEL = 1024
BATCH = 32
SEQ = 256
DEPTH = 2
DEC_BATCH = 2
DEC_SEQ = 2048
PAST_LEN = 512
GRID_W = 64
D_POOL = 256
D_SSM = 256
D_ATTN = 256
D_CHUNK = 256
D_HEADS = 256
POOL_WINDOWS = (2, 4, 8, 16)
POOL_GROUP = 64
SSM_GROUP = 16
N_SSM_GROUPS = 16
SSM_STATE = 64
HEAD_DIM = 64
N_HEADS = 4
N_KV_HEADS = 2
D_KV = 128
CHUNK = 128
N_CHUNK_HEADS = 4
D_FF = 2816
N_MOD = 9
D_IN = 1536
ALPHA = (2 * DEPTH) ** 0.25
LN_EPS = 1e-5
RMS_EPS = 1e-6
ROPE_THETA = 10000.0

LANES = 128
SUBLANES = 8
TM = 512
FFN_SUB = 256
ATT_TQ = 512
N_CTX = BATCH * SEQ
N_LAT = DEC_BATCH * DEC_SEQ
N_TOK = N_CTX + N_LAT
CTX_TILES = N_CTX // TM
LAT_TILES_PER_SEQ = DEC_SEQ // TM
N_TILES = N_TOK // TM
N_COND = 8
MOD_COLS = 2304
POOL_PAD = 8
SSM_LANES = 2 * N_SSM_GROUPS * SSM_STATE
SSM_W = 1024
SSM_LAT_SEGS = SUBLANES
SSM_LAT_SEG = DEC_SEQ // SSM_LAT_SEGS
SSM_BLOCK = 512
VMEM_LIMIT = 56 * 1024 * 1024


def _params(n_grid):
    return pltpu.CompilerParams(dimension_semantics=("arbitrary",) * n_grid,
                                vmem_limit_bytes=VMEM_LIMIT)


def _fixed_spec(tail, *lead):
    idx = tuple(lead) + (0,) * len(tail)
    return pl.BlockSpec((None,) * len(lead) + tuple(tail), lambda *_: idx, pipeline_mode=pl.Buffered(1))


def _cond_row(i):
    return jnp.where(i < CTX_TILES, 0, 1 + (i - CTX_TILES) // LAT_TILES_PER_SEQ)


def _mod_spec(l):
    return pl.BlockSpec((None, None, N_MOD, D_MODEL), lambda i: (l, _cond_row(i), 0, 0))


def _tok_tile(i):
    return (i, 0)


def _ctx_tile(i):
    return (jnp.minimum(i, CTX_TILES - 1), 0)


def _lat_tile(i):
    return (jnp.maximum(i - CTX_TILES, 0), 0)


def _layer_norm(y, g, b):
    mu = jnp.mean(y, axis=-1, keepdims=True)
    d = y - mu
    var = jnp.mean(d * d, axis=-1, keepdims=True)
    return d * lax.rsqrt(var + LN_EPS) * g + b


def _gelu(x):
    return 0.5 * x * (1.0 + jnp.tanh(math.sqrt(2.0 / math.pi) * (x + 0.044715 * (x * x * x))))


def _silu(x):
    return x * jax.nn.sigmoid(x)


def _split_bf16(a):
    hi = a.astype(BF16)
    return hi, (a - hi.astype(F32)).astype(BF16)


def _mod_kernel(cond_ref, w_ref, b_ref, o_ref):
    a_hi, a_lo = _split_bf16(_silu(cond_ref[...]))
    w_hi, w_lo = _split_bf16(w_ref[...])
    acc = jnp.dot(a_hi, w_lo, preferred_element_type=F32)
    acc = acc + jnp.dot(a_lo, w_hi, preferred_element_type=F32)
    acc = acc + jnp.dot(a_hi, w_hi, preferred_element_type=F32)
    o_ref[...] = acc + b_ref[...]


def _modulation(cond, w_mod, b_mod):
    tn = MOD_COLS
    out = pl.pallas_call(
        _mod_kernel,
        grid=(DEPTH, N_MOD * D_MODEL // tn),
        in_specs=[
            pl.BlockSpec((N_COND, D_MODEL), lambda l, j: (0, 0)),
            pl.BlockSpec((None, D_MODEL, tn), lambda l, j: (l, 0, j)),
            pl.BlockSpec((None, 1, tn), lambda l, j: (l, 0, j)),
        ],
        out_specs=pl.BlockSpec((None, N_COND, tn), lambda l, j: (l, 0, j)),
        out_shape=jax.ShapeDtypeStruct((DEPTH, N_COND, N_MOD * D_MODEL), F32),
        compiler_params=_params(2),
        name="modulation",
    )(cond, w_mod, b_mod.reshape(DEPTH, 1, N_MOD * D_MODEL))
    return out.reshape(DEPTH, N_COND, N_MOD, D_MODEL)


def _ffn_kernel(*refs, mod_off, split_in, split_out, mixer_out):
    n_in = 2 if split_in else 1
    n_mix = 11 if mixer_out else 0
    mix_refs = refs[n_in:n_in + n_mix]
    mod_ref, w1_ref, w2_ref, g_ref, b_ref = refs[n_in + n_mix:n_in + n_mix + 5]
    out_refs = refs[n_in + n_mix + 5:]
    is_ctx = pl.program_id(0) < CTX_TILES
    sh = mod_ref[mod_off:mod_off + 1, :]
    sc = mod_ref[mod_off + 1:mod_off + 2, :]
    gate = mod_ref[mod_off + 2:mod_off + 3, :]

    def load_x(rows):
        if split_in:
            return jnp.where(is_ctx, refs[0][rows, :], refs[1][rows, :])
        return refs[0][rows, :]

    def project(rows):
        if not mixer_out:
            return None
        wo_ref = mix_refs[8]
        parts = [jnp.where(is_ctx, mix_refs[2 * j][rows, :], mix_refs[2 * j + 1][rows, :])
                 for j in range(4)]
        y = None
        for j, part in enumerate(parts):
            r = jnp.dot(part.astype(BF16), wo_ref[D_HEADS * j:D_HEADS * (j + 1), :],
                        preferred_element_type=F32)
            y = r if y is None else y + r
        return y

    def mixer_norm(x, y):
        if not mixer_out:
            return x
        go_ref, bo_ref = mix_refs[9:]
        return _layer_norm(ALPHA * x + mod_ref[5:6, :] * y, go_ref[...], bo_ref[...])

    def up(x):
        h = (x * (1.0 + sc) + sh).astype(BF16)
        return jnp.dot(h, w1_ref[...], preferred_element_type=F32)

    def down(gu):
        a = (_silu(gu[:, :D_FF]) * gu[:, D_FF:]).astype(BF16)
        return jnp.dot(a, w2_ref[...], preferred_element_type=F32)

    def finish(x, f):
        return _layer_norm(ALPHA * x + (0.5 * gate) * f, g_ref[...], b_ref[...])

    row_slices = [slice(j * FFN_SUB, (j + 1) * FFN_SUB) for j in range(TM // FFN_SUB)]
    n = len(row_slices)
    ys = []
    x = mixer_norm(load_x(row_slices[0]), project(row_slices[0]))
    gu = up(x)
    for j in range(n):
        y_next = project(row_slices[j + 1]) if j + 1 < n else None
        f = down(gu)
        if j + 1 < n:
            x_next = mixer_norm(load_x(row_slices[j + 1]), y_next)
            gu = up(x_next)
        ys.append(finish(x, f))
        if j + 1 < n:
            x = x_next

    def store(o_ref):
        for j, rows in enumerate(row_slices):
            o_ref[rows, :] = ys[j]

    if split_out:
        @pl.when(is_ctx)
        def _():
            store(out_refs[0])

        @pl.when(jnp.logical_not(is_ctx))
        def _():
            store(out_refs[1])
    else:
        store(out_refs[0])


def _ffn_sublayer(xs, mod, prm, l, sub, split_out=False, mixer=None):
    split_in = isinstance(xs, tuple)
    if split_in:
        x_specs = [pl.BlockSpec((TM, D_MODEL), _ctx_tile), pl.BlockSpec((TM, D_MODEL), _lat_tile)]
    else:
        xs = (xs,)
        x_specs = [pl.BlockSpec((TM, D_MODEL), _tok_tile)]
    mix_args, mix_specs = (), []
    if mixer is not None:
        y_pool, y_ssm, y_attn, y_chunk = mixer
        mix_args = (*y_pool, *y_ssm, *y_attn, *y_chunk, prm['w_out'], prm['ln_g'], prm['ln_b'])
        mix_specs = [pl.BlockSpec((TM, D_HEADS), _ctx_tile), pl.BlockSpec((TM, D_HEADS), _lat_tile)] * 4 + [
            _fixed_spec((D_MODEL, D_MODEL), l), _fixed_spec((1, D_MODEL), l, 1), _fixed_spec((1, D_MODEL), l, 1)]
    if split_out:
        out_specs = [pl.BlockSpec((TM, D_MODEL), _ctx_tile), pl.BlockSpec((TM, D_MODEL), _lat_tile)]
        out_shape = [jax.ShapeDtypeStruct((N_CTX, D_MODEL), F32), jax.ShapeDtypeStruct((N_LAT, D_MODEL), F32)]
    else:
        out_specs = pl.BlockSpec((TM, D_MODEL), _tok_tile)
        out_shape = jax.ShapeDtypeStruct((N_TOK, D_MODEL), F32)
    ln_idx = 2 * sub
    return pl.pallas_call(
        functools.partial(_ffn_kernel, mod_off=6 * sub, split_in=split_in, split_out=split_out,
                          mixer_out=mixer is not None),
        grid=(N_TILES,),
        in_specs=x_specs + mix_specs + [
            _mod_spec(l),
            _fixed_spec((D_MODEL, 2 * D_FF), l, sub),
            _fixed_spec((D_FF, D_MODEL), l, sub),
            _fixed_spec((1, D_MODEL), l, ln_idx),
            _fixed_spec((1, D_MODEL), l, ln_idx),
        ],
        out_specs=out_specs,
        out_shape=out_shape,
        compiler_params=_params(1),
        name="ffn_sublayer",
    )(*xs, *mix_args, mod, prm['ffn_w1'], prm['ffn_w2'], prm['ln_g'], prm['ln_b'])


def _pool_math(x, w_bd, scale, seq_len):
    n = seq_len + 2 * POOL_PAD
    zpad = jnp.zeros((POOL_PAD, D_POOL), F32)
    e = jnp.concatenate([zpad, x, zpad], axis=0)

    def prev(a, d):
        return pltpu.roll(a, d, axis=0)

    def nxt(a, d):
        return pltpu.roll(a, n - d, axis=0)

    s2 = e + prev(e, 1)
    s4 = prev(s2, 1) + nxt(s2, 1)
    s8 = prev(s4, 2) + nxt(s4, 2)
    s16 = prev(s8, 4) + nxt(s8, 4)
    lane = lax.broadcasted_iota(jnp.int32, (seq_len, D_POOL), 1)
    grp = lane // POOL_GROUP
    sl = slice(POOL_PAD, POOL_PAD + seq_len)
    s = jnp.where(grp == 0, s2[sl], jnp.where(grp == 1, s4[sl], jnp.where(grp == 2, s8[sl], s16[sl])))
    half = jnp.where(grp == 0, 1, jnp.where(grp == 1, 2, jnp.where(grp == 2, 4, 8)))
    t = lax.broadcasted_iota(jnp.int32, (seq_len, D_POOL), 0)
    cnt = jnp.minimum(t + half, seq_len) - jnp.maximum(t - half, 0)
    d = (s / cnt.astype(F32) - x).astype(BF16)
    return jnp.dot(d, w_bd, preferred_element_type=F32) * scale


def _attn_math(problems, mxu_sums):
    nt = (((1,), (1,)), ((), ()))
    tq = problems[0][0].shape[0]
    lo = lax.broadcasted_iota(jnp.int32, (tq, LANES), 1) < HEAD_DIM
    scores = []
    for q, k, _ in problems:
        q = q * (HEAD_DIM ** -0.5 * math.log2(math.e))
        q_lo = q[:, :LANES]
        q_hi = q[:, LANES:]
        qs = [jnp.where(lo, q_lo, 0.0), jnp.where(lo, pltpu.roll(q_lo, HEAD_DIM, axis=1), 0.0),
              jnp.where(lo, 0.0, pltpu.roll(q_hi, HEAD_DIM, axis=1)), jnp.where(lo, 0.0, q_hi)]
        for g in range(N_KV_HEADS):
            qq = jnp.concatenate([qs[2 * g], qs[2 * g + 1]], axis=0).astype(BF16)
            scores.append(lax.dot_general(qq, k, nt, preferred_element_type=F32))
    results = []
    for i, (_, _, v) in enumerate(problems):
        if mxu_sums:
            lo_k = lax.broadcasted_iota(jnp.int32, v.shape, 1) < HEAD_DIM
            one = jnp.ones_like(v)
            v_g = [jnp.where(lo_k, v, one), jnp.where(lo_k, one, v)]
        outs = []
        for g in range(N_KV_HEADS):
            s = scores[N_KV_HEADS * i + g]
            p = jnp.exp2(s - jnp.max(s, axis=-1, keepdims=True))
            if mxu_sums:
                o = jnp.dot(p.astype(BF16), v_g[g], preferred_element_type=F32)
            else:
                l = jnp.sum(p, axis=-1, keepdims=True)
                o = jnp.dot(p.astype(BF16), v, preferred_element_type=F32) / l
            outs += [o[:tq], o[tq:]]
        if mxu_sums:
            r = [pltpu.roll(o, HEAD_DIM, axis=1) for o in outs]
            out_lo = jnp.where(lo, outs[0], r[1]) / jnp.where(lo, r[0], outs[1])
            out_hi = jnp.where(lo, r[2], outs[3]) / jnp.where(lo, outs[2], r[3])
        else:
            out_lo = jnp.where(lo, outs[0], pltpu.roll(outs[1], HEAD_DIM, axis=1))
            out_hi = jnp.where(lo, pltpu.roll(outs[2], HEAD_DIM, axis=1), outs[3])
        results.append(jnp.concatenate([out_lo, out_hi], axis=1))
    return results


def _seg_rms(x, gain, n_lanes):
    parts = []
    for j in range(n_lanes // LANES):
        xs = x[:, j * LANES:(j + 1) * LANES]
        sq = xs * xs
        lo = lax.broadcasted_iota(jnp.int32, xs.shape, 1) < HEAD_DIM
        s_all = jnp.sum(sq, axis=-1, keepdims=True)
        s_lo = jnp.sum(jnp.where(lo, sq, 0.0), axis=-1, keepdims=True)
        ms = jnp.where(lo, s_lo, s_all - s_lo) * (1.0 / HEAD_DIM)
        parts.append(xs * lax.rsqrt(ms + RMS_EPS))
    y = parts[0] if len(parts) == 1 else jnp.concatenate(parts, axis=1)
    return y * gain


def _rope(x, cos_t, sin_t, n_lanes):
    parts = []
    for j in range(n_lanes // LANES):
        xs = x[:, j * LANES:(j + 1) * LANES]
        first = (lax.broadcasted_iota(jnp.int32, xs.shape, 1) % HEAD_DIM) < (HEAD_DIM // 2)
        partner = jnp.where(first, pltpu.roll(xs, LANES - HEAD_DIM // 2, axis=1),
                            pltpu.roll(xs, HEAD_DIM // 2, axis=1))
        cs = cos_t[:, j * LANES:(j + 1) * LANES]
        sn = sin_t[:, j * LANES:(j + 1) * LANES]
        parts.append(xs * cs + partner * sn)
    return parts[0] if len(parts) == 1 else jnp.concatenate(parts, axis=1)


def _inproj_kernel(*refs, ctx):
    if ctx:
        (x_ref, mod_ref, w_ref, qn_ref, kn_ref, cg_ref, cb_ref, ws_ref, bs_ref, pw_ref, ps_ref,
         ua_ref, ub_ref, chunk_ref, k_ref, v_ref, yp_ref, ya_ref) = refs
    else:
        (x_ref, mod_ref, w_ref, qn_ref, kn_ref, cg_ref, cb_ref, ws_ref, bs_ref, cos_ref, sin_ref,
         ua_ref, ub_ref, chunk_ref, k_ref, v_ref, p_ref, q_ref) = refs
    x = x_ref[...]
    sh = mod_ref[3:4, :]
    sc = mod_ref[4:5, :]
    h = (x * (1.0 + sc) + sh).astype(BF16)
    proj = jnp.dot(h, w_ref[...], preferred_element_type=F32)
    p = proj[:, 0:256]
    ua_ref[...] = proj[:, 256:384]
    ub_ref[...] = proj[:, 384:512]
    q = _seg_rms(proj[:, 512:768], qn_ref[...], D_ATTN)
    k = _seg_rms(proj[:, 768:896], kn_ref[...], D_KV)
    v = proj[:, 896:1024]
    if not ctx:
        cos_t = cos_ref[...]
        sin_t = sin_ref[...]
        q = _rope(q, cos_t, sin_t, D_ATTN)
        k = _rope(k, cos_t, sin_t, D_KV)
    zu = _gelu(proj[:, 1024:1280])
    zv = _layer_norm(_gelu(proj[:, 1280:1536]), cg_ref[...], cb_ref[...])
    head = lax.broadcasted_iota(jnp.int32, (CHUNK, D_CHUNK), 1) // (D_CHUNK // N_CHUNK_HEADS)
    for c in range(TM // CHUNK):
        vb = zv[c * CHUNK:(c + 1) * CHUNK, :].astype(BF16)
        mixed = bs_ref[...]
        for hd in range(N_CHUNK_HEADS):
            r = jnp.dot(ws_ref[hd], vb, preferred_element_type=F32)
            mixed = mixed + jnp.where(head == hd, r, 0.0)
        chunk_ref[c * CHUNK:(c + 1) * CHUNK, :] = zu[c * CHUNK:(c + 1) * CHUNK, :] * mixed
    if ctx:
        k_ref[...] = k.T
        v_ref[...] = v.T
        for s in range(TM // SEQ):
            r = slice(s * SEQ, (s + 1) * SEQ)
            yp_ref[r, :] = _pool_math(p[r], pw_ref[...], ps_ref[...], SEQ)
            ya_ref[r, :] = _attn_math([(q[r], k[r].astype(BF16), v[r].astype(BF16))], False)[0]
    else:
        k_ref[...] = k.astype(BF16)
        v_ref[...] = v.astype(BF16)
        p_ref[...] = p
        q_ref[...] = q


def _inproj(x, mod, prm, l, cos_t, sin_t):
    tok = lambda i: (i, 0)
    common = [_fixed_spec((D_MODEL, D_IN), l), _fixed_spec((1, D_ATTN), l), _fixed_spec((1, D_KV), l),
              _fixed_spec((1, D_CHUNK), l), _fixed_spec((1, D_CHUNK), l),
              _fixed_spec((N_CHUNK_HEADS, CHUNK, CHUNK), l), _fixed_spec((CHUNK, D_CHUNK), l)]
    common_args = (prm['w_in'], prm['q_norm'], prm['k_norm'], prm['chunk_ln_g'], prm['chunk_ln_b'],
                   prm['chunk_w_s'], prm['chunk_b_s'])
    mod_block = (None, None, N_MOD, D_MODEL)

    def outs(rows, kv_dtype):
        widths = [(LANES, F32), (LANES, F32), (D_CHUNK, F32), (D_KV, kv_dtype), (D_KV, kv_dtype),
                  (D_HEADS, F32), (D_HEADS, F32)]
        return ([jax.ShapeDtypeStruct((rows, w), dt) for w, dt in widths],
                [pl.BlockSpec((TM, w), tok) for w, _ in widths])

    ctx_shapes, ctx_specs = outs(N_CTX, F32)
    for j in (3, 4):
        ctx_shapes[j] = jax.ShapeDtypeStruct((D_KV, N_CTX), F32)
        ctx_specs[j] = pl.BlockSpec((D_KV, TM), lambda i: (0, i))
    ctx = pl.pallas_call(
        functools.partial(_inproj_kernel, ctx=True),
        grid=(CTX_TILES,),
        in_specs=[pl.BlockSpec((TM, D_MODEL), tok), pl.BlockSpec(mod_block, lambda i: (l, 0, 0, 0))]
                 + common + [_fixed_spec((D_POOL, D_POOL), l), _fixed_spec((1, D_POOL), l)],
        out_specs=ctx_specs,
        out_shape=ctx_shapes,
        compiler_params=_params(1),
        name="mixer_inproj_ctx",
    )(x, mod, *common_args, prm['pool_w'], prm['pool_scale'])
    lat_shapes, lat_specs = outs(N_LAT, BF16)
    rope_spec = pl.BlockSpec((TM, D_ATTN), lambda i: (i % LAT_TILES_PER_SEQ, 0))
    lat = pl.pallas_call(
        functools.partial(_inproj_kernel, ctx=False),
        grid=(N_TILES - CTX_TILES,),
        in_specs=[pl.BlockSpec((TM, D_MODEL), lambda i: (CTX_TILES + i, 0)),
                  pl.BlockSpec(mod_block, lambda i: (l, 1 + i // LAT_TILES_PER_SEQ, 0, 0))]
                 + common + [rope_spec, rope_spec],
        out_specs=lat_specs,
        out_shape=lat_shapes,
        compiler_params=_params(1),
        name="mixer_inproj_lat",
    )(x, mod, *common_args, cos_t, sin_t)
    return ctx, lat


def _pool_kernel(p_ref, w_ref, s_ref, o_ref):
    o_ref[...] = _pool_math(p_ref[...], w_ref[...], s_ref[...], DEC_SEQ)


def _pool_lat(p_lat, prm, l):
    return pl.pallas_call(
        _pool_kernel,
        grid=(DEC_BATCH,),
        in_specs=[pl.BlockSpec((DEC_SEQ, D_POOL), lambda b: (b, 0)),
                  _fixed_spec((D_POOL, D_POOL), l), _fixed_spec((1, D_POOL), l)],
        out_specs=pl.BlockSpec((DEC_SEQ, D_POOL), lambda b: (b, 0)),
        out_shape=jax.ShapeDtypeStruct((N_LAT, D_POOL), F32),
        compiler_params=_params(1),
        name="pool_lat",
    )(p_lat, prm['pool_w'], prm['pool_scale'])


def _attn_kernel(q_ref, kc_ref, kl_ref, vc_ref, vl_ref, o_ref):
    k = jnp.concatenate([kc_ref[...], kl_ref[...]], axis=0)
    v = jnp.concatenate([vc_ref[...], vl_ref[...]], axis=0)
    o_ref[...] = _attn_math([(q_ref[...], k, v)], True)[0]


def _attn_lat(q_lat, k_lat, v_lat, cache_k, cache_v, l):
    q_per_seq = DEC_SEQ // ATT_TQ
    cache_spec = pl.BlockSpec((None, None, PAST_LEN, D_KV), lambda b, j: (b, l, 0, 0))
    seq_spec = pl.BlockSpec((DEC_SEQ, D_KV), lambda b, j: (b, 0))
    return pl.pallas_call(
        _attn_kernel,
        grid=(DEC_BATCH, q_per_seq),
        in_specs=[pl.BlockSpec((ATT_TQ, D_ATTN), lambda b, j: (b * q_per_seq + j, 0)),
                  cache_spec, seq_spec, cache_spec, seq_spec],
        out_specs=pl.BlockSpec((ATT_TQ, D_ATTN), lambda b, j: (b * q_per_seq + j, 0)),
        out_shape=jax.ShapeDtypeStruct((N_LAT, D_ATTN), F32),
        compiler_params=_params(2),
        name="attn_lat",
    )(q_lat, cache_k, k_lat, cache_v, v_lat)


def _ssm_prep_kernel(lam_ref, br_ref, bi_ref, bmat_ref, lamb_ref):
    lre = lam_ref[0:1, :]
    lim = lam_ref[1:2, :]
    dt = jnp.exp(lam_ref[2:3, :])
    mag = jnp.exp(lre * dt)
    ar = mag * jnp.cos(lim * dt)
    ai = mag * jnp.sin(lim * dt)
    lamb_ref[0:1, :] = ar
    lamb_ref[1:2, :] = ai
    mag_s = jnp.exp(lre * dt * SSM_LAT_SEG)
    lamb_ref[2:3, :] = mag_s * jnp.cos(lim * dt * SSM_LAT_SEG)
    lamb_ref[3:4, :] = mag_s * jnp.sin(lim * dt * SSM_LAT_SEG)
    den = lre * lre + lim * lim
    cr = ((ar - 1.0) * lre + ai * lim) / den
    ci = (ai * lre - (ar - 1.0) * lim) / den
    n_rep = SSM_LANES // LANES
    br = jnp.concatenate([br_ref[...]] * n_rep, axis=1)
    bi = jnp.concatenate([bi_ref[...]] * n_rep, axis=1)
    lane = lax.broadcasted_iota(jnp.int32, br.shape, 1)
    row_g = lax.broadcasted_iota(jnp.int32, br.shape, 0) // SSM_GROUP
    lane_g = 2 * (lane // (2 * LANES)) + (lane % LANES) // SSM_STATE
    is_re = (lane % (2 * LANES)) < LANES
    b_bar = jnp.where(is_re, cr * br - ci * bi, cr * bi + ci * br)
    bmat_ref[...] = jnp.where(row_g == lane_g, b_bar, 0.0).astype(BF16)


def _ssm_prep(lam_rows, b_re_placed, b_im_placed):
    n = DEPTH * 2
    return pl.pallas_call(
        _ssm_prep_kernel,
        grid=(n,),
        in_specs=[pl.BlockSpec((None, 3, SSM_LANES), lambda i: (i, 0, 0)),
                  pl.BlockSpec((None, D_SSM, LANES), lambda i: (i, 0, 0)),
                  pl.BlockSpec((None, D_SSM, LANES), lambda i: (i, 0, 0))],
        out_specs=[pl.BlockSpec((None, D_SSM, SSM_LANES), lambda i: (i, 0, 0)),
                   pl.BlockSpec((None, 4, SSM_LANES), lambda i: (i, 0, 0))],
        out_shape=[jax.ShapeDtypeStruct((n, D_SSM, SSM_LANES), BF16),
                   jax.ShapeDtypeStruct((n, 4, SSM_LANES), F32)],
        compiler_params=_params(1),
        name="ssm_prep",
    )(lam_rows, b_re_placed, b_im_placed)


def _ssm_kernel(ua_ref, ub_ref, h0_ref, bmat_ref, cmat_ref, lamb_ref, d_ref, wg_ref, bg_ref,
                y_ref, fin_ref, ui_ref, buf0_ref, buf1_ref, yi_ref, *, n_per):
    seg_len = SEQ
    ssm_w = SSM_W
    n_rows = SUBLANES * seg_len
    n_cb = ssm_w // (2 * LANES)

    def step_rows(t):
        return pl.ds(pl.multiple_of(t * SUBLANES, SUBLANES), SUBLANES)

    block_steps = SSM_BLOCK // SUBLANES
    blocks = [slice(b * SSM_BLOCK, (b + 1) * SSM_BLOCK) for b in range(n_rows // SSM_BLOCK)]

    def interleave(b):
        for t in range(b * block_steps, (b + 1) * block_steps):
            rows = slice(t * SUBLANES, (t + 1) * SUBLANES)
            ui_ref[rows, 0:LANES] = ua_ref[pl.ds(t, SUBLANES, stride=seg_len), :]
            ui_ref[rows, LANES:2 * LANES] = ub_ref[pl.ds(t, SUBLANES, stride=seg_len), :]

    seg = lax.broadcasted_iota(jnp.int32, (SUBLANES, LANES), 0) % n_per

    chains = [(dirn, k) for dirn in range(2) for k in range(n_cb)]
    n_parts = SSM_LANES // ssm_w
    bufs = (buf0_ref, buf1_ref)
    assert n_parts == len(bufs)

    def fill(part, rows):
        ub = ui_ref[rows, :].astype(BF16)
        for dirn, k in chains:
            c0 = k * 2 * LANES
            lane0 = part * ssm_w + c0
            bufs[part][dirn, rows, c0:c0 + 2 * LANES] = jnp.dot(
                ub, bmat_ref[dirn, :, lane0:lane0 + 2 * LANES], preferred_element_type=F32)

    def readout(part, rows):
        lo = part * ssm_w
        y = jnp.dot(bufs[part][0, rows, :].astype(BF16), cmat_ref[0, lo:lo + ssm_w, :],
                    preferred_element_type=F32)
        y = y + jnp.dot(bufs[part][1, rows, :].astype(BF16), cmat_ref[1, lo:lo + ssm_w, :],
                        preferred_element_type=F32)
        if part == 0:
            y = y + d_ref[...] * ui_ref[rows, :]
            yi_ref[0, rows, :] = y[:, :LANES]
            yi_ref[1, rows, :] = y[:, LANES:]
        else:
            yi_ref[0, rows, :] += y[:, :LANES]
            yi_ref[1, rows, :] += y[:, LANES:]

    def lam_rows(part, r):
        out = []
        for dirn, k in chains:
            c0 = part * ssm_w + k * 2 * LANES
            out.append(jnp.broadcast_to(lamb_ref[dirn, r:r + 1, c0:c0 + LANES], (SUBLANES, LANES)))
        return out

    def scan(part, init, store, side_work=None):
        buf = bufs[part]
        a_re, a_im = lam_rows(part, 0), lam_rows(part, 1)

        def step(i, hs):
            rows = (step_rows(i), step_rows(seg_len - 1 - i))
            new = []
            for c, (dirn, k) in enumerate(chains):
                re_sl = slice(k * 2 * LANES, k * 2 * LANES + LANES)
                im_sl = slice(k * 2 * LANES + LANES, (k + 1) * 2 * LANES)
                hr, hi = hs[2 * c], hs[2 * c + 1]
                nr = a_re[c] * hr - a_im[c] * hi + buf[dirn, rows[dirn], re_sl]
                ni = a_re[c] * hi + a_im[c] * hr + buf[dirn, rows[dirn], im_sl]
                if store:
                    buf[dirn, rows[dirn], re_sl] = nr
                    buf[dirn, rows[dirn], im_sl] = ni
                new += [nr, ni]
            return tuple(new)

        if side_work is None:
            return lax.fori_loop(0, seg_len, step, tuple(init), unroll=2)

        steps = SSM_BLOCK // SUBLANES

        def block(blk, hs):
            side_work(pl.ds(pl.multiple_of(blk * SSM_BLOCK, SSM_BLOCK), SSM_BLOCK))
            for j in range(steps):
                hs = step(blk * steps + j, hs)
            return hs

        return lax.fori_loop(0, n_rows // SSM_BLOCK, block, tuple(init))

    def initial_states(part, local):
        h0 = []
        for dirn, k in chains:
            c0 = part * ssm_w + k * 2 * LANES
            h0 += [h0_ref[dirn, :, c0:c0 + LANES], h0_ref[dirn, :, c0 + LANES:c0 + 2 * LANES]]
        if local is None:
            return h0
        s_re, s_im = lam_rows(part, 2), lam_rows(part, 3)
        init = []
        for c, (dirn, k) in enumerate(chains):
            edge = seg == (0 if dirn == 0 else n_per - 1)
            shift = 1 if dirn == 0 else SUBLANES - 1
            cr, ci = h0[2 * c], h0[2 * c + 1]
            lr = pltpu.roll(local[2 * c], shift, axis=0)
            li = pltpu.roll(local[2 * c + 1], shift, axis=0)
            for _ in range(n_per - 1):
                pr = pltpu.roll(cr, shift, axis=0)
                pi = pltpu.roll(ci, shift, axis=0)
                cr = jnp.where(edge, h0[2 * c], s_re[c] * pr - s_im[c] * pi + lr)
                ci = jnp.where(edge, h0[2 * c + 1], s_re[c] * pi + s_im[c] * pr + li)
            init += [cr, ci]
        return init

    zeros = [jnp.zeros((SUBLANES, LANES), F32)] * (2 * len(chains))
    interleave(0)
    for b in range(1, len(blocks)):
        fill(0, blocks[b - 1])
        interleave(b)
    fill(0, blocks[-1])
    assert n_parts == 2
    for part in range(n_parts):
        local = scan(part, zeros, False) if n_per > 1 else None
        side_work = functools.partial(fill, 1) if part == 0 else functools.partial(readout, 0)
        hs = scan(part, initial_states(part, local), True, side_work)
        for c, (dirn, k) in enumerate(chains):
            c0 = part * ssm_w + k * 2 * LANES
            fin_ref[dirn, :, c0:c0 + LANES] = hs[2 * c]
            fin_ref[dirn, :, c0 + LANES:c0 + 2 * LANES] = hs[2 * c + 1]

    def glu(rows):
        g = _gelu(jnp.concatenate([yi_ref[0, rows, :], yi_ref[1, rows, :]], axis=1))
        z = jnp.dot(g.astype(BF16), wg_ref[...], preferred_element_type=F32) + bg_ref[...]
        o = g * jax.nn.sigmoid(z)
        yi_ref[0, rows, :] = o[:, :LANES]
        yi_ref[1, rows, :] = o[:, LANES:]

    def deinterleave(b):
        for tb in range(b * block_steps // SUBLANES, (b + 1) * block_steps // SUBLANES):
            for s_idx in range(SUBLANES):
                dst = slice(s_idx * seg_len + tb * SUBLANES, s_idx * seg_len + (tb + 1) * SUBLANES)
                src = pl.ds(tb * SUBLANES * SUBLANES + s_idx, SUBLANES, stride=SUBLANES)
                y_ref[dst, 0:LANES] = yi_ref[0, src, :]
                y_ref[dst, LANES:2 * LANES] = yi_ref[1, src, :]

    last = n_parts - 1
    readout(last, blocks[0])
    for b in range(1, len(blocks)):
        readout(last, blocks[b])
        glu(blocks[b - 1])
        deinterleave(b - 1)
    glu(blocks[-1])
    deinterleave(len(blocks) - 1)


def _ssm(u_ctx, u_lat, prm, l):
    weights = [_fixed_spec((2, D_SSM, SSM_LANES), l), _fixed_spec((2, SSM_LANES, D_SSM), l),
               _fixed_spec((2, 4, SSM_LANES), l), _fixed_spec((1, D_SSM), l),
               _fixed_spec((D_SSM, D_SSM), l), _fixed_spec((1, D_SSM), l)]
    rows = SUBLANES * SEQ
    scratch = [pltpu.VMEM((rows, D_SSM), F32), pltpu.VMEM((2, rows, SSM_W), F32),
               pltpu.VMEM((2, rows, SSM_W), F32), pltpu.VMEM((2, rows, LANES), F32)]
    tile_a = pl.BlockSpec((rows, LANES), lambda i: (i, 0))
    tile_y = pl.BlockSpec((rows, D_SSM), lambda i: (i, 0))
    fin_spec = pl.BlockSpec((None, 2, SUBLANES, SSM_LANES), lambda i: (i, 0, 0, 0))

    tail = (prm['ssm_b'], prm['ssm_c'], prm['ssm_lam'], prm['ssm_d'], prm['ssm_w_glu'], prm['ssm_b_glu'])
    n_ctx_tiles = N_CTX // rows
    h0_ctx = jnp.zeros((2, SUBLANES, SSM_LANES), F32)
    y_ctx, fin_ctx = pl.pallas_call(
        functools.partial(_ssm_kernel, n_per=1),
        grid=(n_ctx_tiles,),
        in_specs=[tile_a, tile_a, _fixed_spec((2, SUBLANES, SSM_LANES))] + weights,
        out_specs=[tile_y, fin_spec],
        out_shape=[jax.ShapeDtypeStruct((N_CTX, D_SSM), F32),
                   jax.ShapeDtypeStruct((n_ctx_tiles, 2, SUBLANES, SSM_LANES), F32)],
        scratch_shapes=scratch,
        compiler_params=_params(1),
        name="ssm_ctx",
    )(*u_ctx, h0_ctx, *tail)
    h0_spec = pl.BlockSpec((None, None, 2, SUBLANES, SSM_LANES), lambda b: (l, b, 0, 0, 0))
    y_lat, _ = pl.pallas_call(
        functools.partial(_ssm_kernel, n_per=SSM_LAT_SEGS),
        grid=(DEC_BATCH,),
        in_specs=[tile_a, tile_a, h0_spec] + weights,
        out_specs=[tile_y, fin_spec],
        out_shape=[jax.ShapeDtypeStruct((N_LAT, D_SSM), F32),
                   jax.ShapeDtypeStruct((DEC_BATCH, 2, SUBLANES, SSM_LANES), F32)],
        scratch_shapes=scratch,
        compiler_params=_params(1),
        name="ssm_lat",
    )(*u_lat, prm['ssm_h0'], *tail)
    return (y_ctx, y_lat), fin_ctx


def _rope_tables():
    rows = DEC_SEQ // GRID_W
    row_idx = jnp.repeat(jnp.arange(rows), GRID_W).astype(F32)
    col_idx = jnp.tile(jnp.arange(GRID_W), rows).astype(F32)
    n_freq = HEAD_DIM // 4
    inv = ROPE_THETA ** (-jnp.arange(n_freq, dtype=F32) / n_freq)
    ang = jnp.concatenate([row_idx[:, None] * inv, col_idx[:, None] * inv], axis=-1)
    cos = jnp.cos(ang)
    sin = jnp.sin(ang)
    cos_h = jnp.concatenate([cos, cos], axis=-1)
    sin_h = jnp.concatenate([-sin, sin], axis=-1)
    return jnp.tile(cos_h, (1, N_HEADS)), jnp.tile(sin_h, (1, N_HEADS))


def _state_lanes(a):
    return a.reshape(a.shape[:-2] + (N_SSM_GROUPS // 2, 2 * SSM_STATE))


def _ssm_lane_rows(a):
    a = _state_lanes(a)
    return jnp.stack([a, a], axis=-2).reshape(a.shape[:-2] + (SSM_LANES,))


def _ssm_pack_state(re, im):
    return jnp.stack([_state_lanes(re), _state_lanes(im)], axis=-2).reshape(re.shape[:-2] + (SSM_LANES,))


def _ssm_unpack_state(s):
    s = s.reshape(s.shape[:-1] + (N_SSM_GROUPS // 2, 2, 2 * SSM_STATE))
    shp = s.shape[:-3] + (N_SSM_GROUPS, SSM_STATE)
    return s[..., 0, :].reshape(shp), s[..., 1, :].reshape(shp)


def _ssm_group_mask():
    ch_g = jnp.arange(D_SSM) // SSM_GROUP
    lane = jnp.arange(SSM_LANES)
    lane_g = 2 * (lane // (2 * LANES)) + (lane % LANES) // SSM_STATE
    return ch_g[:, None] == lane_g[None, :]


def _b_rows(b):
    bt = jnp.swapaxes(b, -1, -2).reshape(DEPTH * 2, D_SSM, SSM_STATE)
    return jnp.concatenate([bt, bt], axis=-1)


def _place_c(c_re, c_im):
    def rows(c):
        ct = jnp.moveaxis(c, -1, -3).reshape(c.shape[:-3] + (SSM_STATE, D_SSM))
        return jnp.tile(ct, (1,) * (ct.ndim - 2) + (SSM_LANES // SSM_STATE, 1))
    is_re = (jnp.arange(SSM_LANES) % (2 * LANES) < LANES)[:, None]
    return jnp.where(_ssm_group_mask().T, jnp.where(is_re, rows(c_re), -rows(c_im)), 0.0)


def _block_diag_pool(pool_w):
    tiled = jnp.tile(pool_w.reshape(DEPTH, D_POOL, POOL_GROUP), (1, 1, len(POOL_WINDOWS)))
    grp = jnp.arange(D_POOL) // POOL_GROUP
    return jnp.where(grp[:, None] == grp[None, :], tiled, 0.0)


def kernel(x_prompt, x_sample, cache_k, cache_v, state_ssm_re, state_ssm_im, c, c_ctx, w_mod, b_mod, ln_g, ln_b, ffn_w1, ffn_w2, w_in, w_out, pool_w, pool_scale, ssm_lam_re, ssm_lam_im, ssm_log_step, ssm_b_re, ssm_b_im, ssm_c_re, ssm_c_im, ssm_d, ssm_w_glu, ssm_b_glu, q_norm, k_norm, chunk_ln_g, chunk_ln_b, chunk_w_s, chunk_b_s):
    cond = jnp.concatenate([c_ctx[None, :], c, jnp.zeros((N_COND - 1 - DEC_BATCH, D_MODEL), F32)], axis=0)
    mod = _modulation(cond, w_mod, b_mod)
    cos_t, sin_t = _rope_tables()

    log_step = jnp.broadcast_to(ssm_log_step[..., None], ssm_lam_re.shape)
    lam_rows = jnp.stack([_ssm_lane_rows(ssm_lam_re), _ssm_lane_rows(ssm_lam_im),
                          _ssm_lane_rows(log_step)], axis=-2)
    bmat, lamb = _ssm_prep(lam_rows.reshape(DEPTH * 2, 3, SSM_LANES),
                           _b_rows(ssm_b_re), _b_rows(ssm_b_im))
    h0 = _ssm_pack_state(state_ssm_re, state_ssm_im)
    h0_lat = jnp.zeros((DEPTH, DEC_BATCH, 2, SSM_LAT_SEGS, SSM_LANES), F32)
    h0_lat = h0_lat.at[:, :, 0, 0].set(jnp.swapaxes(h0[:, :, 0], 0, 1))
    h0_lat = h0_lat.at[:, :, 1, SSM_LAT_SEGS - 1].set(jnp.swapaxes(h0[:, :, 1], 0, 1))

    vec = lambda a: a[..., None, :]
    prm = {
        'ln_g': vec(ln_g), 'ln_b': vec(ln_b),
        'ffn_w1': ffn_w1.astype(BF16), 'ffn_w2': ffn_w2.astype(BF16),
        'w_in': w_in.astype(BF16), 'w_out': w_out.astype(BF16),
        'q_norm': vec(jnp.tile(q_norm, (1, N_HEADS))), 'k_norm': vec(jnp.tile(k_norm, (1, N_KV_HEADS))),
        'chunk_ln_g': vec(chunk_ln_g), 'chunk_ln_b': vec(chunk_ln_b),
        'chunk_w_s': chunk_w_s.astype(BF16),
        'chunk_b_s': jnp.repeat(jnp.swapaxes(chunk_b_s, 1, 2), D_CHUNK // N_CHUNK_HEADS, axis=2),
        'pool_w': _block_diag_pool(pool_w).astype(BF16), 'pool_scale': vec(pool_scale),
        'ssm_b': bmat.reshape(DEPTH, 2, D_SSM, SSM_LANES),
        'ssm_lam': lamb.reshape(DEPTH, 2, 4, SSM_LANES),
        'ssm_c': _place_c(ssm_c_re, ssm_c_im).astype(BF16),
        'ssm_h0': h0_lat,
        'ssm_d': vec(ssm_d), 'ssm_w_glu': ssm_w_glu.astype(BF16), 'ssm_b_glu': vec(ssm_b_glu),
    }
    cache_k = cache_k.reshape(DEC_BATCH, DEPTH, PAST_LEN, D_KV).astype(BF16)
    cache_v = cache_v.reshape(DEC_BATCH, DEPTH, PAST_LEN, D_KV).astype(BF16)

    x = (x_prompt.reshape(N_CTX, D_MODEL), x_sample.reshape(N_LAT, D_MODEL))
    ks, vs, s_re, s_im = [], [], [], []
    for l in range(DEPTH):
        x = _ffn_sublayer(x, mod, prm, l, 0)
        ctx_out, lat_out = _inproj(x, mod, prm, l, cos_t, sin_t)
        ua_ctx, ub_ctx, yc_ctx, k_ctx, v_ctx, yp_ctx, ya_ctx = ctx_out
        ua_lat, ub_lat, yc_lat, k_lat, v_lat, p_lat, q_lat = lat_out
        yp_lat = _pool_lat(p_lat, prm, l)
        y_ssm, fin = _ssm((ua_ctx, ub_ctx), (ua_lat, ub_lat), prm, l)
        ya_lat = _attn_lat(q_lat, k_lat, v_lat, cache_k, cache_v, l)
        x = _ffn_sublayer(x, mod, prm, l, 1, split_out=(l == DEPTH - 1),
                          mixer=((yp_ctx, yp_lat), y_ssm, (ya_ctx, ya_lat), (yc_ctx, yc_lat)))
        ks.append(k_ctx)
        vs.append(v_ctx)
        f_re, f_im = _ssm_unpack_state(jnp.transpose(fin, (0, 2, 1, 3)).reshape(BATCH, 2, SSM_LANES))
        s_re.append(f_re)
        s_im.append(f_im)
    y_p = x[0].reshape(BATCH, SEQ, D_MODEL)
    y_s = x[1].reshape(DEC_BATCH, DEC_SEQ, D_MODEL)
    def cache(ts):
        t = jnp.stack(ts).reshape(DEPTH, N_KV_HEADS, HEAD_DIM, BATCH, SEQ)
        return jnp.transpose(t, (3, 0, 4, 1, 2))

    return (y_p, y_s, cache(ks), cache(vs), jnp.stack(s_re, axis=1), jnp.stack(s_im, axis=1))
```

```python
import functools
import math

import jax
import jax.numpy as jnp
from jax import lax
from jax.experimental import pallas as pl
from jax.experimental.pallas import tpu as pltpu

F32 = jnp.float32
BF16 = jnp.bfloat16

D_MODEL = 1024
BATCH = 32
SEQ = 256
DEPTH = 2
DEC_BATCH = 2
DEC_SEQ = 2048
PAST_LEN = 512
GRID_W = 64
D_POOL = 256
D_SSM = 256
D_ATTN = 256
D_CHUNK = 256
D_HEADS = 256
POOL_WINDOWS = (2, 4, 8, 16)
POOL_GROUP = 64
SSM_GROUP = 16
N_SSM_GROUPS = 16
SSM_STATE = 64
HEAD_DIM = 64
N_HEADS = 4
N_KV_HEADS = 2
D_KV = 128
CHUNK = 128
N_CHUNK_HEADS = 4
D_FF = 2816
N_MOD = 9
D_IN = 1536
ALPHA = (2 * DEPTH) ** 0.25
LN_EPS = 1e-5
RMS_EPS = 1e-6
ROPE_THETA = 10000.0

LANES = 128
SUBLANES = 8
TM = 512
FFN_TM = 1024
FFN_SUB = 256
ATT_TQ = 512
N_CTX = BATCH * SEQ
N_LAT = DEC_BATCH * DEC_SEQ
N_TOK = N_CTX + N_LAT
CTX_TILES = N_CTX // TM
LAT_TILES_PER_SEQ = DEC_SEQ // TM
N_TILES = N_TOK // TM
N_COND = 8
MOD_COLS = 2304
POOL_PAD = 8
SSM_LANES = 2 * N_SSM_GROUPS * SSM_STATE
SSM_W = 1024
SSM_LAT_SEGS = SUBLANES
SSM_LAT_SEG = DEC_SEQ // SSM_LAT_SEGS
SSM_BLOCK = 512
VMEM_LIMIT = 56 * 1024 * 1024


def _params(n_grid):
    return pltpu.CompilerParams(dimension_semantics=("arbitrary",) * n_grid,
                                vmem_limit_bytes=VMEM_LIMIT)


def _fixed_spec(tail, *lead):
    idx = tuple(lead) + (0,) * len(tail)
    return pl.BlockSpec((None,) * len(lead) + tuple(tail), lambda *_: idx, pipeline_mode=pl.Buffered(1))


def _layer_norm(y, g, b):
    mu = jnp.mean(y, axis=-1, keepdims=True)
    d = y - mu
    var = jnp.mean(d * d, axis=-1, keepdims=True)
    return d * lax.rsqrt(var + LN_EPS) * g + b


def _gelu(x):
    return 0.5 * x * (1.0 + jnp.tanh(math.sqrt(2.0 / math.pi) * (x + 0.044715 * (x * x * x))))


def _silu(x):
    return x * jax.nn.sigmoid(x)


def _split_bf16(a):
    hi = a.astype(BF16)
    return hi, (a - hi.astype(F32)).astype(BF16)


def _mod_kernel(cond_ref, w_ref, b_ref, o_ref):
    a_hi, a_lo = _split_bf16(_silu(cond_ref[...]))
    w_hi, w_lo = _split_bf16(w_ref[...])
    acc = jnp.dot(a_hi, w_lo, preferred_element_type=F32)
    acc = acc + jnp.dot(a_lo, w_hi, preferred_element_type=F32)
    acc = acc + jnp.dot(a_hi, w_hi, preferred_element_type=F32)
    o_ref[...] = acc + b_ref[...]


def _modulation(cond, w_mod, b_mod):
    tn = MOD_COLS
    out = pl.pallas_call(
        _mod_kernel,
        grid=(DEPTH, N_MOD * D_MODEL // tn),
        in_specs=[
            pl.BlockSpec((N_COND, D_MODEL), lambda l, j: (0, 0)),
            pl.BlockSpec((None, D_MODEL, tn), lambda l, j: (l, 0, j)),
            pl.BlockSpec((None, 1, tn), lambda l, j: (l, 0, j)),
        ],
        out_specs=pl.BlockSpec((None, N_COND, tn), lambda l, j: (l, 0, j)),
        out_shape=jax.ShapeDtypeStruct((DEPTH, N_COND, N_MOD * D_MODEL), F32),
        compiler_params=_params(2),
        name="modulation",
    )(cond, w_mod, b_mod.reshape(DEPTH, 1, N_MOD * D_MODEL))
    return out.reshape(DEPTH, N_COND, N_MOD, D_MODEL)


def _ffn_kernel(*refs, mod_off, split_in, split_out, mixer_out, tm):
    n_in = 2 if split_in else 1
    n_mix = 11 if mixer_out else 0
    mix_refs = refs[n_in:n_in + n_mix]
    mod_ref, w1_ref, w2_ref, g_ref, b_ref = refs[n_in + n_mix:n_in + n_mix + 5]
    out_refs = refs[n_in + n_mix + 5:]
    is_ctx = pl.program_id(0) < N_CTX // tm
    sh = mod_ref[mod_off:mod_off + 1, :]
    sc = mod_ref[mod_off + 1:mod_off + 2, :]
    gate = mod_ref[mod_off + 2:mod_off + 3, :]

    def load_x(rows):
        if split_in:
            return jnp.where(is_ctx, refs[0][rows, :], refs[1][rows, :])
        return refs[0][rows, :]

    def project(rows):
        if not mixer_out:
            return None
        wo_ref = mix_refs[8]
        parts = [jnp.where(is_ctx, mix_refs[2 * j][rows, :], mix_refs[2 * j + 1][rows, :])
                 for j in range(4)]
        y = None
        for j, part in enumerate(parts):
            r = jnp.dot(part.astype(BF16), wo_ref[D_HEADS * j:D_HEADS * (j + 1), :],
                        preferred_element_type=F32)
            y = r if y is None else y + r
        return y

    def mixer_norm(x, y):
        if not mixer_out:
            return x
        go_ref, bo_ref = mix_refs[9:]
        return _layer_norm(ALPHA * x + mod_ref[5:6, :] * y, go_ref[...], bo_ref[...])

    def up(x):
        h = (x * (1.0 + sc) + sh).astype(BF16)
        return jnp.dot(h, w1_ref[...], preferred_element_type=F32)

    def down(gu):
        a = (_silu(gu[:, :D_FF]) * gu[:, D_FF:]).astype(BF16)
        return jnp.dot(a, w2_ref[...], preferred_element_type=F32)

    def finish(x, f):
        return _layer_norm(ALPHA * x + (0.5 * gate) * f, g_ref[...], b_ref[...])

    row_slices = [slice(j * FFN_SUB, (j + 1) * FFN_SUB) for j in range(tm // FFN_SUB)]
    n = len(row_slices)
    ys = []
    x = mixer_norm(load_x(row_slices[0]), project(row_slices[0]))
    gu = up(x)
    for j in range(n):
        y_next = project(row_slices[j + 1]) if j + 1 < n else None
        f = down(gu)
        if j + 1 < n:
            x_next = mixer_norm(load_x(row_slices[j + 1]), y_next)
            gu = up(x_next)
        ys.append(finish(x, f))
        if j + 1 < n:
            x = x_next

    def store(o_ref):
        for j, rows in enumerate(row_slices):
            o_ref[rows, :] = ys[j]

    if split_out:
        @pl.when(is_ctx)
        def _():
            store(out_refs[0])

        @pl.when(jnp.logical_not(is_ctx))
        def _():
            store(out_refs[1])
    else:
        store(out_refs[0])


def _ffn_sublayer(xs, mod, prm, l, sub, split_out=False, mixer=None):
    tm = TM if mixer is not None else FFN_TM
    ctx_tiles = N_CTX // tm
    lat_tiles_per_seq = DEC_SEQ // tm
    tok = lambda i: (i, 0)
    ctx = lambda i: (jnp.minimum(i, ctx_tiles - 1), 0)
    lat = lambda i: (jnp.maximum(i - ctx_tiles, 0), 0)
    cond = lambda i: (l, jnp.where(i < ctx_tiles, 0, 1 + (i - ctx_tiles) // lat_tiles_per_seq), 0, 0)
    split_in = isinstance(xs, tuple)
    if split_in:
        x_specs = [pl.BlockSpec((tm, D_MODEL), ctx), pl.BlockSpec((tm, D_MODEL), lat)]
    else:
        xs = (xs,)
        x_specs = [pl.BlockSpec((tm, D_MODEL), tok)]
    mix_args, mix_specs = (), []
    if mixer is not None:
        y_pool, y_ssm, y_attn, y_chunk = mixer
        mix_args = (*y_pool, *y_ssm, *y_attn, *y_chunk, prm['w_out'], prm['ln_g'], prm['ln_b'])
        mix_specs = [pl.BlockSpec((tm, D_HEADS), ctx), pl.BlockSpec((tm, D_HEADS), lat)] * 4 + [
            _fixed_spec((D_MODEL, D_MODEL), l), _fixed_spec((1, D_MODEL), l, 1), _fixed_spec((1, D_MODEL), l, 1)]
    if split_out:
        out_specs = [pl.BlockSpec((tm, D_MODEL), ctx), pl.BlockSpec((tm, D_MODEL), lat)]
        out_shape = [jax.ShapeDtypeStruct((N_CTX, D_MODEL), F32), jax.ShapeDtypeStruct((N_LAT, D_MODEL), F32)]
    else:
        out_specs = pl.BlockSpec((tm, D_MODEL), tok)
        out_shape = jax.ShapeDtypeStruct((N_TOK, D_MODEL), F32)
    ln_idx = 2 * sub
    return pl.pallas_call(
        functools.partial(_ffn_kernel, mod_off=6 * sub, split_in=split_in, split_out=split_out,
                          mixer_out=mixer is not None, tm=tm),
        grid=(N_TOK // tm,),
        in_specs=x_specs + mix_specs + [
            pl.BlockSpec((None, None, N_MOD, D_MODEL), cond),
            _fixed_spec((D_MODEL, 2 * D_FF), l, sub),
            _fixed_spec((D_FF, D_MODEL), l, sub),
            _fixed_spec((1, D_MODEL), l, ln_idx),
            _fixed_spec((1, D_MODEL), l, ln_idx),
        ],
        out_specs=out_specs,
        out_shape=out_shape,
        compiler_params=_params(1),
        name="ffn_sublayer",
    )(*xs, *mix_args, mod, prm['ffn_w1'], prm['ffn_w2'], prm['ln_g'], prm['ln_b'])


def _pool_math(x, w_bd, scale, seq_len):
    n = seq_len + 2 * POOL_PAD
    zpad = jnp.zeros((POOL_PAD, D_POOL), F32)
    e = jnp.concatenate([zpad, x, zpad], axis=0)

    def prev(a, d):
        return pltpu.roll(a, d, axis=0)

    def nxt(a, d):
        return pltpu.roll(a, n - d, axis=0)

    s2 = e + prev(e, 1)
    s4 = prev(s2, 1) + nxt(s2, 1)
    s8 = prev(s4, 2) + nxt(s4, 2)
    s16 = prev(s8, 4) + nxt(s8, 4)
    lane = lax.broadcasted_iota(jnp.int32, (seq_len, D_POOL), 1)
    grp = lane // POOL_GROUP
    sl = slice(POOL_PAD, POOL_PAD + seq_len)
    s = jnp.where(grp == 0, s2[sl], jnp.where(grp == 1, s4[sl], jnp.where(grp == 2, s8[sl], s16[sl])))
    half = jnp.where(grp == 0, 1, jnp.where(grp == 1, 2, jnp.where(grp == 2, 4, 8)))
    t = lax.broadcasted_iota(jnp.int32, (seq_len, D_POOL), 0)
    cnt = jnp.minimum(t + half, seq_len) - jnp.maximum(t - half, 0)
    d = (s / cnt.astype(F32) - x).astype(BF16)
    return jnp.dot(d, w_bd, preferred_element_type=F32) * scale


def _attn_math(problems, mxu_sums):
    nt = (((1,), (1,)), ((), ()))
    tq = problems[0][0].shape[0]
    lo = lax.broadcasted_iota(jnp.int32, (tq, LANES), 1) < HEAD_DIM
    scores = []
    for q, k, _ in problems:
        q = q * (HEAD_DIM ** -0.5 * math.log2(math.e))
        q_lo = q[:, :LANES]
        q_hi = q[:, LANES:]
        qs = [jnp.where(lo, q_lo, 0.0), jnp.where(lo, pltpu.roll(q_lo, HEAD_DIM, axis=1), 0.0),
              jnp.where(lo, 0.0, pltpu.roll(q_hi, HEAD_DIM, axis=1)), jnp.where(lo, 0.0, q_hi)]
        for g in range(N_KV_HEADS):
            qq = jnp.concatenate([qs[2 * g], qs[2 * g + 1]], axis=0).astype(BF16)
            scores.append(lax.dot_general(qq, k, nt, preferred_element_type=F32))
    results = []
    for i, (_, _, v) in enumerate(problems):
        if mxu_sums:
            lo_k = lax.broadcasted_iota(jnp.int32, v.shape, 1) < HEAD_DIM
            one = jnp.ones_like(v)
            v_g = [jnp.where(lo_k, v, one), jnp.where(lo_k, one, v)]
        outs = []
        for g in range(N_KV_HEADS):
            s = scores[N_KV_HEADS * i + g]
            p = jnp.exp2(s - jnp.max(s, axis=-1, keepdims=True))
            if mxu_sums:
                o = jnp.dot(p.astype(BF16), v_g[g], preferred_element_type=F32)
            else:
                l = jnp.sum(p, axis=-1, keepdims=True)
                o = jnp.dot(p.astype(BF16), v, preferred_element_type=F32) / l
            outs += [o[:tq], o[tq:]]
        if mxu_sums:
            r = [pltpu.roll(o, HEAD_DIM, axis=1) for o in outs]
            out_lo = jnp.where(lo, outs[0], r[1]) / jnp.where(lo, r[0], outs[1])
            out_hi = jnp.where(lo, r[2], outs[3]) / jnp.where(lo, outs[2], r[3])
        else:
            out_lo = jnp.where(lo, outs[0], pltpu.roll(outs[1], HEAD_DIM, axis=1))
            out_hi = jnp.where(lo, pltpu.roll(outs[2], HEAD_DIM, axis=1), outs[3])
        results.append(jnp.concatenate([out_lo, out_hi], axis=1))
    return results


def _seg_rms(x, gain, n_lanes):
    parts = []
    for j in range(n_lanes // LANES):
        xs = x[:, j * LANES:(j + 1) * LANES]
        sq = xs * xs
        lo = lax.broadcasted_iota(jnp.int32, xs.shape, 1) < HEAD_DIM
        s_all = jnp.sum(sq, axis=-1, keepdims=True)
        s_lo = jnp.sum(jnp.where(lo, sq, 0.0), axis=-1, keepdims=True)
        ms = jnp.where(lo, s_lo, s_all - s_lo) * (1.0 / HEAD_DIM)
        parts.append(xs * lax.rsqrt(ms + RMS_EPS))
    y = parts[0] if len(parts) == 1 else jnp.concatenate(parts, axis=1)
    return y * gain


def _rope(x, cos_t, sin_t, n_lanes):
    parts = []
    for j in range(n_lanes // LANES):
        xs = x[:, j * LANES:(j + 1) * LANES]
        first = (lax.broadcasted_iota(jnp.int32, xs.shape, 1) % HEAD_DIM) < (HEAD_DIM // 2)
        partner = jnp.where(first, pltpu.roll(xs, LANES - HEAD_DIM // 2, axis=1),
                            pltpu.roll(xs, HEAD_DIM // 2, axis=1))
        cs = cos_t[:, j * LANES:(j + 1) * LANES]
        sn = sin_t[:, j * LANES:(j + 1) * LANES]
        parts.append(xs * cs + partner * sn)
    return parts[0] if len(parts) == 1 else jnp.concatenate(parts, axis=1)


def _inproj_kernel(*refs, ctx):
    if ctx:
        (x_ref, mod_ref, w_ref, qn_ref, kn_ref, cg_ref, cb_ref, ws_ref, bs_ref, pw_ref, ps_ref,
         ua_ref, ub_ref, chunk_ref, k_ref, v_ref, yp_ref, ya_ref) = refs
    else:
        (x_ref, mod_ref, w_ref, qn_ref, kn_ref, cg_ref, cb_ref, ws_ref, bs_ref, cos_ref, sin_ref,
         ua_ref, ub_ref, chunk_ref, k_ref, v_ref, p_ref, q_ref) = refs
    x = x_ref[...]
    sh = mod_ref[3:4, :]
    sc = mod_ref[4:5, :]
    h = (x * (1.0 + sc) + sh).astype(BF16)
    proj = jnp.dot(h, w_ref[...], preferred_element_type=F32)
    p = proj[:, 0:256]
    ua_ref[...] = proj[:, 256:384]
    ub_ref[...] = proj[:, 384:512]
    q = _seg_rms(proj[:, 512:768], qn_ref[...], D_ATTN)
    k = _seg_rms(proj[:, 768:896], kn_ref[...], D_KV)
    v = proj[:, 896:1024]
    if not ctx:
        cos_t = cos_ref[...]
        sin_t = sin_ref[...]
        q = _rope(q, cos_t, sin_t, D_ATTN)
        k = _rope(k, cos_t, sin_t, D_KV)
    zu = _gelu(proj[:, 1024:1280])
    zv = _layer_norm(_gelu(proj[:, 1280:1536]), cg_ref[...], cb_ref[...])
    head = lax.broadcasted_iota(jnp.int32, (CHUNK, D_CHUNK), 1) // (D_CHUNK // N_CHUNK_HEADS)
    for c in range(TM // CHUNK):
        vb = zv[c * CHUNK:(c + 1) * CHUNK, :].astype(BF16)
        mixed = bs_ref[...]
        for hd in range(N_CHUNK_HEADS):
            r = jnp.dot(ws_ref[hd], vb, preferred_element_type=F32)
            mixed = mixed + jnp.where(head == hd, r, 0.0)
        chunk_ref[c * CHUNK:(c + 1) * CHUNK, :] = zu[c * CHUNK:(c + 1) * CHUNK, :] * mixed
    if ctx:
        k_ref[...] = k.T
        v_ref[...] = v.T
        for s in range(TM // SEQ):
            r = slice(s * SEQ, (s + 1) * SEQ)
            yp_ref[r, :] = _pool_math(p[r], pw_ref[...], ps_ref[...], SEQ)
            ya_ref[r, :] = _attn_math([(q[r], k[r].astype(BF16), v[r].astype(BF16))], False)[0]
    else:
        k_ref[...] = k.astype(BF16)
        v_ref[...] = v.astype(BF16)
        p_ref[...] = p
        q_ref[...] = q


def _inproj(x, mod, prm, l, cos_t, sin_t):
    tok = lambda i: (i, 0)
    common = [_fixed_spec((D_MODEL, D_IN), l), _fixed_spec((1, D_ATTN), l), _fixed_spec((1, D_KV), l),
              _fixed_spec((1, D_CHUNK), l), _fixed_spec((1, D_CHUNK), l),
              _fixed_spec((N_CHUNK_HEADS, CHUNK, CHUNK), l), _fixed_spec((CHUNK, D_CHUNK), l)]
    common_args = (prm['w_in'], prm['q_norm'], prm['k_norm'], prm['chunk_ln_g'], prm['chunk_ln_b'],
                   prm['chunk_w_s'], prm['chunk_b_s'])
    mod_block = (None, None, N_MOD, D_MODEL)

    def outs(rows, kv_dtype):
        widths = [(LANES, F32), (LANES, F32), (D_CHUNK, F32), (D_KV, kv_dtype), (D_KV, kv_dtype),
                  (D_HEADS, F32), (D_HEADS, F32)]
        return ([jax.ShapeDtypeStruct((rows, w), dt) for w, dt in widths],
                [pl.BlockSpec((TM, w), tok) for w, _ in widths])

    ctx_shapes, ctx_specs = outs(N_CTX, F32)
    for j in (3, 4):
        ctx_shapes[j] = jax.ShapeDtypeStruct((D_KV, N_CTX), F32)
        ctx_specs[j] = pl.BlockSpec((D_KV, TM), lambda i: (0, i))
    ctx = pl.pallas_call(
        functools.partial(_inproj_kernel, ctx=True),
        grid=(CTX_TILES,),
        in_specs=[pl.BlockSpec((TM, D_MODEL), tok), pl.BlockSpec(mod_block, lambda i: (l, 0, 0, 0))]
                 + common + [_fixed_spec((D_POOL, D_POOL), l), _fixed_spec((1, D_POOL), l)],
        out_specs=ctx_specs,
        out_shape=ctx_shapes,
        compiler_params=_params(1),
        name="mixer_inproj_ctx",
    )(x, mod, *common_args, prm['pool_w'], prm['pool_scale'])
    lat_shapes, lat_specs = outs(N_LAT, BF16)
    rope_spec = pl.BlockSpec((TM, D_ATTN), lambda i: (i % LAT_TILES_PER_SEQ, 0))
    lat = pl.pallas_call(
        functools.partial(_inproj_kernel, ctx=False),
        grid=(N_TILES - CTX_TILES,),
        in_specs=[pl.BlockSpec((TM, D_MODEL), lambda i: (CTX_TILES + i, 0)),
                  pl.BlockSpec(mod_block, lambda i: (l, 1 + i // LAT_TILES_PER_SEQ, 0, 0))]
                 + common + [rope_spec, rope_spec],
        out_specs=lat_specs,
        out_shape=lat_shapes,
        compiler_params=_params(1),
        name="mixer_inproj_lat",
    )(x, mod, *common_args, cos_t, sin_t)
    return ctx, lat


def _pool_kernel(p_ref, w_ref, s_ref, o_ref):
    o_ref[...] = _pool_math(p_ref[...], w_ref[...], s_ref[...], DEC_SEQ)


def _pool_lat(p_lat, prm, l):
    return pl.pallas_call(
        _pool_kernel,
        grid=(DEC_BATCH,),
        in_specs=[pl.BlockSpec((DEC_SEQ, D_POOL), lambda b: (b, 0)),
                  _fixed_spec((D_POOL, D_POOL), l), _fixed_spec((1, D_POOL), l)],
        out_specs=pl.BlockSpec((DEC_SEQ, D_POOL), lambda b: (b, 0)),
        out_shape=jax.ShapeDtypeStruct((N_LAT, D_POOL), F32),
        compiler_params=_params(1),
        name="pool_lat",
    )(p_lat, prm['pool_w'], prm['pool_scale'])


def _attn_kernel(q_ref, kc_ref, kl_ref, vc_ref, vl_ref, o_ref):
    k = jnp.concatenate([kc_ref[...], kl_ref[...]], axis=0)
    v = jnp.concatenate([vc_ref[...], vl_ref[...]], axis=0)
    o_ref[...] = _attn_math([(q_ref[...], k, v)], True)[0]


def _attn_lat(q_lat, k_lat, v_lat, cache_k, cache_v, l):
    q_per_seq = DEC_SEQ // ATT_TQ
    cache_spec = pl.BlockSpec((None, None, PAST_LEN, D_KV), lambda b, j: (b, l, 0, 0))
    seq_spec = pl.BlockSpec((DEC_SEQ, D_KV), lambda b, j: (b, 0))
    return pl.pallas_call(
        _attn_kernel,
        grid=(DEC_BATCH, q_per_seq),
        in_specs=[pl.BlockSpec((ATT_TQ, D_ATTN), lambda b, j: (b * q_per_seq + j, 0)),
                  cache_spec, seq_spec, cache_spec, seq_spec],
        out_specs=pl.BlockSpec((ATT_TQ, D_ATTN), lambda b, j: (b * q_per_seq + j, 0)),
        out_shape=jax.ShapeDtypeStruct((N_LAT, D_ATTN), F32),
        compiler_params=_params(2),
        name="attn_lat",
    )(q_lat, cache_k, k_lat, cache_v, v_lat)


def _ssm_prep_kernel(lam_ref, br_ref, bi_ref, bmat_ref, lamb_ref):
    lre = lam_ref[0:1, :]
    lim = lam_ref[1:2, :]
    dt = jnp.exp(lam_ref[2:3, :])
    mag = jnp.exp(lre * dt)
    ar = mag * jnp.cos(lim * dt)
    ai = mag * jnp.sin(lim * dt)
    lamb_ref[0:1, :] = ar
    lamb_ref[1:2, :] = ai
    mag_s = jnp.exp(lre * dt * SSM_LAT_SEG)
    lamb_ref[2:3, :] = mag_s * jnp.cos(lim * dt * SSM_LAT_SEG)
    lamb_ref[3:4, :] = mag_s * jnp.sin(lim * dt * SSM_LAT_SEG)
    den = lre * lre + lim * lim
    cr = ((ar - 1.0) * lre + ai * lim) / den
    ci = (ai * lre - (ar - 1.0) * lim) / den
    n_rep = SSM_LANES // LANES
    br = jnp.concatenate([br_ref[...]] * n_rep, axis=1)
    bi = jnp.concatenate([bi_ref[...]] * n_rep, axis=1)
    lane = lax.broadcasted_iota(jnp.int32, br.shape, 1)
    row_g = lax.broadcasted_iota(jnp.int32, br.shape, 0) // SSM_GROUP
    lane_g = 2 * (lane // (2 * LANES)) + (lane % LANES) // SSM_STATE
    is_re = (lane % (2 * LANES)) < LANES
    b_bar = jnp.where(is_re, cr * br - ci * bi, cr * bi + ci * br)
    bmat_ref[...] = jnp.where(row_g == lane_g, b_bar, 0.0).astype(BF16)


def _ssm_prep(lam_rows, b_re_placed, b_im_placed):
    n = DEPTH * 2
    return pl.pallas_call(
        _ssm_prep_kernel,
        grid=(n,),
        in_specs=[pl.BlockSpec((None, 3, SSM_LANES), lambda i: (i, 0, 0)),
                  pl.BlockSpec((None, D_SSM, LANES), lambda i: (i, 0, 0)),
                  pl.BlockSpec((None, D_SSM, LANES), lambda i: (i, 0, 0))],
        out_specs=[pl.BlockSpec((None, D_SSM, SSM_LANES), lambda i: (i, 0, 0)),
                   pl.BlockSpec((None, 4, SSM_LANES), lambda i: (i, 0, 0))],
        out_shape=[jax.ShapeDtypeStruct((n, D_SSM, SSM_LANES), BF16),
                   jax.ShapeDtypeStruct((n, 4, SSM_LANES), F32)],
        compiler_params=_params(1),
        name="ssm_prep",
    )(lam_rows, b_re_placed, b_im_placed)


def _ssm_kernel(ua_ref, ub_ref, h0_ref, bmat_ref, cmat_ref, lamb_ref, d_ref, wg_ref, bg_ref,
                y_ref, fin_ref, ui_ref, buf0_ref, buf1_ref, yi_ref, *, n_per):
    seg_len = SEQ
    ssm_w = SSM_W
    n_rows = SUBLANES * seg_len
    n_cb = ssm_w // (2 * LANES)

    def step_rows(t):
        return pl.ds(pl.multiple_of(t * SUBLANES, SUBLANES), SUBLANES)

    block_steps = SSM_BLOCK // SUBLANES
    blocks = [slice(b * SSM_BLOCK, (b + 1) * SSM_BLOCK) for b in range(n_rows // SSM_BLOCK)]

    def interleave(b):
        for t in range(b * block_steps, (b + 1) * block_steps):
            rows = slice(t * SUBLANES, (t + 1) * SUBLANES)
            ui_ref[rows, 0:LANES] = ua_ref[pl.ds(t, SUBLANES, stride=seg_len), :]
            ui_ref[rows, LANES:2 * LANES] = ub_ref[pl.ds(t, SUBLANES, stride=seg_len), :]

    seg = lax.broadcasted_iota(jnp.int32, (SUBLANES, LANES), 0) % n_per

    chains = [(dirn, k) for dirn in range(2) for k in range(n_cb)]
    n_parts = SSM_LANES // ssm_w
    bufs = (buf0_ref, buf1_ref)
    assert n_parts == len(bufs)

    def fill(part, rows):
        ub = ui_ref[rows, :].astype(BF16)
        for dirn, k in chains:
            c0 = k * 2 * LANES
            lane0 = part * ssm_w + c0
            bufs[part][dirn, rows, c0:c0 + 2 * LANES] = jnp.dot(
                ub, bmat_ref[dirn, :, lane0:lane0 + 2 * LANES], preferred_element_type=F32)

    def readout(part, rows):
        lo = part * ssm_w
        y = jnp.dot(bufs[part][0, rows, :].astype(BF16), cmat_ref[0, lo:lo + ssm_w, :],
                    preferred_element_type=F32)
        y = y + jnp.dot(bufs[part][1, rows, :].astype(BF16), cmat_ref[1, lo:lo + ssm_w, :],
                        preferred_element_type=F32)
        if part == 0:
            y = y + d_ref[...] * ui_ref[rows, :]
            yi_ref[0, rows, :] = y[:, :LANES]
            yi_ref[1, rows, :] = y[:, LANES:]
        else:
            yi_ref[0, rows, :] += y[:, :LANES]
            yi_ref[1, rows, :] += y[:, LANES:]

    def lam_rows(part, r):
        out = []
        for dirn, k in chains:
            c0 = part * ssm_w + k * 2 * LANES
            out.append(jnp.broadcast_to(lamb_ref[dirn, r:r + 1, c0:c0 + LANES], (SUBLANES, LANES)))
        return out

    def scan(part, init, store, side_work=None):
        buf = bufs[part]
        a_re, a_im = lam_rows(part, 0), lam_rows(part, 1)

        def step(i, hs):
            rows = (step_rows(i), step_rows(seg_len - 1 - i))
            new = []
            for c, (dirn, k) in enumerate(chains):
                re_sl = slice(k * 2 * LANES, k * 2 * LANES + LANES)
                im_sl = slice(k * 2 * LANES + LANES, (k + 1) * 2 * LANES)
                hr, hi = hs[2 * c], hs[2 * c + 1]
                nr = a_re[c] * hr - a_im[c] * hi + buf[dirn, rows[dirn], re_sl]
                ni = a_re[c] * hi + a_im[c] * hr + buf[dirn, rows[dirn], im_sl]
                if store:
                    buf[dirn, rows[dirn], re_sl] = nr
                    buf[dirn, rows[dirn], im_sl] = ni
                new += [nr, ni]
            return tuple(new)

        if side_work is None:
            return lax.fori_loop(0, seg_len, step, tuple(init), unroll=2)

        steps = SSM_BLOCK // SUBLANES

        def block(blk, hs):
            side_work(pl.ds(pl.multiple_of(blk * SSM_BLOCK, SSM_BLOCK), SSM_BLOCK))
            for j in range(steps):
                hs = step(blk * steps + j, hs)
            return hs

        return lax.fori_loop(0, n_rows // SSM_BLOCK, block, tuple(init))

    def initial_states(part, local):
        h0 = []
        for dirn, k in chains:
            c0 = part * ssm_w + k * 2 * LANES
            h0 += [h0_ref[dirn, :, c0:c0 + LANES], h0_ref[dirn, :, c0 + LANES:c0 + 2 * LANES]]
        if local is None:
            return h0
        s_re, s_im = lam_rows(part, 2), lam_rows(part, 3)
        init = []
        for c, (dirn, k) in enumerate(chains):
            edge = seg == (0 if dirn == 0 else n_per - 1)
            shift = 1 if dirn == 0 else SUBLANES - 1
            cr, ci = h0[2 * c], h0[2 * c + 1]
            lr = pltpu.roll(local[2 * c], shift, axis=0)
            li = pltpu.roll(local[2 * c + 1], shift, axis=0)
            for _ in range(n_per - 1):
                pr = pltpu.roll(cr, shift, axis=0)
                pi = pltpu.roll(ci, shift, axis=0)
                cr = jnp.where(edge, h0[2 * c], s_re[c] * pr - s_im[c] * pi + lr)
                ci = jnp.where(edge, h0[2 * c + 1], s_re[c] * pi + s_im[c] * pr + li)
            init += [cr, ci]
        return init

    zeros = [jnp.zeros((SUBLANES, LANES), F32)] * (2 * len(chains))
    interleave(0)
    for b in range(1, len(blocks)):
        fill(0, blocks[b - 1])
        interleave(b)
    fill(0, blocks[-1])
    assert n_parts == 2
    for part in range(n_parts):
        local = scan(part, zeros, False) if n_per > 1 else None
        side_work = functools.partial(fill, 1) if part == 0 else functools.partial(readout, 0)
        hs = scan(part, initial_states(part, local), True, side_work)
        for c, (dirn, k) in enumerate(chains):
            c0 = part * ssm_w + k * 2 * LANES
            fin_ref[dirn, :, c0:c0 + LANES] = hs[2 * c]
            fin_ref[dirn, :, c0 + LANES:c0 + 2 * LANES] = hs[2 * c + 1]

    def glu(rows):
        g = _gelu(jnp.concatenate([yi_ref[0, rows, :], yi_ref[1, rows, :]], axis=1))
        z = jnp.dot(g.astype(BF16), wg_ref[...], preferred_element_type=F32) + bg_ref[...]
        o = g * jax.nn.sigmoid(z)
        yi_ref[0, rows, :] = o[:, :LANES]
        yi_ref[1, rows, :] = o[:, LANES:]

    def deinterleave(b):
        for tb in range(b * block_steps // SUBLANES, (b + 1) * block_steps // SUBLANES):
            for s_idx in range(SUBLANES):
                dst = slice(s_idx * seg_len + tb * SUBLANES, s_idx * seg_len + (tb + 1) * SUBLANES)
                src = pl.ds(tb * SUBLANES * SUBLANES + s_idx, SUBLANES, stride=SUBLANES)
                y_ref[dst, 0:LANES] = yi_ref[0, src, :]
                y_ref[dst, LANES:2 * LANES] = yi_ref[1, src, :]

    last = n_parts - 1
    readout(last, blocks[0])
    for b in range(1, len(blocks)):
        readout(last, blocks[b])
        glu(blocks[b - 1])
        deinterleave(b - 1)
    glu(blocks[-1])
    deinterleave(len(blocks) - 1)


def _ssm(u_ctx, u_lat, prm, l):
    weights = [_fixed_spec((2, D_SSM, SSM_LANES), l), _fixed_spec((2, SSM_LANES, D_SSM), l),
               _fixed_spec((2, 4, SSM_LANES), l), _fixed_spec((1, D_SSM), l),
               _fixed_spec((D_SSM, D_SSM), l), _fixed_spec((1, D_SSM), l)]
    rows = SUBLANES * SEQ
    scratch = [pltpu.VMEM((rows, D_SSM), F32), pltpu.VMEM((2, rows, SSM_W), F32),
               pltpu.VMEM((2, rows, SSM_W), F32), pltpu.VMEM((2, rows, LANES), F32)]
    tile_a = pl.BlockSpec((rows, LANES), lambda i: (i, 0))
    tile_y = pl.BlockSpec((rows, D_SSM), lambda i: (i, 0))
    fin_spec = pl.BlockSpec((None, 2, SUBLANES, SSM_LANES), lambda i: (i, 0, 0, 0))

    tail = (prm['ssm_b'], prm['ssm_c'], prm['ssm_lam'], prm['ssm_d'], prm['ssm_w_glu'], prm['ssm_b_glu'])
    n_ctx_tiles = N_CTX // rows
    h0_ctx = jnp.zeros((2, SUBLANES, SSM_LANES), F32)
    y_ctx, fin_ctx = pl.pallas_call(
        functools.partial(_ssm_kernel, n_per=1),
        grid=(n_ctx_tiles,),
        in_specs=[tile_a, tile_a, _fixed_spec((2, SUBLANES, SSM_LANES))] + weights,
        out_specs=[tile_y, fin_spec],
        out_shape=[jax.ShapeDtypeStruct((N_CTX, D_SSM), F32),
                   jax.ShapeDtypeStruct((n_ctx_tiles, 2, SUBLANES, SSM_LANES), F32)],
        scratch_shapes=scratch,
        compiler_params=_params(1),
        name="ssm_ctx",
    )(*u_ctx, h0_ctx, *tail)
    h0_spec = pl.BlockSpec((None, None, 2, SUBLANES, SSM_LANES), lambda b: (l, b, 0, 0, 0))
    y_lat, _ = pl.pallas_call(
        functools.partial(_ssm_kernel, n_per=SSM_LAT_SEGS),
        grid=(DEC_BATCH,),
        in_specs=[tile_a, tile_a, h0_spec] + weights,
        out_specs=[tile_y, fin_spec],
        out_shape=[jax.ShapeDtypeStruct((N_LAT, D_SSM), F32),
                   jax.ShapeDtypeStruct((DEC_BATCH, 2, SUBLANES, SSM_LANES), F32)],
        scratch_shapes=scratch,
        compiler_params=_params(1),
        name="ssm_lat",
    )(*u_lat, prm['ssm_h0'], *tail)
    return (y_ctx, y_lat), fin_ctx


def _rope_tables():
    rows = DEC_SEQ // GRID_W
    row_idx = jnp.repeat(jnp.arange(rows), GRID_W).astype(F32)
    col_idx = jnp.tile(jnp.arange(GRID_W), rows).astype(F32)
    n_freq = HEAD_DIM // 4
    inv = ROPE_THETA ** (-jnp.arange(n_freq, dtype=F32) / n_freq)
    ang = jnp.concatenate([row_idx[:, None] * inv, col_idx[:, None] * inv], axis=-1)
    cos = jnp.cos(ang)
    sin = jnp.sin(ang)
    cos_h = jnp.concatenate([cos, cos], axis=-1)
    sin_h = jnp.concatenate([-sin, sin], axis=-1)
    return jnp.tile(cos_h, (1, N_HEADS)), jnp.tile(sin_h, (1, N_HEADS))


def _state_lanes(a):
    return a.reshape(a.shape[:-2] + (N_SSM_GROUPS // 2, 2 * SSM_STATE))


def _ssm_lane_rows(a):
    a = _state_lanes(a)
    return jnp.stack([a, a], axis=-2).reshape(a.shape[:-2] + (SSM_LANES,))


def _ssm_pack_state(re, im):
    return jnp.stack([_state_lanes(re), _state_lanes(im)], axis=-2).reshape(re.shape[:-2] + (SSM_LANES,))


def _ssm_unpack_state(s):
    s = s.reshape(s.shape[:-1] + (N_SSM_GROUPS // 2, 2, 2 * SSM_STATE))
    shp = s.shape[:-3] + (N_SSM_GROUPS, SSM_STATE)
    return s[..., 0, :].reshape(shp), s[..., 1, :].reshape(shp)


def _ssm_group_mask():
    ch_g = jnp.arange(D_SSM) // SSM_GROUP
    lane = jnp.arange(SSM_LANES)
    lane_g = 2 * (lane // (2 * LANES)) + (lane % LANES) // SSM_STATE
    return ch_g[:, None] == lane_g[None, :]


def _b_rows(b):
    bt = jnp.swapaxes(b, -1, -2).reshape(DEPTH * 2, D_SSM, SSM_STATE)
    return jnp.concatenate([bt, bt], axis=-1)


def _place_c(c_re, c_im):
    def rows(c):
        ct = jnp.moveaxis(c, -1, -3).reshape(c.shape[:-3] + (SSM_STATE, D_SSM))
        return jnp.tile(ct, (1,) * (ct.ndim - 2) + (SSM_LANES // SSM_STATE, 1))
    is_re = (jnp.arange(SSM_LANES) % (2 * LANES) < LANES)[:, None]
    return jnp.where(_ssm_group_mask().T, jnp.where(is_re, rows(c_re), -rows(c_im)), 0.0)


def _block_diag_pool(pool_w):
    tiled = jnp.tile(pool_w.reshape(DEPTH, D_POOL, POOL_GROUP), (1, 1, len(POOL_WINDOWS)))
    grp = jnp.arange(D_POOL) // POOL_GROUP
    return jnp.where(grp[:, None] == grp[None, :], tiled, 0.0)


def kernel(x_prompt, x_sample, cache_k, cache_v, state_ssm_re, state_ssm_im, c, c_ctx, w_mod, b_mod, ln_g, ln_b, ffn_w1, ffn_w2, w_in, w_out, pool_w, pool_scale, ssm_lam_re, ssm_lam_im, ssm_log_step, ssm_b_re, ssm_b_im, ssm_c_re, ssm_c_im, ssm_d, ssm_w_glu, ssm_b_glu, q_norm, k_norm, chunk_ln_g, chunk_ln_b, chunk_w_s, chunk_b_s):
    cond = jnp.concatenate([c_ctx[None, :], c, jnp.zeros((N_COND - 1 - DEC_BATCH, D_MODEL), F32)], axis=0)
    mod = _modulation(cond, w_mod, b_mod)
    cos_t, sin_t = _rope_tables()

    log_step = jnp.broadcast_to(ssm_log_step[..., None], ssm_lam_re.shape)
    lam_rows = jnp.stack([_ssm_lane_rows(ssm_lam_re), _ssm_lane_rows(ssm_lam_im),
                          _ssm_lane_rows(log_step)], axis=-2)
    bmat, lamb = _ssm_prep(lam_rows.reshape(DEPTH * 2, 3, SSM_LANES),
                           _b_rows(ssm_b_re), _b_rows(ssm_b_im))
    h0 = _ssm_pack_state(state_ssm_re, state_ssm_im)
    h0_lat = jnp.zeros((DEPTH, DEC_BATCH, 2, SSM_LAT_SEGS, SSM_LANES), F32)
    h0_lat = h0_lat.at[:, :, 0, 0].set(jnp.swapaxes(h0[:, :, 0], 0, 1))
    h0_lat = h0_lat.at[:, :, 1, SSM_LAT_SEGS - 1].set(jnp.swapaxes(h0[:, :, 1], 0, 1))

    vec = lambda a: a[..., None, :]
    prm = {
        'ln_g': vec(ln_g), 'ln_b': vec(ln_b),
        'ffn_w1': ffn_w1.astype(BF16), 'ffn_w2': ffn_w2.astype(BF16),
        'w_in': w_in.astype(BF16), 'w_out': w_out.astype(BF16),
        'q_norm': vec(jnp.tile(q_norm, (1, N_HEADS))), 'k_norm': vec(jnp.tile(k_norm, (1, N_KV_HEADS))),
        'chunk_ln_g': vec(chunk_ln_g), 'chunk_ln_b': vec(chunk_ln_b),
        'chunk_w_s': chunk_w_s.astype(BF16),
        'chunk_b_s': jnp.repeat(jnp.swapaxes(chunk_b_s, 1, 2), D_CHUNK // N_CHUNK_HEADS, axis=2),
        'pool_w': _block_diag_pool(pool_w).astype(BF16), 'pool_scale': vec(pool_scale),
        'ssm_b': bmat.reshape(DEPTH, 2, D_SSM, SSM_LANES),
        'ssm_lam': lamb.reshape(DEPTH, 2, 4, SSM_LANES),
        'ssm_c': _place_c(ssm_c_re, ssm_c_im).astype(BF16),
        'ssm_h0': h0_lat,
        'ssm_d': vec(ssm_d), 'ssm_w_glu': ssm_w_glu.astype(BF16), 'ssm_b_glu': vec(ssm_b_glu),
    }
    cache_k = cache_k.reshape(DEC_BATCH, DEPTH, PAST_LEN, D_KV).astype(BF16)
    cache_v = cache_v.reshape(DEC_BATCH, DEPTH, PAST_LEN, D_KV).astype(BF16)

    x = (x_prompt.reshape(N_CTX, D_MODEL), x_sample.reshape(N_LAT, D_MODEL))
    ks, vs, s_re, s_im = [], [], [], []
    for l in range(DEPTH):
        x = _ffn_sublayer(x, mod, prm, l, 0)
        ctx_out, lat_out = _inproj(x, mod, prm, l, cos_t, sin_t)
        ua_ctx, ub_ctx, yc_ctx, k_ctx, v_ctx, yp_ctx, ya_ctx = ctx_out
        ua_lat, ub_lat, yc_lat, k_lat, v_lat, p_lat, q_lat = lat_out
        yp_lat = _pool_lat(p_lat, prm, l)
        y_ssm, fin = _ssm((ua_ctx, ub_ctx), (ua_lat, ub_lat), prm, l)
        ya_lat = _attn_lat(q_lat, k_lat, v_lat, cache_k, cache_v, l)
        x = _ffn_sublayer(x, mod, prm, l, 1, split_out=(l == DEPTH - 1),
                          mixer=((yp_ctx, yp_lat), y_ssm, (ya_ctx, ya_lat), (yc_ctx, yc_lat)))
        ks.append(k_ctx)
        vs.append(v_ctx)
        f_re, f_im = _ssm_unpack_state(jnp.transpose(fin, (0, 2, 1, 3)).reshape(BATCH, 2, SSM_LANES))
        s_re.append(f_re)
        s_im.append(f_im)
    y_p = x[0].reshape(BATCH, SEQ, D_MODEL)
    y_s = x[1].reshape(DEC_BATCH, DEC_SEQ, D_MODEL)
    def cache(ts):
        t = jnp.stack(ts).reshape(DEPTH, N_KV_HEADS, HEAD_DIM, BATCH, SEQ)
        return jnp.transpose(t, (3, 0, 4, 1, 2))

    return (y_p, y_s, cache(ks), cache(vs), jnp.stack(s_re, axis=1), jnp.stack(s_im, axis=1))
```

```python
import functools
import math

import jax
import jax.numpy as jnp
from jax import lax
from jax.experimental import pallas as pl
from jax.experimental.pallas import tpu as pltpu

F32 = jnp.float32
BF16 = jnp.bfloat16

D_MODEL = 1024
BATCH = 32
SEQ = 256
DEPTH = 2
DEC_BATCH = 2
DEC_SEQ = 2048
PAST_LEN = 512
GRID_W = 64
D_POOL = 256
D_SSM = 256
D_ATTN = 256
D_CHUNK = 256
D_HEADS = 256
POOL_WINDOWS = (2, 4, 8, 16)
POOL_GROUP = 64
SSM_GROUP = 16
N_SSM_GROUPS = 16
SSM_STATE = 64
HEAD_DIM = 64
N_HEADS = 4
N_KV_HEADS = 2
D_KV = 128
CHUNK = 128
N_CHUNK_HEADS = 4
D_FF = 2816
N_MOD = 9
D_IN = 1536
ALPHA = (2 * DEPTH) ** 0.25
LN_EPS = 1e-5
RMS_EPS = 1e-6
ROPE_THETA = 10000.0

LANES = 128
SUBLANES = 8
TM = 512
FFN_SUB = 256
ATT_TQ = 512
N_CTX = BATCH * SEQ
N_LAT = DEC_BATCH * DEC_SEQ
N_TOK = N_CTX + N_LAT
CTX_TILES = N_CTX // TM
LAT_TILES_PER_SEQ = DEC_SEQ // TM
N_TILES = N_TOK // TM
N_COND = 8
MOD_COLS = 2304
POOL_PAD = 8
SSM_LANES = 2 * N_SSM_GROUPS * SSM_STATE
SSM_W = 1024
SSM_LAT_SEGS = SUBLANES
SSM_LAT_SEG = DEC_SEQ // SSM_LAT_SEGS
SSM_BLOCK = 512
VMEM_LIMIT = 56 * 1024 * 1024


def _params(n_grid):
    return pltpu.CompilerParams(dimension_semantics=("arbitrary",) * n_grid,
                                vmem_limit_bytes=VMEM_LIMIT)


def _fixed_spec(tail, *lead):
    idx = tuple(lead) + (0,) * len(tail)
    return pl.BlockSpec((None,) * len(lead) + tuple(tail), lambda *_: idx, pipeline_mode=pl.Buffered(1))


def _cond_row(i):
    return jnp.where(i < CTX_TILES, 0, 1 + (i - CTX_TILES) // LAT_TILES_PER_SEQ)


def _mod_spec(l):
    return pl.BlockSpec((None, None, N_MOD, D_MODEL), lambda i: (l, _cond_row(i), 0, 0))


def _tok_tile(i):
    return (i, 0)


def _ctx_tile(i):
    return (jnp.minimum(i, CTX_TILES - 1), 0)


def _lat_tile(i):
    return (jnp.maximum(i - CTX_TILES, 0), 0)


def _layer_norm(y, g, b):
    mu = jnp.mean(y, axis=-1, keepdims=True)
    d = y - mu
    var = jnp.mean(d * d, axis=-1, keepdims=True)
    return d * lax.rsqrt(var + LN_EPS) * g + b


def _gelu(x):
    return 0.5 * x * (1.0 + jnp.tanh(math.sqrt(2.0 / math.pi) * (x + 0.044715 * (x * x * x))))


def _silu(x):
    return x * jax.nn.sigmoid(x)


def _split_bf16(a):
    hi = a.astype(BF16)
    return hi, (a - hi.astype(F32)).astype(BF16)


def _mod_kernel(cond_ref, w_ref, b_ref, o_ref):
    a_hi, a_lo = _split_bf16(_silu(cond_ref[...]))
    w_hi, w_lo = _split_bf16(w_ref[...])
    acc = jnp.dot(a_hi, w_lo, preferred_element_type=F32)
    acc = acc + jnp.dot(a_lo, w_hi, preferred_element_type=F32)
    acc = acc + jnp.dot(a_hi, w_hi, preferred_element_type=F32)
    o_ref[...] = acc + b_ref[...]


def _modulation(cond, w_mod, b_mod):
    tn = MOD_COLS
    out = pl.pallas_call(
        _mod_kernel,
        grid=(DEPTH, N_MOD * D_MODEL // tn),
        in_specs=[
            pl.BlockSpec((N_COND, D_MODEL), lambda l, j: (0, 0)),
            pl.BlockSpec((None, D_MODEL, tn), lambda l, j: (l, 0, j)),
            pl.BlockSpec((None, 1, tn), lambda l, j: (l, 0, j)),
        ],
        out_specs=pl.BlockSpec((None, N_COND, tn), lambda l, j: (l, 0, j)),
        out_shape=jax.ShapeDtypeStruct((DEPTH, N_COND, N_MOD * D_MODEL), F32),
        compiler_params=_params(2),
        name="modulation",
    )(cond, w_mod, b_mod.reshape(DEPTH, 1, N_MOD * D_MODEL))
    return out.reshape(DEPTH, N_COND, N_MOD, D_MODEL)


def _ffn_kernel(*refs, mod_off, split_in, split_out, mixer_out):
    n_in = 2 if split_in else 1
    n_mix = 11 if mixer_out else 0
    mix_refs = refs[n_in:n_in + n_mix]
    mod_ref, w1_ref, w2_ref, g_ref, b_ref = refs[n_in + n_mix:n_in + n_mix + 5]
    out_refs = refs[n_in + n_mix + 5:]
    is_ctx = pl.program_id(0) < CTX_TILES
    sh = mod_ref[mod_off:mod_off + 1, :]
    sc = mod_ref[mod_off + 1:mod_off + 2, :]
    gate = mod_ref[mod_off + 2:mod_off + 3, :]

    def load_x(rows):
        if split_in:
            return jnp.where(is_ctx, refs[0][rows, :], refs[1][rows, :])
        return refs[0][rows, :]

    def project(rows):
        if not mixer_out:
            return None
        wo_ref = mix_refs[8]
        parts = [jnp.where(is_ctx, mix_refs[2 * j][rows, :], mix_refs[2 * j + 1][rows, :])
                 for j in range(4)]
        y = None
        for j, part in enumerate(parts):
            r = jnp.dot(part.astype(BF16), wo_ref[D_HEADS * j:D_HEADS * (j + 1), :],
                        preferred_element_type=F32)
            y = r if y is None else y + r
        return y

    def mixer_norm(x, y):
        if not mixer_out:
            return x
        go_ref, bo_ref = mix_refs[9:]
        return _layer_norm(ALPHA * x + mod_ref[5:6, :] * y, go_ref[...], bo_ref[...])

    def up(x):
        h = (x * (1.0 + sc) + sh).astype(BF16)
        return jnp.dot(h, w1_ref[...], preferred_element_type=F32)

    def down(gu):
        a = (_silu(gu[:, :D_FF]) * gu[:, D_FF:]).astype(BF16)
        return jnp.dot(a, w2_ref[...], preferred_element_type=F32)

    def finish(x, f):
        return _layer_norm(ALPHA * x + (0.5 * gate) * f, g_ref[...], b_ref[...])

    row_slices = [slice(j * FFN_SUB, (j + 1) * FFN_SUB) for j in range(TM // FFN_SUB)]
    n = len(row_slices)
    ys = []
    x = mixer_norm(load_x(row_slices[0]), project(row_slices[0]))
    gu = up(x)
    for j in range(n):
        y_next = project(row_slices[j + 1]) if j + 1 < n else None
        f = down(gu)
        if j + 1 < n:
            x_next = mixer_norm(load_x(row_slices[j + 1]), y_next)
            gu = up(x_next)
        ys.append(finish(x, f))
        if j + 1 < n:
            x = x_next

    def store(o_ref):
        for j, rows in enumerate(row_slices):
            o_ref[rows, :] = ys[j]

    if split_out:
        @pl.when(is_ctx)
        def _():
            store(out_refs[0])

        @pl.when(jnp.logical_not(is_ctx))
        def _():
            store(out_refs[1])
    else:
        store(out_refs[0])


def _ffn_sublayer(xs, mod, prm, l, sub, split_out=False, mixer=None):
    split_in = isinstance(xs, tuple)
    if split_in:
        x_specs = [pl.BlockSpec((TM, D_MODEL), _ctx_tile), pl.BlockSpec((TM, D_MODEL), _lat_tile)]
    else:
        xs = (xs,)
        x_specs = [pl.BlockSpec((TM, D_MODEL), _tok_tile)]
    mix_args, mix_specs = (), []
    if mixer is not None:
        y_pool, y_ssm, y_attn, y_chunk = mixer
        mix_args = (*y_pool, *y_ssm, *y_attn, *y_chunk, prm['w_out'], prm['ln_g'], prm['ln_b'])
        mix_specs = [pl.BlockSpec((TM, D_HEADS), _ctx_tile), pl.BlockSpec((TM, D_HEADS), _lat_tile)] * 4 + [
            _fixed_spec((D_MODEL, D_MODEL), l), _fixed_spec((1, D_MODEL), l, 1), _fixed_spec((1, D_MODEL), l, 1)]
    if split_out:
        out_specs = [pl.BlockSpec((TM, D_MODEL), _ctx_tile), pl.BlockSpec((TM, D_MODEL), _lat_tile)]
        out_shape = [jax.ShapeDtypeStruct((N_CTX, D_MODEL), F32), jax.ShapeDtypeStruct((N_LAT, D_MODEL), F32)]
    else:
        out_specs = pl.BlockSpec((TM, D_MODEL), _tok_tile)
        out_shape = jax.ShapeDtypeStruct((N_TOK, D_MODEL), F32)
    ln_idx = 2 * sub
    return pl.pallas_call(
        functools.partial(_ffn_kernel, mod_off=6 * sub, split_in=split_in, split_out=split_out,
                          mixer_out=mixer is not None),
        grid=(N_TILES,),
        in_specs=x_specs + mix_specs + [
            _mod_spec(l),
            _fixed_spec((D_MODEL, 2 * D_FF), l, sub),
            _fixed_spec((D_FF, D_MODEL), l, sub),
            _fixed_spec((1, D_MODEL), l, ln_idx),
            _fixed_spec((1, D_MODEL), l, ln_idx),
        ],
        out_specs=out_specs,
        out_shape=out_shape,
        compiler_params=_params(1),
        name="ffn_sublayer",
    )(*xs, *mix_args, mod, prm['ffn_w1'], prm['ffn_w2'], prm['ln_g'], prm['ln_b'])


def _pool_math(x, w_bd, scale, seq_len):
    n = seq_len + 2 * POOL_PAD
    zpad = jnp.zeros((POOL_PAD, D_POOL), F32)
    e = jnp.concatenate([zpad, x, zpad], axis=0)

    def prev(a, d):
        return pltpu.roll(a, d, axis=0)

    def nxt(a, d):
        return pltpu.roll(a, n - d, axis=0)

    s2 = e + prev(e, 1)
    s4 = prev(s2, 1) + nxt(s2, 1)
    s8 = prev(s4, 2) + nxt(s4, 2)
    s16 = prev(s8, 4) + nxt(s8, 4)
    lane = lax.broadcasted_iota(jnp.int32, (seq_len, D_POOL), 1)
    grp = lane // POOL_GROUP
    sl = slice(POOL_PAD, POOL_PAD + seq_len)
    s = jnp.where(grp == 0, s2[sl], jnp.where(grp == 1, s4[sl], jnp.where(grp == 2, s8[sl], s16[sl])))
    half = jnp.where(grp == 0, 1, jnp.where(grp == 1, 2, jnp.where(grp == 2, 4, 8)))
    t = lax.broadcasted_iota(jnp.int32, (seq_len, D_POOL), 0)
    cnt = jnp.minimum(t + half, seq_len) - jnp.maximum(t - half, 0)
    d = (s / cnt.astype(F32) - x).astype(BF16)
    return jnp.dot(d, w_bd, preferred_element_type=F32) * scale


def _attn_math(problems, mxu_sums):
    nt = (((1,), (1,)), ((), ()))
    tq = problems[0][0].shape[0]
    lo = lax.broadcasted_iota(jnp.int32, (tq, LANES), 1) < HEAD_DIM
    scores = []
    for q, k, _ in problems:
        q = q * (HEAD_DIM ** -0.5 * math.log2(math.e))
        q_lo = q[:, :LANES]
        q_hi = q[:, LANES:]
        qs = [jnp.where(lo, q_lo, 0.0), jnp.where(lo, pltpu.roll(q_lo, HEAD_DIM, axis=1), 0.0),
              jnp.where(lo, 0.0, pltpu.roll(q_hi, HEAD_DIM, axis=1)), jnp.where(lo, 0.0, q_hi)]
        for g in range(N_KV_HEADS):
            qq = jnp.concatenate([qs[2 * g], qs[2 * g + 1]], axis=0).astype(BF16)
            scores.append(lax.dot_general(qq, k, nt, preferred_element_type=F32))
    results = []
    for i, (_, _, v) in enumerate(problems):
        if mxu_sums:
            lo_k = lax.broadcasted_iota(jnp.int32, v.shape, 1) < HEAD_DIM
            one = jnp.ones_like(v)
            v_g = [jnp.where(lo_k, v, one), jnp.where(lo_k, one, v)]
        outs = []
        for g in range(N_KV_HEADS):
            s = scores[N_KV_HEADS * i + g]
            p = jnp.exp2(s - jnp.max(s, axis=-1, keepdims=True))
            if mxu_sums:
                o = jnp.dot(p.astype(BF16), v_g[g], preferred_element_type=F32)
            else:
                l = jnp.sum(p, axis=-1, keepdims=True)
                o = jnp.dot(p.astype(BF16), v, preferred_element_type=F32) / l
            outs += [o[:tq], o[tq:]]
        if mxu_sums:
            r = [pltpu.roll(o, HEAD_DIM, axis=1) for o in outs]
            out_lo = jnp.where(lo, outs[0], r[1]) / jnp.where(lo, r[0], outs[1])
            out_hi = jnp.where(lo, r[2], outs[3]) / jnp.where(lo, outs[2], r[3])
        else:
            out_lo = jnp.where(lo, outs[0], pltpu.roll(outs[1], HEAD_DIM, axis=1))
            out_hi = jnp.where(lo, pltpu.roll(outs[2], HEAD_DIM, axis=1), outs[3])
        results.append(jnp.concatenate([out_lo, out_hi], axis=1))
    return results


def _seg_rms(x, gain, n_lanes):
    parts = []
    for j in range(n_lanes // LANES):
        xs = x[:, j * LANES:(j + 1) * LANES]
        sq = xs * xs
        lo = lax.broadcasted_iota(jnp.int32, xs.shape, 1) < HEAD_DIM
        s_all = jnp.sum(sq, axis=-1, keepdims=True)
        s_lo = jnp.sum(jnp.where(lo, sq, 0.0), axis=-1, keepdims=True)
        ms = jnp.where(lo, s_lo, s_all - s_lo) * (1.0 / HEAD_DIM)
        parts.append(xs * lax.rsqrt(ms + RMS_EPS))
    y = parts[0] if len(parts) == 1 else jnp.concatenate(parts, axis=1)
    return y * gain


def _rope(x, cos_t, sin_t, n_lanes):
    parts = []
    for j in range(n_lanes // LANES):
        xs = x[:, j * LANES:(j + 1) * LANES]
        first = (lax.broadcasted_iota(jnp.int32, xs.shape, 1) % HEAD_DIM) < (HEAD_DIM // 2)
        partner = jnp.where(first, pltpu.roll(xs, LANES - HEAD_DIM // 2, axis=1),
                            pltpu.roll(xs, HEAD_DIM // 2, axis=1))
        cs = cos_t[:, j * LANES:(j + 1) * LANES]
        sn = sin_t[:, j * LANES:(j + 1) * LANES]
        parts.append(xs * cs + partner * sn)
    return parts[0] if len(parts) == 1 else jnp.concatenate(parts, axis=1)


def _inproj_kernel(*refs, ctx):
    if ctx:
        (x_ref, mod_ref, w_ref, qn_ref, kn_ref, cg_ref, cb_ref, ws_ref, bs_ref, pw_ref, ps_ref,
         ua_ref, ub_ref, chunk_ref, k_ref, v_ref, yp_ref, ya_ref) = refs
    else:
        (x_ref, mod_ref, w_ref, qn_ref, kn_ref, cg_ref, cb_ref, ws_ref, bs_ref, cos_ref, sin_ref,
         ua_ref, ub_ref, chunk_ref, k_ref, v_ref, p_ref, q_ref) = refs
    x = x_ref[...]
    sh = mod_ref[3:4, :]
    sc = mod_ref[4:5, :]
    h = (x * (1.0 + sc) + sh).astype(BF16)
    proj = jnp.dot(h, w_ref[...], preferred_element_type=F32)
    p = proj[:, 0:256]
    ua_ref[...] = proj[:, 256:384]
    ub_ref[...] = proj[:, 384:512]
    q = _seg_rms(proj[:, 512:768], qn_ref[...], D_ATTN)
    k = _seg_rms(proj[:, 768:896], kn_ref[...], D_KV)
    v = proj[:, 896:1024]
    if not ctx:
        cos_t = cos_ref[...]
        sin_t = sin_ref[...]
        q = _rope(q, cos_t, sin_t, D_ATTN)
        k = _rope(k, cos_t, sin_t, D_KV)
    zu = _gelu(proj[:, 1024:1280])
    zv = _layer_norm(_gelu(proj[:, 1280:1536]), cg_ref[...], cb_ref[...])
    head = lax.broadcasted_iota(jnp.int32, (CHUNK, D_CHUNK), 1) // (D_CHUNK // N_CHUNK_HEADS)
    for c in range(TM // CHUNK):
        vb = zv[c * CHUNK:(c + 1) * CHUNK, :].astype(BF16)
        mixed = bs_ref[...]
        for hd in range(N_CHUNK_HEADS):
            r = jnp.dot(ws_ref[hd], vb, preferred_element_type=F32)
            mixed = mixed + jnp.where(head == hd, r, 0.0)
        chunk_ref[c * CHUNK:(c + 1) * CHUNK, :] = (zu[c * CHUNK:(c + 1) * CHUNK, :] * mixed).astype(BF16)
    if ctx:
        k_ref[...] = k.T
        v_ref[...] = v.T
        for s in range(TM // SEQ):
            r = slice(s * SEQ, (s + 1) * SEQ)
            yp_ref[r, :] = _pool_math(p[r], pw_ref[...], ps_ref[...], SEQ).astype(BF16)
            ya_ref[r, :] = _attn_math([(q[r], k[r].astype(BF16), v[r].astype(BF16))], False)[0].astype(BF16)
    else:
        k_ref[...] = k.astype(BF16)
        v_ref[...] = v.astype(BF16)
        p_ref[...] = p
        q_ref[...] = q


def _inproj(x, mod, prm, l, cos_t, sin_t):
    tok = lambda i: (i, 0)
    common = [_fixed_spec((D_MODEL, D_IN), l), _fixed_spec((1, D_ATTN), l), _fixed_spec((1, D_KV), l),
              _fixed_spec((1, D_CHUNK), l), _fixed_spec((1, D_CHUNK), l),
              _fixed_spec((N_CHUNK_HEADS, CHUNK, CHUNK), l), _fixed_spec((CHUNK, D_CHUNK), l)]
    common_args = (prm['w_in'], prm['q_norm'], prm['k_norm'], prm['chunk_ln_g'], prm['chunk_ln_b'],
                   prm['chunk_w_s'], prm['chunk_b_s'])
    mod_block = (None, None, N_MOD, D_MODEL)

    def outs(rows, kv_dtype, tail_dtype):
        widths = [(LANES, F32), (LANES, F32), (D_CHUNK, BF16), (D_KV, kv_dtype), (D_KV, kv_dtype),
                  (D_HEADS, tail_dtype), (D_HEADS, tail_dtype)]
        return ([jax.ShapeDtypeStruct((rows, w), dt) for w, dt in widths],
                [pl.BlockSpec((TM, w), tok) for w, _ in widths])

    ctx_shapes, ctx_specs = outs(N_CTX, F32, BF16)
    for j in (3, 4):
        ctx_shapes[j] = jax.ShapeDtypeStruct((D_KV, N_CTX), F32)
        ctx_specs[j] = pl.BlockSpec((D_KV, TM), lambda i: (0, i))
    ctx = pl.pallas_call(
        functools.partial(_inproj_kernel, ctx=True),
        grid=(CTX_TILES,),
        in_specs=[pl.BlockSpec((TM, D_MODEL), tok), pl.BlockSpec(mod_block, lambda i: (l, 0, 0, 0))]
                 + common + [_fixed_spec((D_POOL, D_POOL), l), _fixed_spec((1, D_POOL), l)],
        out_specs=ctx_specs,
        out_shape=ctx_shapes,
        compiler_params=_params(1),
        name="mixer_inproj_ctx",
    )(x, mod, *common_args, prm['pool_w'], prm['pool_scale'])
    lat_shapes, lat_specs = outs(N_LAT, BF16, F32)
    rope_spec = pl.BlockSpec((TM, D_ATTN), lambda i: (i % LAT_TILES_PER_SEQ, 0))
    lat = pl.pallas_call(
        functools.partial(_inproj_kernel, ctx=False),
        grid=(N_TILES - CTX_TILES,),
        in_specs=[pl.BlockSpec((TM, D_MODEL), lambda i: (CTX_TILES + i, 0)),
                  pl.BlockSpec(mod_block, lambda i: (l, 1 + i // LAT_TILES_PER_SEQ, 0, 0))]
                 + common + [rope_spec, rope_spec],
        out_specs=lat_specs,
        out_shape=lat_shapes,
        compiler_params=_params(1),
        name="mixer_inproj_lat",
    )(x, mod, *common_args, cos_t, sin_t)
    return ctx, lat


def _latent_mix_kernel(q_ref, kc_ref, kl_ref, vc_ref, vl_ref, p_ref, pw_ref, ps_ref, ya_ref, yp_ref):
    k = jnp.concatenate([kc_ref[...], kl_ref[...]], axis=0)
    v = jnp.concatenate([vc_ref[...], vl_ref[...]], axis=0)
    ya_ref[...] = _attn_math([(q_ref[...], k, v)], True)[0].astype(BF16)

    @pl.when(pl.program_id(1) == 0)
    def _():
        yp_ref[...] = _pool_math(p_ref[...], pw_ref[...], ps_ref[...], DEC_SEQ).astype(BF16)


def _latent_mix(q_lat, k_lat, v_lat, p_lat, cache_k, cache_v, prm, l):
    q_per_seq = DEC_SEQ // ATT_TQ
    cache_spec = pl.BlockSpec((None, None, PAST_LEN, D_KV), lambda b, j: (b, l, 0, 0))
    seq_kv = pl.BlockSpec((DEC_SEQ, D_KV), lambda b, j: (b, 0))
    seq_pool = pl.BlockSpec((DEC_SEQ, D_POOL), lambda b, j: (b, 0))
    q_tile = pl.BlockSpec((ATT_TQ, D_ATTN), lambda b, j: (b * q_per_seq + j, 0))
    return pl.pallas_call(
        _latent_mix_kernel,
        grid=(DEC_BATCH, q_per_seq),
        in_specs=[q_tile, cache_spec, seq_kv, cache_spec, seq_kv, seq_pool,
                  _fixed_spec((D_POOL, D_POOL), l), _fixed_spec((1, D_POOL), l)],
        out_specs=[q_tile, seq_pool],
        out_shape=[jax.ShapeDtypeStruct((N_LAT, D_ATTN), BF16), jax.ShapeDtypeStruct((N_LAT, D_POOL), BF16)],
        compiler_params=_params(2),
        name="latent_mix",
    )(q_lat, cache_k, k_lat, cache_v, v_lat, p_lat, prm['pool_w'], prm['pool_scale'])


def _ssm_prep_kernel(lam_ref, br_ref, bi_ref, bmat_ref, lamb_ref):
    lre = lam_ref[0:1, :]
    lim = lam_ref[1:2, :]
    dt = jnp.exp(lam_ref[2:3, :])
    mag = jnp.exp(lre * dt)
    ar = mag * jnp.cos(lim * dt)
    ai = mag * jnp.sin(lim * dt)
    lamb_ref[0:1, :] = ar
    lamb_ref[1:2, :] = ai
    mag_s = jnp.exp(lre * dt * SSM_LAT_SEG)
    lamb_ref[2:3, :] = mag_s * jnp.cos(lim * dt * SSM_LAT_SEG)
    lamb_ref[3:4, :] = mag_s * jnp.sin(lim * dt * SSM_LAT_SEG)
    den = lre * lre + lim * lim
    cr = ((ar - 1.0) * lre + ai * lim) / den
    ci = (ai * lre - (ar - 1.0) * lim) / den
    n_rep = SSM_LANES // LANES
    br = jnp.concatenate([br_ref[...]] * n_rep, axis=1)
    bi = jnp.concatenate([bi_ref[...]] * n_rep, axis=1)
    lane = lax.broadcasted_iota(jnp.int32, br.shape, 1)
    row_g = lax.broadcasted_iota(jnp.int32, br.shape, 0) // SSM_GROUP
    lane_g = 2 * (lane // (2 * LANES)) + (lane % LANES) // SSM_STATE
    is_re = (lane % (2 * LANES)) < LANES
    b_bar = jnp.where(is_re, cr * br - ci * bi, cr * bi + ci * br)
    bmat_ref[...] = jnp.where(row_g == lane_g, b_bar, 0.0).astype(BF16)


def _ssm_prep(lam_rows, b_re_placed, b_im_placed):
    n = DEPTH * 2
    return pl.pallas_call(
        _ssm_prep_kernel,
        grid=(n,),
        in_specs=[pl.BlockSpec((None, 3, SSM_LANES), lambda i: (i, 0, 0)),
                  pl.BlockSpec((None, D_SSM, LANES), lambda i: (i, 0, 0)),
                  pl.BlockSpec((None, D_SSM, LANES), lambda i: (i, 0, 0))],
        out_specs=[pl.BlockSpec((None, D_SSM, SSM_LANES), lambda i: (i, 0, 0)),
                   pl.BlockSpec((None, 4, SSM_LANES), lambda i: (i, 0, 0))],
        out_shape=[jax.ShapeDtypeStruct((n, D_SSM, SSM_LANES), BF16),
                   jax.ShapeDtypeStruct((n, 4, SSM_LANES), F32)],
        compiler_params=_params(1),
        name="ssm_prep",
    )(lam_rows, b_re_placed, b_im_placed)


def _ssm_kernel(ua_ref, ub_ref, h0_ref, bmat_ref, cmat_ref, lamb_ref, d_ref, wg_ref, bg_ref,
                y_ref, fin_ref, ui_ref, buf0_ref, buf1_ref, yi_ref, *, n_per):
    seg_len = SEQ
    ssm_w = SSM_W
    n_rows = SUBLANES * seg_len
    n_cb = ssm_w // (2 * LANES)

    def step_rows(t):
        return pl.ds(pl.multiple_of(t * SUBLANES, SUBLANES), SUBLANES)

    block_steps = SSM_BLOCK // SUBLANES
    blocks = [slice(b * SSM_BLOCK, (b + 1) * SSM_BLOCK) for b in range(n_rows // SSM_BLOCK)]

    def interleave(b):
        for t in range(b * block_steps, (b + 1) * block_steps):
            rows = slice(t * SUBLANES, (t + 1) * SUBLANES)
            ui_ref[rows, 0:LANES] = ua_ref[pl.ds(t, SUBLANES, stride=seg_len), :]
            ui_ref[rows, LANES:2 * LANES] = ub_ref[pl.ds(t, SUBLANES, stride=seg_len), :]

    seg = lax.broadcasted_iota(jnp.int32, (SUBLANES, LANES), 0) % n_per

    chains = [(dirn, k) for dirn in range(2) for k in range(n_cb)]
    n_parts = SSM_LANES // ssm_w
    bufs = (buf0_ref, buf1_ref)
    assert n_parts == len(bufs)

    def fill(part, rows):
        ub = ui_ref[rows, :].astype(BF16)
        for dirn, k in chains:
            c0 = k * 2 * LANES
            lane0 = part * ssm_w + c0
            bufs[part][dirn, rows, c0:c0 + 2 * LANES] = jnp.dot(
                ub, bmat_ref[dirn, :, lane0:lane0 + 2 * LANES], preferred_element_type=F32)

    def readout(part, rows):
        lo = part * ssm_w
        y = jnp.dot(bufs[part][0, rows, :].astype(BF16), cmat_ref[0, lo:lo + ssm_w, :],
                    preferred_element_type=F32)
        y = y + jnp.dot(bufs[part][1, rows, :].astype(BF16), cmat_ref[1, lo:lo + ssm_w, :],
                        preferred_element_type=F32)
        if part == 0:
            y = y + d_ref[...] * ui_ref[rows, :]
            yi_ref[0, rows, :] = y[:, :LANES]
            yi_ref[1, rows, :] = y[:, LANES:]
        else:
            yi_ref[0, rows, :] += y[:, :LANES]
            yi_ref[1, rows, :] += y[:, LANES:]

    def lam_rows(part, r):
        out = []
        for dirn, k in chains:
            c0 = part * ssm_w + k * 2 * LANES
            out.append(jnp.broadcast_to(lamb_ref[dirn, r:r + 1, c0:c0 + LANES], (SUBLANES, LANES)))
        return out

    def scan(part, init, store, side_work=None):
        buf = bufs[part]
        a_re, a_im = lam_rows(part, 0), lam_rows(part, 1)

        def step(i, hs):
            rows = (step_rows(i), step_rows(seg_len - 1 - i))
            new = []
            for c, (dirn, k) in enumerate(chains):
                re_sl = slice(k * 2 * LANES, k * 2 * LANES + LANES)
                im_sl = slice(k * 2 * LANES + LANES, (k + 1) * 2 * LANES)
                hr, hi = hs[2 * c], hs[2 * c + 1]
                nr = a_re[c] * hr - a_im[c] * hi + buf[dirn, rows[dirn], re_sl]
                ni = a_re[c] * hi + a_im[c] * hr + buf[dirn, rows[dirn], im_sl]
                if store:
                    buf[dirn, rows[dirn], re_sl] = nr
                    buf[dirn, rows[dirn], im_sl] = ni
                new += [nr, ni]
            return tuple(new)

        if side_work is None:
            return lax.fori_loop(0, seg_len, step, tuple(init), unroll=2)

        steps = SSM_BLOCK // SUBLANES

        def block(blk, hs):
            side_work(pl.ds(pl.multiple_of(blk * SSM_BLOCK, SSM_BLOCK), SSM_BLOCK))
            for j in range(steps):
                hs = step(blk * steps + j, hs)
            return hs

        return lax.fori_loop(0, n_rows // SSM_BLOCK, block, tuple(init))

    def initial_states(part, local):
        h0 = []
        for dirn, k in chains:
            c0 = part * ssm_w + k * 2 * LANES
            h0 += [h0_ref[dirn, :, c0:c0 + LANES], h0_ref[dirn, :, c0 + LANES:c0 + 2 * LANES]]
        if local is None:
            return h0
        s_re, s_im = lam_rows(part, 2), lam_rows(part, 3)
        init = []
        for c, (dirn, k) in enumerate(chains):
            edge = seg == (0 if dirn == 0 else n_per - 1)
            shift = 1 if dirn == 0 else SUBLANES - 1
            cr, ci = h0[2 * c], h0[2 * c + 1]
            lr = pltpu.roll(local[2 * c], shift, axis=0)
            li = pltpu.roll(local[2 * c + 1], shift, axis=0)
            for _ in range(n_per - 1):
                pr = pltpu.roll(cr, shift, axis=0)
                pi = pltpu.roll(ci, shift, axis=0)
                cr = jnp.where(edge, h0[2 * c], s_re[c] * pr - s_im[c] * pi + lr)
                ci = jnp.where(edge, h0[2 * c + 1], s_re[c] * pi + s_im[c] * pr + li)
            init += [cr, ci]
        return init

    zeros = [jnp.zeros((SUBLANES, LANES), F32)] * (2 * len(chains))
    interleave(0)
    for b in range(1, len(blocks)):
        fill(0, blocks[b - 1])
        interleave(b)
    fill(0, blocks[-1])
    assert n_parts == 2
    for part in range(n_parts):
        local = scan(part, zeros, False) if n_per > 1 else None
        side_work = functools.partial(fill, 1) if part == 0 else functools.partial(readout, 0)
        hs = scan(part, initial_states(part, local), True, side_work)
        for c, (dirn, k) in enumerate(chains):
            c0 = part * ssm_w + k * 2 * LANES
            fin_ref[dirn, :, c0:c0 + LANES] = hs[2 * c]
            fin_ref[dirn, :, c0 + LANES:c0 + 2 * LANES] = hs[2 * c + 1]

    def glu(rows):
        g = _gelu(jnp.concatenate([yi_ref[0, rows, :], yi_ref[1, rows, :]], axis=1))
        z = jnp.dot(g.astype(BF16), wg_ref[...], preferred_element_type=F32) + bg_ref[...]
        o = g * jax.nn.sigmoid(z)
        yi_ref[0, rows, :] = o[:, :LANES]
        yi_ref[1, rows, :] = o[:, LANES:]

    def deinterleave(b):
        for tb in range(b * block_steps // SUBLANES, (b + 1) * block_steps // SUBLANES):
            for s_idx in range(SUBLANES):
                dst = slice(s_idx * seg_len + tb * SUBLANES, s_idx * seg_len + (tb + 1) * SUBLANES)
                src = pl.ds(tb * SUBLANES * SUBLANES + s_idx, SUBLANES, stride=SUBLANES)
                y_ref[dst, 0:LANES] = yi_ref[0, src, :]
                y_ref[dst, LANES:2 * LANES] = yi_ref[1, src, :]

    last = n_parts - 1
    readout(last, blocks[0])
    for b in range(1, len(blocks)):
        readout(last, blocks[b])
        glu(blocks[b - 1])
        deinterleave(b - 1)
    glu(blocks[-1])
    deinterleave(len(blocks) - 1)


def _ssm(u_ctx, u_lat, prm, l):
    weights = [_fixed_spec((2, D_SSM, SSM_LANES), l), _fixed_spec((2, SSM_LANES, D_SSM), l),
               _fixed_spec((2, 4, SSM_LANES), l), _fixed_spec((1, D_SSM), l),
               _fixed_spec((D_SSM, D_SSM), l), _fixed_spec((1, D_SSM), l)]
    rows = SUBLANES * SEQ
    scratch = [pltpu.VMEM((rows, D_SSM), F32), pltpu.VMEM((2, rows, SSM_W), F32),
               pltpu.VMEM((2, rows, SSM_W), F32), pltpu.VMEM((2, rows, LANES), F32)]
    tile_a = pl.BlockSpec((rows, LANES), lambda i: (i, 0))
    tile_y = pl.BlockSpec((rows, D_SSM), lambda i: (i, 0))
    fin_spec = pl.BlockSpec((None, 2, SUBLANES, SSM_LANES), lambda i: (i, 0, 0, 0))

    tail = (prm['ssm_b'], prm['ssm_c'], prm['ssm_lam'], prm['ssm_d'], prm['ssm_w_glu'], prm['ssm_b_glu'])
    n_ctx_tiles = N_CTX // rows
    h0_ctx = jnp.zeros((2, SUBLANES, SSM_LANES), F32)
    y_ctx, fin_ctx = pl.pallas_call(
        functools.partial(_ssm_kernel, n_per=1),
        grid=(n_ctx_tiles,),
        in_specs=[tile_a, tile_a, _fixed_spec((2, SUBLANES, SSM_LANES))] + weights,
        out_specs=[tile_y, fin_spec],
        out_shape=[jax.ShapeDtypeStruct((N_CTX, D_SSM), F32),
                   jax.ShapeDtypeStruct((n_ctx_tiles, 2, SUBLANES, SSM_LANES), F32)],
        scratch_shapes=scratch,
        compiler_params=_params(1),
        name="ssm_ctx",
    )(*u_ctx, h0_ctx, *tail)
    h0_spec = pl.BlockSpec((None, None, 2, SUBLANES, SSM_LANES), lambda b: (l, b, 0, 0, 0))
    y_lat, _ = pl.pallas_call(
        functools.partial(_ssm_kernel, n_per=SSM_LAT_SEGS),
        grid=(DEC_BATCH,),
        in_specs=[tile_a, tile_a, h0_spec] + weights,
        out_specs=[tile_y, fin_spec],
        out_shape=[jax.ShapeDtypeStruct((N_LAT, D_SSM), F32),
                   jax.ShapeDtypeStruct((DEC_BATCH, 2, SUBLANES, SSM_LANES), F32)],
        scratch_shapes=scratch,
        compiler_params=_params(1),
        name="ssm_lat",
    )(*u_lat, prm['ssm_h0'], *tail)
    return (y_ctx, y_lat), fin_ctx


def _rope_tables():
    rows = DEC_SEQ // GRID_W
    row_idx = jnp.repeat(jnp.arange(rows), GRID_W).astype(F32)
    col_idx = jnp.tile(jnp.arange(GRID_W), rows).astype(F32)
    n_freq = HEAD_DIM // 4
    inv = ROPE_THETA ** (-jnp.arange(n_freq, dtype=F32) / n_freq)
    ang = jnp.concatenate([row_idx[:, None] * inv, col_idx[:, None] * inv], axis=-1)
    cos = jnp.cos(ang)
    sin = jnp.sin(ang)
    cos_h = jnp.concatenate([cos, cos], axis=-1)
    sin_h = jnp.concatenate([-sin, sin], axis=-1)
    return jnp.tile(cos_h, (1, N_HEADS)), jnp.tile(sin_h, (1, N_HEADS))


def _state_lanes(a):
    return a.reshape(a.shape[:-2] + (N_SSM_GROUPS // 2, 2 * SSM_STATE))


def _ssm_lane_rows(a):
    a = _state_lanes(a)
    return jnp.stack([a, a], axis=-2).reshape(a.shape[:-2] + (SSM_LANES,))


def _ssm_pack_state(re, im):
    return jnp.stack([_state_lanes(re), _state_lanes(im)], axis=-2).reshape(re.shape[:-2] + (SSM_LANES,))


def _ssm_unpack_state(s):
    s = s.reshape(s.shape[:-1] + (N_SSM_GROUPS // 2, 2, 2 * SSM_STATE))
    shp = s.shape[:-3] + (N_SSM_GROUPS, SSM_STATE)
    return s[..., 0, :].reshape(shp), s[..., 1, :].reshape(shp)


def _ssm_group_mask():
    ch_g = jnp.arange(D_SSM) // SSM_GROUP
    lane = jnp.arange(SSM_LANES)
    lane_g = 2 * (lane // (2 * LANES)) + (lane % LANES) // SSM_STATE
    return ch_g[:, None] == lane_g[None, :]


def _b_rows(b):
    bt = jnp.swapaxes(b, -1, -2).reshape(DEPTH * 2, D_SSM, SSM_STATE)
    return jnp.concatenate([bt, bt], axis=-1)


def _place_c(c_re, c_im):
    def rows(c):
        ct = jnp.moveaxis(c, -1, -3).reshape(c.shape[:-3] + (SSM_STATE, D_SSM))
        return jnp.tile(ct, (1,) * (ct.ndim - 2) + (SSM_LANES // SSM_STATE, 1))
    is_re = (jnp.arange(SSM_LANES) % (2 * LANES) < LANES)[:, None]
    return jnp.where(_ssm_group_mask().T, jnp.where(is_re, rows(c_re), -rows(c_im)), 0.0)


def _block_diag_pool(pool_w):
    tiled = jnp.tile(pool_w.reshape(DEPTH, D_POOL, POOL_GROUP), (1, 1, len(POOL_WINDOWS)))
    grp = jnp.arange(D_POOL) // POOL_GROUP
    return jnp.where(grp[:, None] == grp[None, :], tiled, 0.0)


def kernel(x_prompt, x_sample, cache_k, cache_v, state_ssm_re, state_ssm_im, c, c_ctx, w_mod, b_mod, ln_g, ln_b, ffn_w1, ffn_w2, w_in, w_out, pool_w, pool_scale, ssm_lam_re, ssm_lam_im, ssm_log_step, ssm_b_re, ssm_b_im, ssm_c_re, ssm_c_im, ssm_d, ssm_w_glu, ssm_b_glu, q_norm, k_norm, chunk_ln_g, chunk_ln_b, chunk_w_s, chunk_b_s):
    cond = jnp.concatenate([c_ctx[None, :], c, jnp.zeros((N_COND - 1 - DEC_BATCH, D_MODEL), F32)], axis=0)
    mod = _modulation(cond, w_mod, b_mod)
    cos_t, sin_t = _rope_tables()

    log_step = jnp.broadcast_to(ssm_log_step[..., None], ssm_lam_re.shape)
    lam_rows = jnp.stack([_ssm_lane_rows(ssm_lam_re), _ssm_lane_rows(ssm_lam_im),
                          _ssm_lane_rows(log_step)], axis=-2)
    bmat, lamb = _ssm_prep(lam_rows.reshape(DEPTH * 2, 3, SSM_LANES),
                           _b_rows(ssm_b_re), _b_rows(ssm_b_im))
    h0 = _ssm_pack_state(state_ssm_re, state_ssm_im)
    h0_lat = jnp.zeros((DEPTH, DEC_BATCH, 2, SSM_LAT_SEGS, SSM_LANES), F32)
    h0_lat = h0_lat.at[:, :, 0, 0].set(jnp.swapaxes(h0[:, :, 0], 0, 1))
    h0_lat = h0_lat.at[:, :, 1, SSM_LAT_SEGS - 1].set(jnp.swapaxes(h0[:, :, 1], 0, 1))

    vec = lambda a: a[..., None, :]
    prm = {
        'ln_g': vec(ln_g), 'ln_b': vec(ln_b),
        'ffn_w1': ffn_w1.astype(BF16), 'ffn_w2': ffn_w2.astype(BF16),
        'w_in': w_in.astype(BF16), 'w_out': w_out.astype(BF16),
        'q_norm': vec(jnp.tile(q_norm, (1, N_HEADS))), 'k_norm': vec(jnp.tile(k_norm, (1, N_KV_HEADS))),
        'chunk_ln_g': vec(chunk_ln_g), 'chunk_ln_b': vec(chunk_ln_b),
        'chunk_w_s': chunk_w_s.astype(BF16),
        'chunk_b_s': jnp.repeat(jnp.swapaxes(chunk_b_s, 1, 2), D_CHUNK // N_CHUNK_HEADS, axis=2),
        'pool_w': _block_diag_pool(pool_w).astype(BF16), 'pool_scale': vec(pool_scale),
        'ssm_b': bmat.reshape(DEPTH, 2, D_SSM, SSM_LANES),
        'ssm_lam': lamb.reshape(DEPTH, 2, 4, SSM_LANES),
        'ssm_c': _place_c(ssm_c_re, ssm_c_im).astype(BF16),
        'ssm_h0': h0_lat,
        'ssm_d': vec(ssm_d), 'ssm_w_glu': ssm_w_glu.astype(BF16), 'ssm_b_glu': vec(ssm_b_glu),
    }
    cache_k = cache_k.reshape(DEC_BATCH, DEPTH, PAST_LEN, D_KV).astype(BF16)
    cache_v = cache_v.reshape(DEC_BATCH, DEPTH, PAST_LEN, D_KV).astype(BF16)

    x = (x_prompt.reshape(N_CTX, D_MODEL), x_sample.reshape(N_LAT, D_MODEL))
    ks, vs, s_re, s_im = [], [], [], []
    for l in range(DEPTH):
        x = _ffn_sublayer(x, mod, prm, l, 0)
        ctx_out, lat_out = _inproj(x, mod, prm, l, cos_t, sin_t)
        ua_ctx, ub_ctx, yc_ctx, k_ctx, v_ctx, yp_ctx, ya_ctx = ctx_out
        ua_lat, ub_lat, yc_lat, k_lat, v_lat, p_lat, q_lat = lat_out
        y_ssm, fin = _ssm((ua_ctx, ub_ctx), (ua_lat, ub_lat), prm, l)
        ya_lat, yp_lat = _latent_mix(q_lat, k_lat, v_lat, p_lat, cache_k, cache_v, prm, l)
        x = _ffn_sublayer(x, mod, prm, l, 1, split_out=(l == DEPTH - 1),
                          mixer=((yp_ctx, yp_lat), y_ssm, (ya_ctx, ya_lat), (yc_ctx, yc_lat)))
        ks.append(k_ctx)
        vs.append(v_ctx)
        f_re, f_im = _ssm_unpack_state(jnp.transpose(fin, (0, 2, 1, 3)).reshape(BATCH, 2, SSM_LANES))
        s_re.append(f_re)
        s_im.append(f_im)
    y_p = x[0].reshape(BATCH, SEQ, D_MODEL)
    y_s = x[1].reshape(DEC_BATCH, DEC_SEQ, D_MODEL)
    def cache(ts):
        t = jnp.stack(ts).reshape(DEPTH, N_KV_HEADS, HEAD_DIM, BATCH, SEQ)
        return jnp.transpose(t, (3, 0, 4, 1, 2))

    return (y_p, y_s, cache(ks), cache(vs), jnp.stack(s_re, axis=1), jnp.stack(s_im, axis=1))
```

```python
import functools
import math

import jax
import jax.numpy as jnp
from jax import lax
from jax.experimental import pallas as pl
from jax.experimental.pallas import tpu as pltpu

F32 = jnp.float32
BF16 = jnp.bfloat16

D_MODEL = 1024
BATCH = 32
SEQ = 256
DEPTH = 2
DEC_BATCH = 2
DEC_SEQ = 2048
PAST_LEN = 512
GRID_W = 64
D_POOL = 256
D_SSM = 256
D_ATTN = 256
D_CHUNK = 256
D_HEADS = 256
POOL_WINDOWS = (2, 4, 8, 16)
POOL_GROUP = 64
SSM_GROUP = 16
N_SSM_GROUPS = 16
SSM_STATE = 64
HEAD_DIM = 64
N_HEADS = 4
N_KV_HEADS = 2
D_KV = 128
CHUNK = 128
N_CHUNK_HEADS = 4
D_FF = 2816
N_MOD = 9
D_IN = 1536
ALPHA = (2 * DEPTH) ** 0.25
LN_EPS = 1e-5
RMS_EPS = 1e-6
ROPE_THETA = 10000.0

LANES = 128
SUBLANES = 8
TM = 512
FFN_SUB = 256
FFN_CONV_STEPS = 16
ATT_TQ = 512
N_CTX = BATCH * SEQ
N_LAT = DEC_BATCH * DEC_SEQ
N_TOK = N_CTX + N_LAT
CTX_TILES = N_CTX // TM
LAT_TILES_PER_SEQ = DEC_SEQ // TM
N_TILES = N_TOK // TM
N_COND = 8
MOD_COLS = 2304
POOL_PAD = 8
SSM_LANES = 2 * N_SSM_GROUPS * SSM_STATE
SSM_W = 1024
SSM_LAT_SEGS = SUBLANES
SSM_LAT_SEG = DEC_SEQ // SSM_LAT_SEGS
SSM_BLOCK = 512
VMEM_LIMIT = 56 * 1024 * 1024


def _params(n_grid):
    return pltpu.CompilerParams(dimension_semantics=("arbitrary",) * n_grid,
                                vmem_limit_bytes=VMEM_LIMIT)


def _fixed_spec(tail, *lead):
    idx = tuple(lead) + (0,) * len(tail)
    return pl.BlockSpec((None,) * len(lead) + tuple(tail), lambda *_: idx, pipeline_mode=pl.Buffered(1))


def _cond_row(i):
    return jnp.where(i < CTX_TILES, 0, 1 + (i - CTX_TILES) // LAT_TILES_PER_SEQ)


def _mod_spec(l):
    return pl.BlockSpec((None, None, N_MOD, D_MODEL), lambda i: (l, _cond_row(i), 0, 0))


def _tok_tile(i):
    return (i, 0)


def _ctx_tile(i):
    return (jnp.minimum(i, CTX_TILES - 1), 0)


def _lat_tile(i):
    return (jnp.maximum(i - CTX_TILES, 0), 0)


def _layer_norm(y, g, b):
    mu = jnp.mean(y, axis=-1, keepdims=True)
    d = y - mu
    var = jnp.mean(d * d, axis=-1, keepdims=True)
    return d * lax.rsqrt(var + LN_EPS) * g + b


def _gelu(x):
    return 0.5 * x * (1.0 + jnp.tanh(math.sqrt(2.0 / math.pi) * (x + 0.044715 * (x * x * x))))


def _silu(x):
    return x * jax.nn.sigmoid(x)


def _split_bf16(a):
    hi = a.astype(BF16)
    return hi, (a - hi.astype(F32)).astype(BF16)


def _mod_kernel(cond_ref, w_ref, b_ref, o_ref):
    a_hi, a_lo = _split_bf16(_silu(cond_ref[...]))
    w_hi, w_lo = _split_bf16(w_ref[...])
    acc = jnp.dot(a_hi, w_lo, preferred_element_type=F32)
    acc = acc + jnp.dot(a_lo, w_hi, preferred_element_type=F32)
    acc = acc + jnp.dot(a_hi, w_hi, preferred_element_type=F32)
    o_ref[...] = acc + b_ref[...]


def _modulation(cond, w_mod, b_mod):
    tn = MOD_COLS
    out = pl.pallas_call(
        _mod_kernel,
        grid=(DEPTH, N_MOD * D_MODEL // tn),
        in_specs=[
            pl.BlockSpec((N_COND, D_MODEL), lambda l, j: (0, 0)),
            pl.BlockSpec((None, D_MODEL, tn), lambda l, j: (l, 0, j)),
            pl.BlockSpec((None, 1, tn), lambda l, j: (l, 0, j)),
        ],
        out_specs=pl.BlockSpec((None, N_COND, tn), lambda l, j: (l, 0, j)),
        out_shape=jax.ShapeDtypeStruct((DEPTH, N_COND, N_MOD * D_MODEL), F32),
        compiler_params=_params(2),
        name="modulation",
    )(cond, w_mod, b_mod.reshape(DEPTH, 1, N_MOD * D_MODEL))
    return out.reshape(DEPTH, N_COND, N_MOD, D_MODEL)


def _ffn_kernel(*refs, mod_off, split_in, split_out, mixer_out, convert):
    n_in = 2 if split_in else 1
    n_mix = 11 if mixer_out else 0
    n_conv = 2 if convert else 0
    mix_refs = refs[n_in:n_in + n_mix]
    mod_ref, w1_ref, w2_ref, g_ref, b_ref = refs[n_in + n_mix:n_in + n_mix + 5]
    conv_in = refs[n_in + n_mix + 5:n_in + n_mix + 5 + n_conv]
    out_refs = refs[n_in + n_mix + 5 + n_conv:]
    if convert:
        conv_out = out_refs[-n_conv:]
        out_refs = out_refs[:-n_conv]

        @pl.when(pl.program_id(0) < FFN_CONV_STEPS)
        def _():
            for src, dst in zip(conv_in, conv_out):
                dst[...] = src[...].astype(BF16)
    is_ctx = pl.program_id(0) < CTX_TILES
    sh = mod_ref[mod_off:mod_off + 1, :]
    sc = mod_ref[mod_off + 1:mod_off + 2, :]
    gate = mod_ref[mod_off + 2:mod_off + 3, :]

    def load_x(rows):
        if split_in:
            return jnp.where(is_ctx, refs[0][rows, :], refs[1][rows, :])
        return refs[0][rows, :]

    def project(rows):
        if not mixer_out:
            return None
        wo_ref = mix_refs[8]
        parts = [jnp.where(is_ctx, mix_refs[2 * j][rows, :], mix_refs[2 * j + 1][rows, :])
                 for j in range(4)]
        y = None
        for j, part in enumerate(parts):
            r = jnp.dot(part.astype(BF16), wo_ref[D_HEADS * j:D_HEADS * (j + 1), :],
                        preferred_element_type=F32)
            y = r if y is None else y + r
        return y

    def mixer_norm(x, y):
        if not mixer_out:
            return x
        go_ref, bo_ref = mix_refs[9:]
        return _layer_norm(ALPHA * x + mod_ref[5:6, :] * y, go_ref[...], bo_ref[...])

    def up(x):
        h = (x * (1.0 + sc) + sh).astype(BF16)
        return jnp.dot(h, w1_ref[...], preferred_element_type=F32)

    def down(gu):
        a = (_silu(gu[:, :D_FF]) * gu[:, D_FF:]).astype(BF16)
        return jnp.dot(a, w2_ref[...], preferred_element_type=F32)

    def finish(x, f):
        return _layer_norm(ALPHA * x + (0.5 * gate) * f, g_ref[...], b_ref[...])

    row_slices = [slice(j * FFN_SUB, (j + 1) * FFN_SUB) for j in range(TM // FFN_SUB)]
    n = len(row_slices)
    ys = []
    x = mixer_norm(load_x(row_slices[0]), project(row_slices[0]))
    gu = up(x)
    for j in range(n):
        y_next = project(row_slices[j + 1]) if j + 1 < n else None
        f = down(gu)
        if j + 1 < n:
            x_next = mixer_norm(load_x(row_slices[j + 1]), y_next)
            gu = up(x_next)
        ys.append(finish(x, f))
        if j + 1 < n:
            x = x_next

    def store(o_ref):
        for j, rows in enumerate(row_slices):
            o_ref[rows, :] = ys[j]

    if split_out:
        @pl.when(is_ctx)
        def _():
            store(out_refs[0])

        @pl.when(jnp.logical_not(is_ctx))
        def _():
            store(out_refs[1])
    else:
        store(out_refs[0])


def _ffn_sublayer(xs, mod, prm, l, sub, weights, convert=None, split_out=False, mixer=None):
    split_in = isinstance(xs, tuple)
    if split_in:
        x_specs = [pl.BlockSpec((TM, D_MODEL), _ctx_tile), pl.BlockSpec((TM, D_MODEL), _lat_tile)]
    else:
        xs = (xs,)
        x_specs = [pl.BlockSpec((TM, D_MODEL), _tok_tile)]
    mix_args, mix_specs = (), []
    if mixer is not None:
        y_pool, y_ssm, y_attn, y_chunk = mixer
        mix_args = (*y_pool, *y_ssm, *y_attn, *y_chunk, prm['w_out'], prm['ln_g'], prm['ln_b'])
        mix_specs = [pl.BlockSpec((TM, D_HEADS), _ctx_tile), pl.BlockSpec((TM, D_HEADS), _lat_tile)] * 4 + [
            _fixed_spec((D_MODEL, D_MODEL), l), _fixed_spec((1, D_MODEL), l, 1), _fixed_spec((1, D_MODEL), l, 1)]
    if split_out:
        out_specs = [pl.BlockSpec((TM, D_MODEL), _ctx_tile), pl.BlockSpec((TM, D_MODEL), _lat_tile)]
        out_shape = [jax.ShapeDtypeStruct((N_CTX, D_MODEL), F32), jax.ShapeDtypeStruct((N_LAT, D_MODEL), F32)]
    else:
        out_specs = pl.BlockSpec((TM, D_MODEL), _tok_tile)
        out_shape = jax.ShapeDtypeStruct((N_TOK, D_MODEL), F32)
    ln_idx = 2 * sub
    out_specs = list(out_specs) if split_out else [out_specs]
    out_shape = list(out_shape) if split_out else [out_shape]
    conv_args, conv_specs = (), []
    if convert is not None:
        cl, cs = convert
        step = lambda i: jnp.minimum(i, FFN_CONV_STEPS - 1)
        r1, r2 = D_MODEL // FFN_CONV_STEPS, D_FF // FFN_CONV_STEPS
        conv_args = (prm['ffn_w1_f32'], prm['ffn_w2_f32'])
        conv_specs = [pl.BlockSpec((None, None, r1, 2 * D_FF), lambda i: (cl, cs, step(i), 0)),
                      pl.BlockSpec((None, None, r2, D_MODEL), lambda i: (cl, cs, step(i), 0))]
        out_specs += [pl.BlockSpec((r1, 2 * D_FF), lambda i: (step(i), 0)),
                      pl.BlockSpec((r2, D_MODEL), lambda i: (step(i), 0))]
        out_shape += [jax.ShapeDtypeStruct((D_MODEL, 2 * D_FF), BF16),
                      jax.ShapeDtypeStruct((D_FF, D_MODEL), BF16)]
    outs = pl.pallas_call(
        functools.partial(_ffn_kernel, mod_off=6 * sub, split_in=split_in, split_out=split_out,
                          mixer_out=mixer is not None, convert=convert is not None),
        grid=(N_TILES,),
        in_specs=x_specs + mix_specs + [
            _mod_spec(l),
            _fixed_spec((D_MODEL, 2 * D_FF)),
            _fixed_spec((D_FF, D_MODEL)),
            _fixed_spec((1, D_MODEL), l, ln_idx),
            _fixed_spec((1, D_MODEL), l, ln_idx),
        ] + conv_specs,
        out_specs=out_specs,
        out_shape=out_shape,
        compiler_params=_params(1),
        name="ffn_sublayer",
    )(*xs, *mix_args, mod, *weights, prm['ln_g'], prm['ln_b'], *conv_args)
    n_main = 2 if split_out else 1
    x_out = tuple(outs[:n_main]) if split_out else outs[0]
    return x_out, tuple(outs[n_main:])


def _pool_math(x, w_bd, scale, seq_len):
    n = seq_len + 2 * POOL_PAD
    zpad = jnp.zeros((POOL_PAD, D_POOL), F32)
    e = jnp.concatenate([zpad, x, zpad], axis=0)

    def prev(a, d):
        return pltpu.roll(a, d, axis=0)

    def nxt(a, d):
        return pltpu.roll(a, n - d, axis=0)

    s2 = e + prev(e, 1)
    s4 = prev(s2, 1) + nxt(s2, 1)
    s8 = prev(s4, 2) + nxt(s4, 2)
    s16 = prev(s8, 4) + nxt(s8, 4)
    lane = lax.broadcasted_iota(jnp.int32, (seq_len, D_POOL), 1)
    grp = lane // POOL_GROUP
    sl = slice(POOL_PAD, POOL_PAD + seq_len)
    s = jnp.where(grp == 0, s2[sl], jnp.where(grp == 1, s4[sl], jnp.where(grp == 2, s8[sl], s16[sl])))
    half = jnp.where(grp == 0, 1, jnp.where(grp == 1, 2, jnp.where(grp == 2, 4, 8)))
    t = lax.broadcasted_iota(jnp.int32, (seq_len, D_POOL), 0)
    cnt = jnp.minimum(t + half, seq_len) - jnp.maximum(t - half, 0)
    d = (s / cnt.astype(F32) - x).astype(BF16)
    return jnp.dot(d, w_bd, preferred_element_type=F32) * scale


def _attn_math(problems, mxu_sums):
    nt = (((1,), (1,)), ((), ()))
    tq = problems[0][0].shape[0]
    lo = lax.broadcasted_iota(jnp.int32, (tq, LANES), 1) < HEAD_DIM
    scores = []
    for q, k, _ in problems:
        q = q * (HEAD_DIM ** -0.5 * math.log2(math.e))
        q_lo = q[:, :LANES]
        q_hi = q[:, LANES:]
        qs = [jnp.where(lo, q_lo, 0.0), jnp.where(lo, pltpu.roll(q_lo, HEAD_DIM, axis=1), 0.0),
              jnp.where(lo, 0.0, pltpu.roll(q_hi, HEAD_DIM, axis=1)), jnp.where(lo, 0.0, q_hi)]
        for g in range(N_KV_HEADS):
            qq = jnp.concatenate([qs[2 * g], qs[2 * g + 1]], axis=0).astype(BF16)
            scores.append(lax.dot_general(qq, k, nt, preferred_element_type=F32))
    results = []
    for i, (_, _, v) in enumerate(problems):
        if mxu_sums:
            lo_k = lax.broadcasted_iota(jnp.int32, v.shape, 1) < HEAD_DIM
            one = jnp.ones_like(v)
            v_g = [jnp.where(lo_k, v, one), jnp.where(lo_k, one, v)]
        outs = []
        for g in range(N_KV_HEADS):
            s = scores[N_KV_HEADS * i + g]
            p = jnp.exp2(s - jnp.max(s, axis=-1, keepdims=True))
            if mxu_sums:
                o = jnp.dot(p.astype(BF16), v_g[g], preferred_element_type=F32)
            else:
                l = jnp.sum(p, axis=-1, keepdims=True)
                o = jnp.dot(p.astype(BF16), v, preferred_element_type=F32) / l
            outs += [o[:tq], o[tq:]]
        if mxu_sums:
            r = [pltpu.roll(o, HEAD_DIM, axis=1) for o in outs]
            out_lo = jnp.where(lo, outs[0], r[1]) / jnp.where(lo, r[0], outs[1])
            out_hi = jnp.where(lo, r[2], outs[3]) / jnp.where(lo, outs[2], r[3])
        else:
            out_lo = jnp.where(lo, outs[0], pltpu.roll(outs[1], HEAD_DIM, axis=1))
            out_hi = jnp.where(lo, pltpu.roll(outs[2], HEAD_DIM, axis=1), outs[3])
        results.append(jnp.concatenate([out_lo, out_hi], axis=1))
    return results


def _seg_rms(x, gain, n_lanes):
    parts = []
    for j in range(n_lanes // LANES):
        xs = x[:, j * LANES:(j + 1) * LANES]
        sq = xs * xs
        lo = lax.broadcasted_iota(jnp.int32, xs.shape, 1) < HEAD_DIM
        s_all = jnp.sum(sq, axis=-1, keepdims=True)
        s_lo = jnp.sum(jnp.where(lo, sq, 0.0), axis=-1, keepdims=True)
        ms = jnp.where(lo, s_lo, s_all - s_lo) * (1.0 / HEAD_DIM)
        parts.append(xs * lax.rsqrt(ms + RMS_EPS))
    y = parts[0] if len(parts) == 1 else jnp.concatenate(parts, axis=1)
    return y * gain


def _rope(x, cos_t, sin_t, n_lanes):
    parts = []
    for j in range(n_lanes // LANES):
        xs = x[:, j * LANES:(j + 1) * LANES]
        first = (lax.broadcasted_iota(jnp.int32, xs.shape, 1) % HEAD_DIM) < (HEAD_DIM // 2)
        partner = jnp.where(first, pltpu.roll(xs, LANES - HEAD_DIM // 2, axis=1),
                            pltpu.roll(xs, HEAD_DIM // 2, axis=1))
        cs = cos_t[:, j * LANES:(j + 1) * LANES]
        sn = sin_t[:, j * LANES:(j + 1) * LANES]
        parts.append(xs * cs + partner * sn)
    return parts[0] if len(parts) == 1 else jnp.concatenate(parts, axis=1)


def _inproj_kernel(*refs, ctx):
    if ctx:
        (x_ref, mod_ref, w_ref, qn_ref, kn_ref, cg_ref, cb_ref, ws_ref, bs_ref, pw_ref, ps_ref,
         ua_ref, ub_ref, chunk_ref, k_ref, v_ref, yp_ref, ya_ref) = refs
    else:
        (x_ref, mod_ref, w_ref, qn_ref, kn_ref, cg_ref, cb_ref, ws_ref, bs_ref, cos_ref, sin_ref,
         ua_ref, ub_ref, chunk_ref, k_ref, v_ref, p_ref, q_ref) = refs
    x = x_ref[...]
    sh = mod_ref[3:4, :]
    sc = mod_ref[4:5, :]
    h = (x * (1.0 + sc) + sh).astype(BF16)
    proj = jnp.dot(h, w_ref[...], preferred_element_type=F32)
    p = proj[:, 0:256]
    ua_ref[...] = proj[:, 256:384]
    ub_ref[...] = proj[:, 384:512]
    q = _seg_rms(proj[:, 512:768], qn_ref[...], D_ATTN)
    k = _seg_rms(proj[:, 768:896], kn_ref[...], D_KV)
    v = proj[:, 896:1024]
    if not ctx:
        cos_t = cos_ref[...]
        sin_t = sin_ref[...]
        q = _rope(q, cos_t, sin_t, D_ATTN)
        k = _rope(k, cos_t, sin_t, D_KV)
    zu = _gelu(proj[:, 1024:1280])
    zv = _layer_norm(_gelu(proj[:, 1280:1536]), cg_ref[...], cb_ref[...])
    head = lax.broadcasted_iota(jnp.int32, (CHUNK, D_CHUNK), 1) // (D_CHUNK // N_CHUNK_HEADS)
    for c in range(TM // CHUNK):
        vb = zv[c * CHUNK:(c + 1) * CHUNK, :].astype(BF16)
        mixed = bs_ref[...]
        for hd in range(N_CHUNK_HEADS):
            r = jnp.dot(ws_ref[hd], vb, preferred_element_type=F32)
            mixed = mixed + jnp.where(head == hd, r, 0.0)
        chunk_ref[c * CHUNK:(c + 1) * CHUNK, :] = (zu[c * CHUNK:(c + 1) * CHUNK, :] * mixed).astype(BF16)
    if ctx:
        k_ref[...] = k.T
        v_ref[...] = v.T
        for s in range(TM // SEQ):
            r = slice(s * SEQ, (s + 1) * SEQ)
            yp_ref[r, :] = _pool_math(p[r], pw_ref[...], ps_ref[...], SEQ).astype(BF16)
            ya_ref[r, :] = _attn_math([(q[r], k[r].astype(BF16), v[r].astype(BF16))], False)[0].astype(BF16)
    else:
        k_ref[...] = k.astype(BF16)
        v_ref[...] = v.astype(BF16)
        p_ref[...] = p
        q_ref[...] = q


def _inproj(x, mod, prm, l, cos_t, sin_t):
    tok = lambda i: (i, 0)
    common = [_fixed_spec((D_MODEL, D_IN), l), _fixed_spec((1, D_ATTN), l), _fixed_spec((1, D_KV), l),
              _fixed_spec((1, D_CHUNK), l), _fixed_spec((1, D_CHUNK), l),
              _fixed_spec((N_CHUNK_HEADS, CHUNK, CHUNK), l), _fixed_spec((CHUNK, D_CHUNK), l)]
    common_args = (prm['w_in'], prm['q_norm'], prm['k_norm'], prm['chunk_ln_g'], prm['chunk_ln_b'],
                   prm['chunk_w_s'], prm['chunk_b_s'])
    mod_block = (None, None, N_MOD, D_MODEL)

    def outs(rows, kv_dtype, tail_dtype):
        widths = [(LANES, F32), (LANES, F32), (D_CHUNK, BF16), (D_KV, kv_dtype), (D_KV, kv_dtype),
                  (D_HEADS, tail_dtype), (D_HEADS, tail_dtype)]
        return ([jax.ShapeDtypeStruct((rows, w), dt) for w, dt in widths],
                [pl.BlockSpec((TM, w), tok) for w, _ in widths])

    ctx_shapes, ctx_specs = outs(N_CTX, F32, BF16)
    for j in (3, 4):
        ctx_shapes[j] = jax.ShapeDtypeStruct((D_KV, N_CTX), F32)
        ctx_specs[j] = pl.BlockSpec((D_KV, TM), lambda i: (0, i))
    ctx = pl.pallas_call(
        functools.partial(_inproj_kernel, ctx=True),
        grid=(CTX_TILES,),
        in_specs=[pl.BlockSpec((TM, D_MODEL), tok), pl.BlockSpec(mod_block, lambda i: (l, 0, 0, 0))]
                 + common + [_fixed_spec((D_POOL, D_POOL), l), _fixed_spec((1, D_POOL), l)],
        out_specs=ctx_specs,
        out_shape=ctx_shapes,
        compiler_params=_params(1),
        name="mixer_inproj_ctx",
    )(x, mod, *common_args, prm['pool_w'], prm['pool_scale'])
    lat_shapes, lat_specs = outs(N_LAT, BF16, F32)
    rope_spec = pl.BlockSpec((TM, D_ATTN), lambda i: (i % LAT_TILES_PER_SEQ, 0))
    lat = pl.pallas_call(
        functools.partial(_inproj_kernel, ctx=False),
        grid=(N_TILES - CTX_TILES,),
        in_specs=[pl.BlockSpec((TM, D_MODEL), lambda i: (CTX_TILES + i, 0)),
                  pl.BlockSpec(mod_block, lambda i: (l, 1 + i // LAT_TILES_PER_SEQ, 0, 0))]
                 + common + [rope_spec, rope_spec],
        out_specs=lat_specs,
        out_shape=lat_shapes,
        compiler_params=_params(1),
        name="mixer_inproj_lat",
    )(x, mod, *common_args, cos_t, sin_t)
    return ctx, lat


def _latent_mix_kernel(q_ref, kc_ref, kl_ref, vc_ref, vl_ref, p_ref, pw_ref, ps_ref, ya_ref, yp_ref):
    k = jnp.concatenate([kc_ref[...], kl_ref[...]], axis=0)
    v = jnp.concatenate([vc_ref[...], vl_ref[...]], axis=0)
    ya_ref[...] = _attn_math([(q_ref[...], k, v)], True)[0].astype(BF16)

    @pl.when(pl.program_id(1) == 0)
    def _():
        yp_ref[...] = _pool_math(p_ref[...], pw_ref[...], ps_ref[...], DEC_SEQ).astype(BF16)


def _latent_mix(q_lat, k_lat, v_lat, p_lat, cache_k, cache_v, prm, l):
    q_per_seq = DEC_SEQ // ATT_TQ
    cache_spec = pl.BlockSpec((None, None, PAST_LEN, D_KV), lambda b, j: (b, l, 0, 0))
    seq_kv = pl.BlockSpec((DEC_SEQ, D_KV), lambda b, j: (b, 0))
    seq_pool = pl.BlockSpec((DEC_SEQ, D_POOL), lambda b, j: (b, 0))
    q_tile = pl.BlockSpec((ATT_TQ, D_ATTN), lambda b, j: (b * q_per_seq + j, 0))
    return pl.pallas_call(
        _latent_mix_kernel,
        grid=(DEC_BATCH, q_per_seq),
        in_specs=[q_tile, cache_spec, seq_kv, cache_spec, seq_kv, seq_pool,
                  _fixed_spec((D_POOL, D_POOL), l), _fixed_spec((1, D_POOL), l)],
        out_specs=[q_tile, seq_pool],
        out_shape=[jax.ShapeDtypeStruct((N_LAT, D_ATTN), BF16), jax.ShapeDtypeStruct((N_LAT, D_POOL), BF16)],
        compiler_params=_params(2),
        name="latent_mix",
    )(q_lat, cache_k, k_lat, cache_v, v_lat, p_lat, prm['pool_w'], prm['pool_scale'])


def _ssm_prep_kernel(lam_ref, br_ref, bi_ref, bmat_ref, lamb_ref):
    lre = lam_ref[0:1, :]
    lim = lam_ref[1:2, :]
    dt = jnp.exp(lam_ref[2:3, :])
    mag = jnp.exp(lre * dt)
    ar = mag * jnp.cos(lim * dt)
    ai = mag * jnp.sin(lim * dt)
    lamb_ref[0:1, :] = ar
    lamb_ref[1:2, :] = ai
    mag_s = jnp.exp(lre * dt * SSM_LAT_SEG)
    lamb_ref[2:3, :] = mag_s * jnp.cos(lim * dt * SSM_LAT_SEG)
    lamb_ref[3:4, :] = mag_s * jnp.sin(lim * dt * SSM_LAT_SEG)
    den = lre * lre + lim * lim
    cr = ((ar - 1.0) * lre + ai * lim) / den
    ci = (ai * lre - (ar - 1.0) * lim) / den
    n_rep = SSM_LANES // LANES
    br = jnp.concatenate([br_ref[...]] * n_rep, axis=1)
    bi = jnp.concatenate([bi_ref[...]] * n_rep, axis=1)
    lane = lax.broadcasted_iota(jnp.int32, br.shape, 1)
    row_g = lax.broadcasted_iota(jnp.int32, br.shape, 0) // SSM_GROUP
    lane_g = 2 * (lane // (2 * LANES)) + (lane % LANES) // SSM_STATE
    is_re = (lane % (2 * LANES)) < LANES
    b_bar = jnp.where(is_re, cr * br - ci * bi, cr * bi + ci * br)
    bmat_ref[...] = jnp.where(row_g == lane_g, b_bar, 0.0).astype(BF16)


def _ssm_prep(lam_rows, b_re_placed, b_im_placed):
    n = DEPTH * 2
    return pl.pallas_call(
        _ssm_prep_kernel,
        grid=(n,),
        in_specs=[pl.BlockSpec((None, 3, SSM_LANES), lambda i: (i, 0, 0)),
                  pl.BlockSpec((None, D_SSM, LANES), lambda i: (i, 0, 0)),
                  pl.BlockSpec((None, D_SSM, LANES), lambda i: (i, 0, 0))],
        out_specs=[pl.BlockSpec((None, D_SSM, SSM_LANES), lambda i: (i, 0, 0)),
                   pl.BlockSpec((None, 4, SSM_LANES), lambda i: (i, 0, 0))],
        out_shape=[jax.ShapeDtypeStruct((n, D_SSM, SSM_LANES), BF16),
                   jax.ShapeDtypeStruct((n, 4, SSM_LANES), F32)],
        compiler_params=_params(1),
        name="ssm_prep",
    )(lam_rows, b_re_placed, b_im_placed)


def _ssm_kernel(ua_ref, ub_ref, h0_ref, bmat_ref, cmat_ref, lamb_ref, d_ref, wg_ref, bg_ref,
                y_ref, fin_ref, ui_ref, buf0_ref, buf1_ref, yi_ref, *, n_per):
    seg_len = SEQ
    ssm_w = SSM_W
    n_rows = SUBLANES * seg_len
    n_cb = ssm_w // (2 * LANES)

    def step_rows(t):
        return pl.ds(pl.multiple_of(t * SUBLANES, SUBLANES), SUBLANES)

    block_steps = SSM_BLOCK // SUBLANES
    blocks = [slice(b * SSM_BLOCK, (b + 1) * SSM_BLOCK) for b in range(n_rows // SSM_BLOCK)]

    def interleave(b):
        for t in range(b * block_steps, (b + 1) * block_steps):
            rows = slice(t * SUBLANES, (t + 1) * SUBLANES)
            ui_ref[rows, 0:LANES] = ua_ref[pl.ds(t, SUBLANES, stride=seg_len), :]
            ui_ref[rows, LANES:2 * LANES] = ub_ref[pl.ds(t, SUBLANES, stride=seg_len), :]

    seg = lax.broadcasted_iota(jnp.int32, (SUBLANES, LANES), 0) % n_per

    chains = [(dirn, k) for dirn in range(2) for k in range(n_cb)]
    n_parts = SSM_LANES // ssm_w
    bufs = (buf0_ref, buf1_ref)
    assert n_parts == len(bufs)

    def fill(part, rows):
        ub = ui_ref[rows, :].astype(BF16)
        for dirn, k in chains:
            c0 = k * 2 * LANES
            lane0 = part * ssm_w + c0
            bufs[part][dirn, rows, c0:c0 + 2 * LANES] = jnp.dot(
                ub, bmat_ref[dirn, :, lane0:lane0 + 2 * LANES], preferred_element_type=F32)

    def readout(part, rows):
        lo = part * ssm_w
        y = jnp.dot(bufs[part][0, rows, :].astype(BF16), cmat_ref[0, lo:lo + ssm_w, :],
                    preferred_element_type=F32)
        y = y + jnp.dot(bufs[part][1, rows, :].astype(BF16), cmat_ref[1, lo:lo + ssm_w, :],
                        preferred_element_type=F32)
        if part == 0:
            y = y + d_ref[...] * ui_ref[rows, :]
            yi_ref[0, rows, :] = y[:, :LANES]
            yi_ref[1, rows, :] = y[:, LANES:]
        else:
            yi_ref[0, rows, :] += y[:, :LANES]
            yi_ref[1, rows, :] += y[:, LANES:]

    def lam_rows(part, r):
        out = []
        for dirn, k in chains:
            c0 = part * ssm_w + k * 2 * LANES
            out.append(jnp.broadcast_to(lamb_ref[dirn, r:r + 1, c0:c0 + LANES], (SUBLANES, LANES)))
        return out

    def scan(part, init, store, side_work=None):
        buf = bufs[part]
        a_re, a_im = lam_rows(part, 0), lam_rows(part, 1)

        def step(i, hs):
            rows = (step_rows(i), step_rows(seg_len - 1 - i))
            new = []
            for c, (dirn, k) in enumerate(chains):
                re_sl = slice(k * 2 * LANES, k * 2 * LANES + LANES)
                im_sl = slice(k * 2 * LANES + LANES, (k + 1) * 2 * LANES)
                hr, hi = hs[2 * c], hs[2 * c + 1]
                nr = a_re[c] * hr - a_im[c] * hi + buf[dirn, rows[dirn], re_sl]
                ni = a_re[c] * hi + a_im[c] * hr + buf[dirn, rows[dirn], im_sl]
                if store:
                    buf[dirn, rows[dirn], re_sl] = nr
                    buf[dirn, rows[dirn], im_sl] = ni
                new += [nr, ni]
            return tuple(new)

        if side_work is None:
            return lax.fori_loop(0, seg_len, step, tuple(init), unroll=2)

        steps = SSM_BLOCK // SUBLANES

        def block(blk, hs):
            side_work(pl.ds(pl.multiple_of(blk * SSM_BLOCK, SSM_BLOCK), SSM_BLOCK))
            for j in range(steps):
                hs = step(blk * steps + j, hs)
            return hs

        return lax.fori_loop(0, n_rows // SSM_BLOCK, block, tuple(init))

    def initial_states(part, local):
        h0 = []
        for dirn, k in chains:
            c0 = part * ssm_w + k * 2 * LANES
            h0 += [h0_ref[dirn, :, c0:c0 + LANES], h0_ref[dirn, :, c0 + LANES:c0 + 2 * LANES]]
        if local is None:
            return h0
        s_re, s_im = lam_rows(part, 2), lam_rows(part, 3)
        init = []
        for c, (dirn, k) in enumerate(chains):
            edge = seg == (0 if dirn == 0 else n_per - 1)
            shift = 1 if dirn == 0 else SUBLANES - 1
            cr, ci = h0[2 * c], h0[2 * c + 1]
            lr = pltpu.roll(local[2 * c], shift, axis=0)
            li = pltpu.roll(local[2 * c + 1], shift, axis=0)
            for _ in range(n_per - 1):
                pr = pltpu.roll(cr, shift, axis=0)
                pi = pltpu.roll(ci, shift, axis=0)
                cr = jnp.where(edge, h0[2 * c], s_re[c] * pr - s_im[c] * pi + lr)
                ci = jnp.where(edge, h0[2 * c + 1], s_re[c] * pi + s_im[c] * pr + li)
            init += [cr, ci]
        return init

    zeros = [jnp.zeros((SUBLANES, LANES), F32)] * (2 * len(chains))
    interleave(0)
    for b in range(1, len(blocks)):
        fill(0, blocks[b - 1])
        interleave(b)
    fill(0, blocks[-1])
    assert n_parts == 2
    for part in range(n_parts):
        local = scan(part, zeros, False) if n_per > 1 else None
        side_work = functools.partial(fill, 1) if part == 0 else functools.partial(readout, 0)
        hs = scan(part, initial_states(part, local), True, side_work)
        for c, (dirn, k) in enumerate(chains):
            c0 = part * ssm_w + k * 2 * LANES
            fin_ref[dirn, :, c0:c0 + LANES] = hs[2 * c]
            fin_ref[dirn, :, c0 + LANES:c0 + 2 * LANES] = hs[2 * c + 1]

    def glu(rows):
        g = _gelu(jnp.concatenate([yi_ref[0, rows, :], yi_ref[1, rows, :]], axis=1))
        z = jnp.dot(g.astype(BF16), wg_ref[...], preferred_element_type=F32) + bg_ref[...]
        o = g * jax.nn.sigmoid(z)
        yi_ref[0, rows, :] = o[:, :LANES]
        yi_ref[1, rows, :] = o[:, LANES:]

    def deinterleave(b):
        for tb in range(b * block_steps // SUBLANES, (b + 1) * block_steps // SUBLANES):
            for s_idx in range(SUBLANES):
                dst = slice(s_idx * seg_len + tb * SUBLANES, s_idx * seg_len + (tb + 1) * SUBLANES)
                src = pl.ds(tb * SUBLANES * SUBLANES + s_idx, SUBLANES, stride=SUBLANES)
                y_ref[dst, 0:LANES] = yi_ref[0, src, :]
                y_ref[dst, LANES:2 * LANES] = yi_ref[1, src, :]

    last = n_parts - 1
    readout(last, blocks[0])
    for b in range(1, len(blocks)):
        readout(last, blocks[b])
        glu(blocks[b - 1])
        deinterleave(b - 1)
    glu(blocks[-1])
    deinterleave(len(blocks) - 1)


def _ssm(u_ctx, u_lat, prm, l):
    weights = [_fixed_spec((2, D_SSM, SSM_LANES), l), _fixed_spec((2, SSM_LANES, D_SSM), l),
               _fixed_spec((2, 4, SSM_LANES), l), _fixed_spec((1, D_SSM), l),
               _fixed_spec((D_SSM, D_SSM), l), _fixed_spec((1, D_SSM), l)]
    rows = SUBLANES * SEQ
    scratch = [pltpu.VMEM((rows, D_SSM), F32), pltpu.VMEM((2, rows, SSM_W), F32),
               pltpu.VMEM((2, rows, SSM_W), F32), pltpu.VMEM((2, rows, LANES), F32)]
    tile_a = pl.BlockSpec((rows, LANES), lambda i: (i, 0))
    tile_y = pl.BlockSpec((rows, D_SSM), lambda i: (i, 0))
    fin_spec = pl.BlockSpec((None, 2, SUBLANES, SSM_LANES), lambda i: (i, 0, 0, 0))

    tail = (prm['ssm_b'], prm['ssm_c'], prm['ssm_lam'], prm['ssm_d'], prm['ssm_w_glu'], prm['ssm_b_glu'])
    n_ctx_tiles = N_CTX // rows
    h0_ctx = jnp.zeros((2, SUBLANES, SSM_LANES), F32)
    y_ctx, fin_ctx = pl.pallas_call(
        functools.partial(_ssm_kernel, n_per=1),
        grid=(n_ctx_tiles,),
        in_specs=[tile_a, tile_a, _fixed_spec((2, SUBLANES, SSM_LANES))] + weights,
        out_specs=[tile_y, fin_spec],
        out_shape=[jax.ShapeDtypeStruct((N_CTX, D_SSM), F32),
                   jax.ShapeDtypeStruct((n_ctx_tiles, 2, SUBLANES, SSM_LANES), F32)],
        scratch_shapes=scratch,
        compiler_params=_params(1),
        name="ssm_ctx",
    )(*u_ctx, h0_ctx, *tail)
    h0_spec = pl.BlockSpec((None, None, 2, SUBLANES, SSM_LANES), lambda b: (l, b, 0, 0, 0))
    y_lat, _ = pl.pallas_call(
        functools.partial(_ssm_kernel, n_per=SSM_LAT_SEGS),
        grid=(DEC_BATCH,),
        in_specs=[tile_a, tile_a, h0_spec] + weights,
        out_specs=[tile_y, fin_spec],
        out_shape=[jax.ShapeDtypeStruct((N_LAT, D_SSM), F32),
                   jax.ShapeDtypeStruct((DEC_BATCH, 2, SUBLANES, SSM_LANES), F32)],
        scratch_shapes=scratch,
        compiler_params=_params(1),
        name="ssm_lat",
    )(*u_lat, prm['ssm_h0'], *tail)
    return (y_ctx, y_lat), fin_ctx


def _rope_tables():
    rows = DEC_SEQ // GRID_W
    row_idx = jnp.repeat(jnp.arange(rows), GRID_W).astype(F32)
    col_idx = jnp.tile(jnp.arange(GRID_W), rows).astype(F32)
    n_freq = HEAD_DIM // 4
    inv = ROPE_THETA ** (-jnp.arange(n_freq, dtype=F32) / n_freq)
    ang = jnp.concatenate([row_idx[:, None] * inv, col_idx[:, None] * inv], axis=-1)
    cos = jnp.cos(ang)
    sin = jnp.sin(ang)
    cos_h = jnp.concatenate([cos, cos], axis=-1)
    sin_h = jnp.concatenate([-sin, sin], axis=-1)
    return jnp.tile(cos_h, (1, N_HEADS)), jnp.tile(sin_h, (1, N_HEADS))


def _state_lanes(a):
    return a.reshape(a.shape[:-2] + (N_SSM_GROUPS // 2, 2 * SSM_STATE))


def _ssm_lane_rows(a):
    a = _state_lanes(a)
    return jnp.stack([a, a], axis=-2).reshape(a.shape[:-2] + (SSM_LANES,))


def _ssm_pack_state(re, im):
    return jnp.stack([_state_lanes(re), _state_lanes(im)], axis=-2).reshape(re.shape[:-2] + (SSM_LANES,))


def _ssm_unpack_state(s):
    s = s.reshape(s.shape[:-1] + (N_SSM_GROUPS // 2, 2, 2 * SSM_STATE))
    shp = s.shape[:-3] + (N_SSM_GROUPS, SSM_STATE)
    return s[..., 0, :].reshape(shp), s[..., 1, :].reshape(shp)


def _ssm_group_mask():
    ch_g = jnp.arange(D_SSM) // SSM_GROUP
    lane = jnp.arange(SSM_LANES)
    lane_g = 2 * (lane // (2 * LANES)) + (lane % LANES) // SSM_STATE
    return ch_g[:, None] == lane_g[None, :]


def _b_rows(b):
    bt = jnp.swapaxes(b, -1, -2).reshape(DEPTH * 2, D_SSM, SSM_STATE)
    return jnp.concatenate([bt, bt], axis=-1)


def _place_c(c_re, c_im):
    def rows(c):
        ct = jnp.moveaxis(c, -1, -3).reshape(c.shape[:-3] + (SSM_STATE, D_SSM))
        return jnp.tile(ct, (1,) * (ct.ndim - 2) + (SSM_LANES // SSM_STATE, 1))
    is_re = (jnp.arange(SSM_LANES) % (2 * LANES) < LANES)[:, None]
    return jnp.where(_ssm_group_mask().T, jnp.where(is_re, rows(c_re), -rows(c_im)), 0.0)


def _block_diag_pool(pool_w):
    tiled = jnp.tile(pool_w.reshape(DEPTH, D_POOL, POOL_GROUP), (1, 1, len(POOL_WINDOWS)))
    grp = jnp.arange(D_POOL) // POOL_GROUP
    return jnp.where(grp[:, None] == grp[None, :], tiled, 0.0)


def kernel(x_prompt, x_sample, cache_k, cache_v, state_ssm_re, state_ssm_im, c, c_ctx, w_mod, b_mod, ln_g, ln_b, ffn_w1, ffn_w2, w_in, w_out, pool_w, pool_scale, ssm_lam_re, ssm_lam_im, ssm_log_step, ssm_b_re, ssm_b_im, ssm_c_re, ssm_c_im, ssm_d, ssm_w_glu, ssm_b_glu, q_norm, k_norm, chunk_ln_g, chunk_ln_b, chunk_w_s, chunk_b_s):
    cond = jnp.concatenate([c_ctx[None, :], c, jnp.zeros((N_COND - 1 - DEC_BATCH, D_MODEL), F32)], axis=0)
    mod = _modulation(cond, w_mod, b_mod)
    cos_t, sin_t = _rope_tables()

    log_step = jnp.broadcast_to(ssm_log_step[..., None], ssm_lam_re.shape)
    lam_rows = jnp.stack([_ssm_lane_rows(ssm_lam_re), _ssm_lane_rows(ssm_lam_im),
                          _ssm_lane_rows(log_step)], axis=-2)
    bmat, lamb = _ssm_prep(lam_rows.reshape(DEPTH * 2, 3, SSM_LANES),
                           _b_rows(ssm_b_re), _b_rows(ssm_b_im))
    h0 = _ssm_pack_state(state_ssm_re, state_ssm_im)
    h0_lat = jnp.zeros((DEPTH, DEC_BATCH, 2, SSM_LAT_SEGS, SSM_LANES), F32)
    h0_lat = h0_lat.at[:, :, 0, 0].set(jnp.swapaxes(h0[:, :, 0], 0, 1))
    h0_lat = h0_lat.at[:, :, 1, SSM_LAT_SEGS - 1].set(jnp.swapaxes(h0[:, :, 1], 0, 1))

    vec = lambda a: a[..., None, :]
    prm = {
        'ln_g': vec(ln_g), 'ln_b': vec(ln_b),
        'ffn_w1_f32': ffn_w1, 'ffn_w2_f32': ffn_w2,
        'w_in': w_in.astype(BF16), 'w_out': w_out.astype(BF16),
        'q_norm': vec(jnp.tile(q_norm, (1, N_HEADS))), 'k_norm': vec(jnp.tile(k_norm, (1, N_KV_HEADS))),
        'chunk_ln_g': vec(chunk_ln_g), 'chunk_ln_b': vec(chunk_ln_b),
        'chunk_w_s': chunk_w_s.astype(BF16),
        'chunk_b_s': jnp.repeat(jnp.swapaxes(chunk_b_s, 1, 2), D_CHUNK // N_CHUNK_HEADS, axis=2),
        'pool_w': _block_diag_pool(pool_w).astype(BF16), 'pool_scale': vec(pool_scale),
        'ssm_b': bmat.reshape(DEPTH, 2, D_SSM, SSM_LANES),
        'ssm_lam': lamb.reshape(DEPTH, 2, 4, SSM_LANES),
        'ssm_c': _place_c(ssm_c_re, ssm_c_im).astype(BF16),
        'ssm_h0': h0_lat,
        'ssm_d': vec(ssm_d), 'ssm_w_glu': ssm_w_glu.astype(BF16), 'ssm_b_glu': vec(ssm_b_glu),
    }
    cache_k = cache_k.reshape(DEC_BATCH, DEPTH, PAST_LEN, D_KV).astype(BF16)
    cache_v = cache_v.reshape(DEC_BATCH, DEPTH, PAST_LEN, D_KV).astype(BF16)

    x = (x_prompt.reshape(N_CTX, D_MODEL), x_sample.reshape(N_LAT, D_MODEL))
    w_ffn = (ffn_w1[0, 0].astype(BF16), ffn_w2[0, 0].astype(BF16))
    ks, vs, s_re, s_im = [], [], [], []
    for l in range(DEPTH):
        x, w_ffn = _ffn_sublayer(x, mod, prm, l, 0, w_ffn, convert=(l, 1))
        ctx_out, lat_out = _inproj(x, mod, prm, l, cos_t, sin_t)
        ua_ctx, ub_ctx, yc_ctx, k_ctx, v_ctx, yp_ctx, ya_ctx = ctx_out
        ua_lat, ub_lat, yc_lat, k_lat, v_lat, p_lat, q_lat = lat_out
        y_ssm, fin = _ssm((ua_ctx, ub_ctx), (ua_lat, ub_lat), prm, l)
        ya_lat, yp_lat = _latent_mix(q_lat, k_lat, v_lat, p_lat, cache_k, cache_v, prm, l)
        x, w_ffn = _ffn_sublayer(x, mod, prm, l, 1, w_ffn, convert=(l + 1, 0) if l + 1 < DEPTH else None,
                                 split_out=(l == DEPTH - 1),
                                 mixer=((yp_ctx, yp_lat), y_ssm, (ya_ctx, ya_lat), (yc_ctx, yc_lat)))
        ks.append(k_ctx)
        vs.append(v_ctx)
        f_re, f_im = _ssm_unpack_state(jnp.transpose(fin, (0, 2, 1, 3)).reshape(BATCH, 2, SSM_LANES))
        s_re.append(f_re)
        s_im.append(f_im)
    y_p = x[0].reshape(BATCH, SEQ, D_MODEL)
    y_s = x[1].reshape(DEC_BATCH, DEC_SEQ, D_MODEL)
    def cache(ts):
        t = jnp.stack(ts).reshape(DEPTH, N_KV_HEADS, HEAD_DIM, BATCH, SEQ)
        return jnp.transpose(t, (3, 0, 4, 1, 2))

    return (y_p, y_s, cache(ks), cache(vs), jnp.stack(s_re, axis=1), jnp.stack(s_im, axis=1))
```

```python
import functools
import math

import jax
import jax.numpy as jnp
from jax import lax
from jax.experimental import pallas as pl
from jax.experimental.pallas import tpu as pltpu

F32 = jnp.float32
BF16 = jnp.bfloat16

D_MODEL = 1024
BATCH = 32
SEQ = 256
DEPTH = 2
DEC_BATCH = 2
DEC_SEQ = 2048
PAST_LEN = 512
GRID_W = 64
D_POOL = 256
D_SSM = 256
D_ATTN = 256
D_CHUNK = 256
D_HEADS = 256
POOL_WINDOWS = (2, 4, 8, 16)
POOL_GROUP = 64
SSM_GROUP = 16
N_SSM_GROUPS = 16
SSM_STATE = 64
HEAD_DIM = 64
N_HEADS = 4
N_KV_HEADS = 2
D_KV = 128
CHUNK = 128
N_CHUNK_HEADS = 4
D_FF = 2816
N_MOD = 9
D_IN = 1536
ALPHA = (2 * DEPTH) ** 0.25
LN_EPS = 1e-5
RMS_EPS = 1e-6
ROPE_THETA = 10000.0

LANES = 128
SUBLANES = 8
TM = 512
FFN_SUB = 256
FFN_CONV_STEPS = 16
ATT_TQ = 512
N_CTX = BATCH * SEQ
N_LAT = DEC_BATCH * DEC_SEQ
N_TOK = N_CTX + N_LAT
CTX_TILES = N_CTX // TM
LAT_TILES_PER_SEQ = DEC_SEQ // TM
N_TILES = N_TOK // TM
N_COND = 8
MOD_COLS = 2304
POOL_PAD = 8
SSM_LANES = 2 * N_SSM_GROUPS * SSM_STATE
SSM_W = 1024
SSM_LAT_SEGS = SUBLANES
SSM_LAT_SEG = DEC_SEQ // SSM_LAT_SEGS
SSM_BLOCK = 512
VMEM_LIMIT = 56 * 1024 * 1024


def _params(n_grid):
    return pltpu.CompilerParams(dimension_semantics=("arbitrary",) * n_grid,
                                vmem_limit_bytes=VMEM_LIMIT)


def _fixed_spec(tail, *lead):
    idx = tuple(lead) + (0,) * len(tail)
    return pl.BlockSpec((None,) * len(lead) + tuple(tail), lambda *_: idx, pipeline_mode=pl.Buffered(1))


def _cond_row(i):
    return jnp.where(i < CTX_TILES, 0, 1 + (i - CTX_TILES) // LAT_TILES_PER_SEQ)


def _mod_spec(l):
    return pl.BlockSpec((None, None, N_MOD, D_MODEL), lambda i: (l, _cond_row(i), 0, 0))


def _tok_tile(i):
    return (i, 0)


def _ctx_tile(i):
    return (jnp.minimum(i, CTX_TILES - 1), 0)


def _lat_tile(i):
    return (jnp.maximum(i - CTX_TILES, 0), 0)


def _layer_norm(y, g, b):
    mu = jnp.mean(y, axis=-1, keepdims=True)
    d = y - mu
    var = jnp.mean(d * d, axis=-1, keepdims=True)
    return d * lax.rsqrt(var + LN_EPS) * g + b


def _gelu(x):
    return 0.5 * x * (1.0 + jnp.tanh(math.sqrt(2.0 / math.pi) * (x + 0.044715 * (x * x * x))))


def _silu(x):
    return x * jax.nn.sigmoid(x)


def _split_bf16(a):
    hi = a.astype(BF16)
    return hi, (a - hi.astype(F32)).astype(BF16)


def _mod_kernel(cond_ref, w_ref, b_ref, o_ref):
    a_hi, a_lo = _split_bf16(_silu(cond_ref[...]))
    w_hi, w_lo = _split_bf16(w_ref[...])
    acc = jnp.dot(a_hi, w_lo, preferred_element_type=F32)
    acc = acc + jnp.dot(a_lo, w_hi, preferred_element_type=F32)
    acc = acc + jnp.dot(a_hi, w_hi, preferred_element_type=F32)
    o_ref[...] = acc + b_ref[...]


def _modulation(cond, w_mod, b_mod):
    tn = MOD_COLS
    out = pl.pallas_call(
        _mod_kernel,
        grid=(DEPTH, N_MOD * D_MODEL // tn),
        in_specs=[
            pl.BlockSpec((N_COND, D_MODEL), lambda l, j: (0, 0)),
            pl.BlockSpec((None, D_MODEL, tn), lambda l, j: (l, 0, j)),
            pl.BlockSpec((None, 1, tn), lambda l, j: (l, 0, j)),
        ],
        out_specs=pl.BlockSpec((None, N_COND, tn), lambda l, j: (l, 0, j)),
        out_shape=jax.ShapeDtypeStruct((DEPTH, N_COND, N_MOD * D_MODEL), F32),
        compiler_params=_params(2),
        name="modulation",
    )(cond, w_mod, b_mod.reshape(DEPTH, 1, N_MOD * D_MODEL))
    return out.reshape(DEPTH, N_COND, N_MOD, D_MODEL)


def _ffn_kernel(*refs, mod_off, split_in, split_out, mixer_out, n_conv):
    n_in = 2 if split_in else 1
    n_mix = 11 if mixer_out else 0
    mix_refs = refs[n_in:n_in + n_mix]
    mod_ref, w1_ref, w2_ref, g_ref, b_ref = refs[n_in + n_mix:n_in + n_mix + 5]
    conv_in = refs[n_in + n_mix + 5:n_in + n_mix + 5 + n_conv]
    out_refs = refs[n_in + n_mix + 5 + n_conv:]
    if n_conv:
        conv_out = out_refs[-n_conv:]
        out_refs = out_refs[:-n_conv]

        @pl.when(pl.program_id(0) < FFN_CONV_STEPS)
        def _():
            for src, dst in zip(conv_in, conv_out):
                dst[...] = src[...].astype(BF16)
    is_ctx = pl.program_id(0) < CTX_TILES
    sh = mod_ref[mod_off:mod_off + 1, :]
    sc = mod_ref[mod_off + 1:mod_off + 2, :]
    gate = mod_ref[mod_off + 2:mod_off + 3, :]

    def load_x(rows):
        if split_in:
            return jnp.where(is_ctx, refs[0][rows, :], refs[1][rows, :])
        return refs[0][rows, :]

    def project(rows):
        if not mixer_out:
            return None
        wo_ref = mix_refs[8]
        parts = [jnp.where(is_ctx, mix_refs[2 * j][rows, :], mix_refs[2 * j + 1][rows, :])
                 for j in range(4)]
        y = None
        for j, part in enumerate(parts):
            r = jnp.dot(part.astype(BF16), wo_ref[D_HEADS * j:D_HEADS * (j + 1), :],
                        preferred_element_type=F32)
            y = r if y is None else y + r
        return y

    def mixer_norm(x, y):
        if not mixer_out:
            return x
        go_ref, bo_ref = mix_refs[9:]
        return _layer_norm(ALPHA * x + mod_ref[5:6, :] * y, go_ref[...], bo_ref[...])

    def up(x):
        h = (x * (1.0 + sc) + sh).astype(BF16)
        return jnp.dot(h, w1_ref[...], preferred_element_type=F32)

    def down(gu):
        a = (_silu(gu[:, :D_FF]) * gu[:, D_FF:]).astype(BF16)
        return jnp.dot(a, w2_ref[...], preferred_element_type=F32)

    def finish(x, f):
        return _layer_norm(ALPHA * x + (0.5 * gate) * f, g_ref[...], b_ref[...])

    row_slices = [slice(j * FFN_SUB, (j + 1) * FFN_SUB) for j in range(TM // FFN_SUB)]
    n = len(row_slices)
    ys = []
    x = mixer_norm(load_x(row_slices[0]), project(row_slices[0]))
    gu = up(x)
    for j in range(n):
        y_next = project(row_slices[j + 1]) if j + 1 < n else None
        f = down(gu)
        if j + 1 < n:
            x_next = mixer_norm(load_x(row_slices[j + 1]), y_next)
            gu = up(x_next)
        ys.append(finish(x, f))
        if j + 1 < n:
            x = x_next

    def store(o_ref):
        for j, rows in enumerate(row_slices):
            o_ref[rows, :] = ys[j]

    if split_out:
        @pl.when(is_ctx)
        def _():
            store(out_refs[0])

        @pl.when(jnp.logical_not(is_ctx))
        def _():
            store(out_refs[1])
    else:
        store(out_refs[0])


def _ffn_sublayer(xs, mod, prm, l, sub, weights, convert=None, convert_stacked=(), split_out=False, mixer=None):
    split_in = isinstance(xs, tuple)
    if split_in:
        x_specs = [pl.BlockSpec((TM, D_MODEL), _ctx_tile), pl.BlockSpec((TM, D_MODEL), _lat_tile)]
    else:
        xs = (xs,)
        x_specs = [pl.BlockSpec((TM, D_MODEL), _tok_tile)]
    mix_args, mix_specs = (), []
    if mixer is not None:
        y_pool, y_ssm, y_attn, y_chunk = mixer
        mix_args = (*y_pool, *y_ssm, *y_attn, *y_chunk, prm['w_out'], prm['ln_g'], prm['ln_b'])
        mix_specs = [pl.BlockSpec((TM, D_HEADS), _ctx_tile), pl.BlockSpec((TM, D_HEADS), _lat_tile)] * 4 + [
            _fixed_spec((D_MODEL, D_MODEL), l), _fixed_spec((1, D_MODEL), l, 1), _fixed_spec((1, D_MODEL), l, 1)]
    if split_out:
        out_specs = [pl.BlockSpec((TM, D_MODEL), _ctx_tile), pl.BlockSpec((TM, D_MODEL), _lat_tile)]
        out_shape = [jax.ShapeDtypeStruct((N_CTX, D_MODEL), F32), jax.ShapeDtypeStruct((N_LAT, D_MODEL), F32)]
    else:
        out_specs = pl.BlockSpec((TM, D_MODEL), _tok_tile)
        out_shape = jax.ShapeDtypeStruct((N_TOK, D_MODEL), F32)
    ln_idx = 2 * sub
    out_specs = list(out_specs) if split_out else [out_specs]
    out_shape = list(out_shape) if split_out else [out_shape]
    conv_args, conv_specs = (), []
    if convert is not None:
        cl, cs = convert
        step = lambda i: jnp.minimum(i, FFN_CONV_STEPS - 1)
        r1, r2 = D_MODEL // FFN_CONV_STEPS, D_FF // FFN_CONV_STEPS
        conv_args = (prm['ffn_w1_f32'], prm['ffn_w2_f32'])
        conv_specs = [pl.BlockSpec((None, None, r1, 2 * D_FF), lambda i: (cl, cs, step(i), 0)),
                      pl.BlockSpec((None, None, r2, D_MODEL), lambda i: (cl, cs, step(i), 0))]
        out_specs += [pl.BlockSpec((r1, 2 * D_FF), lambda i: (step(i), 0)),
                      pl.BlockSpec((r2, D_MODEL), lambda i: (step(i), 0))]
        out_shape += [jax.ShapeDtypeStruct((D_MODEL, 2 * D_FF), BF16),
                      jax.ShapeDtypeStruct((D_FF, D_MODEL), BF16)]
    for w in convert_stacked:
        block = (w.shape[0], w.shape[1] // FFN_CONV_STEPS, w.shape[2])
        spec = pl.BlockSpec(block, lambda i: (0, jnp.minimum(i, FFN_CONV_STEPS - 1), 0))
        conv_args += (w,)
        conv_specs.append(spec)
        out_specs.append(spec)
        out_shape.append(jax.ShapeDtypeStruct(w.shape, BF16))
    outs = pl.pallas_call(
        functools.partial(_ffn_kernel, mod_off=6 * sub, split_in=split_in, split_out=split_out,
                          mixer_out=mixer is not None, n_conv=len(conv_args)),
        grid=(N_TILES,),
        in_specs=x_specs + mix_specs + [
            _mod_spec(l),
            _fixed_spec((D_MODEL, 2 * D_FF)),
            _fixed_spec((D_FF, D_MODEL)),
            _fixed_spec((1, D_MODEL), l, ln_idx),
            _fixed_spec((1, D_MODEL), l, ln_idx),
        ] + conv_specs,
        out_specs=out_specs,
        out_shape=out_shape,
        compiler_params=_params(1),
        name="ffn_sublayer",
    )(*xs, *mix_args, mod, *weights, prm['ln_g'], prm['ln_b'], *conv_args)
    n_main = 2 if split_out else 1
    x_out = tuple(outs[:n_main]) if split_out else outs[0]
    return x_out, tuple(outs[n_main:])


def _pool_math(x, w_bd, scale, seq_len):
    n = seq_len + 2 * POOL_PAD
    zpad = jnp.zeros((POOL_PAD, D_POOL), F32)
    e = jnp.concatenate([zpad, x, zpad], axis=0)

    def prev(a, d):
        return pltpu.roll(a, d, axis=0)

    def nxt(a, d):
        return pltpu.roll(a, n - d, axis=0)

    s2 = e + prev(e, 1)
    s4 = prev(s2, 1) + nxt(s2, 1)
    s8 = prev(s4, 2) + nxt(s4, 2)
    s16 = prev(s8, 4) + nxt(s8, 4)
    lane = lax.broadcasted_iota(jnp.int32, (seq_len, D_POOL), 1)
    grp = lane // POOL_GROUP
    sl = slice(POOL_PAD, POOL_PAD + seq_len)
    s = jnp.where(grp == 0, s2[sl], jnp.where(grp == 1, s4[sl], jnp.where(grp == 2, s8[sl], s16[sl])))
    half = jnp.where(grp == 0, 1, jnp.where(grp == 1, 2, jnp.where(grp == 2, 4, 8)))
    t = lax.broadcasted_iota(jnp.int32, (seq_len, D_POOL), 0)
    cnt = jnp.minimum(t + half, seq_len) - jnp.maximum(t - half, 0)
    d = (s / cnt.astype(F32) - x).astype(BF16)
    return jnp.dot(d, w_bd, preferred_element_type=F32) * scale


def _attn_math(problems, mxu_sums):
    nt = (((1,), (1,)), ((), ()))
    tq = problems[0][0].shape[0]
    lo = lax.broadcasted_iota(jnp.int32, (tq, LANES), 1) < HEAD_DIM
    scores = []
    for q, k, _ in problems:
        q = q * (HEAD_DIM ** -0.5 * math.log2(math.e))
        q_lo = q[:, :LANES]
        q_hi = q[:, LANES:]
        qs = [jnp.where(lo, q_lo, 0.0), jnp.where(lo, pltpu.roll(q_lo, HEAD_DIM, axis=1), 0.0),
              jnp.where(lo, 0.0, pltpu.roll(q_hi, HEAD_DIM, axis=1)), jnp.where(lo, 0.0, q_hi)]
        for g in range(N_KV_HEADS):
            qq = jnp.concatenate([qs[2 * g], qs[2 * g + 1]], axis=0).astype(BF16)
            scores.append(lax.dot_general(qq, k, nt, preferred_element_type=F32))
    results = []
    for i, (_, _, v) in enumerate(problems):
        if mxu_sums:
            lo_k = lax.broadcasted_iota(jnp.int32, v.shape, 1) < HEAD_DIM
            one = jnp.ones_like(v)
            v_g = [jnp.where(lo_k, v, one), jnp.where(lo_k, one, v)]
        outs = []
        for g in range(N_KV_HEADS):
            s = scores[N_KV_HEADS * i + g]
            p = jnp.exp2(s - jnp.max(s, axis=-1, keepdims=True))
            if mxu_sums:
                o = jnp.dot(p.astype(BF16), v_g[g], preferred_element_type=F32)
            else:
                l = jnp.sum(p, axis=-1, keepdims=True)
                o = jnp.dot(p.astype(BF16), v, preferred_element_type=F32) / l
            outs += [o[:tq], o[tq:]]
        if mxu_sums:
            r = [pltpu.roll(o, HEAD_DIM, axis=1) for o in outs]
            out_lo = jnp.where(lo, outs[0], r[1]) / jnp.where(lo, r[0], outs[1])
            out_hi = jnp.where(lo, r[2], outs[3]) / jnp.where(lo, outs[2], r[3])
        else:
            out_lo = jnp.where(lo, outs[0], pltpu.roll(outs[1], HEAD_DIM, axis=1))
            out_hi = jnp.where(lo, pltpu.roll(outs[2], HEAD_DIM, axis=1), outs[3])
        results.append(jnp.concatenate([out_lo, out_hi], axis=1))
    return results


def _seg_rms(x, gain, n_lanes):
    parts = []
    for j in range(n_lanes // LANES):
        xs = x[:, j * LANES:(j + 1) * LANES]
        sq = xs * xs
        lo = lax.broadcasted_iota(jnp.int32, xs.shape, 1) < HEAD_DIM
        s_all = jnp.sum(sq, axis=-1, keepdims=True)
        s_lo = jnp.sum(jnp.where(lo, sq, 0.0), axis=-1, keepdims=True)
        ms = jnp.where(lo, s_lo, s_all - s_lo) * (1.0 / HEAD_DIM)
        parts.append(xs * lax.rsqrt(ms + RMS_EPS))
    y = parts[0] if len(parts) == 1 else jnp.concatenate(parts, axis=1)
    return y * gain


def _rope(x, cos_t, sin_t, n_lanes):
    parts = []
    for j in range(n_lanes // LANES):
        xs = x[:, j * LANES:(j + 1) * LANES]
        first = (lax.broadcasted_iota(jnp.int32, xs.shape, 1) % HEAD_DIM) < (HEAD_DIM // 2)
        partner = jnp.where(first, pltpu.roll(xs, LANES - HEAD_DIM // 2, axis=1),
                            pltpu.roll(xs, HEAD_DIM // 2, axis=1))
        cs = cos_t[:, j * LANES:(j + 1) * LANES]
        sn = sin_t[:, j * LANES:(j + 1) * LANES]
        parts.append(xs * cs + partner * sn)
    return parts[0] if len(parts) == 1 else jnp.concatenate(parts, axis=1)


def _inproj_kernel(*refs, ctx):
    if ctx:
        (x_ref, mod_ref, w_ref, qn_ref, kn_ref, cg_ref, cb_ref, ws_ref, bs_ref, pw_ref, ps_ref,
         ua_ref, ub_ref, chunk_ref, k_ref, v_ref, yp_ref, ya_ref) = refs
    else:
        (x_ref, mod_ref, w_ref, qn_ref, kn_ref, cg_ref, cb_ref, ws_ref, bs_ref, cos_ref, sin_ref,
         ua_ref, ub_ref, chunk_ref, k_ref, v_ref, p_ref, q_ref) = refs
    x = x_ref[...]
    sh = mod_ref[3:4, :]
    sc = mod_ref[4:5, :]
    h = (x * (1.0 + sc) + sh).astype(BF16)
    proj = jnp.dot(h, w_ref[...], preferred_element_type=F32)
    p = proj[:, 0:256]
    ua_ref[...] = proj[:, 256:384]
    ub_ref[...] = proj[:, 384:512]
    q = _seg_rms(proj[:, 512:768], qn_ref[...], D_ATTN)
    k = _seg_rms(proj[:, 768:896], kn_ref[...], D_KV)
    v = proj[:, 896:1024]
    if not ctx:
        cos_t = cos_ref[...]
        sin_t = sin_ref[...]
        q = _rope(q, cos_t, sin_t, D_ATTN)
        k = _rope(k, cos_t, sin_t, D_KV)
    zu = _gelu(proj[:, 1024:1280])
    zv = _layer_norm(_gelu(proj[:, 1280:1536]), cg_ref[...], cb_ref[...])
    head = lax.broadcasted_iota(jnp.int32, (CHUNK, D_CHUNK), 1) // (D_CHUNK // N_CHUNK_HEADS)
    for c in range(TM // CHUNK):
        vb = zv[c * CHUNK:(c + 1) * CHUNK, :].astype(BF16)
        mixed = bs_ref[...]
        for hd in range(N_CHUNK_HEADS):
            r = jnp.dot(ws_ref[hd], vb, preferred_element_type=F32)
            mixed = mixed + jnp.where(head == hd, r, 0.0)
        chunk_ref[c * CHUNK:(c + 1) * CHUNK, :] = (zu[c * CHUNK:(c + 1) * CHUNK, :] * mixed).astype(BF16)
    if ctx:
        k_ref[...] = k.T
        v_ref[...] = v.T
        for s in range(TM // SEQ):
            r = slice(s * SEQ, (s + 1) * SEQ)
            yp_ref[r, :] = _pool_math(p[r], pw_ref[...], ps_ref[...], SEQ).astype(BF16)
            ya_ref[r, :] = _attn_math([(q[r], k[r].astype(BF16), v[r].astype(BF16))], False)[0].astype(BF16)
    else:
        k_ref[...] = k.astype(BF16)
        v_ref[...] = v.astype(BF16)
        p_ref[...] = p
        q_ref[...] = q


def _inproj(x, mod, prm, l, cos_t, sin_t):
    tok = lambda i: (i, 0)
    common = [_fixed_spec((D_MODEL, D_IN), l), _fixed_spec((1, D_ATTN), l), _fixed_spec((1, D_KV), l),
              _fixed_spec((1, D_CHUNK), l), _fixed_spec((1, D_CHUNK), l),
              _fixed_spec((N_CHUNK_HEADS, CHUNK, CHUNK), l), _fixed_spec((CHUNK, D_CHUNK), l)]
    common_args = (prm['w_in'], prm['q_norm'], prm['k_norm'], prm['chunk_ln_g'], prm['chunk_ln_b'],
                   prm['chunk_w_s'], prm['chunk_b_s'])
    mod_block = (None, None, N_MOD, D_MODEL)

    def outs(rows, kv_dtype, tail_dtype):
        widths = [(LANES, F32), (LANES, F32), (D_CHUNK, BF16), (D_KV, kv_dtype), (D_KV, kv_dtype),
                  (D_HEADS, tail_dtype), (D_HEADS, tail_dtype)]
        return ([jax.ShapeDtypeStruct((rows, w), dt) for w, dt in widths],
                [pl.BlockSpec((TM, w), tok) for w, _ in widths])

    ctx_shapes, ctx_specs = outs(N_CTX, F32, BF16)
    for j in (3, 4):
        ctx_shapes[j] = jax.ShapeDtypeStruct((D_KV, N_CTX), F32)
        ctx_specs[j] = pl.BlockSpec((D_KV, TM), lambda i: (0, i))
    ctx = pl.pallas_call(
        functools.partial(_inproj_kernel, ctx=True),
        grid=(CTX_TILES,),
        in_specs=[pl.BlockSpec((TM, D_MODEL), tok), pl.BlockSpec(mod_block, lambda i: (l, 0, 0, 0))]
                 + common + [_fixed_spec((D_POOL, D_POOL), l), _fixed_spec((1, D_POOL), l)],
        out_specs=ctx_specs,
        out_shape=ctx_shapes,
        compiler_params=_params(1),
        name="mixer_inproj_ctx",
    )(x, mod, *common_args, prm['pool_w'], prm['pool_scale'])
    lat_shapes, lat_specs = outs(N_LAT, BF16, F32)
    rope_spec = pl.BlockSpec((TM, D_ATTN), lambda i: (i % LAT_TILES_PER_SEQ, 0))
    lat = pl.pallas_call(
        functools.partial(_inproj_kernel, ctx=False),
        grid=(N_TILES - CTX_TILES,),
        in_specs=[pl.BlockSpec((TM, D_MODEL), lambda i: (CTX_TILES + i, 0)),
                  pl.BlockSpec(mod_block, lambda i: (l, 1 + i // LAT_TILES_PER_SEQ, 0, 0))]
                 + common + [rope_spec, rope_spec],
        out_specs=lat_specs,
        out_shape=lat_shapes,
        compiler_params=_params(1),
        name="mixer_inproj_lat",
    )(x, mod, *common_args, cos_t, sin_t)
    return ctx, lat


def _latent_mix_kernel(q_ref, kc_ref, kl_ref, vc_ref, vl_ref, p_ref, pw_ref, ps_ref, ya_ref, yp_ref):
    k = jnp.concatenate([kc_ref[...], kl_ref[...]], axis=0)
    v = jnp.concatenate([vc_ref[...], vl_ref[...]], axis=0)
    ya_ref[...] = _attn_math([(q_ref[...], k, v)], True)[0].astype(BF16)

    @pl.when(pl.program_id(1) == 0)
    def _():
        yp_ref[...] = _pool_math(p_ref[...], pw_ref[...], ps_ref[...], DEC_SEQ).astype(BF16)


def _latent_mix(q_lat, k_lat, v_lat, p_lat, cache_k, cache_v, prm, l):
    q_per_seq = DEC_SEQ // ATT_TQ
    cache_spec = pl.BlockSpec((None, None, PAST_LEN, D_KV), lambda b, j: (b, l, 0, 0))
    seq_kv = pl.BlockSpec((DEC_SEQ, D_KV), lambda b, j: (b, 0))
    seq_pool = pl.BlockSpec((DEC_SEQ, D_POOL), lambda b, j: (b, 0))
    q_tile = pl.BlockSpec((ATT_TQ, D_ATTN), lambda b, j: (b * q_per_seq + j, 0))
    return pl.pallas_call(
        _latent_mix_kernel,
        grid=(DEC_BATCH, q_per_seq),
        in_specs=[q_tile, cache_spec, seq_kv, cache_spec, seq_kv, seq_pool,
                  _fixed_spec((D_POOL, D_POOL), l), _fixed_spec((1, D_POOL), l)],
        out_specs=[q_tile, seq_pool],
        out_shape=[jax.ShapeDtypeStruct((N_LAT, D_ATTN), BF16), jax.ShapeDtypeStruct((N_LAT, D_POOL), BF16)],
        compiler_params=_params(2),
        name="latent_mix",
    )(q_lat, cache_k, k_lat, cache_v, v_lat, p_lat, prm['pool_w'], prm['pool_scale'])


def _ssm_prep_kernel(lam_ref, br_ref, bi_ref, bmat_ref, lamb_ref):
    lre = lam_ref[0:1, :]
    lim = lam_ref[1:2, :]
    dt = jnp.exp(lam_ref[2:3, :])
    mag = jnp.exp(lre * dt)
    ar = mag * jnp.cos(lim * dt)
    ai = mag * jnp.sin(lim * dt)
    lamb_ref[0:1, :] = ar
    lamb_ref[1:2, :] = ai
    mag_s = jnp.exp(lre * dt * SSM_LAT_SEG)
    lamb_ref[2:3, :] = mag_s * jnp.cos(lim * dt * SSM_LAT_SEG)
    lamb_ref[3:4, :] = mag_s * jnp.sin(lim * dt * SSM_LAT_SEG)
    den = lre * lre + lim * lim
    cr = ((ar - 1.0) * lre + ai * lim) / den
    ci = (ai * lre - (ar - 1.0) * lim) / den
    n_rep = SSM_LANES // LANES
    br = jnp.concatenate([br_ref[...]] * n_rep, axis=1)
    bi = jnp.concatenate([bi_ref[...]] * n_rep, axis=1)
    lane = lax.broadcasted_iota(jnp.int32, br.shape, 1)
    row_g = lax.broadcasted_iota(jnp.int32, br.shape, 0) // SSM_GROUP
    lane_g = 2 * (lane // (2 * LANES)) + (lane % LANES) // SSM_STATE
    is_re = (lane % (2 * LANES)) < LANES
    b_bar = jnp.where(is_re, cr * br - ci * bi, cr * bi + ci * br)
    bmat_ref[...] = jnp.where(row_g == lane_g, b_bar, 0.0).astype(BF16)


def _ssm_prep(lam_rows, b_re_placed, b_im_placed):
    n = DEPTH * 2
    return pl.pallas_call(
        _ssm_prep_kernel,
        grid=(n,),
        in_specs=[pl.BlockSpec((None, 3, SSM_LANES), lambda i: (i, 0, 0)),
                  pl.BlockSpec((None, D_SSM, LANES), lambda i: (i, 0, 0)),
                  pl.BlockSpec((None, D_SSM, LANES), lambda i: (i, 0, 0))],
        out_specs=[pl.BlockSpec((None, D_SSM, SSM_LANES), lambda i: (i, 0, 0)),
                   pl.BlockSpec((None, 4, SSM_LANES), lambda i: (i, 0, 0))],
        out_shape=[jax.ShapeDtypeStruct((n, D_SSM, SSM_LANES), BF16),
                   jax.ShapeDtypeStruct((n, 4, SSM_LANES), F32)],
        compiler_params=_params(1),
        name="ssm_prep",
    )(lam_rows, b_re_placed, b_im_placed)


def _ssm_kernel(ua_ref, ub_ref, h0_ref, bmat_ref, cmat_ref, lamb_ref, d_ref, wg_ref, bg_ref,
                y_ref, fin_ref, ui_ref, buf0_ref, buf1_ref, yi_ref, *, n_per):
    seg_len = SEQ
    ssm_w = SSM_W
    n_rows = SUBLANES * seg_len
    n_cb = ssm_w // (2 * LANES)

    def step_rows(t):
        return pl.ds(pl.multiple_of(t * SUBLANES, SUBLANES), SUBLANES)

    block_steps = SSM_BLOCK // SUBLANES
    blocks = [slice(b * SSM_BLOCK, (b + 1) * SSM_BLOCK) for b in range(n_rows // SSM_BLOCK)]

    def interleave(b):
        for t in range(b * block_steps, (b + 1) * block_steps):
            rows = slice(t * SUBLANES, (t + 1) * SUBLANES)
            ui_ref[rows, 0:LANES] = ua_ref[pl.ds(t, SUBLANES, stride=seg_len), :]
            ui_ref[rows, LANES:2 * LANES] = ub_ref[pl.ds(t, SUBLANES, stride=seg_len), :]

    seg = lax.broadcasted_iota(jnp.int32, (SUBLANES, LANES), 0) % n_per

    chains = [(dirn, k) for dirn in range(2) for k in range(n_cb)]
    n_parts = SSM_LANES // ssm_w
    bufs = (buf0_ref, buf1_ref)
    assert n_parts == len(bufs)

    def fill(part, rows):
        ub = ui_ref[rows, :].astype(BF16)
        for dirn, k in chains:
            c0 = k * 2 * LANES
            lane0 = part * ssm_w + c0
            bufs[part][dirn, rows, c0:c0 + 2 * LANES] = jnp.dot(
                ub, bmat_ref[dirn, :, lane0:lane0 + 2 * LANES], preferred_element_type=F32)

    def readout(part, rows):
        lo = part * ssm_w
        y = jnp.dot(bufs[part][0, rows, :].astype(BF16), cmat_ref[0, lo:lo + ssm_w, :],
                    preferred_element_type=F32)
        y = y + jnp.dot(bufs[part][1, rows, :].astype(BF16), cmat_ref[1, lo:lo + ssm_w, :],
                        preferred_element_type=F32)
        if part == 0:
            y = y + d_ref[...] * ui_ref[rows, :]
            yi_ref[0, rows, :] = y[:, :LANES]
            yi_ref[1, rows, :] = y[:, LANES:]
        else:
            yi_ref[0, rows, :] += y[:, :LANES]
            yi_ref[1, rows, :] += y[:, LANES:]

    def lam_rows(part, r):
        out = []
        for dirn, k in chains:
            c0 = part * ssm_w + k * 2 * LANES
            out.append(jnp.broadcast_to(lamb_ref[dirn, r:r + 1, c0:c0 + LANES], (SUBLANES, LANES)))
        return out

    def scan(part, init, store, side_work=None):
        buf = bufs[part]
        a_re, a_im = lam_rows(part, 0), lam_rows(part, 1)

        def step(i, hs):
            rows = (step_rows(i), step_rows(seg_len - 1 - i))
            new = []
            for c, (dirn, k) in enumerate(chains):
                re_sl = slice(k * 2 * LANES, k * 2 * LANES + LANES)
                im_sl = slice(k * 2 * LANES + LANES, (k + 1) * 2 * LANES)
                hr, hi = hs[2 * c], hs[2 * c + 1]
                nr = a_re[c] * hr - a_im[c] * hi + buf[dirn, rows[dirn], re_sl]
                ni = a_re[c] * hi + a_im[c] * hr + buf[dirn, rows[dirn], im_sl]
                if store:
                    buf[dirn, rows[dirn], re_sl] = nr
                    buf[dirn, rows[dirn], im_sl] = ni
                new += [nr, ni]
            return tuple(new)

        if side_work is None:
            return lax.fori_loop(0, seg_len, step, tuple(init), unroll=2)

        steps = SSM_BLOCK // SUBLANES

        def block(blk, hs):
            side_work(pl.ds(pl.multiple_of(blk * SSM_BLOCK, SSM_BLOCK), SSM_BLOCK))
            for j in range(steps):
                hs = step(blk * steps + j, hs)
            return hs

        return lax.fori_loop(0, n_rows // SSM_BLOCK, block, tuple(init))

    def initial_states(part, local):
        h0 = []
        for dirn, k in chains:
            c0 = part * ssm_w + k * 2 * LANES
            h0 += [h0_ref[dirn, :, c0:c0 + LANES], h0_ref[dirn, :, c0 + LANES:c0 + 2 * LANES]]
        if local is None:
            return h0
        s_re, s_im = lam_rows(part, 2), lam_rows(part, 3)
        init = []
        for c, (dirn, k) in enumerate(chains):
            edge = seg == (0 if dirn == 0 else n_per - 1)
            shift = 1 if dirn == 0 else SUBLANES - 1
            cr, ci = h0[2 * c], h0[2 * c + 1]
            lr = pltpu.roll(local[2 * c], shift, axis=0)
            li = pltpu.roll(local[2 * c + 1], shift, axis=0)
            for _ in range(n_per - 1):
                pr = pltpu.roll(cr, shift, axis=0)
                pi = pltpu.roll(ci, shift, axis=0)
                cr = jnp.where(edge, h0[2 * c], s_re[c] * pr - s_im[c] * pi + lr)
                ci = jnp.where(edge, h0[2 * c + 1], s_re[c] * pi + s_im[c] * pr + li)
            init += [cr, ci]
        return init

    zeros = [jnp.zeros((SUBLANES, LANES), F32)] * (2 * len(chains))
    interleave(0)
    for b in range(1, len(blocks)):
        fill(0, blocks[b - 1])
        interleave(b)
    fill(0, blocks[-1])
    assert n_parts == 2
    for part in range(n_parts):
        local = scan(part, zeros, False) if n_per > 1 else None
        side_work = functools.partial(fill, 1) if part == 0 else functools.partial(readout, 0)
        hs = scan(part, initial_states(part, local), True, side_work)
        for c, (dirn, k) in enumerate(chains):
            c0 = part * ssm_w + k * 2 * LANES
            fin_ref[dirn, :, c0:c0 + LANES] = hs[2 * c]
            fin_ref[dirn, :, c0 + LANES:c0 + 2 * LANES] = hs[2 * c + 1]

    def glu(rows):
        g = _gelu(jnp.concatenate([yi_ref[0, rows, :], yi_ref[1, rows, :]], axis=1))
        z = jnp.dot(g.astype(BF16), wg_ref[...], preferred_element_type=F32) + bg_ref[...]
        o = g * jax.nn.sigmoid(z)
        yi_ref[0, rows, :] = o[:, :LANES]
        yi_ref[1, rows, :] = o[:, LANES:]

    def deinterleave(b):
        for tb in range(b * block_steps // SUBLANES, (b + 1) * block_steps // SUBLANES):
            for s_idx in range(SUBLANES):
                dst = slice(s_idx * seg_len + tb * SUBLANES, s_idx * seg_len + (tb + 1) * SUBLANES)
                src = pl.ds(tb * SUBLANES * SUBLANES + s_idx, SUBLANES, stride=SUBLANES)
                y_ref[dst, 0:LANES] = yi_ref[0, src, :]
                y_ref[dst, LANES:2 * LANES] = yi_ref[1, src, :]

    last = n_parts - 1
    readout(last, blocks[0])
    for b in range(1, len(blocks)):
        readout(last, blocks[b])
        glu(blocks[b - 1])
        deinterleave(b - 1)
    glu(blocks[-1])
    deinterleave(len(blocks) - 1)


def _ssm(u_ctx, u_lat, prm, l):
    weights = [_fixed_spec((2, D_SSM, SSM_LANES), l), _fixed_spec((2, SSM_LANES, D_SSM), l),
               _fixed_spec((2, 4, SSM_LANES), l), _fixed_spec((1, D_SSM), l),
               _fixed_spec((D_SSM, D_SSM), l), _fixed_spec((1, D_SSM), l)]
    rows = SUBLANES * SEQ
    scratch = [pltpu.VMEM((rows, D_SSM), F32), pltpu.VMEM((2, rows, SSM_W), F32),
               pltpu.VMEM((2, rows, SSM_W), F32), pltpu.VMEM((2, rows, LANES), F32)]
    tile_a = pl.BlockSpec((rows, LANES), lambda i: (i, 0))
    tile_y = pl.BlockSpec((rows, D_SSM), lambda i: (i, 0))
    fin_spec = pl.BlockSpec((None, 2, SUBLANES, SSM_LANES), lambda i: (i, 0, 0, 0))

    tail = (prm['ssm_b'], prm['ssm_c'], prm['ssm_lam'], prm['ssm_d'], prm['ssm_w_glu'], prm['ssm_b_glu'])
    n_ctx_tiles = N_CTX // rows
    h0_ctx = jnp.zeros((2, SUBLANES, SSM_LANES), F32)
    y_ctx, fin_ctx = pl.pallas_call(
        functools.partial(_ssm_kernel, n_per=1),
        grid=(n_ctx_tiles,),
        in_specs=[tile_a, tile_a, _fixed_spec((2, SUBLANES, SSM_LANES))] + weights,
        out_specs=[tile_y, fin_spec],
        out_shape=[jax.ShapeDtypeStruct((N_CTX, D_SSM), F32),
                   jax.ShapeDtypeStruct((n_ctx_tiles, 2, SUBLANES, SSM_LANES), F32)],
        scratch_shapes=scratch,
        compiler_params=_params(1),
        name="ssm_ctx",
    )(*u_ctx, h0_ctx, *tail)
    h0_spec = pl.BlockSpec((None, None, 2, SUBLANES, SSM_LANES), lambda b: (l, b, 0, 0, 0))
    y_lat, _ = pl.pallas_call(
        functools.partial(_ssm_kernel, n_per=SSM_LAT_SEGS),
        grid=(DEC_BATCH,),
        in_specs=[tile_a, tile_a, h0_spec] + weights,
        out_specs=[tile_y, fin_spec],
        out_shape=[jax.ShapeDtypeStruct((N_LAT, D_SSM), F32),
                   jax.ShapeDtypeStruct((DEC_BATCH, 2, SUBLANES, SSM_LANES), F32)],
        scratch_shapes=scratch,
        compiler_params=_params(1),
        name="ssm_lat",
    )(*u_lat, prm['ssm_h0'], *tail)
    return (y_ctx, y_lat), fin_ctx


def _rope_tables():
    rows = DEC_SEQ // GRID_W
    row_idx = jnp.repeat(jnp.arange(rows), GRID_W).astype(F32)
    col_idx = jnp.tile(jnp.arange(GRID_W), rows).astype(F32)
    n_freq = HEAD_DIM // 4
    inv = ROPE_THETA ** (-jnp.arange(n_freq, dtype=F32) / n_freq)
    ang = jnp.concatenate([row_idx[:, None] * inv, col_idx[:, None] * inv], axis=-1)
    cos = jnp.cos(ang)
    sin = jnp.sin(ang)
    cos_h = jnp.concatenate([cos, cos], axis=-1)
    sin_h = jnp.concatenate([-sin, sin], axis=-1)
    return jnp.tile(cos_h, (1, N_HEADS)), jnp.tile(sin_h, (1, N_HEADS))


def _state_lanes(a):
    return a.reshape(a.shape[:-2] + (N_SSM_GROUPS // 2, 2 * SSM_STATE))


def _ssm_lane_rows(a):
    a = _state_lanes(a)
    return jnp.stack([a, a], axis=-2).reshape(a.shape[:-2] + (SSM_LANES,))


def _ssm_pack_state(re, im):
    return jnp.stack([_state_lanes(re), _state_lanes(im)], axis=-2).reshape(re.shape[:-2] + (SSM_LANES,))


def _ssm_unpack_state(s):
    s = s.reshape(s.shape[:-1] + (N_SSM_GROUPS // 2, 2, 2 * SSM_STATE))
    shp = s.shape[:-3] + (N_SSM_GROUPS, SSM_STATE)
    return s[..., 0, :].reshape(shp), s[..., 1, :].reshape(shp)


def _ssm_group_mask():
    ch_g = jnp.arange(D_SSM) // SSM_GROUP
    lane = jnp.arange(SSM_LANES)
    lane_g = 2 * (lane // (2 * LANES)) + (lane % LANES) // SSM_STATE
    return ch_g[:, None] == lane_g[None, :]


def _b_rows(b):
    bt = jnp.swapaxes(b, -1, -2).reshape(DEPTH * 2, D_SSM, SSM_STATE)
    return jnp.concatenate([bt, bt], axis=-1)


def _place_c(c_re, c_im):
    def rows(c):
        ct = jnp.moveaxis(c, -1, -3).reshape(c.shape[:-3] + (SSM_STATE, D_SSM))
        return jnp.tile(ct, (1,) * (ct.ndim - 2) + (SSM_LANES // SSM_STATE, 1))
    is_re = (jnp.arange(SSM_LANES) % (2 * LANES) < LANES)[:, None]
    return jnp.where(_ssm_group_mask().T, jnp.where(is_re, rows(c_re), -rows(c_im)), 0.0)


def _block_diag_pool(pool_w):
    tiled = jnp.tile(pool_w.reshape(DEPTH, D_POOL, POOL_GROUP), (1, 1, len(POOL_WINDOWS)))
    grp = jnp.arange(D_POOL) // POOL_GROUP
    return jnp.where(grp[:, None] == grp[None, :], tiled, 0.0)


def kernel(x_prompt, x_sample, cache_k, cache_v, state_ssm_re, state_ssm_im, c, c_ctx, w_mod, b_mod, ln_g, ln_b, ffn_w1, ffn_w2, w_in, w_out, pool_w, pool_scale, ssm_lam_re, ssm_lam_im, ssm_log_step, ssm_b_re, ssm_b_im, ssm_c_re, ssm_c_im, ssm_d, ssm_w_glu, ssm_b_glu, q_norm, k_norm, chunk_ln_g, chunk_ln_b, chunk_w_s, chunk_b_s):
    cond = jnp.concatenate([c_ctx[None, :], c, jnp.zeros((N_COND - 1 - DEC_BATCH, D_MODEL), F32)], axis=0)
    mod = _modulation(cond, w_mod, b_mod)
    cos_t, sin_t = _rope_tables()

    log_step = jnp.broadcast_to(ssm_log_step[..., None], ssm_lam_re.shape)
    lam_rows = jnp.stack([_ssm_lane_rows(ssm_lam_re), _ssm_lane_rows(ssm_lam_im),
                          _ssm_lane_rows(log_step)], axis=-2)
    bmat, lamb = _ssm_prep(lam_rows.reshape(DEPTH * 2, 3, SSM_LANES),
                           _b_rows(ssm_b_re), _b_rows(ssm_b_im))
    h0 = _ssm_pack_state(state_ssm_re, state_ssm_im)
    h0_lat = jnp.zeros((DEPTH, DEC_BATCH, 2, SSM_LAT_SEGS, SSM_LANES), F32)
    h0_lat = h0_lat.at[:, :, 0, 0].set(jnp.swapaxes(h0[:, :, 0], 0, 1))
    h0_lat = h0_lat.at[:, :, 1, SSM_LAT_SEGS - 1].set(jnp.swapaxes(h0[:, :, 1], 0, 1))

    vec = lambda a: a[..., None, :]
    prm = {
        'ln_g': vec(ln_g), 'ln_b': vec(ln_b),
        'ffn_w1_f32': ffn_w1, 'ffn_w2_f32': ffn_w2,
        'q_norm': vec(jnp.tile(q_norm, (1, N_HEADS))), 'k_norm': vec(jnp.tile(k_norm, (1, N_KV_HEADS))),
        'chunk_ln_g': vec(chunk_ln_g), 'chunk_ln_b': vec(chunk_ln_b),
        'chunk_w_s': chunk_w_s.astype(BF16),
        'chunk_b_s': jnp.repeat(jnp.swapaxes(chunk_b_s, 1, 2), D_CHUNK // N_CHUNK_HEADS, axis=2),
        'pool_w': _block_diag_pool(pool_w).astype(BF16), 'pool_scale': vec(pool_scale),
        'ssm_b': bmat.reshape(DEPTH, 2, D_SSM, SSM_LANES),
        'ssm_lam': lamb.reshape(DEPTH, 2, 4, SSM_LANES),
        'ssm_c': _place_c(ssm_c_re, ssm_c_im).astype(BF16),
        'ssm_h0': h0_lat,
        'ssm_d': vec(ssm_d), 'ssm_w_glu': ssm_w_glu.astype(BF16), 'ssm_b_glu': vec(ssm_b_glu),
    }
    cache_k = cache_k.reshape(DEC_BATCH, DEPTH, PAST_LEN, D_KV).astype(BF16)
    cache_v = cache_v.reshape(DEC_BATCH, DEPTH, PAST_LEN, D_KV).astype(BF16)

    x = (x_prompt.reshape(N_CTX, D_MODEL), x_sample.reshape(N_LAT, D_MODEL))
    w_ffn = (ffn_w1[0, 0].astype(BF16), ffn_w2[0, 0].astype(BF16))
    ks, vs, s_re, s_im = [], [], [], []
    for l in range(DEPTH):
        x, w_next = _ffn_sublayer(x, mod, prm, l, 0, w_ffn, convert=(l, 1),
                                  convert_stacked=(w_in, w_out) if l == 0 else ())
        w_ffn = w_next[:2]
        if l == 0:
            prm['w_in'], prm['w_out'] = w_next[2:]
        ctx_out, lat_out = _inproj(x, mod, prm, l, cos_t, sin_t)
        ua_ctx, ub_ctx, yc_ctx, k_ctx, v_ctx, yp_ctx, ya_ctx = ctx_out
        ua_lat, ub_lat, yc_lat, k_lat, v_lat, p_lat, q_lat = lat_out
        y_ssm, fin = _ssm((ua_ctx, ub_ctx), (ua_lat, ub_lat), prm, l)
        ya_lat, yp_lat = _latent_mix(q_lat, k_lat, v_lat, p_lat, cache_k, cache_v, prm, l)
        x, w_ffn = _ffn_sublayer(x, mod, prm, l, 1, w_ffn, convert=(l + 1, 0) if l + 1 < DEPTH else None,
                                 split_out=(l == DEPTH - 1),
                                 mixer=((yp_ctx, yp_lat), y_ssm, (ya_ctx, ya_lat), (yc_ctx, yc_lat)))
        ks.append(k_ctx)
        vs.append(v_ctx)
        f_re, f_im = _ssm_unpack_state(jnp.transpose(fin, (0, 2, 1, 3)).reshape(BATCH, 2, SSM_LANES))
        s_re.append(f_re)
        s_im.append(f_im)
    y_p = x[0].reshape(BATCH, SEQ, D_MODEL)
    y_s = x[1].reshape(DEC_BATCH, DEC_SEQ, D_MODEL)
    def cache(ts):
        t = jnp.stack(ts).reshape(DEPTH, N_KV_HEADS, HEAD_DIM, BATCH, SEQ)
        return jnp.transpose(t, (3, 0, 4, 1, 2))

    return (y_p, y_s, cache(ks), cache(vs), jnp.stack(s_re, axis=1), jnp.stack(s_im, axis=1))
```

```python
import functools
import math

import jax
import jax.numpy as jnp
from jax import lax
from jax.experimental import pallas as pl
from jax.experimental.pallas import tpu as pltpu

F32 = jnp.float32
BF16 = jnp.bfloat16

D_MODEL = 1024
BATCH = 32
SEQ = 256
DEPTH = 2
DEC_BATCH = 2
DEC_SEQ = 2048
PAST_LEN = 512
GRID_W = 64
D_POOL = 256
D_SSM = 256
D_ATTN = 256
D_CHUNK = 256
D_HEADS = 256
POOL_WINDOWS = (2, 4, 8, 16)
POOL_GROUP = 64
SSM_GROUP = 16
N_SSM_GROUPS = 16
SSM_STATE = 64
HEAD_DIM = 64
N_HEADS = 4
N_KV_HEADS = 2
D_KV = 128
CHUNK = 128
N_CHUNK_HEADS = 4
D_FF = 2816
N_MOD = 9
D_IN = 1536
ALPHA = (2 * DEPTH) ** 0.25
LN_EPS = 1e-5
RMS_EPS = 1e-6
ROPE_THETA = 10000.0

LANES = 128
SUBLANES = 8
TM = 512
FFN_SUB = 256
FFN_CONV_STEPS = 16
ATT_TQ = 512
N_CTX = BATCH * SEQ
N_LAT = DEC_BATCH * DEC_SEQ
N_TOK = N_CTX + N_LAT
CTX_TILES = N_CTX // TM
LAT_TILES_PER_SEQ = DEC_SEQ // TM
N_TILES = N_TOK // TM
N_COND = 8
MOD_COLS = 2304
POOL_PAD = 8
SSM_LANES = 2 * N_SSM_GROUPS * SSM_STATE
SSM_W = 1024
SSM_LAT_SEGS = SUBLANES
SSM_LAT_SEG = DEC_SEQ // SSM_LAT_SEGS
SSM_BLOCK = 512
VMEM_LIMIT = 56 * 1024 * 1024


def _params(n_grid):
    return pltpu.CompilerParams(dimension_semantics=("arbitrary",) * n_grid,
                                vmem_limit_bytes=VMEM_LIMIT)


def _fixed_spec(tail, *lead):
    idx = tuple(lead) + (0,) * len(tail)
    return pl.BlockSpec((None,) * len(lead) + tuple(tail), lambda *_: idx, pipeline_mode=pl.Buffered(1))


def _cond_row(i):
    return jnp.where(i < CTX_TILES, 0, 1 + (i - CTX_TILES) // LAT_TILES_PER_SEQ)


def _mod_spec(l):
    return pl.BlockSpec((None, None, N_MOD, D_MODEL), lambda i: (l, _cond_row(i), 0, 0))


def _tok_tile(i):
    return (i, 0)


def _ctx_tile(i):
    return (jnp.minimum(i, CTX_TILES - 1), 0)


def _lat_tile(i):
    return (jnp.maximum(i - CTX_TILES, 0), 0)


def _layer_norm(y, g, b):
    mu = jnp.mean(y, axis=-1, keepdims=True)
    d = y - mu
    var = jnp.mean(d * d, axis=-1, keepdims=True)
    return d * lax.rsqrt(var + LN_EPS) * g + b


def _gelu(x):
    return 0.5 * x * (1.0 + jnp.tanh(math.sqrt(2.0 / math.pi) * (x + 0.044715 * (x * x * x))))


def _silu(x):
    return x * jax.nn.sigmoid(x)


def _split_bf16(a):
    hi = a.astype(BF16)
    return hi, (a - hi.astype(F32)).astype(BF16)


def _mod_kernel(cond_ref, w_ref, b_ref, o_ref):
    a_hi, a_lo = _split_bf16(_silu(cond_ref[...]))
    w_hi, w_lo = _split_bf16(w_ref[...])
    acc = jnp.dot(a_hi, w_lo, preferred_element_type=F32)
    acc = acc + jnp.dot(a_lo, w_hi, preferred_element_type=F32)
    acc = acc + jnp.dot(a_hi, w_hi, preferred_element_type=F32)
    o_ref[...] = acc + b_ref[...]


def _modulation(cond, w_mod, b_mod):
    tn = MOD_COLS
    out = pl.pallas_call(
        _mod_kernel,
        grid=(DEPTH, N_MOD * D_MODEL // tn),
        in_specs=[
            pl.BlockSpec((N_COND, D_MODEL), lambda l, j: (0, 0)),
            pl.BlockSpec((None, D_MODEL, tn), lambda l, j: (l, 0, j)),
            pl.BlockSpec((None, 1, tn), lambda l, j: (l, 0, j)),
        ],
        out_specs=pl.BlockSpec((None, N_COND, tn), lambda l, j: (l, 0, j)),
        out_shape=jax.ShapeDtypeStruct((DEPTH, N_COND, N_MOD * D_MODEL), F32),
        compiler_params=_params(2),
        name="modulation",
    )(cond, w_mod, b_mod.reshape(DEPTH, 1, N_MOD * D_MODEL))
    return out.reshape(DEPTH, N_COND, N_MOD, D_MODEL)


def _ffn_kernel(*refs, mod_off, split_in, split_out, mixer_out, n_conv):
    n_in = 2 if split_in else 1
    n_mix = 11 if mixer_out else 0
    mix_refs = refs[n_in:n_in + n_mix]
    mod_ref, w1_ref, w2_ref, g_ref, b_ref = refs[n_in + n_mix:n_in + n_mix + 5]
    conv_in = refs[n_in + n_mix + 5:n_in + n_mix + 5 + n_conv]
    out_refs = refs[n_in + n_mix + 5 + n_conv:]
    if n_conv:
        conv_out = out_refs[-n_conv:]
        out_refs = out_refs[:-n_conv]

        @pl.when(pl.program_id(0) < FFN_CONV_STEPS)
        def _():
            for src, dst in zip(conv_in, conv_out):
                dst[...] = src[...].astype(BF16)
    is_ctx = pl.program_id(0) < CTX_TILES
    sh = mod_ref[mod_off:mod_off + 1, :]
    sc = mod_ref[mod_off + 1:mod_off + 2, :]
    gate = mod_ref[mod_off + 2:mod_off + 3, :]

    def load_x(rows):
        if split_in:
            return jnp.where(is_ctx, refs[0][rows, :], refs[1][rows, :])
        return refs[0][rows, :]

    def project(rows):
        if not mixer_out:
            return None
        wo_ref = mix_refs[8]
        parts = [jnp.where(is_ctx, mix_refs[2 * j][rows, :], mix_refs[2 * j + 1][rows, :])
                 for j in range(4)]
        y = None
        for j, part in enumerate(parts):
            r = jnp.dot(part.astype(BF16), wo_ref[D_HEADS * j:D_HEADS * (j + 1), :],
                        preferred_element_type=F32)
            y = r if y is None else y + r
        return y

    def mixer_norm(x, y):
        if not mixer_out:
            return x
        go_ref, bo_ref = mix_refs[9:]
        return _layer_norm(ALPHA * x + mod_ref[5:6, :] * y, go_ref[...], bo_ref[...])

    def up(x):
        h = (x * (1.0 + sc) + sh).astype(BF16)
        return jnp.dot(h, w1_ref[...], preferred_element_type=F32)

    def down(gu):
        a = (_silu(gu[:, :D_FF]) * gu[:, D_FF:]).astype(BF16)
        return jnp.dot(a, w2_ref[...], preferred_element_type=F32)

    def finish(x, f):
        return _layer_norm(ALPHA * x + (0.5 * gate) * f, g_ref[...], b_ref[...])

    row_slices = [slice(j * FFN_SUB, (j + 1) * FFN_SUB) for j in range(TM // FFN_SUB)]
    n = len(row_slices)
    ys = []
    x = mixer_norm(load_x(row_slices[0]), project(row_slices[0]))
    gu = up(x)
    for j in range(n):
        y_next = project(row_slices[j + 1]) if j + 1 < n else None
        f = down(gu)
        if j + 1 < n:
            x_next = mixer_norm(load_x(row_slices[j + 1]), y_next)
            gu = up(x_next)
        ys.append(finish(x, f))
        if j + 1 < n:
            x = x_next

    def store(o_ref):
        for j, rows in enumerate(row_slices):
            o_ref[rows, :] = ys[j]

    if split_out:
        @pl.when(is_ctx)
        def _():
            store(out_refs[0])

        @pl.when(jnp.logical_not(is_ctx))
        def _():
            store(out_refs[1])
    else:
        store(out_refs[0])


def _ffn_sublayer(xs, mod, prm, l, sub, weights, convert=None, convert_stacked=(), split_out=False, mixer=None):
    split_in = isinstance(xs, tuple)
    if split_in:
        x_specs = [pl.BlockSpec((TM, D_MODEL), _ctx_tile), pl.BlockSpec((TM, D_MODEL), _lat_tile)]
    else:
        xs = (xs,)
        x_specs = [pl.BlockSpec((TM, D_MODEL), _tok_tile)]
    mix_args, mix_specs = (), []
    if mixer is not None:
        y_pool, y_ssm, y_attn, y_chunk = mixer
        mix_args = (*y_pool, *y_ssm, *y_attn, *y_chunk, prm['w_out'], prm['ln_g'], prm['ln_b'])
        mix_specs = [pl.BlockSpec((TM, D_HEADS), _ctx_tile), pl.BlockSpec((TM, D_HEADS), _lat_tile)] * 4 + [
            _fixed_spec((D_MODEL, D_MODEL), l), _fixed_spec((1, D_MODEL), l, 1), _fixed_spec((1, D_MODEL), l, 1)]
    if split_out:
        out_specs = [pl.BlockSpec((TM, D_MODEL), _ctx_tile), pl.BlockSpec((TM, D_MODEL), _lat_tile)]
        out_shape = [jax.ShapeDtypeStruct((N_CTX, D_MODEL), F32), jax.ShapeDtypeStruct((N_LAT, D_MODEL), F32)]
    else:
        out_specs = pl.BlockSpec((TM, D_MODEL), _tok_tile)
        out_shape = jax.ShapeDtypeStruct((N_TOK, D_MODEL), F32)
    ln_idx = 2 * sub
    out_specs = list(out_specs) if split_out else [out_specs]
    out_shape = list(out_shape) if split_out else [out_shape]
    conv_args, conv_specs = (), []
    if convert is not None:
        cl, cs = convert
        step = lambda i: jnp.minimum(i, FFN_CONV_STEPS - 1)
        r1, r2 = D_MODEL // FFN_CONV_STEPS, D_FF // FFN_CONV_STEPS
        conv_args = (prm['ffn_w1_f32'], prm['ffn_w2_f32'])
        conv_specs = [pl.BlockSpec((None, None, r1, 2 * D_FF), lambda i: (cl, cs, step(i), 0)),
                      pl.BlockSpec((None, None, r2, D_MODEL), lambda i: (cl, cs, step(i), 0))]
        out_specs += [pl.BlockSpec((r1, 2 * D_FF), lambda i: (step(i), 0)),
                      pl.BlockSpec((r2, D_MODEL), lambda i: (step(i), 0))]
        out_shape += [jax.ShapeDtypeStruct((D_MODEL, 2 * D_FF), BF16),
                      jax.ShapeDtypeStruct((D_FF, D_MODEL), BF16)]
    for w in convert_stacked:
        block = (w.shape[0], w.shape[1] // FFN_CONV_STEPS, w.shape[2])
        spec = pl.BlockSpec(block, lambda i: (0, jnp.minimum(i, FFN_CONV_STEPS - 1), 0))
        conv_args += (w,)
        conv_specs.append(spec)
        out_specs.append(spec)
        out_shape.append(jax.ShapeDtypeStruct(w.shape, BF16))
    outs = pl.pallas_call(
        functools.partial(_ffn_kernel, mod_off=6 * sub, split_in=split_in, split_out=split_out,
                          mixer_out=mixer is not None, n_conv=len(conv_args)),
        grid=(N_TILES,),
        in_specs=x_specs + mix_specs + [
            _mod_spec(l),
            _fixed_spec((D_MODEL, 2 * D_FF)),
            _fixed_spec((D_FF, D_MODEL)),
            _fixed_spec((1, D_MODEL), l, ln_idx),
            _fixed_spec((1, D_MODEL), l, ln_idx),
        ] + conv_specs,
        out_specs=out_specs,
        out_shape=out_shape,
        compiler_params=_params(1),
        name="ffn_sublayer",
    )(*xs, *mix_args, mod, *weights, prm['ln_g'], prm['ln_b'], *conv_args)
    n_main = 2 if split_out else 1
    x_out = tuple(outs[:n_main]) if split_out else outs[0]
    return x_out, tuple(outs[n_main:])


def _pool_math(x, w_bd, scale, seq_len):
    n = seq_len + 2 * POOL_PAD
    zpad = jnp.zeros((POOL_PAD, D_POOL), F32)
    e = jnp.concatenate([zpad, x, zpad], axis=0)

    def prev(a, d):
        return pltpu.roll(a, d, axis=0)

    def nxt(a, d):
        return pltpu.roll(a, n - d, axis=0)

    s2 = e + prev(e, 1)
    s4 = prev(s2, 1) + nxt(s2, 1)
    s8 = prev(s4, 2) + nxt(s4, 2)
    s16 = prev(s8, 4) + nxt(s8, 4)
    lane = lax.broadcasted_iota(jnp.int32, (seq_len, D_POOL), 1)
    grp = lane // POOL_GROUP
    sl = slice(POOL_PAD, POOL_PAD + seq_len)
    s = jnp.where(grp == 0, s2[sl], jnp.where(grp == 1, s4[sl], jnp.where(grp == 2, s8[sl], s16[sl])))
    half = jnp.where(grp == 0, 1, jnp.where(grp == 1, 2, jnp.where(grp == 2, 4, 8)))
    t = lax.broadcasted_iota(jnp.int32, (seq_len, D_POOL), 0)
    cnt = jnp.minimum(t + half, seq_len) - jnp.maximum(t - half, 0)
    d = (s / cnt.astype(F32) - x).astype(BF16)
    return jnp.dot(d, w_bd, preferred_element_type=F32) * scale


def _attn_math(problems, mxu_sums):
    nt = (((1,), (1,)), ((), ()))
    tq = problems[0][0].shape[0]
    lo = lax.broadcasted_iota(jnp.int32, (tq, LANES), 1) < HEAD_DIM
    scores = []
    for q, k, _ in problems:
        q = q * (HEAD_DIM ** -0.5 * math.log2(math.e))
        q_lo = q[:, :LANES]
        q_hi = q[:, LANES:]
        qs = [jnp.where(lo, q_lo, 0.0), jnp.where(lo, pltpu.roll(q_lo, HEAD_DIM, axis=1), 0.0),
              jnp.where(lo, 0.0, pltpu.roll(q_hi, HEAD_DIM, axis=1)), jnp.where(lo, 0.0, q_hi)]
        for g in range(N_KV_HEADS):
            qq = jnp.concatenate([qs[2 * g], qs[2 * g + 1]], axis=0).astype(BF16)
            scores.append(lax.dot_general(qq, k, nt, preferred_element_type=F32))
    results = []
    for i, (_, _, v) in enumerate(problems):
        if mxu_sums:
            lo_k = lax.broadcasted_iota(jnp.int32, v.shape, 1) < HEAD_DIM
            one = jnp.ones_like(v)
            v_g = [jnp.where(lo_k, v, one), jnp.where(lo_k, one, v)]
        outs = []
        for g in range(N_KV_HEADS):
            s = scores[N_KV_HEADS * i + g]
            p = jnp.exp2(s - jnp.max(s, axis=-1, keepdims=True))
            if mxu_sums:
                o = jnp.dot(p.astype(BF16), v_g[g], preferred_element_type=F32)
            else:
                l = jnp.sum(p, axis=-1, keepdims=True)
                o = jnp.dot(p.astype(BF16), v, preferred_element_type=F32) / l
            outs += [o[:tq], o[tq:]]
        if mxu_sums:
            r = [pltpu.roll(o, HEAD_DIM, axis=1) for o in outs]
            out_lo = jnp.where(lo, outs[0], r[1]) / jnp.where(lo, r[0], outs[1])
            out_hi = jnp.where(lo, r[2], outs[3]) / jnp.where(lo, outs[2], r[3])
        else:
            out_lo = jnp.where(lo, outs[0], pltpu.roll(outs[1], HEAD_DIM, axis=1))
            out_hi = jnp.where(lo, pltpu.roll(outs[2], HEAD_DIM, axis=1), outs[3])
        results.append(jnp.concatenate([out_lo, out_hi], axis=1))
    return results


def _seg_rms(x, gain, n_lanes):
    parts = []
    for j in range(n_lanes // LANES):
        xs = x[:, j * LANES:(j + 1) * LANES]
        sq = xs * xs
        lo = lax.broadcasted_iota(jnp.int32, xs.shape, 1) < HEAD_DIM
        s_all = jnp.sum(sq, axis=-1, keepdims=True)
        s_lo = jnp.sum(jnp.where(lo, sq, 0.0), axis=-1, keepdims=True)
        ms = jnp.where(lo, s_lo, s_all - s_lo) * (1.0 / HEAD_DIM)
        parts.append(xs * lax.rsqrt(ms + RMS_EPS))
    y = parts[0] if len(parts) == 1 else jnp.concatenate(parts, axis=1)
    return y * gain


def _rope(x, cos_t, sin_t, n_lanes):
    parts = []
    for j in range(n_lanes // LANES):
        xs = x[:, j * LANES:(j + 1) * LANES]
        first = (lax.broadcasted_iota(jnp.int32, xs.shape, 1) % HEAD_DIM) < (HEAD_DIM // 2)
        partner = jnp.where(first, pltpu.roll(xs, LANES - HEAD_DIM // 2, axis=1),
                            pltpu.roll(xs, HEAD_DIM // 2, axis=1))
        cs = cos_t[:, j * LANES:(j + 1) * LANES]
        sn = sin_t[:, j * LANES:(j + 1) * LANES]
        parts.append(xs * cs + partner * sn)
    return parts[0] if len(parts) == 1 else jnp.concatenate(parts, axis=1)


def _inproj_kernel(*refs, ctx):
    if ctx:
        (x_ref, mod_ref, w_ref, qn_ref, kn_ref, cg_ref, cb_ref, ws_ref, bs_ref, pw_ref, ps_ref,
         ua_ref, ub_ref, chunk_ref, k_ref, v_ref, yp_ref, ya_ref) = refs
    else:
        (x_ref, mod_ref, w_ref, qn_ref, kn_ref, cg_ref, cb_ref, ws_ref, bs_ref, cos_ref, sin_ref,
         ua_ref, ub_ref, chunk_ref, k_ref, v_ref, p_ref, q_ref) = refs
    x = x_ref[...]
    sh = mod_ref[3:4, :]
    sc = mod_ref[4:5, :]
    h = (x * (1.0 + sc) + sh).astype(BF16)
    proj = jnp.dot(h, w_ref[...], preferred_element_type=F32)
    p = proj[:, 0:256]
    ua_ref[...] = proj[:, 256:384]
    ub_ref[...] = proj[:, 384:512]
    q = _seg_rms(proj[:, 512:768], qn_ref[...], D_ATTN)
    k = _seg_rms(proj[:, 768:896], kn_ref[...], D_KV)
    v = proj[:, 896:1024]
    if not ctx:
        cos_t = cos_ref[...]
        sin_t = sin_ref[...]
        q = _rope(q, cos_t, sin_t, D_ATTN)
        k = _rope(k, cos_t, sin_t, D_KV)
    zu = _gelu(proj[:, 1024:1280])
    zv = _layer_norm(_gelu(proj[:, 1280:1536]), cg_ref[...], cb_ref[...])
    head = lax.broadcasted_iota(jnp.int32, (CHUNK, D_CHUNK), 1) // (D_CHUNK // N_CHUNK_HEADS)
    for c in range(TM // CHUNK):
        vb = zv[c * CHUNK:(c + 1) * CHUNK, :].astype(BF16)
        mixed = bs_ref[...]
        for hd in range(N_CHUNK_HEADS):
            r = jnp.dot(ws_ref[hd], vb, preferred_element_type=F32)
            mixed = mixed + jnp.where(head == hd, r, 0.0)
        chunk_ref[c * CHUNK:(c + 1) * CHUNK, :] = (zu[c * CHUNK:(c + 1) * CHUNK, :] * mixed).astype(BF16)
    if ctx:
        k_ref[...] = k.T
        v_ref[...] = v.T
        for s in range(TM // SEQ):
            r = slice(s * SEQ, (s + 1) * SEQ)
            yp_ref[r, :] = _pool_math(p[r], pw_ref[...], ps_ref[...], SEQ).astype(BF16)
            ya_ref[r, :] = _attn_math([(q[r], k[r].astype(BF16), v[r].astype(BF16))], False)[0].astype(BF16)
    else:
        k_ref[...] = k.astype(BF16)
        v_ref[...] = v.astype(BF16)
        p_ref[...] = p
        q_ref[...] = q


def _inproj(x, mod, prm, l, cos_t, sin_t):
    tok = lambda i: (i, 0)
    common = [_fixed_spec((D_MODEL, D_IN), l), _fixed_spec((1, D_ATTN), l), _fixed_spec((1, D_KV), l),
              _fixed_spec((1, D_CHUNK), l), _fixed_spec((1, D_CHUNK), l),
              _fixed_spec((N_CHUNK_HEADS, CHUNK, CHUNK), l), _fixed_spec((CHUNK, D_CHUNK), l)]
    common_args = (prm['w_in'], prm['q_norm'], prm['k_norm'], prm['chunk_ln_g'], prm['chunk_ln_b'],
                   prm['chunk_w_s'], prm['chunk_b_s'])
    mod_block = (None, None, N_MOD, D_MODEL)

    def outs(rows, kv_dtype, tail_dtype):
        widths = [(LANES, F32), (LANES, F32), (D_CHUNK, BF16), (D_KV, kv_dtype), (D_KV, kv_dtype),
                  (D_HEADS, tail_dtype), (D_HEADS, tail_dtype)]
        return ([jax.ShapeDtypeStruct((rows, w), dt) for w, dt in widths],
                [pl.BlockSpec((TM, w), tok) for w, _ in widths])

    ctx_shapes, ctx_specs = outs(N_CTX, F32, BF16)
    for j in (3, 4):
        ctx_shapes[j] = jax.ShapeDtypeStruct((D_KV, N_CTX), F32)
        ctx_specs[j] = pl.BlockSpec((D_KV, TM), lambda i: (0, i))
    ctx = pl.pallas_call(
        functools.partial(_inproj_kernel, ctx=True),
        grid=(CTX_TILES,),
        in_specs=[pl.BlockSpec((TM, D_MODEL), tok), pl.BlockSpec(mod_block, lambda i: (l, 0, 0, 0))]
                 + common + [_fixed_spec((D_POOL, D_POOL), l), _fixed_spec((1, D_POOL), l)],
        out_specs=ctx_specs,
        out_shape=ctx_shapes,
        compiler_params=_params(1),
        name="mixer_inproj_ctx",
    )(x, mod, *common_args, prm['pool_w'], prm['pool_scale'])
    lat_shapes, lat_specs = outs(N_LAT, BF16, F32)
    rope_spec = pl.BlockSpec((TM, D_ATTN), lambda i: (i % LAT_TILES_PER_SEQ, 0))
    lat = pl.pallas_call(
        functools.partial(_inproj_kernel, ctx=False),
        grid=(N_TILES - CTX_TILES,),
        in_specs=[pl.BlockSpec((TM, D_MODEL), lambda i: (CTX_TILES + i, 0)),
                  pl.BlockSpec(mod_block, lambda i: (l, 1 + i // LAT_TILES_PER_SEQ, 0, 0))]
                 + common + [rope_spec, rope_spec],
        out_specs=lat_specs,
        out_shape=lat_shapes,
        compiler_params=_params(1),
        name="mixer_inproj_lat",
    )(x, mod, *common_args, cos_t, sin_t)
    return ctx, lat


def _latent_mix_kernel(q_ref, kc_ref, kl_ref, vc_ref, vl_ref, p_ref, pw_ref, ps_ref, ya_ref, yp_ref):
    k = jnp.concatenate([kc_ref[...], kl_ref[...]], axis=0)
    v = jnp.concatenate([vc_ref[...], vl_ref[...]], axis=0)
    ya_ref[...] = _attn_math([(q_ref[...], k, v)], True)[0].astype(BF16)

    @pl.when(pl.program_id(1) == 0)
    def _():
        yp_ref[...] = _pool_math(p_ref[...], pw_ref[...], ps_ref[...], DEC_SEQ).astype(BF16)


def _latent_mix(q_lat, k_lat, v_lat, p_lat, cache_k, cache_v, prm, l):
    q_per_seq = DEC_SEQ // ATT_TQ
    cache_spec = pl.BlockSpec((None, None, PAST_LEN, D_KV), lambda b, j: (b, l, 0, 0))
    seq_kv = pl.BlockSpec((DEC_SEQ, D_KV), lambda b, j: (b, 0))
    seq_pool = pl.BlockSpec((DEC_SEQ, D_POOL), lambda b, j: (b, 0))
    q_tile = pl.BlockSpec((ATT_TQ, D_ATTN), lambda b, j: (b * q_per_seq + j, 0))
    return pl.pallas_call(
        _latent_mix_kernel,
        grid=(DEC_BATCH, q_per_seq),
        in_specs=[q_tile, cache_spec, seq_kv, cache_spec, seq_kv, seq_pool,
                  _fixed_spec((D_POOL, D_POOL), l), _fixed_spec((1, D_POOL), l)],
        out_specs=[q_tile, seq_pool],
        out_shape=[jax.ShapeDtypeStruct((N_LAT, D_ATTN), BF16), jax.ShapeDtypeStruct((N_LAT, D_POOL), BF16)],
        compiler_params=_params(2),
        name="latent_mix",
    )(q_lat, cache_k, k_lat, cache_v, v_lat, p_lat, prm['pool_w'], prm['pool_scale'])


def _ssm_prep_kernel(lam_ref, br_ref, bi_ref, bmat_ref, lamb_ref):
    lre = lam_ref[0:1, :]
    lim = lam_ref[1:2, :]
    dt = jnp.exp(lam_ref[2:3, :])
    mag = jnp.exp(lre * dt)
    ar = mag * jnp.cos(lim * dt)
    ai = mag * jnp.sin(lim * dt)
    lamb_ref[0:1, :] = ar
    lamb_ref[1:2, :] = ai
    mag_s = jnp.exp(lre * dt * SSM_LAT_SEG)
    lamb_ref[2:3, :] = mag_s * jnp.cos(lim * dt * SSM_LAT_SEG)
    lamb_ref[3:4, :] = mag_s * jnp.sin(lim * dt * SSM_LAT_SEG)
    den = lre * lre + lim * lim
    cr = ((ar - 1.0) * lre + ai * lim) / den
    ci = (ai * lre - (ar - 1.0) * lim) / den
    n_rep = SSM_LANES // LANES
    br = jnp.concatenate([br_ref[...]] * n_rep, axis=1)
    bi = jnp.concatenate([bi_ref[...]] * n_rep, axis=1)
    lane = lax.broadcasted_iota(jnp.int32, br.shape, 1)
    row_g = lax.broadcasted_iota(jnp.int32, br.shape, 0) // SSM_GROUP
    lane_g = 2 * (lane // (2 * LANES)) + (lane % LANES) // SSM_STATE
    is_re = (lane % (2 * LANES)) < LANES
    b_bar = jnp.where(is_re, cr * br - ci * bi, cr * bi + ci * br)
    bmat_ref[...] = jnp.where(row_g == lane_g, b_bar, 0.0).astype(BF16)


def _ssm_prep(lam_rows, b_re_placed, b_im_placed):
    n = DEPTH * 2
    return pl.pallas_call(
        _ssm_prep_kernel,
        grid=(n,),
        in_specs=[pl.BlockSpec((None, 3, SSM_LANES), lambda i: (i, 0, 0)),
                  pl.BlockSpec((None, D_SSM, LANES), lambda i: (i, 0, 0)),
                  pl.BlockSpec((None, D_SSM, LANES), lambda i: (i, 0, 0))],
        out_specs=[pl.BlockSpec((None, D_SSM, SSM_LANES), lambda i: (i, 0, 0)),
                   pl.BlockSpec((None, 4, SSM_LANES), lambda i: (i, 0, 0))],
        out_shape=[jax.ShapeDtypeStruct((n, D_SSM, SSM_LANES), BF16),
                   jax.ShapeDtypeStruct((n, 4, SSM_LANES), F32)],
        compiler_params=_params(1),
        name="ssm_prep",
    )(lam_rows, b_re_placed, b_im_placed)


def _ssm_kernel(ua_ref, ub_ref, h0_ref, bmat_ref, cmat_ref, lamb_ref, d_ref, wg_ref, bg_ref,
                y_ref, fin_ref, ui_ref, buf0_ref, buf1_ref, yi_ref, *, n_per):
    seg_len = SEQ
    ssm_w = SSM_W
    n_rows = SUBLANES * seg_len
    n_cb = ssm_w // (2 * LANES)

    def step_rows(t):
        return pl.ds(pl.multiple_of(t * SUBLANES, SUBLANES), SUBLANES)

    block_steps = SSM_BLOCK // SUBLANES
    blocks = [slice(b * SSM_BLOCK, (b + 1) * SSM_BLOCK) for b in range(n_rows // SSM_BLOCK)]

    def interleave(b):
        for t in range(b * block_steps, (b + 1) * block_steps):
            rows = slice(t * SUBLANES, (t + 1) * SUBLANES)
            ui_ref[rows, 0:LANES] = ua_ref[pl.ds(t, SUBLANES, stride=seg_len), :]
            ui_ref[rows, LANES:2 * LANES] = ub_ref[pl.ds(t, SUBLANES, stride=seg_len), :]

    seg = lax.broadcasted_iota(jnp.int32, (SUBLANES, LANES), 0) % n_per

    chains = [(dirn, k) for dirn in range(2) for k in range(n_cb)]
    n_parts = SSM_LANES // ssm_w
    bufs = (buf0_ref, buf1_ref)
    assert n_parts == len(bufs)

    def fill(part, rows):
        ub = ui_ref[rows, :].astype(BF16)
        for dirn, k in chains:
            c0 = k * 2 * LANES
            lane0 = part * ssm_w + c0
            bufs[part][dirn, rows, c0:c0 + 2 * LANES] = jnp.dot(
                ub, bmat_ref[dirn, :, lane0:lane0 + 2 * LANES], preferred_element_type=F32)

    def readout(part, rows):
        lo = part * ssm_w
        y = jnp.dot(bufs[part][0, rows, :].astype(BF16), cmat_ref[0, lo:lo + ssm_w, :],
                    preferred_element_type=F32)
        y = y + jnp.dot(bufs[part][1, rows, :].astype(BF16), cmat_ref[1, lo:lo + ssm_w, :],
                        preferred_element_type=F32)
        if part == 0:
            y = y + d_ref[...] * ui_ref[rows, :]
            yi_ref[0, rows, :] = y[:, :LANES]
            yi_ref[1, rows, :] = y[:, LANES:]
        else:
            yi_ref[0, rows, :] += y[:, :LANES]
            yi_ref[1, rows, :] += y[:, LANES:]

    def lam_rows(part, r):
        out = []
        for dirn, k in chains:
            c0 = part * ssm_w + k * 2 * LANES
            out.append(jnp.broadcast_to(lamb_ref[dirn, r:r + 1, c0:c0 + LANES], (SUBLANES, LANES)))
        return out

    def scan(part, init, store, side_work=None):
        buf = bufs[part]
        a_re, a_im = lam_rows(part, 0), lam_rows(part, 1)

        def step(i, hs):
            rows = (step_rows(i), step_rows(seg_len - 1 - i))
            new = []
            for c, (dirn, k) in enumerate(chains):
                re_sl = slice(k * 2 * LANES, k * 2 * LANES + LANES)
                im_sl = slice(k * 2 * LANES + LANES, (k + 1) * 2 * LANES)
                hr, hi = hs[2 * c], hs[2 * c + 1]
                nr = a_re[c] * hr - a_im[c] * hi + buf[dirn, rows[dirn], re_sl]
                ni = a_re[c] * hi + a_im[c] * hr + buf[dirn, rows[dirn], im_sl]
                if store:
                    buf[dirn, rows[dirn], re_sl] = nr
                    buf[dirn, rows[dirn], im_sl] = ni
                new += [nr, ni]
            return tuple(new)

        if side_work is None:
            return lax.fori_loop(0, seg_len, step, tuple(init), unroll=2)

        steps = SSM_BLOCK // SUBLANES

        def block(blk, hs):
            side_work(pl.ds(pl.multiple_of(blk * SSM_BLOCK, SSM_BLOCK), SSM_BLOCK))
            for j in range(steps):
                hs = step(blk * steps + j, hs)
            return hs

        return lax.fori_loop(0, n_rows // SSM_BLOCK, block, tuple(init))

    def initial_states(part, local):
        h0 = []
        for dirn, k in chains:
            c0 = part * ssm_w + k * 2 * LANES
            h0 += [h0_ref[dirn, :, c0:c0 + LANES], h0_ref[dirn, :, c0 + LANES:c0 + 2 * LANES]]
        if local is None:
            return h0
        s_re, s_im = lam_rows(part, 2), lam_rows(part, 3)
        init = []
        for c, (dirn, k) in enumerate(chains):
            edge = seg == (0 if dirn == 0 else n_per - 1)
            shift = 1 if dirn == 0 else SUBLANES - 1
            cr, ci = h0[2 * c], h0[2 * c + 1]
            lr = pltpu.roll(local[2 * c], shift, axis=0)
            li = pltpu.roll(local[2 * c + 1], shift, axis=0)
            for _ in range(n_per - 1):
                pr = pltpu.roll(cr, shift, axis=0)
                pi = pltpu.roll(ci, shift, axis=0)
                cr = jnp.where(edge, h0[2 * c], s_re[c] * pr - s_im[c] * pi + lr)
                ci = jnp.where(edge, h0[2 * c + 1], s_re[c] * pi + s_im[c] * pr + li)
            init += [cr, ci]
        return init

    zeros = [jnp.zeros((SUBLANES, LANES), F32)] * (2 * len(chains))
    interleave(0)
    for b in range(1, len(blocks)):
        fill(0, blocks[b - 1])
        interleave(b)
    fill(0, blocks[-1])
    assert n_parts == 2
    for part in range(n_parts):
        local = scan(part, zeros, False) if n_per > 1 else None
        side_work = functools.partial(fill, 1) if part == 0 else functools.partial(readout, 0)
        hs = scan(part, initial_states(part, local), True, side_work)
        for c, (dirn, k) in enumerate(chains):
            c0 = part * ssm_w + k * 2 * LANES
            fin_ref[dirn, :, c0:c0 + LANES] = hs[2 * c]
            fin_ref[dirn, :, c0 + LANES:c0 + 2 * LANES] = hs[2 * c + 1]

    def glu(rows):
        g = _gelu(jnp.concatenate([yi_ref[0, rows, :], yi_ref[1, rows, :]], axis=1))
        z = jnp.dot(g.astype(BF16), wg_ref[...], preferred_element_type=F32) + bg_ref[...]
        o = g * jax.nn.sigmoid(z)
        yi_ref[0, rows, :] = o[:, :LANES]
        yi_ref[1, rows, :] = o[:, LANES:]

    def deinterleave(b):
        for tb in range(b * block_steps // SUBLANES, (b + 1) * block_steps // SUBLANES):
            for s_idx in range(SUBLANES):
                dst = slice(s_idx * seg_len + tb * SUBLANES, s_idx * seg_len + (tb + 1) * SUBLANES)
                src = pl.ds(tb * SUBLANES * SUBLANES + s_idx, SUBLANES, stride=SUBLANES)
                y_ref[dst, 0:LANES] = yi_ref[0, src, :]
                y_ref[dst, LANES:2 * LANES] = yi_ref[1, src, :]

    last = n_parts - 1
    readout(last, blocks[0])
    for b in range(1, len(blocks)):
        readout(last, blocks[b])
        glu(blocks[b - 1])
        deinterleave(b - 1)
    glu(blocks[-1])
    deinterleave(len(blocks) - 1)


def _ssm(u_ctx, u_lat, prm, l):
    weights = [_fixed_spec((2, D_SSM, SSM_LANES), l), _fixed_spec((2, SSM_LANES, D_SSM), l),
               _fixed_spec((2, 4, SSM_LANES), l), _fixed_spec((1, D_SSM), l),
               _fixed_spec((D_SSM, D_SSM), l), _fixed_spec((1, D_SSM), l)]
    rows = SUBLANES * SEQ
    scratch = [pltpu.VMEM((rows, D_SSM), F32), pltpu.VMEM((2, rows, SSM_W), F32),
               pltpu.VMEM((2, rows, SSM_W), F32), pltpu.VMEM((2, rows, LANES), F32)]
    tile_a = pl.BlockSpec((rows, LANES), lambda i: (i, 0))
    tile_y = pl.BlockSpec((rows, D_SSM), lambda i: (i, 0))
    fin_spec = pl.BlockSpec((None, 2, SUBLANES, SSM_LANES), lambda i: (i, 0, 0, 0))

    tail = (prm['ssm_b'], prm['ssm_c'], prm['ssm_lam'], prm['ssm_d'], prm['ssm_w_glu'], prm['ssm_b_glu'])
    n_ctx_tiles = N_CTX // rows
    h0_ctx = jnp.zeros((2, SUBLANES, SSM_LANES), F32)
    y_ctx, fin_ctx = pl.pallas_call(
        functools.partial(_ssm_kernel, n_per=1),
        grid=(n_ctx_tiles,),
        in_specs=[tile_a, tile_a, _fixed_spec((2, SUBLANES, SSM_LANES))] + weights,
        out_specs=[tile_y, fin_spec],
        out_shape=[jax.ShapeDtypeStruct((N_CTX, D_SSM), F32),
                   jax.ShapeDtypeStruct((n_ctx_tiles, 2, SUBLANES, SSM_LANES), F32)],
        scratch_shapes=scratch,
        compiler_params=_params(1),
        name="ssm_ctx",
    )(*u_ctx, h0_ctx, *tail)
    h0_spec = pl.BlockSpec((None, None, 2, SUBLANES, SSM_LANES), lambda b: (l, b, 0, 0, 0))
    y_lat, _ = pl.pallas_call(
        functools.partial(_ssm_kernel, n_per=SSM_LAT_SEGS),
        grid=(DEC_BATCH,),
        in_specs=[tile_a, tile_a, h0_spec] + weights,
        out_specs=[tile_y, fin_spec],
        out_shape=[jax.ShapeDtypeStruct((N_LAT, D_SSM), F32),
                   jax.ShapeDtypeStruct((DEC_BATCH, 2, SUBLANES, SSM_LANES), F32)],
        scratch_shapes=scratch,
        compiler_params=_params(1),
        name="ssm_lat",
    )(*u_lat, prm['ssm_h0'], *tail)
    return (y_ctx, y_lat), fin_ctx


def _rope_tables():
    rows = DEC_SEQ // GRID_W
    row_idx = jnp.repeat(jnp.arange(rows), GRID_W).astype(F32)
    col_idx = jnp.tile(jnp.arange(GRID_W), rows).astype(F32)
    n_freq = HEAD_DIM // 4
    inv = ROPE_THETA ** (-jnp.arange(n_freq, dtype=F32) / n_freq)
    ang = jnp.concatenate([row_idx[:, None] * inv, col_idx[:, None] * inv], axis=-1)
    cos = jnp.cos(ang)
    sin = jnp.sin(ang)
    cos_h = jnp.concatenate([cos, cos], axis=-1)
    sin_h = jnp.concatenate([-sin, sin], axis=-1)
    return jnp.tile(cos_h, (1, N_HEADS)), jnp.tile(sin_h, (1, N_HEADS))


def _state_lanes(a):
    return a.reshape(a.shape[:-2] + (N_SSM_GROUPS // 2, 2 * SSM_STATE))


def _ssm_lane_rows(a):
    a = _state_lanes(a)
    return jnp.stack([a, a], axis=-2).reshape(a.shape[:-2] + (SSM_LANES,))


def _ssm_pack_state(re, im):
    return jnp.stack([_state_lanes(re), _state_lanes(im)], axis=-2).reshape(re.shape[:-2] + (SSM_LANES,))


def _ssm_unpack_state(s):
    s = s.reshape(s.shape[:-1] + (N_SSM_GROUPS // 2, 2, 2 * SSM_STATE))
    shp = s.shape[:-3] + (N_SSM_GROUPS, SSM_STATE)
    return s[..., 0, :].reshape(shp), s[..., 1, :].reshape(shp)


def _ssm_group_mask():
    ch_g = jnp.arange(D_SSM) // SSM_GROUP
    lane = jnp.arange(SSM_LANES)
    lane_g = 2 * (lane // (2 * LANES)) + (lane % LANES) // SSM_STATE
    return ch_g[:, None] == lane_g[None, :]


def _b_rows(b):
    bt = jnp.swapaxes(b, -1, -2).reshape(DEPTH * 2, D_SSM, SSM_STATE)
    return jnp.concatenate([bt, bt], axis=-1)


def _place_c(c_re, c_im):
    def rows(c):
        ct = jnp.moveaxis(c, -1, -3).reshape(c.shape[:-3] + (SSM_STATE, D_SSM))
        return jnp.tile(ct, (1,) * (ct.ndim - 2) + (SSM_LANES // SSM_STATE, 1))
    is_re = (jnp.arange(SSM_LANES) % (2 * LANES) < LANES)[:, None]
    return jnp.where(_ssm_group_mask().T, jnp.where(is_re, rows(c_re), -rows(c_im)), 0.0)


def _block_diag_pool(pool_w):
    tiled = jnp.tile(pool_w.reshape(DEPTH, D_POOL, POOL_GROUP), (1, 1, len(POOL_WINDOWS)))
    grp = jnp.arange(D_POOL) // POOL_GROUP
    return jnp.where(grp[:, None] == grp[None, :], tiled, 0.0)


def kernel(x_prompt, x_sample, cache_k, cache_v, state_ssm_re, state_ssm_im, c, c_ctx, w_mod, b_mod, ln_g, ln_b, ffn_w1, ffn_w2, w_in, w_out, pool_w, pool_scale, ssm_lam_re, ssm_lam_im, ssm_log_step, ssm_b_re, ssm_b_im, ssm_c_re, ssm_c_im, ssm_d, ssm_w_glu, ssm_b_glu, q_norm, k_norm, chunk_ln_g, chunk_ln_b, chunk_w_s, chunk_b_s):
    cond = jnp.concatenate([c_ctx[None, :], c, jnp.zeros((N_COND - 1 - DEC_BATCH, D_MODEL), F32)], axis=0)
    mod = _modulation(cond, w_mod, b_mod)
    cos_t, sin_t = _rope_tables()

    log_step = jnp.broadcast_to(ssm_log_step[..., None], ssm_lam_re.shape)
    lam_rows = jnp.stack([_ssm_lane_rows(ssm_lam_re), _ssm_lane_rows(ssm_lam_im),
                          _ssm_lane_rows(log_step)], axis=-2)
    bmat, lamb = _ssm_prep(lam_rows.reshape(DEPTH * 2, 3, SSM_LANES),
                           _b_rows(ssm_b_re), _b_rows(ssm_b_im))
    h0 = _ssm_pack_state(state_ssm_re, state_ssm_im)
    h0_lat = jnp.zeros((DEPTH, DEC_BATCH, 2, SSM_LAT_SEGS, SSM_LANES), F32)
    h0_lat = h0_lat.at[:, :, 0, 0].set(jnp.swapaxes(h0[:, :, 0], 0, 1))
    h0_lat = h0_lat.at[:, :, 1, SSM_LAT_SEGS - 1].set(jnp.swapaxes(h0[:, :, 1], 0, 1))

    vec = lambda a: a[..., None, :]
    prm = {
        'ln_g': vec(ln_g), 'ln_b': vec(ln_b),
        'ffn_w1_f32': ffn_w1, 'ffn_w2_f32': ffn_w2,
        'q_norm': vec(jnp.tile(q_norm, (1, N_HEADS))), 'k_norm': vec(jnp.tile(k_norm, (1, N_KV_HEADS))),
        'chunk_ln_g': vec(chunk_ln_g), 'chunk_ln_b': vec(chunk_ln_b),
        'chunk_b_s': jnp.repeat(jnp.swapaxes(chunk_b_s, 1, 2), D_CHUNK // N_CHUNK_HEADS, axis=2),
        'pool_scale': vec(pool_scale),
        'ssm_b': bmat.reshape(DEPTH, 2, D_SSM, SSM_LANES),
        'ssm_lam': lamb.reshape(DEPTH, 2, 4, SSM_LANES),
        'ssm_h0': h0_lat,
        'ssm_d': vec(ssm_d), 'ssm_b_glu': vec(ssm_b_glu),
    }
    cache_k = cache_k.reshape(DEC_BATCH, DEPTH, PAST_LEN, D_KV).astype(BF16)
    cache_v = cache_v.reshape(DEC_BATCH, DEPTH, PAST_LEN, D_KV).astype(BF16)

    x = (x_prompt.reshape(N_CTX, D_MODEL), x_sample.reshape(N_LAT, D_MODEL))
    w_ffn = (ffn_w1[0, 0].astype(BF16), ffn_w2[0, 0].astype(BF16))
    f32_weights = (w_in, w_out, ssm_w_glu, _block_diag_pool(pool_w),
                   chunk_w_s.reshape(DEPTH, N_CHUNK_HEADS * CHUNK, CHUNK),
                   _place_c(ssm_c_re, ssm_c_im).reshape(DEPTH, 2 * SSM_LANES, D_SSM))
    ks, vs, s_re, s_im = [], [], [], []
    for l in range(DEPTH):
        x, w_next = _ffn_sublayer(x, mod, prm, l, 0, w_ffn, convert=(l, 1),
                                  convert_stacked=f32_weights if l == 0 else ())
        w_ffn = w_next[:2]
        if l == 0:
            prm['w_in'], prm['w_out'], prm['ssm_w_glu'], prm['pool_w'], cws, cmat = w_next[2:]
            prm['chunk_w_s'] = cws.reshape(chunk_w_s.shape)
            prm['ssm_c'] = cmat.reshape(DEPTH, 2, SSM_LANES, D_SSM)
        ctx_out, lat_out = _inproj(x, mod, prm, l, cos_t, sin_t)
        ua_ctx, ub_ctx, yc_ctx, k_ctx, v_ctx, yp_ctx, ya_ctx = ctx_out
        ua_lat, ub_lat, yc_lat, k_lat, v_lat, p_lat, q_lat = lat_out
        y_ssm, fin = _ssm((ua_ctx, ub_ctx), (ua_lat, ub_lat), prm, l)
        ya_lat, yp_lat = _latent_mix(q_lat, k_lat, v_lat, p_lat, cache_k, cache_v, prm, l)
        x, w_ffn = _ffn_sublayer(x, mod, prm, l, 1, w_ffn, convert=(l + 1, 0) if l + 1 < DEPTH else None,
                                 split_out=(l == DEPTH - 1),
                                 mixer=((yp_ctx, yp_lat), y_ssm, (ya_ctx, ya_lat), (yc_ctx, yc_lat)))
        ks.append(k_ctx)
        vs.append(v_ctx)
        f_re, f_im = _ssm_unpack_state(jnp.transpose(fin, (0, 2, 1, 3)).reshape(BATCH, 2, SSM_LANES))
        s_re.append(f_re)
        s_im.append(f_im)
    y_p = x[0].reshape(BATCH, SEQ, D_MODEL)
    y_s = x[1].reshape(DEC_BATCH, DEC_SEQ, D_MODEL)
    def cache(ts):
        t = jnp.stack(ts).reshape(DEPTH, N_KV_HEADS, HEAD_DIM, BATCH, SEQ)
        return jnp.transpose(t, (3, 0, 4, 1, 2))

    return (y_p, y_s, cache(ks), cache(vs), jnp.stack(s_re, axis=1), jnp.stack(s_im, axis=1))
```

```python
import functools
import math

import jax
import jax.numpy as jnp
from jax import lax
from jax.experimental import pallas as pl
from jax.experimental.pallas import tpu as pltpu

F32 = jnp.float32
BF16 = jnp.bfloat16

D_MODEL = 1024
BATCH = 32
SEQ = 256
DEPTH = 2
DEC_BATCH = 2
DEC_SEQ = 2048
PAST_LEN = 512
GRID_W = 64
D_POOL = 256
D_SSM = 256
D_ATTN = 256
D_CHUNK = 256
D_HEADS = 256
POOL_WINDOWS = (2, 4, 8, 16)
POOL_GROUP = 64
SSM_GROUP = 16
N_SSM_GROUPS = 16
SSM_STATE = 64
HEAD_DIM = 64
N_HEADS = 4
N_KV_HEADS = 2
D_KV = 128
CHUNK = 128
N_CHUNK_HEADS = 4
D_FF = 2816
N_MOD = 9
D_IN = 1536
ALPHA = (2 * DEPTH) ** 0.25
LN_EPS = 1e-5
RMS_EPS = 1e-6
ROPE_THETA = 10000.0

LANES = 128
SUBLANES = 8
TM = 512
FFN_SUB = 256
FFN_CONV_STEPS = 16
ATT_TQ = 512
N_CTX = BATCH * SEQ
N_LAT = DEC_BATCH * DEC_SEQ
N_TOK = N_CTX + N_LAT
CTX_TILES = N_CTX // TM
LAT_TILES_PER_SEQ = DEC_SEQ // TM
N_TILES = N_TOK // TM
N_COND = 8
MOD_COLS = 2304
POOL_PAD = 8
SSM_LANES = 2 * N_SSM_GROUPS * SSM_STATE
SSM_W = 1024
SSM_LAT_SEGS = SUBLANES
SSM_LAT_SEG = DEC_SEQ // SSM_LAT_SEGS
SSM_BLOCK = 512
VMEM_LIMIT = 56 * 1024 * 1024


def _params(n_grid, independent=False):
    semantics = "parallel" if independent else "arbitrary"
    return pltpu.CompilerParams(dimension_semantics=(semantics,) * n_grid,
                                vmem_limit_bytes=VMEM_LIMIT)


def _fixed_spec(tail, *lead):
    idx = tuple(lead) + (0,) * len(tail)
    return pl.BlockSpec((None,) * len(lead) + tuple(tail), lambda *_: idx, pipeline_mode=pl.Buffered(1))


def _cond_row(i):
    return jnp.where(i < CTX_TILES, 0, 1 + (i - CTX_TILES) // LAT_TILES_PER_SEQ)


def _mod_spec(l):
    return pl.BlockSpec((None, None, N_MOD, D_MODEL), lambda i: (l, _cond_row(i), 0, 0))


def _tok_tile(i):
    return (i, 0)


def _ctx_tile(i):
    return (jnp.minimum(i, CTX_TILES - 1), 0)


def _lat_tile(i):
    return (jnp.maximum(i - CTX_TILES, 0), 0)


def _layer_norm(y, g, b):
    mu = jnp.mean(y, axis=-1, keepdims=True)
    d = y - mu
    var = jnp.mean(d * d, axis=-1, keepdims=True)
    return d * lax.rsqrt(var + LN_EPS) * g + b


def _gelu(x):
    return 0.5 * x * (1.0 + jnp.tanh(math.sqrt(2.0 / math.pi) * (x + 0.044715 * (x * x * x))))


def _silu(x):
    return x * jax.nn.sigmoid(x)


def _split_bf16(a):
    hi = a.astype(BF16)
    return hi, (a - hi.astype(F32)).astype(BF16)


def _mod_kernel(cond_ref, w_ref, b_ref, o_ref):
    a_hi, a_lo = _split_bf16(_silu(cond_ref[...]))
    w_hi, w_lo = _split_bf16(w_ref[...])
    acc = jnp.dot(a_hi, w_lo, preferred_element_type=F32)
    acc = acc + jnp.dot(a_lo, w_hi, preferred_element_type=F32)
    acc = acc + jnp.dot(a_hi, w_hi, preferred_element_type=F32)
    o_ref[...] = acc + b_ref[...]


def _modulation(cond, w_mod, b_mod):
    tn = MOD_COLS
    out = pl.pallas_call(
        _mod_kernel,
        grid=(DEPTH, N_MOD * D_MODEL // tn),
        in_specs=[
            pl.BlockSpec((N_COND, D_MODEL), lambda l, j: (0, 0)),
            pl.BlockSpec((None, D_MODEL, tn), lambda l, j: (l, 0, j)),
            pl.BlockSpec((None, 1, tn), lambda l, j: (l, 0, j)),
        ],
        out_specs=pl.BlockSpec((None, N_COND, tn), lambda l, j: (l, 0, j)),
        out_shape=jax.ShapeDtypeStruct((DEPTH, N_COND, N_MOD * D_MODEL), F32),
        compiler_params=_params(2),
        name="modulation",
    )(cond, w_mod, b_mod.reshape(DEPTH, 1, N_MOD * D_MODEL))
    return out.reshape(DEPTH, N_COND, N_MOD, D_MODEL)


def _ffn_kernel(*refs, mod_off, split_in, split_out, mixer_out, n_conv):
    n_in = 2 if split_in else 1
    n_mix = 11 if mixer_out else 0
    mix_refs = refs[n_in:n_in + n_mix]
    mod_ref, w1_ref, w2_ref, g_ref, b_ref = refs[n_in + n_mix:n_in + n_mix + 5]
    conv_in = refs[n_in + n_mix + 5:n_in + n_mix + 5 + n_conv]
    out_refs = refs[n_in + n_mix + 5 + n_conv:]
    if n_conv:
        conv_out = out_refs[-n_conv:]
        out_refs = out_refs[:-n_conv]

        @pl.when(pl.program_id(0) < FFN_CONV_STEPS)
        def _():
            for src, dst in zip(conv_in, conv_out):
                dst[...] = src[...].astype(BF16)
    is_ctx = pl.program_id(0) < CTX_TILES
    sh = mod_ref[mod_off:mod_off + 1, :]
    sc = mod_ref[mod_off + 1:mod_off + 2, :]
    gate = mod_ref[mod_off + 2:mod_off + 3, :]

    def load_x(rows):
        if split_in:
            return jnp.where(is_ctx, refs[0][rows, :], refs[1][rows, :])
        return refs[0][rows, :]

    def project(rows):
        if not mixer_out:
            return None
        wo_ref = mix_refs[8]
        parts = [jnp.where(is_ctx, mix_refs[2 * j][rows, :], mix_refs[2 * j + 1][rows, :])
                 for j in range(4)]
        y = None
        for j, part in enumerate(parts):
            r = jnp.dot(part.astype(BF16), wo_ref[D_HEADS * j:D_HEADS * (j + 1), :],
                        preferred_element_type=F32)
            y = r if y is None else y + r
        return y

    def mixer_norm(x, y):
        if not mixer_out:
            return x
        go_ref, bo_ref = mix_refs[9:]
        return _layer_norm(ALPHA * x + mod_ref[5:6, :] * y, go_ref[...], bo_ref[...])

    def up(x):
        h = (x * (1.0 + sc) + sh).astype(BF16)
        return jnp.dot(h, w1_ref[...], preferred_element_type=F32)

    def down(gu):
        a = (_silu(gu[:, :D_FF]) * gu[:, D_FF:]).astype(BF16)
        return jnp.dot(a, w2_ref[...], preferred_element_type=F32)

    def finish(x, f):
        return _layer_norm(ALPHA * x + (0.5 * gate) * f, g_ref[...], b_ref[...])

    row_slices = [slice(j * FFN_SUB, (j + 1) * FFN_SUB) for j in range(TM // FFN_SUB)]
    n = len(row_slices)
    ys = []
    x = mixer_norm(load_x(row_slices[0]), project(row_slices[0]))
    gu = up(x)
    for j in range(n):
        y_next = project(row_slices[j + 1]) if j + 1 < n else None
        f = down(gu)
        if j + 1 < n:
            x_next = mixer_norm(load_x(row_slices[j + 1]), y_next)
            gu = up(x_next)
        ys.append(finish(x, f))
        if j + 1 < n:
            x = x_next

    def store(o_ref):
        for j, rows in enumerate(row_slices):
            o_ref[rows, :] = ys[j]

    if split_out:
        @pl.when(is_ctx)
        def _():
            store(out_refs[0])

        @pl.when(jnp.logical_not(is_ctx))
        def _():
            store(out_refs[1])
    else:
        store(out_refs[0])


def _ffn_sublayer(xs, mod, prm, l, sub, weights, convert=None, convert_stacked=(), split_out=False, mixer=None):
    split_in = isinstance(xs, tuple)
    if split_in:
        x_specs = [pl.BlockSpec((TM, D_MODEL), _ctx_tile), pl.BlockSpec((TM, D_MODEL), _lat_tile)]
    else:
        xs = (xs,)
        x_specs = [pl.BlockSpec((TM, D_MODEL), _tok_tile)]
    mix_args, mix_specs = (), []
    if mixer is not None:
        y_pool, y_ssm, y_attn, y_chunk = mixer
        mix_args = (*y_pool, *y_ssm, *y_attn, *y_chunk, prm['w_out'], prm['ln_g'], prm['ln_b'])
        mix_specs = [pl.BlockSpec((TM, D_HEADS), _ctx_tile), pl.BlockSpec((TM, D_HEADS), _lat_tile)] * 4 + [
            _fixed_spec((D_MODEL, D_MODEL), l), _fixed_spec((1, D_MODEL), l, 1), _fixed_spec((1, D_MODEL), l, 1)]
    if split_out:
        out_specs = [pl.BlockSpec((TM, D_MODEL), _ctx_tile), pl.BlockSpec((TM, D_MODEL), _lat_tile)]
        out_shape = [jax.ShapeDtypeStruct((N_CTX, D_MODEL), F32), jax.ShapeDtypeStruct((N_LAT, D_MODEL), F32)]
    else:
        out_specs = pl.BlockSpec((TM, D_MODEL), _tok_tile)
        out_shape = jax.ShapeDtypeStruct((N_TOK, D_MODEL), F32)
    ln_idx = 2 * sub
    out_specs = list(out_specs) if split_out else [out_specs]
    out_shape = list(out_shape) if split_out else [out_shape]
    conv_args, conv_specs = (), []
    if convert is not None:
        cl, cs = convert
        step = lambda i: jnp.minimum(i, FFN_CONV_STEPS - 1)
        r1, r2 = D_MODEL // FFN_CONV_STEPS, D_FF // FFN_CONV_STEPS
        conv_args = (prm['ffn_w1_f32'], prm['ffn_w2_f32'])
        conv_specs = [pl.BlockSpec((None, None, r1, 2 * D_FF), lambda i: (cl, cs, step(i), 0)),
                      pl.BlockSpec((None, None, r2, D_MODEL), lambda i: (cl, cs, step(i), 0))]
        out_specs += [pl.BlockSpec((r1, 2 * D_FF), lambda i: (step(i), 0)),
                      pl.BlockSpec((r2, D_MODEL), lambda i: (step(i), 0))]
        out_shape += [jax.ShapeDtypeStruct((D_MODEL, 2 * D_FF), BF16),
                      jax.ShapeDtypeStruct((D_FF, D_MODEL), BF16)]
    for w in convert_stacked:
        block = (w.shape[0], w.shape[1] // FFN_CONV_STEPS, w.shape[2])
        spec = pl.BlockSpec(block, lambda i: (0, jnp.minimum(i, FFN_CONV_STEPS - 1), 0))
        conv_args += (w,)
        conv_specs.append(spec)
        out_specs.append(spec)
        out_shape.append(jax.ShapeDtypeStruct(w.shape, BF16))
    outs = pl.pallas_call(
        functools.partial(_ffn_kernel, mod_off=6 * sub, split_in=split_in, split_out=split_out,
                          mixer_out=mixer is not None, n_conv=len(conv_args)),
        grid=(N_TILES,),
        in_specs=x_specs + mix_specs + [
            _mod_spec(l),
            _fixed_spec((D_MODEL, 2 * D_FF)),
            _fixed_spec((D_FF, D_MODEL)),
            _fixed_spec((1, D_MODEL), l, ln_idx),
            _fixed_spec((1, D_MODEL), l, ln_idx),
        ] + conv_specs,
        out_specs=out_specs,
        out_shape=out_shape,
        compiler_params=_params(1),
        name="ffn_sublayer",
    )(*xs, *mix_args, mod, *weights, prm['ln_g'], prm['ln_b'], *conv_args)
    n_main = 2 if split_out else 1
    x_out = tuple(outs[:n_main]) if split_out else outs[0]
    return x_out, tuple(outs[n_main:])


def _pool_math(x, w_bd, scale, seq_len):
    n = seq_len + 2 * POOL_PAD
    zpad = jnp.zeros((POOL_PAD, D_POOL), F32)
    e = jnp.concatenate([zpad, x, zpad], axis=0)

    def prev(a, d):
        return pltpu.roll(a, d, axis=0)

    def nxt(a, d):
        return pltpu.roll(a, n - d, axis=0)

    s2 = e + prev(e, 1)
    s4 = prev(s2, 1) + nxt(s2, 1)
    s8 = prev(s4, 2) + nxt(s4, 2)
    s16 = prev(s8, 4) + nxt(s8, 4)
    lane = lax.broadcasted_iota(jnp.int32, (seq_len, D_POOL), 1)
    grp = lane // POOL_GROUP
    sl = slice(POOL_PAD, POOL_PAD + seq_len)
    s = jnp.where(grp == 0, s2[sl], jnp.where(grp == 1, s4[sl], jnp.where(grp == 2, s8[sl], s16[sl])))
    half = jnp.where(grp == 0, 1, jnp.where(grp == 1, 2, jnp.where(grp == 2, 4, 8)))
    t = lax.broadcasted_iota(jnp.int32, (seq_len, D_POOL), 0)
    cnt = jnp.minimum(t + half, seq_len) - jnp.maximum(t - half, 0)
    d = (s / cnt.astype(F32) - x).astype(BF16)
    return jnp.dot(d, w_bd, preferred_element_type=F32) * scale


def _attn_math(problems, mxu_sums):
    nt = (((1,), (1,)), ((), ()))
    tq = problems[0][0].shape[0]
    lo = lax.broadcasted_iota(jnp.int32, (tq, LANES), 1) < HEAD_DIM
    scores = []
    for q, k, _ in problems:
        q = q * (HEAD_DIM ** -0.5 * math.log2(math.e))
        q_lo = q[:, :LANES]
        q_hi = q[:, LANES:]
        qs = [jnp.where(lo, q_lo, 0.0), jnp.where(lo, pltpu.roll(q_lo, HEAD_DIM, axis=1), 0.0),
              jnp.where(lo, 0.0, pltpu.roll(q_hi, HEAD_DIM, axis=1)), jnp.where(lo, 0.0, q_hi)]
        for g in range(N_KV_HEADS):
            qq = jnp.concatenate([qs[2 * g], qs[2 * g + 1]], axis=0).astype(BF16)
            scores.append(lax.dot_general(qq, k, nt, preferred_element_type=F32))
    results = []
    for i, (_, _, v) in enumerate(problems):
        if mxu_sums:
            lo_k = lax.broadcasted_iota(jnp.int32, v.shape, 1) < HEAD_DIM
            one = jnp.ones_like(v)
            v_g = [jnp.where(lo_k, v, one), jnp.where(lo_k, one, v)]
        outs = []
        for g in range(N_KV_HEADS):
            s = scores[N_KV_HEADS * i + g]
            p = jnp.exp2(s - jnp.max(s, axis=-1, keepdims=True))
            if mxu_sums:
                o = jnp.dot(p.astype(BF16), v_g[g], preferred_element_type=F32)
            else:
                l = jnp.sum(p, axis=-1, keepdims=True)
                o = jnp.dot(p.astype(BF16), v, preferred_element_type=F32) / l
            outs += [o[:tq], o[tq:]]
        if mxu_sums:
            r = [pltpu.roll(o, HEAD_DIM, axis=1) for o in outs]
            out_lo = jnp.where(lo, outs[0], r[1]) / jnp.where(lo, r[0], outs[1])
            out_hi = jnp.where(lo, r[2], outs[3]) / jnp.where(lo, outs[2], r[3])
        else:
            out_lo = jnp.where(lo, outs[0], pltpu.roll(outs[1], HEAD_DIM, axis=1))
            out_hi = jnp.where(lo, pltpu.roll(outs[2], HEAD_DIM, axis=1), outs[3])
        results.append(jnp.concatenate([out_lo, out_hi], axis=1))
    return results


def _seg_rms(x, gain, n_lanes):
    parts = []
    for j in range(n_lanes // LANES):
        xs = x[:, j * LANES:(j + 1) * LANES]
        sq = xs * xs
        lo = lax.broadcasted_iota(jnp.int32, xs.shape, 1) < HEAD_DIM
        s_all = jnp.sum(sq, axis=-1, keepdims=True)
        s_lo = jnp.sum(jnp.where(lo, sq, 0.0), axis=-1, keepdims=True)
        ms = jnp.where(lo, s_lo, s_all - s_lo) * (1.0 / HEAD_DIM)
        parts.append(xs * lax.rsqrt(ms + RMS_EPS))
    y = parts[0] if len(parts) == 1 else jnp.concatenate(parts, axis=1)
    return y * gain


def _rope(x, cos_t, sin_t, n_lanes):
    parts = []
    for j in range(n_lanes // LANES):
        xs = x[:, j * LANES:(j + 1) * LANES]
        first = (lax.broadcasted_iota(jnp.int32, xs.shape, 1) % HEAD_DIM) < (HEAD_DIM // 2)
        partner = jnp.where(first, pltpu.roll(xs, LANES - HEAD_DIM // 2, axis=1),
                            pltpu.roll(xs, HEAD_DIM // 2, axis=1))
        cs = cos_t[:, j * LANES:(j + 1) * LANES]
        sn = sin_t[:, j * LANES:(j + 1) * LANES]
        parts.append(xs * cs + partner * sn)
    return parts[0] if len(parts) == 1 else jnp.concatenate(parts, axis=1)


def _inproj_kernel(*refs, ctx):
    if ctx:
        (x_ref, mod_ref, w_ref, qn_ref, kn_ref, cg_ref, cb_ref, ws_ref, bs_ref, pw_ref, ps_ref,
         ua_ref, ub_ref, chunk_ref, k_ref, v_ref, yp_ref, ya_ref) = refs
    else:
        (x_ref, mod_ref, w_ref, qn_ref, kn_ref, cg_ref, cb_ref, ws_ref, bs_ref, cos_ref, sin_ref,
         ua_ref, ub_ref, chunk_ref, k_ref, v_ref, p_ref, q_ref) = refs
    x = x_ref[...]
    sh = mod_ref[3:4, :]
    sc = mod_ref[4:5, :]
    h = (x * (1.0 + sc) + sh).astype(BF16)
    proj = jnp.dot(h, w_ref[...], preferred_element_type=F32)
    p = proj[:, 0:256]
    ua_ref[...] = proj[:, 256:384]
    ub_ref[...] = proj[:, 384:512]
    q = _seg_rms(proj[:, 512:768], qn_ref[...], D_ATTN)
    k = _seg_rms(proj[:, 768:896], kn_ref[...], D_KV)
    v = proj[:, 896:1024]
    if not ctx:
        cos_t = cos_ref[...]
        sin_t = sin_ref[...]
        q = _rope(q, cos_t, sin_t, D_ATTN)
        k = _rope(k, cos_t, sin_t, D_KV)
    zu = _gelu(proj[:, 1024:1280])
    zv = _layer_norm(_gelu(proj[:, 1280:1536]), cg_ref[...], cb_ref[...])
    head = lax.broadcasted_iota(jnp.int32, (CHUNK, D_CHUNK), 1) // (D_CHUNK // N_CHUNK_HEADS)
    for c in range(TM // CHUNK):
        vb = zv[c * CHUNK:(c + 1) * CHUNK, :].astype(BF16)
        mixed = bs_ref[...]
        for hd in range(N_CHUNK_HEADS):
            r = jnp.dot(ws_ref[hd], vb, preferred_element_type=F32)
            mixed = mixed + jnp.where(head == hd, r, 0.0)
        chunk_ref[c * CHUNK:(c + 1) * CHUNK, :] = (zu[c * CHUNK:(c + 1) * CHUNK, :] * mixed).astype(BF16)
    if ctx:
        k_ref[...] = k.T
        v_ref[...] = v.T
        for s in range(TM // SEQ):
            r = slice(s * SEQ, (s + 1) * SEQ)
            yp_ref[r, :] = _pool_math(p[r], pw_ref[...], ps_ref[...], SEQ).astype(BF16)
            ya_ref[r, :] = _attn_math([(q[r], k[r].astype(BF16), v[r].astype(BF16))], False)[0].astype(BF16)
    else:
        k_ref[...] = k.astype(BF16)
        v_ref[...] = v.astype(BF16)
        p_ref[...] = p
        q_ref[...] = q


def _inproj(x, mod, prm, l, cos_t, sin_t):
    tok = lambda i: (i, 0)
    common = [_fixed_spec((D_MODEL, D_IN), l), _fixed_spec((1, D_ATTN), l), _fixed_spec((1, D_KV), l),
              _fixed_spec((1, D_CHUNK), l), _fixed_spec((1, D_CHUNK), l),
              _fixed_spec((N_CHUNK_HEADS, CHUNK, CHUNK), l), _fixed_spec((CHUNK, D_CHUNK), l)]
    common_args = (prm['w_in'], prm['q_norm'], prm['k_norm'], prm['chunk_ln_g'], prm['chunk_ln_b'],
                   prm['chunk_w_s'], prm['chunk_b_s'])
    mod_block = (None, None, N_MOD, D_MODEL)

    def outs(rows, kv_dtype, tail_dtype):
        widths = [(LANES, F32), (LANES, F32), (D_CHUNK, BF16), (D_KV, kv_dtype), (D_KV, kv_dtype),
                  (D_HEADS, tail_dtype), (D_HEADS, tail_dtype)]
        return ([jax.ShapeDtypeStruct((rows, w), dt) for w, dt in widths],
                [pl.BlockSpec((TM, w), tok) for w, _ in widths])

    ctx_shapes, ctx_specs = outs(N_CTX, F32, BF16)
    for j in (3, 4):
        ctx_shapes[j] = jax.ShapeDtypeStruct((D_KV, N_CTX), F32)
        ctx_specs[j] = pl.BlockSpec((D_KV, TM), lambda i: (0, i))
    ctx = pl.pallas_call(
        functools.partial(_inproj_kernel, ctx=True),
        grid=(CTX_TILES,),
        in_specs=[pl.BlockSpec((TM, D_MODEL), tok), pl.BlockSpec(mod_block, lambda i: (l, 0, 0, 0))]
                 + common + [_fixed_spec((D_POOL, D_POOL), l), _fixed_spec((1, D_POOL), l)],
        out_specs=ctx_specs,
        out_shape=ctx_shapes,
        compiler_params=_params(1, independent=True),
        name="mixer_inproj_ctx",
    )(x, mod, *common_args, prm['pool_w'], prm['pool_scale'])
    lat_shapes, lat_specs = outs(N_LAT, BF16, F32)
    rope_spec = pl.BlockSpec((TM, D_ATTN), lambda i: (i % LAT_TILES_PER_SEQ, 0))
    lat = pl.pallas_call(
        functools.partial(_inproj_kernel, ctx=False),
        grid=(N_TILES - CTX_TILES,),
        in_specs=[pl.BlockSpec((TM, D_MODEL), lambda i: (CTX_TILES + i, 0)),
                  pl.BlockSpec(mod_block, lambda i: (l, 1 + i // LAT_TILES_PER_SEQ, 0, 0))]
                 + common + [rope_spec, rope_spec],
        out_specs=lat_specs,
        out_shape=lat_shapes,
        compiler_params=_params(1, independent=True),
        name="mixer_inproj_lat",
    )(x, mod, *common_args, cos_t, sin_t)
    return ctx, lat


def _latent_mix_kernel(q_ref, kc_ref, kl_ref, vc_ref, vl_ref, p_ref, pw_ref, ps_ref, ya_ref, yp_ref):
    k = jnp.concatenate([kc_ref[...], kl_ref[...]], axis=0)
    v = jnp.concatenate([vc_ref[...], vl_ref[...]], axis=0)
    ya_ref[...] = _attn_math([(q_ref[...], k, v)], True)[0].astype(BF16)

    @pl.when(pl.program_id(1) == 0)
    def _():
        yp_ref[...] = _pool_math(p_ref[...], pw_ref[...], ps_ref[...], DEC_SEQ).astype(BF16)


def _latent_mix(q_lat, k_lat, v_lat, p_lat, cache_k, cache_v, prm, l):
    q_per_seq = DEC_SEQ // ATT_TQ
    cache_spec = pl.BlockSpec((None, None, PAST_LEN, D_KV), lambda b, j: (b, l, 0, 0))
    seq_kv = pl.BlockSpec((DEC_SEQ, D_KV), lambda b, j: (b, 0))
    seq_pool = pl.BlockSpec((DEC_SEQ, D_POOL), lambda b, j: (b, 0))
    q_tile = pl.BlockSpec((ATT_TQ, D_ATTN), lambda b, j: (b * q_per_seq + j, 0))
    return pl.pallas_call(
        _latent_mix_kernel,
        grid=(DEC_BATCH, q_per_seq),
        in_specs=[q_tile, cache_spec, seq_kv, cache_spec, seq_kv, seq_pool,
                  _fixed_spec((D_POOL, D_POOL), l), _fixed_spec((1, D_POOL), l)],
        out_specs=[q_tile, seq_pool],
        out_shape=[jax.ShapeDtypeStruct((N_LAT, D_ATTN), BF16), jax.ShapeDtypeStruct((N_LAT, D_POOL), BF16)],
        compiler_params=_params(2),
        name="latent_mix",
    )(q_lat, cache_k, k_lat, cache_v, v_lat, p_lat, prm['pool_w'], prm['pool_scale'])


def _ssm_prep_kernel(lam_ref, br_ref, bi_ref, bmat_ref, lamb_ref):
    lre = lam_ref[0:1, :]
    lim = lam_ref[1:2, :]
    dt = jnp.exp(lam_ref[2:3, :])
    mag = jnp.exp(lre * dt)
    ar = mag * jnp.cos(lim * dt)
    ai = mag * jnp.sin(lim * dt)
    lamb_ref[0:1, :] = ar
    lamb_ref[1:2, :] = ai
    mag_s = jnp.exp(lre * dt * SSM_LAT_SEG)
    lamb_ref[2:3, :] = mag_s * jnp.cos(lim * dt * SSM_LAT_SEG)
    lamb_ref[3:4, :] = mag_s * jnp.sin(lim * dt * SSM_LAT_SEG)
    den = lre * lre + lim * lim
    cr = ((ar - 1.0) * lre + ai * lim) / den
    ci = (ai * lre - (ar - 1.0) * lim) / den
    n_rep = SSM_LANES // LANES
    br = jnp.concatenate([br_ref[...]] * n_rep, axis=1)
    bi = jnp.concatenate([bi_ref[...]] * n_rep, axis=1)
    lane = lax.broadcasted_iota(jnp.int32, br.shape, 1)
    row_g = lax.broadcasted_iota(jnp.int32, br.shape, 0) // SSM_GROUP
    lane_g = 2 * (lane // (2 * LANES)) + (lane % LANES) // SSM_STATE
    is_re = (lane % (2 * LANES)) < LANES
    b_bar = jnp.where(is_re, cr * br - ci * bi, cr * bi + ci * br)
    bmat_ref[...] = jnp.where(row_g == lane_g, b_bar, 0.0).astype(BF16)


def _ssm_prep(lam_rows, b_re_placed, b_im_placed):
    n = DEPTH * 2
    return pl.pallas_call(
        _ssm_prep_kernel,
        grid=(n,),
        in_specs=[pl.BlockSpec((None, 3, SSM_LANES), lambda i: (i, 0, 0)),
                  pl.BlockSpec((None, D_SSM, LANES), lambda i: (i, 0, 0)),
                  pl.BlockSpec((None, D_SSM, LANES), lambda i: (i, 0, 0))],
        out_specs=[pl.BlockSpec((None, D_SSM, SSM_LANES), lambda i: (i, 0, 0)),
                   pl.BlockSpec((None, 4, SSM_LANES), lambda i: (i, 0, 0))],
        out_shape=[jax.ShapeDtypeStruct((n, D_SSM, SSM_LANES), BF16),
                   jax.ShapeDtypeStruct((n, 4, SSM_LANES), F32)],
        compiler_params=_params(1),
        name="ssm_prep",
    )(lam_rows, b_re_placed, b_im_placed)


def _ssm_kernel(ua_ref, ub_ref, h0_ref, bmat_ref, cmat_ref, lamb_ref, d_ref, wg_ref, bg_ref,
                y_ref, fin_ref, ui_ref, buf0_ref, buf1_ref, yi_ref, *, n_per):
    seg_len = SEQ
    ssm_w = SSM_W
    n_rows = SUBLANES * seg_len
    n_cb = ssm_w // (2 * LANES)

    def step_rows(t):
        return pl.ds(pl.multiple_of(t * SUBLANES, SUBLANES), SUBLANES)

    block_steps = SSM_BLOCK // SUBLANES
    blocks = [slice(b * SSM_BLOCK, (b + 1) * SSM_BLOCK) for b in range(n_rows // SSM_BLOCK)]

    def interleave(b):
        for t in range(b * block_steps, (b + 1) * block_steps):
            rows = slice(t * SUBLANES, (t + 1) * SUBLANES)
            ui_ref[rows, 0:LANES] = ua_ref[pl.ds(t, SUBLANES, stride=seg_len), :]
            ui_ref[rows, LANES:2 * LANES] = ub_ref[pl.ds(t, SUBLANES, stride=seg_len), :]

    seg = lax.broadcasted_iota(jnp.int32, (SUBLANES, LANES), 0) % n_per

    chains = [(dirn, k) for dirn in range(2) for k in range(n_cb)]
    n_parts = SSM_LANES // ssm_w
    bufs = (buf0_ref, buf1_ref)
    assert n_parts == len(bufs)

    def fill(part, rows):
        ub = ui_ref[rows, :].astype(BF16)
        for dirn, k in chains:
            c0 = k * 2 * LANES
            lane0 = part * ssm_w + c0
            bufs[part][dirn, rows, c0:c0 + 2 * LANES] = jnp.dot(
                ub, bmat_ref[dirn, :, lane0:lane0 + 2 * LANES], preferred_element_type=F32)

    def readout(part, rows):
        lo = part * ssm_w
        y = jnp.dot(bufs[part][0, rows, :].astype(BF16), cmat_ref[0, lo:lo + ssm_w, :],
                    preferred_element_type=F32)
        y = y + jnp.dot(bufs[part][1, rows, :].astype(BF16), cmat_ref[1, lo:lo + ssm_w, :],
                        preferred_element_type=F32)
        if part == 0:
            y = y + d_ref[...] * ui_ref[rows, :]
            yi_ref[0, rows, :] = y[:, :LANES]
            yi_ref[1, rows, :] = y[:, LANES:]
        else:
            yi_ref[0, rows, :] += y[:, :LANES]
            yi_ref[1, rows, :] += y[:, LANES:]

    def lam_rows(part, r):
        out = []
        for dirn, k in chains:
            c0 = part * ssm_w + k * 2 * LANES
            out.append(jnp.broadcast_to(lamb_ref[dirn, r:r + 1, c0:c0 + LANES], (SUBLANES, LANES)))
        return out

    def scan(part, init, store, side_work=None):
        buf = bufs[part]
        a_re, a_im = lam_rows(part, 0), lam_rows(part, 1)

        def step(i, hs):
            rows = (step_rows(i), step_rows(seg_len - 1 - i))
            new = []
            for c, (dirn, k) in enumerate(chains):
                re_sl = slice(k * 2 * LANES, k * 2 * LANES + LANES)
                im_sl = slice(k * 2 * LANES + LANES, (k + 1) * 2 * LANES)
                hr, hi = hs[2 * c], hs[2 * c + 1]
                nr = a_re[c] * hr - a_im[c] * hi + buf[dirn, rows[dirn], re_sl]
                ni = a_re[c] * hi + a_im[c] * hr + buf[dirn, rows[dirn], im_sl]
                if store:
                    buf[dirn, rows[dirn], re_sl] = nr
                    buf[dirn, rows[dirn], im_sl] = ni
                new += [nr, ni]
            return tuple(new)

        if side_work is None:
            return lax.fori_loop(0, seg_len, step, tuple(init), unroll=2)

        steps = SSM_BLOCK // SUBLANES

        def block(blk, hs):
            side_work(pl.ds(pl.multiple_of(blk * SSM_BLOCK, SSM_BLOCK), SSM_BLOCK))
            for j in range(steps):
                hs = step(blk * steps + j, hs)
            return hs

        return lax.fori_loop(0, n_rows // SSM_BLOCK, block, tuple(init))

    def initial_states(part, local):
        h0 = []
        for dirn, k in chains:
            c0 = part * ssm_w + k * 2 * LANES
            h0 += [h0_ref[dirn, :, c0:c0 + LANES], h0_ref[dirn, :, c0 + LANES:c0 + 2 * LANES]]
        if local is None:
            return h0
        s_re, s_im = lam_rows(part, 2), lam_rows(part, 3)
        init = []
        for c, (dirn, k) in enumerate(chains):
            edge = seg == (0 if dirn == 0 else n_per - 1)
            shift = 1 if dirn == 0 else SUBLANES - 1
            cr, ci = h0[2 * c], h0[2 * c + 1]
            lr = pltpu.roll(local[2 * c], shift, axis=0)
            li = pltpu.roll(local[2 * c + 1], shift, axis=0)
            for _ in range(n_per - 1):
                pr = pltpu.roll(cr, shift, axis=0)
                pi = pltpu.roll(ci, shift, axis=0)
                cr = jnp.where(edge, h0[2 * c], s_re[c] * pr - s_im[c] * pi + lr)
                ci = jnp.where(edge, h0[2 * c + 1], s_re[c] * pi + s_im[c] * pr + li)
            init += [cr, ci]
        return init

    zeros = [jnp.zeros((SUBLANES, LANES), F32)] * (2 * len(chains))
    interleave(0)
    for b in range(1, len(blocks)):
        fill(0, blocks[b - 1])
        interleave(b)
    fill(0, blocks[-1])
    assert n_parts == 2
    for part in range(n_parts):
        local = scan(part, zeros, False) if n_per > 1 else None
        side_work = functools.partial(fill, 1) if part == 0 else functools.partial(readout, 0)
        hs = scan(part, initial_states(part, local), True, side_work)
        for c, (dirn, k) in enumerate(chains):
            c0 = part * ssm_w + k * 2 * LANES
            fin_ref[dirn, :, c0:c0 + LANES] = hs[2 * c]
            fin_ref[dirn, :, c0 + LANES:c0 + 2 * LANES] = hs[2 * c + 1]

    def glu(rows):
        g = _gelu(jnp.concatenate([yi_ref[0, rows, :], yi_ref[1, rows, :]], axis=1))
        z = jnp.dot(g.astype(BF16), wg_ref[...], preferred_element_type=F32) + bg_ref[...]
        o = g * jax.nn.sigmoid(z)
        yi_ref[0, rows, :] = o[:, :LANES]
        yi_ref[1, rows, :] = o[:, LANES:]

    def deinterleave(b):
        for tb in range(b * block_steps // SUBLANES, (b + 1) * block_steps // SUBLANES):
            for s_idx in range(SUBLANES):
                dst = slice(s_idx * seg_len + tb * SUBLANES, s_idx * seg_len + (tb + 1) * SUBLANES)
                src = pl.ds(tb * SUBLANES * SUBLANES + s_idx, SUBLANES, stride=SUBLANES)
                y_ref[dst, 0:LANES] = yi_ref[0, src, :]
                y_ref[dst, LANES:2 * LANES] = yi_ref[1, src, :]

    last = n_parts - 1
    readout(last, blocks[0])
    for b in range(1, len(blocks)):
        readout(last, blocks[b])
        glu(blocks[b - 1])
        deinterleave(b - 1)
    glu(blocks[-1])
    deinterleave(len(blocks) - 1)


def _ssm(u_ctx, u_lat, prm, l):
    weights = [_fixed_spec((2, D_SSM, SSM_LANES), l), _fixed_spec((2, SSM_LANES, D_SSM), l),
               _fixed_spec((2, 4, SSM_LANES), l), _fixed_spec((1, D_SSM), l),
               _fixed_spec((D_SSM, D_SSM), l), _fixed_spec((1, D_SSM), l)]
    rows = SUBLANES * SEQ
    scratch = [pltpu.VMEM((rows, D_SSM), F32), pltpu.VMEM((2, rows, SSM_W), F32),
               pltpu.VMEM((2, rows, SSM_W), F32), pltpu.VMEM((2, rows, LANES), F32)]
    tile_a = pl.BlockSpec((rows, LANES), lambda i: (i, 0))
    tile_y = pl.BlockSpec((rows, D_SSM), lambda i: (i, 0))
    fin_spec = pl.BlockSpec((None, 2, SUBLANES, SSM_LANES), lambda i: (i, 0, 0, 0))

    tail = (prm['ssm_b'], prm['ssm_c'], prm['ssm_lam'], prm['ssm_d'], prm['ssm_w_glu'], prm['ssm_b_glu'])
    n_ctx_tiles = N_CTX // rows
    h0_ctx = jnp.zeros((2, SUBLANES, SSM_LANES), F32)
    y_ctx, fin_ctx = pl.pallas_call(
        functools.partial(_ssm_kernel, n_per=1),
        grid=(n_ctx_tiles,),
        in_specs=[tile_a, tile_a, _fixed_spec((2, SUBLANES, SSM_LANES))] + weights,
        out_specs=[tile_y, fin_spec],
        out_shape=[jax.ShapeDtypeStruct((N_CTX, D_SSM), F32),
                   jax.ShapeDtypeStruct((n_ctx_tiles, 2, SUBLANES, SSM_LANES), F32)],
        scratch_shapes=scratch,
        compiler_params=_params(1),
        name="ssm_ctx",
    )(*u_ctx, h0_ctx, *tail)
    h0_spec = pl.BlockSpec((None, None, 2, SUBLANES, SSM_LANES), lambda b: (l, b, 0, 0, 0))
    y_lat, _ = pl.pallas_call(
        functools.partial(_ssm_kernel, n_per=SSM_LAT_SEGS),
        grid=(DEC_BATCH,),
        in_specs=[tile_a, tile_a, h0_spec] + weights,
        out_specs=[tile_y, fin_spec],
        out_shape=[jax.ShapeDtypeStruct((N_LAT, D_SSM), F32),
                   jax.ShapeDtypeStruct((DEC_BATCH, 2, SUBLANES, SSM_LANES), F32)],
        scratch_shapes=scratch,
        compiler_params=_params(1),
        name="ssm_lat",
    )(*u_lat, prm['ssm_h0'], *tail)
    return (y_ctx, y_lat), fin_ctx


def _rope_tables():
    rows = DEC_SEQ // GRID_W
    row_idx = jnp.repeat(jnp.arange(rows), GRID_W).astype(F32)
    col_idx = jnp.tile(jnp.arange(GRID_W), rows).astype(F32)
    n_freq = HEAD_DIM // 4
    inv = ROPE_THETA ** (-jnp.arange(n_freq, dtype=F32) / n_freq)
    ang = jnp.concatenate([row_idx[:, None] * inv, col_idx[:, None] * inv], axis=-1)
    cos = jnp.cos(ang)
    sin = jnp.sin(ang)
    cos_h = jnp.concatenate([cos, cos], axis=-1)
    sin_h = jnp.concatenate([-sin, sin], axis=-1)
    return jnp.tile(cos_h, (1, N_HEADS)), jnp.tile(sin_h, (1, N_HEADS))


def _state_lanes(a):
    return a.reshape(a.shape[:-2] + (N_SSM_GROUPS // 2, 2 * SSM_STATE))


def _ssm_lane_rows(a):
    a = _state_lanes(a)
    return jnp.stack([a, a], axis=-2).reshape(a.shape[:-2] + (SSM_LANES,))


def _ssm_pack_state(re, im):
    return jnp.stack([_state_lanes(re), _state_lanes(im)], axis=-2).reshape(re.shape[:-2] + (SSM_LANES,))


def _ssm_unpack_state(s):
    s = s.reshape(s.shape[:-1] + (N_SSM_GROUPS // 2, 2, 2 * SSM_STATE))
    shp = s.shape[:-3] + (N_SSM_GROUPS, SSM_STATE)
    return s[..., 0, :].reshape(shp), s[..., 1, :].reshape(shp)


def _ssm_group_mask():
    ch_g = jnp.arange(D_SSM) // SSM_GROUP
    lane = jnp.arange(SSM_LANES)
    lane_g = 2 * (lane // (2 * LANES)) + (lane % LANES) // SSM_STATE
    return ch_g[:, None] == lane_g[None, :]


def _b_rows(b):
    bt = jnp.swapaxes(b, -1, -2).reshape(DEPTH * 2, D_SSM, SSM_STATE)
    return jnp.concatenate([bt, bt], axis=-1)


def _place_c(c_re, c_im):
    def rows(c):
        ct = jnp.moveaxis(c, -1, -3).reshape(c.shape[:-3] + (SSM_STATE, D_SSM))
        return jnp.tile(ct, (1,) * (ct.ndim - 2) + (SSM_LANES // SSM_STATE, 1))
    is_re = (jnp.arange(SSM_LANES) % (2 * LANES) < LANES)[:, None]
    return jnp.where(_ssm_group_mask().T, jnp.where(is_re, rows(c_re), -rows(c_im)), 0.0)


def _block_diag_pool(pool_w):
    tiled = jnp.tile(pool_w.reshape(DEPTH, D_POOL, POOL_GROUP), (1, 1, len(POOL_WINDOWS)))
    grp = jnp.arange(D_POOL) // POOL_GROUP
    return jnp.where(grp[:, None] == grp[None, :], tiled, 0.0)


def kernel(x_prompt, x_sample, cache_k, cache_v, state_ssm_re, state_ssm_im, c, c_ctx, w_mod, b_mod, ln_g, ln_b, ffn_w1, ffn_w2, w_in, w_out, pool_w, pool_scale, ssm_lam_re, ssm_lam_im, ssm_log_step, ssm_b_re, ssm_b_im, ssm_c_re, ssm_c_im, ssm_d, ssm_w_glu, ssm_b_glu, q_norm, k_norm, chunk_ln_g, chunk_ln_b, chunk_w_s, chunk_b_s):
    cond = jnp.concatenate([c_ctx[None, :], c, jnp.zeros((N_COND - 1 - DEC_BATCH, D_MODEL), F32)], axis=0)
    mod = _modulation(cond, w_mod, b_mod)
    cos_t, sin_t = _rope_tables()

    log_step = jnp.broadcast_to(ssm_log_step[..., None], ssm_lam_re.shape)
    lam_rows = jnp.stack([_ssm_lane_rows(ssm_lam_re), _ssm_lane_rows(ssm_lam_im),
                          _ssm_lane_rows(log_step)], axis=-2)
    bmat, lamb = _ssm_prep(lam_rows.reshape(DEPTH * 2, 3, SSM_LANES),
                           _b_rows(ssm_b_re), _b_rows(ssm_b_im))
    h0 = _ssm_pack_state(state_ssm_re, state_ssm_im)
    h0_lat = jnp.zeros((DEPTH, DEC_BATCH, 2, SSM_LAT_SEGS, SSM_LANES), F32)
    h0_lat = h0_lat.at[:, :, 0, 0].set(jnp.swapaxes(h0[:, :, 0], 0, 1))
    h0_lat = h0_lat.at[:, :, 1, SSM_LAT_SEGS - 1].set(jnp.swapaxes(h0[:, :, 1], 0, 1))

    vec = lambda a: a[..., None, :]
    prm = {
        'ln_g': vec(ln_g), 'ln_b': vec(ln_b),
        'ffn_w1_f32': ffn_w1, 'ffn_w2_f32': ffn_w2,
        'q_norm': vec(jnp.tile(q_norm, (1, N_HEADS))), 'k_norm': vec(jnp.tile(k_norm, (1, N_KV_HEADS))),
        'chunk_ln_g': vec(chunk_ln_g), 'chunk_ln_b': vec(chunk_ln_b),
        'chunk_w_s': chunk_w_s.astype(BF16),
        'chunk_b_s': jnp.repeat(jnp.swapaxes(chunk_b_s, 1, 2), D_CHUNK // N_CHUNK_HEADS, axis=2),
        'pool_w': _block_diag_pool(pool_w).astype(BF16), 'pool_scale': vec(pool_scale),
        'ssm_b': bmat.reshape(DEPTH, 2, D_SSM, SSM_LANES),
        'ssm_lam': lamb.reshape(DEPTH, 2, 4, SSM_LANES),
        'ssm_c': _place_c(ssm_c_re, ssm_c_im).astype(BF16),
        'ssm_h0': h0_lat,
        'ssm_d': vec(ssm_d), 'ssm_w_glu': ssm_w_glu.astype(BF16), 'ssm_b_glu': vec(ssm_b_glu),
    }
    cache_k = cache_k.reshape(DEC_BATCH, DEPTH, PAST_LEN, D_KV).astype(BF16)
    cache_v = cache_v.reshape(DEC_BATCH, DEPTH, PAST_LEN, D_KV).astype(BF16)

    x = (x_prompt.reshape(N_CTX, D_MODEL), x_sample.reshape(N_LAT, D_MODEL))
    w_ffn = (ffn_w1[0, 0].astype(BF16), ffn_w2[0, 0].astype(BF16))
    ks, vs, s_re, s_im = [], [], [], []
    for l in range(DEPTH):
        x, w_next = _ffn_sublayer(x, mod, prm, l, 0, w_ffn, convert=(l, 1),
                                  convert_stacked=(w_in, w_out) if l == 0 else ())
        w_ffn = w_next[:2]
        if l == 0:
            prm['w_in'], prm['w_out'] = w_next[2:]
        ctx_out, lat_out = _inproj(x, mod, prm, l, cos_t, sin_t)
        ua_ctx, ub_ctx, yc_ctx, k_ctx, v_ctx, yp_ctx, ya_ctx = ctx_out
        ua_lat, ub_lat, yc_lat, k_lat, v_lat, p_lat, q_lat = lat_out
        y_ssm, fin = _ssm((ua_ctx, ub_ctx), (ua_lat, ub_lat), prm, l)
        ya_lat, yp_lat = _latent_mix(q_lat, k_lat, v_lat, p_lat, cache_k, cache_v, prm, l)
        x, w_ffn = _ffn_sublayer(x, mod, prm, l, 1, w_ffn, convert=(l + 1, 0) if l + 1 < DEPTH else None,
                                 split_out=(l == DEPTH - 1),
                                 mixer=((yp_ctx, yp_lat), y_ssm, (ya_ctx, ya_lat), (yc_ctx, yc_lat)))
        ks.append(k_ctx)
        vs.append(v_ctx)
        f_re, f_im = _ssm_unpack_state(jnp.transpose(fin, (0, 2, 1, 3)).reshape(BATCH, 2, SSM_LANES))
        s_re.append(f_re)
        s_im.append(f_im)
    y_p = x[0].reshape(BATCH, SEQ, D_MODEL)
    y_s = x[1].reshape(DEC_BATCH, DEC_SEQ, D_MODEL)
    def cache(ts):
        t = jnp.stack(ts).reshape(DEPTH, N_KV_HEADS, HEAD_DIM, BATCH, SEQ)
        return jnp.transpose(t, (3, 0, 4, 1, 2))

    return (y_p, y_s, cache(ks), cache(vs), jnp.stack(s_re, axis=1), jnp.stack(s_im, axis=1))
```
